```python
import jax, jax.numpy as jnp
from jax import lax
import numpy as np

D_MODEL = 1024
BATCH = 16
SEQ = 4096
DEPTH = 4

CTX_LEN = 256
GRID_W = 64
EPS = 1e-6
NEG_INF = -1e30

HEAD_DIM = 64
ATTN_HEADS = D_MODEL // 128
ATTN_KV_HEADS = ATTN_HEADS // 4
ATTN_GROUP = ATTN_HEADS // ATTN_KV_HEADS
ATTN_WIDTH = ATTN_HEADS * HEAD_DIM
KV_WIDTH = ATTN_KV_HEADS * HEAD_DIM
WINDOW = 128
BLOCK = 128
ROPE_BASE = 10000.0
ROPE_FREQS = HEAD_DIM // 4

SSM_WIDTH = D_MODEL // 2
SSM_GROUP = 16
SSM_GROUPS = SSM_WIDTH // SSM_GROUP
SSM_STATE = 64
DT_MIN = 1e-3
DT_MAX = 1e-1

EVEN_IN = 2 * ATTN_WIDTH + 2 * KV_WIDTH + 2 * SSM_WIDTH
EVEN_MIX = ATTN_WIDTH + SSM_WIDTH

POOL_WIDTH = D_MODEL
POOL_WINDOWS = (2, 4, 8, 16)
POOL_GROUP = POOL_WIDTH // len(POOL_WINDOWS)

kernel_name = 'hybrid_swa_s5_pool_prefix_dit'


def rmsnorm(x, g):
    xf = x.astype(jnp.float32)
    y = xf * lax.rsqrt(jnp.mean(xf * xf, axis=-1, keepdims=True) + EPS)
    return (y * g.astype(jnp.float32)).astype(x.dtype)


def axial_rope_tables(n_tokens):
    rows = n_tokens // GRID_W
    row = jnp.repeat(jnp.arange(rows, dtype=jnp.float32), GRID_W)
    col = jnp.tile(jnp.arange(GRID_W, dtype=jnp.float32), rows)
    inv_freq = ROPE_BASE ** (-jnp.arange(ROPE_FREQS, dtype=jnp.float32) / ROPE_FREQS)
    ang = jnp.stack([row[:, None] * inv_freq, col[:, None] * inv_freq], axis=1)
    ang = ang[:, None]
    return jnp.cos(ang), jnp.sin(ang)


def apply_axial_rope(x, cos, sin):
    b, l, h, _ = x.shape
    xr = x.astype(jnp.float32).reshape(b, l, h, 2, 2, ROPE_FREQS)
    x1, x2 = xr[..., 0, :], xr[..., 1, :]
    out = jnp.stack([x1 * cos - x2 * sin, x2 * cos + x1 * sin], axis=-2)
    return out.reshape(b, l, h, HEAD_DIM).astype(x.dtype)


def window_attention(q, k, v, kc, vc, sink):
    b, l, hkv, g, dh = q.shape
    nb = l // BLOCK
    scale = dh ** -0.5
    qb = q.reshape(b, nb, BLOCK, hkv, g, dh)
    pad = ((0, 0), (BLOCK, BLOCK), (0, 0), (0, 0))
    kp = jnp.pad(k, pad).reshape(b, nb + 2, BLOCK, hkv, dh)
    vp = jnp.pad(v, pad).reshape(b, nb + 2, BLOCK, hkv, dh)
    kb = jnp.concatenate([kp[:, :-2], kp[:, 1:-1], kp[:, 2:]], axis=2)
    vb = jnp.concatenate([vp[:, :-2], vp[:, 1:-1], vp[:, 2:]], axis=2)
    s_loc = jnp.einsum('bnqhgd,bnkhd->bnhgqk', qb, kb).astype(jnp.float32) * scale
    s_ctx = jnp.einsum('bnqhgd,bkhd->bnhgqk', qb, kc).astype(jnp.float32) * scale
    qpos = np.arange(nb)[:, None] * BLOCK + np.arange(BLOCK)[None, :]
    kpos = (np.arange(nb)[:, None] - 1) * BLOCK + np.arange(3 * BLOCK)[None, :]
    valid = ((np.abs(qpos[:, :, None] - kpos[:, None, :]) <= WINDOW)
             & (kpos[:, None, :] >= 0) & (kpos[:, None, :] < l))
    s_loc = jnp.where(valid[None, :, None, None], s_loc, NEG_INF)
    s_sink = jnp.broadcast_to(sink.astype(jnp.float32)[None, None, :, :, None, None],
                              s_loc.shape[:-1] + (1,))
    p = jax.nn.softmax(jnp.concatenate([s_loc, s_ctx, s_sink], axis=-1), axis=-1)
    n_loc = 3 * BLOCK
    n_ctx = kc.shape[1]
    p_loc = p[..., :n_loc].astype(v.dtype)
    p_ctx = p[..., n_loc:n_loc + n_ctx].astype(v.dtype)
    out = (jnp.einsum('bnhgqk,bnkhd->bnqhgd', p_loc, vb)
           + jnp.einsum('bnhgqk,bkhd->bnqhgd', p_ctx, vc))
    return out.reshape(b, l, hkv * g * dh)


def context_attention(qc, kc, vc, sink):
    b, lc, hkv, g, dh = qc.shape
    s = jnp.einsum('bqhgd,bkhd->bhgqk', qc, kc).astype(jnp.float32) * dh ** -0.5
    s_sink = jnp.broadcast_to(sink.astype(jnp.float32)[None, :, :, None, None], s.shape[:-1] + (1,))
    p = jax.nn.softmax(jnp.concatenate([s, s_sink], axis=-1), axis=-1)[..., :lc].astype(vc.dtype)
    return jnp.einsum('bhgqk,bkhd->bqhgd', p, vc).reshape(b, lc, hkv * g * dh)


def s5_discretize(a_re, a_im, log_dt, b_re, b_im):
    lam = lax.complex(a_re.astype(jnp.float32), a_im.astype(jnp.float32))
    dt = jnp.exp(log_dt.astype(jnp.float32))[:, None]
    a_bar = jnp.exp(lam * dt)
    bmat = lax.complex(b_re.astype(jnp.float32), b_im.astype(jnp.float32))
    b_bar = ((a_bar - 1.0) / lam)[..., None] * bmat
    return a_bar, b_bar


def _scan_combine(left, right):
    a_l, h_l = left
    a_r, h_r = right
    return a_l * a_r, a_r * h_l + h_r


def diag_scan(a_bar, bu, h0):
    if h0 is not None:
        bu = bu.at[:, 0].add(a_bar * h0)
    a = jnp.broadcast_to(a_bar, bu.shape)
    _, h = lax.associative_scan(_scan_combine, (a, bu), axis=1)
    return h


def s5_branch(u, uc, a_re, a_im, log_dt, b_re, b_im, c_re, c_im, d_skip, glu_w, glu_b, need_ctx):
    b, l, _ = u.shape
    lc = uc.shape[1]
    ul = u.astype(jnp.float32).reshape(b, l, SSM_GROUPS, SSM_GROUP).astype(jnp.complex64)
    ucg = uc.astype(jnp.float32).reshape(b, lc, SSM_GROUPS, SSM_GROUP).astype(jnp.complex64)
    d = d_skip.astype(jnp.float32)
    y = u.astype(jnp.float32) * d
    yc = uc.astype(jnp.float32) * d if need_ctx else None
    for direction in range(2):
        a_bar, b_bar = s5_discretize(a_re[direction], a_im[direction], log_dt[direction],
                                     b_re[direction], b_im[direction])
        cmat = lax.complex(c_re[direction].astype(jnp.float32), c_im[direction].astype(jnp.float32))
        bu = jnp.einsum('blgc,gpc->blgp', ul, b_bar)
        buc = jnp.einsum('blgc,gpc->blgp', ucg, b_bar)
        if direction == 1:
            bu, buc = bu[:, ::-1], buc[:, ::-1]
        hc = diag_scan(a_bar, buc, None)
        h = diag_scan(a_bar, bu, hc[:, -1])
        if direction == 1:
            h, hc = h[:, ::-1], hc[:, ::-1]
        y = y + jnp.real(jnp.einsum('blgp,gcp->blgc', h, cmat)).reshape(b, l, SSM_WIDTH)
        if need_ctx:
            yc = yc + jnp.real(jnp.einsum('blgp,gcp->blgc', hc, cmat)).reshape(b, lc, SSM_WIDTH)
    gw = glu_w.astype(jnp.float32)
    gb = glu_b.astype(jnp.float32)

    def glu(z):
        z = jax.nn.gelu(z)
        return z * jax.nn.sigmoid(z @ gw + gb)

    out = glu(y).astype(u.dtype)
    out_c = glu(yc).astype(u.dtype) if need_ctx else None
    return out, out_c


def attn_ssm_mixer(a, ac, cos, sin, w_in, w_out, sink, a_re, a_im, log_dt, b_re, b_im,
                   c_re, c_im, d_skip, glu_w, glu_b, need_ctx):
    b, l, _ = a.shape
    lc = ac.shape[1]
    cuts = [int(v) for v in np.cumsum([ATTN_WIDTH, KV_WIDTH, KV_WIDTH, ATTN_WIDTH, SSM_WIDTH])]
    q, k, v, g_attn, u, g_ssm = jnp.split(a @ w_in, cuts, axis=-1)
    qc, kc, vc, g_attn_c, uc, g_ssm_c = jnp.split(ac @ w_in, cuts, axis=-1)
    q = apply_axial_rope(q.reshape(b, l, ATTN_HEADS, HEAD_DIM), cos, sin)
    q = q.reshape(b, l, ATTN_KV_HEADS, ATTN_GROUP, HEAD_DIM)
    k = apply_axial_rope(k.reshape(b, l, ATTN_KV_HEADS, HEAD_DIM), cos, sin)
    v = v.reshape(b, l, ATTN_KV_HEADS, HEAD_DIM)
    kc = kc.reshape(b, lc, ATTN_KV_HEADS, HEAD_DIM)
    vc = vc.reshape(b, lc, ATTN_KV_HEADS, HEAD_DIM)
    sink = sink.reshape(ATTN_KV_HEADS, ATTN_GROUP)
    o_attn = window_attention(q, k, v, kc, vc, sink) * jax.nn.silu(g_attn)
    o_ssm, o_ssm_c = s5_branch(u, uc, a_re, a_im, log_dt, b_re, b_im, c_re, c_im,
                               d_skip, glu_w, glu_b, need_ctx)
    y = jnp.concatenate([o_attn, o_ssm * jax.nn.silu(g_ssm)], axis=-1) @ w_out
    yc = None
    if need_ctx:
        qc = qc.reshape(b, lc, ATTN_KV_HEADS, ATTN_GROUP, HEAD_DIM)
        o_attn_c = context_attention(qc, kc, vc, sink) * jax.nn.silu(g_attn_c)
        yc = jnp.concatenate([o_attn_c, o_ssm_c * jax.nn.silu(g_ssm_c)], axis=-1) @ w_out
    return y, yc


def multiscale_pool(u):
    t_len = u.shape[1]
    uf = u.astype(jnp.float32)
    cs = jnp.pad(jnp.cumsum(uf, axis=1), ((0, 0), (1, 0), (0, 0)))
    pos = np.arange(t_len)
    outs = []
    for gi, w in enumerate(POOL_WINDOWS):
        r = w // 2
        lo = np.clip(pos - r, 0, t_len)
        hi = np.clip(pos + r + 1, 0, t_len)
        inv_cnt = jnp.asarray(1.0 / (hi - lo), dtype=jnp.float32)[None, :, None]
        sl = slice(gi * POOL_GROUP, (gi + 1) * POOL_GROUP)
        csg = cs[..., sl]
        outs.append((csg[:, hi] - csg[:, lo]) * inv_cnt - uf[..., sl])
    return jnp.concatenate(outs, axis=-1)


def pool_mixer(a, w_in, w_out, pool_w, pool_scale):
    u, gate = jnp.split(a @ w_in, 2, axis=-1)
    b, t, _ = u.shape
    p = multiscale_pool(u).reshape(b, t, len(POOL_WINDOWS), POOL_GROUP)
    p = jnp.einsum('btgc,gcd->btgd', p, pool_w.astype(jnp.float32)).reshape(b, t, POOL_WIDTH)
    p = (p * pool_scale.astype(jnp.float32)).astype(a.dtype)
    return (p * jax.nn.silu(gate)) @ w_out


def _fwd_setup_inputs(seed: int = 0) -> dict:
    key = jax.random.key(seed)
    ks = iter(jax.random.split(key, 32))
    n_even = (DEPTH + 1) // 2
    n_odd = DEPTH // 2

    def nrm(shape, scale):
        return jax.random.normal(next(ks), shape, jnp.float32) * scale

    n_idx = jnp.arange(SSM_STATE, dtype=jnp.float32)
    ssm_shape = (n_even, 2, SSM_GROUPS, SSM_STATE)
    return {
        'x': nrm((BATCH, SEQ, D_MODEL), 1.0),
        'c': nrm((BATCH, D_MODEL), 1.0),
        'ctx': nrm((BATCH, CTX_LEN, D_MODEL), 1.0),
        'c_ctx': nrm((D_MODEL,), 1.0),
        'ada_w': nrm((DEPTH, D_MODEL, 3 * D_MODEL), 0.5 * D_MODEL ** -0.5),
        'ada_b': nrm((DEPTH, 3 * D_MODEL), 0.01),
        'norm_g': 1.0 + nrm((DEPTH, D_MODEL), 0.02),
        'even_w_in': nrm((n_even, D_MODEL, EVEN_IN), D_MODEL ** -0.5),
        'even_w_out': nrm((n_even, EVEN_MIX, D_MODEL), EVEN_MIX ** -0.5),
        'attn_sink': nrm((n_even, ATTN_HEADS), 0.5),
        'ssm_a_re': -0.5 + nrm(ssm_shape, 0.01),
        'ssm_a_im': jnp.pi * n_idx + nrm(ssm_shape, 0.01),
        'ssm_log_dt': jax.random.uniform(next(ks), (n_even, 2, SSM_GROUPS), jnp.float32,
                                         np.log(DT_MIN), np.log(DT_MAX)),
        'ssm_b_re': nrm((n_even, 2, SSM_GROUPS, SSM_STATE, SSM_GROUP), (2 * SSM_GROUP) ** -0.5),
        'ssm_b_im': nrm((n_even, 2, SSM_GROUPS, SSM_STATE, SSM_GROUP), (2 * SSM_GROUP) ** -0.5),
        'ssm_c_re': nrm((n_even, 2, SSM_GROUPS, SSM_GROUP, SSM_STATE), (2 * SSM_STATE) ** -0.5),
        'ssm_c_im': nrm((n_even, 2, SSM_GROUPS, SSM_GROUP, SSM_STATE), (2 * SSM_STATE) ** -0.5),
        'ssm_d': nrm((n_even, SSM_WIDTH), 0.5),
        'glu_w': nrm((n_even, SSM_WIDTH, SSM_WIDTH), SSM_WIDTH ** -0.5),
        'glu_b': nrm((n_even, SSM_WIDTH), 0.01),
        'odd_w_in': nrm((n_odd, D_MODEL, 2 * POOL_WIDTH), D_MODEL ** -0.5),
        'odd_w_out': nrm((n_odd, POOL_WIDTH, D_MODEL), POOL_WIDTH ** -0.5),
        'pool_w': nrm((n_odd, len(POOL_WINDOWS), POOL_GROUP, POOL_GROUP), POOL_GROUP ** -0.5),
        'pool_scale': 1.0 + nrm((n_odd, POOL_WIDTH), 0.02),
        'final_g': 1.0 + nrm((D_MODEL,), 0.02),
    }


def _fwd_reference(x, c, ctx, c_ctx, ada_w, ada_b, norm_g, even_w_in, even_w_out, attn_sink,
              ssm_a_re, ssm_a_im, ssm_log_dt, ssm_b_re, ssm_b_im, ssm_c_re, ssm_c_im, ssm_d,
              glu_w, glu_b, odd_w_in, odd_w_out, pool_w, pool_scale, final_g):
    cos, sin = axial_rope_tables(x.shape[1])
    h, hc = x, ctx
    s_lat = jax.nn.silu(c)
    s_ctx = jax.nn.silu(c_ctx)
    for i in range(DEPTH):
        need_ctx = i < DEPTH - 1
        shift, scale, gate = jnp.split((s_lat @ ada_w[i] + ada_b[i])[:, None, :], 3, axis=-1)
        shift_c, scale_c, gate_c = jnp.split(s_ctx @ ada_w[i] + ada_b[i], 3, axis=-1)
        a = rmsnorm(h, norm_g[i]) * (1.0 + scale) + shift
        ac = rmsnorm(hc, norm_g[i]) * (1.0 + scale_c) + shift_c
        j = i // 2
        if i % 2 == 0:
            y, yc = attn_ssm_mixer(a, ac, cos, sin, even_w_in[j], even_w_out[j], attn_sink[j],
                                   ssm_a_re[j], ssm_a_im[j], ssm_log_dt[j], ssm_b_re[j], ssm_b_im[j],
                                   ssm_c_re[j], ssm_c_im[j], ssm_d[j], glu_w[j], glu_b[j], need_ctx)
        else:
            y = pool_mixer(a, odd_w_in[j], odd_w_out[j], pool_w[j], pool_scale[j])
            yc = pool_mixer(ac, odd_w_in[j], odd_w_out[j], pool_w[j], pool_scale[j]) if need_ctx else None
        h = h + gate * y
        if need_ctx:
            hc = hc + gate_c * yc
    return rmsnorm(h, final_g)


import jax as _jax
import jax.numpy as _jnp

TWIN_FORMAT = 'train_step'
FWD_PARAMS = ['x', 'c', 'ctx', 'c_ctx', 'ada_w', 'ada_b', 'norm_g', 'even_w_in', 'even_w_out', 'attn_sink', 'ssm_a_re', 'ssm_a_im', 'ssm_log_dt', 'ssm_b_re', 'ssm_b_im', 'ssm_c_re', 'ssm_c_im', 'ssm_d', 'glu_w', 'glu_b', 'odd_w_in', 'odd_w_out', 'pool_w', 'pool_scale', 'final_g']
TWIN_WEIGHTS = ['c_ctx', 'ada_w', 'ada_b', 'norm_g', 'even_w_in', 'even_w_out', 'attn_sink', 'ssm_a_re', 'ssm_a_im', 'ssm_log_dt', 'ssm_b_re', 'ssm_b_im', 'ssm_c_re', 'ssm_c_im', 'ssm_d', 'glu_w', 'glu_b', 'odd_w_in', 'odd_w_out', 'pool_w', 'pool_scale', 'final_g']
TWIN_DIFF_INPUT = 'x'
TWIN_INPUTS = ['x', 'c', 'ctx', 'c_ctx', 'ada_w', 'ada_b', 'norm_g', 'even_w_in', 'even_w_out', 'attn_sink', 'ssm_a_re', 'ssm_a_im', 'ssm_log_dt', 'ssm_b_re', 'ssm_b_im', 'ssm_c_re', 'ssm_c_im', 'ssm_d', 'glu_w', 'glu_b', 'odd_w_in', 'odd_w_out', 'pool_w', 'pool_scale', 'final_g', 'loss_target', 'm_c_ctx', 'm_ada_w', 'm_ada_b', 'm_norm_g', 'm_even_w_in', 'm_even_w_out', 'm_attn_sink', 'm_ssm_a_re', 'm_ssm_a_im', 'm_ssm_log_dt', 'm_ssm_b_re', 'm_ssm_b_im', 'm_ssm_c_re', 'm_ssm_c_im', 'm_ssm_d', 'm_glu_w', 'm_glu_b', 'm_odd_w_in', 'm_odd_w_out', 'm_pool_w', 'm_pool_scale', 'm_final_g', 'v_c_ctx', 'v_ada_w', 'v_ada_b', 'v_norm_g', 'v_even_w_in', 'v_even_w_out', 'v_attn_sink', 'v_ssm_a_re', 'v_ssm_a_im', 'v_ssm_log_dt', 'v_ssm_b_re', 'v_ssm_b_im', 'v_ssm_c_re', 'v_ssm_c_im', 'v_ssm_d', 'v_glu_w', 'v_glu_b', 'v_odd_w_in', 'v_odd_w_out', 'v_pool_w', 'v_pool_scale', 'v_final_g']
TWIN_OUTPUTS = ['loss', 'grad_x', 'grad_c_ctx', 'grad_ada_w', 'grad_ada_b', 'grad_norm_g', 'grad_even_w_in', 'grad_even_w_out', 'grad_attn_sink', 'grad_ssm_a_re', 'grad_ssm_a_im', 'grad_ssm_log_dt', 'grad_ssm_b_re', 'grad_ssm_b_im', 'grad_ssm_c_re', 'grad_ssm_c_im', 'grad_ssm_d', 'grad_glu_w', 'grad_glu_b', 'grad_odd_w_in', 'grad_odd_w_out', 'grad_pool_w', 'grad_pool_scale', 'grad_final_g', 'delta_c_ctx', 'delta_ada_w', 'delta_ada_b', 'delta_norm_g', 'delta_even_w_in', 'delta_even_w_out', 'delta_attn_sink', 'delta_ssm_a_re', 'delta_ssm_a_im', 'delta_ssm_log_dt', 'delta_ssm_b_re', 'delta_ssm_b_im', 'delta_ssm_c_re', 'delta_ssm_c_im', 'delta_ssm_d', 'delta_glu_w', 'delta_glu_b', 'delta_odd_w_in', 'delta_odd_w_out', 'delta_pool_w', 'delta_pool_scale', 'delta_final_g', 'new_m_c_ctx', 'new_m_ada_w', 'new_m_ada_b', 'new_m_norm_g', 'new_m_even_w_in', 'new_m_even_w_out', 'new_m_attn_sink', 'new_m_ssm_a_re', 'new_m_ssm_a_im', 'new_m_ssm_log_dt', 'new_m_ssm_b_re', 'new_m_ssm_b_im', 'new_m_ssm_c_re', 'new_m_ssm_c_im', 'new_m_ssm_d', 'new_m_glu_w', 'new_m_glu_b', 'new_m_odd_w_in', 'new_m_odd_w_out', 'new_m_pool_w', 'new_m_pool_scale', 'new_m_final_g', 'new_v_c_ctx', 'new_v_ada_w', 'new_v_ada_b', 'new_v_norm_g', 'new_v_even_w_in', 'new_v_even_w_out', 'new_v_attn_sink', 'new_v_ssm_a_re', 'new_v_ssm_a_im', 'new_v_ssm_log_dt', 'new_v_ssm_b_re', 'new_v_ssm_b_im', 'new_v_ssm_c_re', 'new_v_ssm_c_im', 'new_v_ssm_d', 'new_v_glu_w', 'new_v_glu_b', 'new_v_odd_w_in', 'new_v_odd_w_out', 'new_v_pool_w', 'new_v_pool_scale', 'new_v_final_g']
TWIN_LEAF_KINDS = {'loss': 'loss', 'grad_x': 'grad_x', 'grad_c_ctx': 'grad_w', 'grad_ada_w': 'grad_w', 'grad_ada_b': 'grad_w', 'grad_norm_g': 'grad_w', 'grad_even_w_in': 'grad_w', 'grad_even_w_out': 'grad_w', 'grad_attn_sink': 'grad_w', 'grad_ssm_a_re': 'grad_w', 'grad_ssm_a_im': 'grad_w', 'grad_ssm_log_dt': 'grad_w', 'grad_ssm_b_re': 'grad_w', 'grad_ssm_b_im': 'grad_w', 'grad_ssm_c_re': 'grad_w', 'grad_ssm_c_im': 'grad_w', 'grad_ssm_d': 'grad_w', 'grad_glu_w': 'grad_w', 'grad_glu_b': 'grad_w', 'grad_odd_w_in': 'grad_w', 'grad_odd_w_out': 'grad_w', 'grad_pool_w': 'grad_w', 'grad_pool_scale': 'grad_w', 'grad_final_g': 'grad_w', 'delta_c_ctx': 'delta_w', 'delta_ada_w': 'delta_w', 'delta_ada_b': 'delta_w', 'delta_norm_g': 'delta_w', 'delta_even_w_in': 'delta_w', 'delta_even_w_out': 'delta_w', 'delta_attn_sink': 'delta_w', 'delta_ssm_a_re': 'delta_w', 'delta_ssm_a_im': 'delta_w', 'delta_ssm_log_dt': 'delta_w', 'delta_ssm_b_re': 'delta_w', 'delta_ssm_b_im': 'delta_w', 'delta_ssm_c_re': 'delta_w', 'delta_ssm_c_im': 'delta_w', 'delta_ssm_d': 'delta_w', 'delta_glu_w': 'delta_w', 'delta_glu_b': 'delta_w', 'delta_odd_w_in': 'delta_w', 'delta_odd_w_out': 'delta_w', 'delta_pool_w': 'delta_w', 'delta_pool_scale': 'delta_w', 'delta_final_g': 'delta_w', 'new_m_c_ctx': 'new_m', 'new_m_ada_w': 'new_m', 'new_m_ada_b': 'new_m', 'new_m_norm_g': 'new_m', 'new_m_even_w_in': 'new_m', 'new_m_even_w_out': 'new_m', 'new_m_attn_sink': 'new_m', 'new_m_ssm_a_re': 'new_m', 'new_m_ssm_a_im': 'new_m', 'new_m_ssm_log_dt': 'new_m', 'new_m_ssm_b_re': 'new_m', 'new_m_ssm_b_im': 'new_m', 'new_m_ssm_c_re': 'new_m', 'new_m_ssm_c_im': 'new_m', 'new_m_ssm_d': 'new_m', 'new_m_glu_w': 'new_m', 'new_m_glu_b': 'new_m', 'new_m_odd_w_in': 'new_m', 'new_m_odd_w_out': 'new_m', 'new_m_pool_w': 'new_m', 'new_m_pool_scale': 'new_m', 'new_m_final_g': 'new_m', 'new_v_c_ctx': 'new_v', 'new_v_ada_w': 'new_v', 'new_v_ada_b': 'new_v', 'new_v_norm_g': 'new_v', 'new_v_even_w_in': 'new_v', 'new_v_even_w_out': 'new_v', 'new_v_attn_sink': 'new_v', 'new_v_ssm_a_re': 'new_v', 'new_v_ssm_a_im': 'new_v', 'new_v_ssm_log_dt': 'new_v', 'new_v_ssm_b_re': 'new_v', 'new_v_ssm_b_im': 'new_v', 'new_v_ssm_c_re': 'new_v', 'new_v_ssm_c_im': 'new_v', 'new_v_ssm_d': 'new_v', 'new_v_glu_w': 'new_v', 'new_v_glu_b': 'new_v', 'new_v_odd_w_in': 'new_v', 'new_v_odd_w_out': 'new_v', 'new_v_pool_w': 'new_v', 'new_v_pool_scale': 'new_v', 'new_v_final_g': 'new_v'}


def _forward(args):
    return _fwd_reference(*[args[k] for k in FWD_PARAMS])


def _output_shape():
    out = _jax.eval_shape(lambda: _forward(_fwd_setup_inputs(0)))
    return out.shape, out.dtype

N_MICROBATCH = 1
ADAM_LR = 0.001
ADAM_B1 = 0.9
ADAM_B2 = 0.999
ADAM_EPS = 1e-08
ADAM_WD = 0.01
ADAM_STEP = 10
PER_EXAMPLE_BATCH_AXIS = {'x': 0, 'c': 0, 'ctx': 0, 'loss_target': 0}
SHARED_INPUTS = []
_WEIGHT_DTYPES = {'c_ctx': _jnp.float32, 'ada_w': _jnp.float32, 'ada_b': _jnp.float32, 'norm_g': _jnp.float32, 'even_w_in': _jnp.float32, 'even_w_out': _jnp.float32, 'attn_sink': _jnp.float32, 'ssm_a_re': _jnp.float32, 'ssm_a_im': _jnp.float32, 'ssm_log_dt': _jnp.float32, 'ssm_b_re': _jnp.float32, 'ssm_b_im': _jnp.float32, 'ssm_c_re': _jnp.float32, 'ssm_c_im': _jnp.float32, 'ssm_d': _jnp.float32, 'glu_w': _jnp.float32, 'glu_b': _jnp.float32, 'odd_w_in': _jnp.float32, 'odd_w_out': _jnp.float32, 'pool_w': _jnp.float32, 'pool_scale': _jnp.float32, 'final_g': _jnp.float32}
MOMENT_SCALE = {'c_ctx': 9.412196e-03, 'ada_w': 4.645482e-02, 'ada_b': 7.648127e-02, 'norm_g': 4.878924e-02, 'even_w_in': 1.186685e-02, 'even_w_out': 1.161006e-02, 'attn_sink': 1.659299e-04, 'ssm_a_re': 1.444669e-03, 'ssm_a_im': 1.137619e-03, 'ssm_log_dt': 6.537915e-01, 'ssm_b_re': 7.403694e-04, 'ssm_b_im': 6.972295e-04, 'ssm_c_re': 1.384238e-03, 'ssm_c_im': 1.360657e-03, 'ssm_d': 1.842699e-02, 'glu_w': 1.437371e-03, 'glu_b': 4.138380e-03, 'odd_w_in': 4.818546e-02, 'odd_w_out': 4.728972e-02, 'pool_w': 4.742791e-02, 'pool_scale': 4.776570e-02, 'final_g': 6.404196e+01}


def _to_microbatches(a, axis):
    t = _jnp.moveaxis(a, axis, 0)
    t = t.reshape((N_MICROBATCH, t.shape[0] // N_MICROBATCH) + t.shape[1:])
    return _jnp.moveaxis(t, 1, axis + 1)


def setup_inputs(seed: int = 0) -> dict:
    inp = _fwd_setup_inputs(seed)
    key = _jax.random.fold_in(_jax.random.key(seed), 7919)
    shape, _ = _output_shape()
    out = dict(inp)
    out["loss_target"] = _jax.random.normal(_jax.random.fold_in(key, 0), shape, _jnp.float32)
    for i, name in enumerate(TWIN_WEIGHTS):
        w = inp[name].astype(_jnp.float32)
        if MOMENT_SCALE is None:
            s = _jnp.sqrt(_jnp.mean(_jnp.square(w)) + 1e-30)
        else:
            s = MOMENT_SCALE[name]
        km, kv = _jax.random.split(_jax.random.fold_in(key, i + 1))
        out[name] = w
        out["m_" + name] = s * _jax.random.normal(km, w.shape, _jnp.float32)
        out["v_" + name] = (s * s) * _jax.random.uniform(kv, w.shape, _jnp.float32, 0.5, 1.5)
    if N_MICROBATCH > 1:
        for name, axis in PER_EXAMPLE_BATCH_AXIS.items():
            out[name] = _to_microbatches(out[name], axis)
    return {'x': out['x'], 'c': out['c'], 'ctx': out['ctx'], 'c_ctx': out['c_ctx'], 'ada_w': out['ada_w'], 'ada_b': out['ada_b'], 'norm_g': out['norm_g'], 'even_w_in': out['even_w_in'], 'even_w_out': out['even_w_out'], 'attn_sink': out['attn_sink'], 'ssm_a_re': out['ssm_a_re'], 'ssm_a_im': out['ssm_a_im'], 'ssm_log_dt': out['ssm_log_dt'], 'ssm_b_re': out['ssm_b_re'], 'ssm_b_im': out['ssm_b_im'], 'ssm_c_re': out['ssm_c_re'], 'ssm_c_im': out['ssm_c_im'], 'ssm_d': out['ssm_d'], 'glu_w': out['glu_w'], 'glu_b': out['glu_b'], 'odd_w_in': out['odd_w_in'], 'odd_w_out': out['odd_w_out'], 'pool_w': out['pool_w'], 'pool_scale': out['pool_scale'], 'final_g': out['final_g'], 'loss_target': out['loss_target'], 'm_c_ctx': out['m_c_ctx'], 'm_ada_w': out['m_ada_w'], 'm_ada_b': out['m_ada_b'], 'm_norm_g': out['m_norm_g'], 'm_even_w_in': out['m_even_w_in'], 'm_even_w_out': out['m_even_w_out'], 'm_attn_sink': out['m_attn_sink'], 'm_ssm_a_re': out['m_ssm_a_re'], 'm_ssm_a_im': out['m_ssm_a_im'], 'm_ssm_log_dt': out['m_ssm_log_dt'], 'm_ssm_b_re': out['m_ssm_b_re'], 'm_ssm_b_im': out['m_ssm_b_im'], 'm_ssm_c_re': out['m_ssm_c_re'], 'm_ssm_c_im': out['m_ssm_c_im'], 'm_ssm_d': out['m_ssm_d'], 'm_glu_w': out['m_glu_w'], 'm_glu_b': out['m_glu_b'], 'm_odd_w_in': out['m_odd_w_in'], 'm_odd_w_out': out['m_odd_w_out'], 'm_pool_w': out['m_pool_w'], 'm_pool_scale': out['m_pool_scale'], 'm_final_g': out['m_final_g'], 'v_c_ctx': out['v_c_ctx'], 'v_ada_w': out['v_ada_w'], 'v_ada_b': out['v_ada_b'], 'v_norm_g': out['v_norm_g'], 'v_even_w_in': out['v_even_w_in'], 'v_even_w_out': out['v_even_w_out'], 'v_attn_sink': out['v_attn_sink'], 'v_ssm_a_re': out['v_ssm_a_re'], 'v_ssm_a_im': out['v_ssm_a_im'], 'v_ssm_log_dt': out['v_ssm_log_dt'], 'v_ssm_b_re': out['v_ssm_b_re'], 'v_ssm_b_im': out['v_ssm_b_im'], 'v_ssm_c_re': out['v_ssm_c_re'], 'v_ssm_c_im': out['v_ssm_c_im'], 'v_ssm_d': out['v_ssm_d'], 'v_glu_w': out['v_glu_w'], 'v_glu_b': out['v_glu_b'], 'v_odd_w_in': out['v_odd_w_in'], 'v_odd_w_out': out['v_odd_w_out'], 'v_pool_w': out['v_pool_w'], 'v_pool_scale': out['v_pool_scale'], 'v_final_g': out['v_final_g']}


def _loss(weights, diff, rest, loss_target):
    with _jax.named_scope("forward"):
        args = {**rest, TWIN_DIFF_INPUT: diff, **{k: w.astype(_WEIGHT_DTYPES[k]) for k, w in weights.items()}}
        y = _forward(args)
    with _jax.named_scope("loss_head"):
        err = _jnp.square(y.astype(_jnp.float32) - loss_target)
        return 0.5 * _jnp.sum(_jnp.mean(err, axis=-1)) if err.ndim else 0.5 * err


def _adamw(w, g, m, v):
    m = ADAM_B1 * m + (1.0 - ADAM_B1) * g
    v = ADAM_B2 * v + (1.0 - ADAM_B2) * _jnp.square(g)
    m_hat = m / (1.0 - ADAM_B1 ** ADAM_STEP)
    v_hat = v / (1.0 - ADAM_B2 ** ADAM_STEP)
    delta = -ADAM_LR * (m_hat / (_jnp.sqrt(v_hat) + ADAM_EPS) + ADAM_WD * w)
    return delta, m, v


def reference(x, c, ctx, c_ctx, ada_w, ada_b, norm_g, even_w_in, even_w_out, attn_sink, ssm_a_re, ssm_a_im, ssm_log_dt, ssm_b_re, ssm_b_im, ssm_c_re, ssm_c_im, ssm_d, glu_w, glu_b, odd_w_in, odd_w_out, pool_w, pool_scale, final_g, loss_target, m_c_ctx, m_ada_w, m_ada_b, m_norm_g, m_even_w_in, m_even_w_out, m_attn_sink, m_ssm_a_re, m_ssm_a_im, m_ssm_log_dt, m_ssm_b_re, m_ssm_b_im, m_ssm_c_re, m_ssm_c_im, m_ssm_d, m_glu_w, m_glu_b, m_odd_w_in, m_odd_w_out, m_pool_w, m_pool_scale, m_final_g, v_c_ctx, v_ada_w, v_ada_b, v_norm_g, v_even_w_in, v_even_w_out, v_attn_sink, v_ssm_a_re, v_ssm_a_im, v_ssm_log_dt, v_ssm_b_re, v_ssm_b_im, v_ssm_c_re, v_ssm_c_im, v_ssm_d, v_glu_w, v_glu_b, v_odd_w_in, v_odd_w_out, v_pool_w, v_pool_scale, v_final_g):
    given = dict(x=x, c=c, ctx=ctx, c_ctx=c_ctx, ada_w=ada_w, ada_b=ada_b, norm_g=norm_g, even_w_in=even_w_in, even_w_out=even_w_out, attn_sink=attn_sink, ssm_a_re=ssm_a_re, ssm_a_im=ssm_a_im, ssm_log_dt=ssm_log_dt, ssm_b_re=ssm_b_re, ssm_b_im=ssm_b_im, ssm_c_re=ssm_c_re, ssm_c_im=ssm_c_im, ssm_d=ssm_d, glu_w=glu_w, glu_b=glu_b, odd_w_in=odd_w_in, odd_w_out=odd_w_out, pool_w=pool_w, pool_scale=pool_scale, final_g=final_g, loss_target=loss_target, m_c_ctx=m_c_ctx, m_ada_w=m_ada_w, m_ada_b=m_ada_b, m_norm_g=m_norm_g, m_even_w_in=m_even_w_in, m_even_w_out=m_even_w_out, m_attn_sink=m_attn_sink, m_ssm_a_re=m_ssm_a_re, m_ssm_a_im=m_ssm_a_im, m_ssm_log_dt=m_ssm_log_dt, m_ssm_b_re=m_ssm_b_re, m_ssm_b_im=m_ssm_b_im, m_ssm_c_re=m_ssm_c_re, m_ssm_c_im=m_ssm_c_im, m_ssm_d=m_ssm_d, m_glu_w=m_glu_w, m_glu_b=m_glu_b, m_odd_w_in=m_odd_w_in, m_odd_w_out=m_odd_w_out, m_pool_w=m_pool_w, m_pool_scale=m_pool_scale, m_final_g=m_final_g, v_c_ctx=v_c_ctx, v_ada_w=v_ada_w, v_ada_b=v_ada_b, v_norm_g=v_norm_g, v_even_w_in=v_even_w_in, v_even_w_out=v_even_w_out, v_attn_sink=v_attn_sink, v_ssm_a_re=v_ssm_a_re, v_ssm_a_im=v_ssm_a_im, v_ssm_log_dt=v_ssm_log_dt, v_ssm_b_re=v_ssm_b_re, v_ssm_b_im=v_ssm_b_im, v_ssm_c_re=v_ssm_c_re, v_ssm_c_im=v_ssm_c_im, v_ssm_d=v_ssm_d, v_glu_w=v_glu_w, v_glu_b=v_glu_b, v_odd_w_in=v_odd_w_in, v_odd_w_out=v_odd_w_out, v_pool_w=v_pool_w, v_pool_scale=v_pool_scale, v_final_g=v_final_g)
    weights = {n: given[n] for n in TWIN_WEIGHTS}
    shared = {n: given[n] for n in SHARED_INPUTS}
    per_example = {n: given[n] for n in ['x', 'c', 'ctx']}
    grad_fn = _jax.value_and_grad(_loss, argnums=(0, 1))

    def one_microbatch(ex, loss_target):
        ex = dict(ex)
        diff = ex.pop(TWIN_DIFF_INPUT)
        return grad_fn(weights, diff, {**shared, **ex}, loss_target)

    if N_MICROBATCH == 1:
        loss, (grad_w, grad_x) = one_microbatch(per_example, given["loss_target"])
    else:
        def body(carry, xs):
            loss_sum, grad_sum = carry
            l_k, (gw_k, gx_k) = one_microbatch(xs[0], xs[1])
            with _jax.named_scope("update"):
                return (loss_sum + l_k, _jax.tree.map(_jnp.add, grad_sum, gw_k)), gx_k

        init = (_jnp.zeros((), _jnp.float32), _jax.tree.map(_jnp.zeros_like, weights))
        (loss, grad_w), grad_x = _jax.lax.scan(body, init, (per_example, given["loss_target"]))
    with _jax.named_scope("update"):
        delta_w, new_m, new_v = {}, {}, {}
        for n in TWIN_WEIGHTS:
            delta_w[n], new_m[n], new_v[n] = _adamw(weights[n], grad_w[n], given["m_" + n], given["v_" + n])
    return (loss, grad_x, *[grad_w[n] for n in TWIN_WEIGHTS], *[delta_w[n] for n in TWIN_WEIGHTS],
            *[new_m[n] for n in TWIN_WEIGHTS], *[new_v[n] for n in TWIN_WEIGHTS])
```

```python
import functools

import numpy as np
import jax
import jax.numpy as jnp
from jax import lax
from jax.experimental import pallas as pl
from jax.experimental.pallas import tpu as pltpu

F32 = jnp.float32
BF16 = jnp.bfloat16
EPS = 1e-6
NEG_INF = -1e30
N_DEV = 8
GRID_W = 64
WINDOW = 128
QBLK = 128
ROWT = 256
CHUNK = 16
SSM_GROUPS = 32
SSM_GROUP = 16
SSM_STATE = 64
POOL_WINDOWS = (2, 4, 8, 16)
ROPE_BASE = 10000.0
ADAM_LR, ADAM_B1, ADAM_B2, ADAM_EPS, ADAM_WD, ADAM_STEP = 0.001, 0.9, 0.999, 1e-08, 0.01, 10
LANES = 128
PACK_W = 1024
PACK_ROWS = 64
VMEM_BIG = 56 << 20


def _cp(vmem=None):
    return pltpu.CompilerParams(vmem_limit_bytes=vmem) if vmem else pltpu.CompilerParams()


def _nt(a, b):
    return lax.dot_general(a, b, (((1,), (1,)), ((), ())), preferred_element_type=F32)


def _tn(a, b):
    return lax.dot_general(a, b, (((0,), (0,)), ((), ())), preferred_element_type=F32)


def _nn(a, b):
    return jnp.dot(a, b, preferred_element_type=F32)


def _bf(x):
    return x.astype(BF16)


def _my_index():
    return 4 * lax.axis_index("x") + 2 * lax.axis_index("y") + lax.axis_index("c")


def _exchange(inputs, specs, name):
    def part_shape(ii, li):
        return inputs[ii].shape if li is None else inputs[ii].shape[1:]

    def piece_shape(kind, shp, r):
        if kind == "gather":
            return tuple(shp)
        if kind == "rows":
            return (r,) + tuple(shp[1:])
        return (shp[0], r) + tuple(shp[2:])

    out_shapes = []
    for kind, parts, r in specs:
        ii, li = parts[0]
        out_shapes.append(jax.ShapeDtypeStruct((len(parts), N_DEV) + piece_shape(kind, part_shape(ii, li), r),
                                               inputs[ii].dtype))
    n_parts = sum(len(parts) for _, parts, _ in specs)
    ni, ns = len(inputs), len(specs)

    def body(*refs):
        in_refs, out_refs = refs[:ni], refs[ni:ni + ns]
        send_sems, recv_sems, loc_sems = refs[ni + ns:]
        x, y, c = lax.axis_index("x"), lax.axis_index("y"), lax.axis_index("c")
        me = 4 * x + 2 * y + c

        def piece(ref, kind, r, p):
            if kind == "gather":
                return ref
            if kind == "rows":
                return ref.at[pl.ds(p * r, r)]
            return ref.at[:, pl.ds(p * r, r)]

        copies, q = [], 0
        for si, (kind, parts, r) in enumerate(specs):
            for pi, (ii, li) in enumerate(parts):
                src = in_refs[ii] if li is None else in_refs[ii].at[li]
                dst = out_refs[si].at[pi, me]
                own = pltpu.make_async_copy(piece(src, kind, r, me), dst, loc_sems.at[q])
                own.start()
                copies.append(own)
                for k in range(1, N_DEV):
                    px = 1 - x if k & 4 else x
                    py = 1 - y if k & 2 else y
                    pc = 1 - c if k & 1 else c
                    cp = pltpu.make_async_remote_copy(
                        src_ref=piece(src, kind, r, 4 * px + 2 * py + pc), dst_ref=dst,
                        send_sem=send_sems.at[q * (N_DEV - 1) + k - 1], recv_sem=recv_sems.at[q * (N_DEV - 1) + k - 1],
                        device_id=(px, py, pc), device_id_type=pl.DeviceIdType.MESH)
                    cp.start()
                    copies.append(cp)
                q += 1
        for cp in copies:
            cp.wait()

    return pl.pallas_call(
        body, name=name, out_shape=tuple(out_shapes),
        in_specs=[pl.BlockSpec(memory_space=pl.ANY)] * ni,
        out_specs=tuple(pl.BlockSpec(memory_space=pl.ANY) for _ in specs),
        scratch_shapes=[pltpu.SemaphoreType.DMA((n_parts * (N_DEV - 1),)),
                        pltpu.SemaphoreType.DMA((n_parts * (N_DEV - 1),)),
                        pltpu.SemaphoreType.DMA((n_parts,))],
        compiler_params=pltpu.CompilerParams(has_side_effects=True),
    )(*inputs)


def _adaln_fwd(craw, ada_w, ada_b):
    nl, d, n = ada_w.shape
    r = craw.shape[0]

    def body(c_ref, w_ref, b_ref, o_ref):
        s = jax.nn.silu(c_ref[...])
        o_ref[...] = _nn(_bf(s), _bf(w_ref[...])) + b_ref[...]

    return pl.pallas_call(
        body, name="adaln_fwd", grid=(nl,),
        in_specs=[pl.BlockSpec((r, d), lambda i: (0, 0)), pl.BlockSpec((None, d, n), lambda i: (i, 0, 0)),
                  pl.BlockSpec((None, 1, n), lambda i: (i, 0, 0))],
        out_specs=pl.BlockSpec((None, r, n), lambda i: (i, 0, 0)),
        out_shape=jax.ShapeDtypeStruct((nl, r, n), F32),
    )(craw, ada_w, ada_b)


def _adaln_bwd(craw, dm, ada_w):
    nl, d, n = ada_w.shape
    r = craw.shape[0]

    def body(c_ref, dm_ref, w_ref, gw_ref, dc_ref, acc_ref):
        i = pl.program_id(0)
        s, vj = jax.vjp(jax.nn.silu, c_ref[...])
        dmb = _bf(dm_ref[...])
        gw_ref[...] = _tn(_bf(s), dmb)

        @pl.when(i == 0)
        def _():
            acc_ref[...] = jnp.zeros_like(acc_ref)
        acc_ref[...] += _nt(dmb, _bf(w_ref[...]))

        @pl.when(i == nl - 1)
        def _():
            dc_ref[...] = vj(acc_ref[...])[0]

    return pl.pallas_call(
        body, name="adaln_bwd", grid=(nl,),
        in_specs=[pl.BlockSpec((r, d), lambda i: (0, 0)), pl.BlockSpec((None, r, n), lambda i: (i, 0, 0)),
                  pl.BlockSpec((None, d, n), lambda i: (i, 0, 0))],
        out_specs=[pl.BlockSpec((None, d, n), lambda i: (i, 0, 0)), pl.BlockSpec((r, d), lambda i: (0, 0))],
        out_shape=(jax.ShapeDtypeStruct((nl, d, n), F32), jax.ShapeDtypeStruct((r, d), F32)),
        scratch_shapes=[pltpu.VMEM((r, d), F32)],
    )(craw, dm, ada_w)


def _norm_mod(h, mod, g):
    ms = jnp.mean(h * h, axis=-1, keepdims=True)
    y = h * lax.rsqrt(ms + EPS) * g
    return y * (1.0 + mod[1:2]) + mod[0:1]


def _mod_spec(i, nct, d):
    return pl.BlockSpec((None, None, 3, d), lambda b, j: (i, 2 * b + jnp.where(j >= nct, 1, 0), 0, 0))


def _dmod_spec(nct, d):
    return pl.BlockSpec((None, 3, d), lambda b, j: (2 * b + jnp.where(j >= nct, 1, 0), 0, 0))


def _layer_spec(li, *shape):
    return pl.BlockSpec((None,) + tuple(shape), lambda b, j: (li,) + (0,) * len(shape))


def _tile(width, col_blk=0):
    return pl.BlockSpec((None, ROWT, width), lambda b, j: (b, j, col_blk))


def _fused_in_fwd(hs, mods, g, wt, i, jl, nct, name):
    bsz, t, d = hs.shape
    n = wt.shape[1]

    def body(h_ref, mod_ref, g_ref, w_ref, z_ref):
        a = _norm_mod(h_ref[...], mod_ref[...], g_ref[...])
        z_ref[...] = _nt(_bf(a), w_ref[...])

    return pl.pallas_call(
        body, name=name, grid=(bsz, t // ROWT),
        in_specs=[_tile(d), _mod_spec(i, nct, d), _layer_spec(i, 1, d), _layer_spec(jl, n, d)],
        out_specs=_tile(n), out_shape=jax.ShapeDtypeStruct((bsz, t, n), F32),
        compiler_params=_cp(VMEM_BIG),
    )(hs, mods, g, wt)


def _fused_in_bwd(hs, mods, g, wt, dh_out, pieces, small, assemble, i, jl, nct, name):
    bsz, t, d = hs.shape
    n = wt.shape[1]
    npc, nsm = len(pieces), int(small is not None)

    def body(*refs):
        h_ref, mod_ref, g_ref, w_ref, dho_ref = refs[:5]
        p_refs = refs[5:5 + npc]
        s_val = refs[5 + npc][...] if nsm else None
        dhi_ref, dmod_ref, dg_ref, dw_ref = refs[5 + npc + nsm:]
        b, j = pl.program_id(0), pl.program_id(1)
        a, vj = jax.vjp(_norm_mod, h_ref[...], mod_ref[...], g_ref[...])
        dz = _bf(assemble([r[...] for r in p_refs], s_val))
        dh, dmod, dg = vj(_nn(dz, w_ref[...]))
        dhi_ref[...] = dho_ref[...] + dh

        @pl.when((j == 0) | (j == nct))
        def _():
            dmod_ref[...] = jnp.zeros_like(dmod_ref)

        @pl.when((b == 0) & (j == 0))
        def _():
            dg_ref[...] = jnp.zeros_like(dg_ref)
            dw_ref[...] = jnp.zeros_like(dw_ref)
        dmod_ref[...] += dmod
        dg_ref[...] += dg
        dw_ref[...] += _tn(dz, _bf(a))

    full = lambda r, c: pl.BlockSpec((r, c), lambda b, j: (0, 0))
    return pl.pallas_call(
        body, name=name, grid=(bsz, t // ROWT),
        in_specs=[_tile(d), _mod_spec(i, nct, d), _layer_spec(i, 1, d), _layer_spec(jl, n, d), _tile(d)]
                 + [_tile(wd, cb) for _, wd, cb in pieces]
                 + ([_layer_spec(small[1], 1, small[0].shape[2])] if nsm else []),
        out_specs=[_tile(d), _dmod_spec(nct, d), full(1, d), full(n, d)],
        out_shape=(jax.ShapeDtypeStruct((bsz, t, d), F32), jax.ShapeDtypeStruct((2 * bsz, 3, d), F32),
                   jax.ShapeDtypeStruct((1, d), F32), jax.ShapeDtypeStruct((n, d), F32)),
        compiler_params=_cp(VMEM_BIG),
    )(hs, mods, g, wt, dh_out, *[p for p, _, _ in pieces], *([small[0]] if nsm else []))


def _rope_tables(lc, l):
    tpos = np.arange(l)
    row = (tpos // GRID_W).astype(np.float32)
    col = (tpos % GRID_W).astype(np.float32)
    nf = 16
    inv = (np.float32(ROPE_BASE) ** (-np.arange(nf, dtype=np.float32) / np.float32(nf))).astype(np.float32)
    ang_r = row[:, None] * inv[None]
    ang_c = col[:, None] * inv[None]
    cos64 = np.concatenate([np.cos(ang_r), np.cos(ang_r), np.cos(ang_c), np.cos(ang_c)], axis=1)
    sin64 = np.concatenate([-np.sin(ang_r), np.sin(ang_r), -np.sin(ang_c), np.sin(ang_c)], axis=1)
    cos = np.concatenate([np.ones((lc, 128), np.float32), np.tile(cos64, (1, 2)).astype(np.float32)], axis=0)
    sin = np.concatenate([np.zeros((lc, 128), np.float32), np.tile(sin64, (1, 2)).astype(np.float32)], axis=0)
    return jnp.asarray(cos), jnp.asarray(sin)


def _rot(x, cos, sin):
    lane = lax.broadcasted_iota(jnp.int32, x.shape, 1)
    first = jnp.bitwise_and(jnp.right_shift(lane, 4), 1) == 0
    partner = jnp.where(first, pltpu.roll(x, LANES - 16, 1), pltpu.roll(x, 16, 1))
    return x * cos + partner * sin


def _split_heads(x):
    low = lax.broadcasted_iota(jnp.int32, x.shape, 1) < 64
    return jnp.where(low, x, 0.0), jnp.where(low, pltpu.roll(x, 64, 1), 0.0)


def _merge_heads(a, b):
    return a + pltpu.roll(b, 64, 1)


def _rope_fwd(z, cos, sin, kv_blk, name):
    bsz, t, _ = z.shape

    def body(q_ref, kv_ref, cos_ref, sin_ref, qp_ref, kvp_ref):
        c, s = cos_ref[...], sin_ref[...]
        outs = []
        for sl in range(4):
            xr = _rot(q_ref[:, sl * LANES:(sl + 1) * LANES], c, s) * 0.125
            outs += list(_split_heads(xr))
        qp_ref[...] = _bf(jnp.concatenate(outs, axis=1))
        k0, k1 = _split_heads(_rot(kv_ref[:, 0:LANES], c, s))
        v0, v1 = _split_heads(kv_ref[:, LANES:2 * LANES])
        kvp_ref[...] = _bf(jnp.concatenate([k0, k1, v0, v1], axis=1))

    return pl.pallas_call(
        body, name=name, grid=(bsz, t // ROWT),
        in_specs=[_tile(512), _tile(256, kv_blk),
                  pl.BlockSpec((ROWT, LANES), lambda b, j: (j, 0)), pl.BlockSpec((ROWT, LANES), lambda b, j: (j, 0))],
        out_specs=[_tile(1024), _tile(512)],
        out_shape=(jax.ShapeDtypeStruct((bsz, t, 1024), BF16), jax.ShapeDtypeStruct((bsz, t, 512), BF16)),
    )(z, z, cos, sin)


def _rope_bwd(dq_r, dkv_acc, cos, sin, t, name):
    bsz = dq_r.shape[0]

    def body(dq_ref, acc_ref, cos_ref, sin_ref, dqo_ref, dkvo_ref):
        c, s = cos_ref[...], -sin_ref[...]
        dqo_ref[...] = jnp.concatenate(
            [_rot(dq_ref[:, sl * LANES:(sl + 1) * LANES], c, s) for sl in range(4)], axis=1)
        dk = _merge_heads(acc_ref[:, 0:LANES], acc_ref[:, LANES:2 * LANES])
        dv = _merge_heads(acc_ref[:, 2 * LANES:3 * LANES], acc_ref[:, 3 * LANES:4 * LANES])
        dkvo_ref[...] = jnp.concatenate([_rot(dk, c, s), dv], axis=1)

    return pl.pallas_call(
        body, name=name, grid=(bsz, t // ROWT),
        in_specs=[_tile(512), _tile(512),
                  pl.BlockSpec((ROWT, LANES), lambda b, j: (j, 0)), pl.BlockSpec((ROWT, LANES), lambda b, j: (j, 0))],
        out_specs=[_tile(512), _tile(256)],
        out_shape=(jax.ShapeDtypeStruct((bsz, t, 512), F32), jax.ShapeDtypeStruct((bsz, t, 256), F32)),
    )(dq_r, dkv_acc, cos, sin)


def _attn_bias(j, ncb, l):
    n = j - ncb
    r = lax.broadcasted_iota(jnp.int32, (QBLK, 3 * QBLK), 0)
    cidx = lax.broadcasted_iota(jnp.int32, (QBLK, 3 * QBLK), 1)
    kpos = (n - 1) * QBLK + cidx
    valid = (jnp.abs(r - cidx + QBLK) <= WINDOW) & (kpos >= 0) & (kpos < l) & (j >= ncb)
    return jnp.where(valid, 0.0, NEG_INF).astype(F32)


def _kv_specs(nb, lc):
    return [pl.BlockSpec((None, QBLK, 512), lambda b, j: (b, jnp.maximum(j - 1, 0), 0)),
            pl.BlockSpec((None, QBLK, 512), lambda b, j: (b, j, 0)),
            pl.BlockSpec((None, QBLK, 512), lambda b, j: (b, jnp.minimum(j + 1, nb - 1), 0)),
            pl.BlockSpec((None, lc, 512), lambda b, j: (b, 0, 0))]


def _lane_pick(x, h):
    lane = lax.broadcasted_iota(jnp.int32, x.shape, 1)
    return jnp.sum(jnp.where(lane == h, x, 0.0), axis=1, keepdims=True)


def _attn_fwd(qp, kvp, sinks, jl, lc, name):
    bsz, t, _ = qp.shape
    nb, ncb, l = t // QBLK, lc // QBLK, t - lc

    def body(sink_ref, q_ref, kp_ref, kc_ref, kn_ref, kx_ref, o_ref, lse_ref):
        j = pl.program_id(1)
        kall = jnp.concatenate([kp_ref[...], kc_ref[...], kn_ref[...], kx_ref[...]], axis=0)
        bias = jnp.concatenate([_attn_bias(j, ncb, l), jnp.zeros((QBLK, lc), F32)], axis=1)
        lane = lax.broadcasted_iota(jnp.int32, (QBLK, LANES), 1)
        lse_all = jnp.zeros((QBLK, LANES), F32)
        outs = []
        for h in range(8):
            hk = h // 4
            kh = kall[:, hk * LANES:(hk + 1) * LANES]
            vh = kall[:, (2 + hk) * LANES:(3 + hk) * LANES]
            s = _nt(q_ref[:, h * LANES:(h + 1) * LANES], kh) + bias
            sk = sink_ref[jl, h]
            m = jnp.maximum(jnp.max(s, axis=1, keepdims=True), sk)
            p = jnp.exp(s - m)
            den = jnp.sum(p, axis=1, keepdims=True) + jnp.exp(sk - m)
            outs.append(_nn(_bf(p), vh) / den)
            lse_all = lse_all + jnp.where(lane == h, m + jnp.log(den), 0.0)
        o_ref[...] = jnp.concatenate([_merge_heads(outs[2 * i], outs[2 * i + 1]) for i in range(4)], axis=1)
        lse_ref[...] = lse_all

    return pl.pallas_call(
        body, name=name, grid=(bsz, nb),
        in_specs=[pl.BlockSpec(memory_space=pltpu.SMEM),
                  pl.BlockSpec((None, QBLK, 1024), lambda b, j: (b, j, 0))] + _kv_specs(nb, lc),
        out_specs=[pl.BlockSpec((None, QBLK, 512), lambda b, j: (b, j, 0)),
                   pl.BlockSpec((None, QBLK, LANES), lambda b, j: (b, j, 0))],
        out_shape=(jax.ShapeDtypeStruct((bsz, t, 512), F32), jax.ShapeDtypeStruct((bsz, t, LANES), F32)),
    )(sinks, qp, kvp, kvp, kvp, kvp)


def _attn_bwd(qp, kvp, sinks, jl, o, lse, do, lc, name):
    bsz, t, _ = qp.shape
    nb, ncb, l = t // QBLK, lc // QBLK, t - lc
    nk = 3 * QBLK

    def body(sink_ref, q_ref, kp_ref, kc_ref, kn_ref, kx_ref, o_ref, lse_ref, do_ref, dq_ref, acc_ref, ds_ref):
        j = pl.program_id(1)

        @pl.when(j == 0)
        def _():
            acc_ref[...] = jnp.zeros_like(acc_ref)
            ds_ref[...] = jnp.zeros_like(ds_ref)
        kall = jnp.concatenate([kp_ref[...], kc_ref[...], kn_ref[...], kx_ref[...]], axis=0)
        bias = jnp.concatenate([_attn_bias(j, ncb, l), jnp.zeros((QBLK, lc), F32)], axis=1)
        lse = lse_ref[...]
        row8 = lax.broadcasted_iota(jnp.int32, (8, LANES), 0)
        dsink = jnp.zeros((8, LANES), F32)
        dqs = []
        dkv = [jnp.zeros((nk + lc, LANES), F32) for _ in range(4)]
        for s2 in range(4):
            do_pair = _split_heads(do_ref[:, s2 * LANES:(s2 + 1) * LANES])
            o_pair = _split_heads(o_ref[:, s2 * LANES:(s2 + 1) * LANES])
            for e in range(2):
                h = 2 * s2 + e
                hk = h // 4
                kh = kall[:, hk * LANES:(hk + 1) * LANES]
                vh = kall[:, (2 + hk) * LANES:(3 + hk) * LANES]
                qh = q_ref[:, h * LANES:(h + 1) * LANES]
                doh = do_pair[e]
                delta = jnp.sum(doh * o_pair[e], axis=1, keepdims=True)
                lse_h = _lane_pick(lse, h)
                p = jnp.exp(_nt(qh, kh) + bias - lse_h)
                dob = _bf(doh)
                dp = _nt(dob, vh)
                dsc = _bf(p * (dp - delta))
                dqs.append(_nn(dsc, kh) * 0.125)
                dkv[hk] = dkv[hk] + _tn(dsc, qh)
                dkv[2 + hk] = dkv[2 + hk] + _tn(_bf(p), dob)
                ps = jnp.exp(sink_ref[jl, h] - lse_h)
                dsink = dsink + jnp.where(row8 == h, -jnp.sum(ps * delta), 0.0)
        dq_ref[...] = jnp.concatenate([_merge_heads(dqs[2 * i], dqs[2 * i + 1]) for i in range(4)], axis=1)
        ds_ref[...] += dsink
        dall = jnp.concatenate(dkv, axis=1)
        acc_ref[pl.ds(0, lc), :] += dall[nk:, :]

        @pl.when(j >= ncb)
        def _():
            st = pl.multiple_of((j - 1) * QBLK, QBLK)
            acc_ref[pl.ds(st, nk), :] += dall[:nk, :]

    blk = lambda wd: pl.BlockSpec((None, QBLK, wd), lambda b, j: (b, j, 0))
    return pl.pallas_call(
        body, name=name, grid=(bsz, nb),
        in_specs=[pl.BlockSpec(memory_space=pltpu.SMEM), blk(1024)] + _kv_specs(nb, lc)
                 + [blk(512), blk(LANES), blk(512)],
        out_specs=[blk(512), pl.BlockSpec((None, t + QBLK, 512), lambda b, j: (b, 0, 0)),
                   pl.BlockSpec((None, 8, LANES), lambda b, j: (b, 0, 0))],
        out_shape=(jax.ShapeDtypeStruct((bsz, t, 512), F32), jax.ShapeDtypeStruct((bsz, t + QBLK, 512), F32),
                   jax.ShapeDtypeStruct((bsz, 8, LANES), F32)),
        compiler_params=_cp(VMEM_BIG),
    )(sinks, qp, kvp, kvp, kvp, kvp, o, lse, do)


def _s5_operators(a_re, a_im, log_dt, b_re, b_im, c_re, c_im):
    hi = lax.Precision.HIGHEST
    dt = jnp.exp(log_dt)[..., None]
    mag = jnp.exp(a_re * dt)
    ar, ai = mag * jnp.cos(a_im * dt), mag * jnp.sin(a_im * dt)
    den = a_re * a_re + a_im * a_im
    fr = ((ar - 1.0) * a_re + ai * a_im) / den
    fi = (ai * a_re - (ar - 1.0) * a_im) / den
    bbr = fr[..., None] * b_re - fi[..., None] * b_im
    bbi = fr[..., None] * b_im + fi[..., None] * b_re
    pr, pi = [jnp.ones_like(ar)], [jnp.zeros_like(ar)]
    for _ in range(CHUNK):
        pr, pi = pr + [pr[-1] * ar - pi[-1] * ai], pi + [pr[-1] * ai + pi[-1] * ar]
    pwr, pwi = jnp.stack(pr), jnp.stack(pi)
    wr = pwr[:CHUNK, ..., None] * bbr - pwi[:CHUNK, ..., None] * bbi
    wi = pwr[:CHUNK, ..., None] * bbi + pwi[:CHUNK, ..., None] * bbr
    kern = (jnp.einsum("dgcp,tdgpk->dgtck", c_re, wr, precision=hi)
            - jnp.einsum("dgcp,tdgpk->dgtck", c_im, wi, precision=hi))
    g = a_re.shape[1]
    nrow = CHUNK * SSM_GROUP
    tt, ss = np.arange(CHUNK)[:, None], np.arange(CHUNK)[None, :]
    msk = lambda m: jnp.asarray(m.astype(np.float32))[None, :, :, None, None]
    toe = (kern[0][:, np.clip(tt - ss, 0, CHUNK - 1)] * msk(tt >= ss)
           + kern[1][:, np.clip(ss - tt, 0, CHUNK - 1)] * msk(ss >= tt))
    mt = jnp.transpose(toe, (0, 2, 4, 1, 3)).reshape(g, nrow, nrow)
    to_rows = lambda w: jnp.transpose(w, (1, 0, 3, 2)).reshape(g, nrow, SSM_STATE)
    bt = jnp.concatenate([to_rows(wr[::-1, 0]), to_rows(wi[::-1, 0]), to_rows(wr[:, 1]), to_rows(wi[:, 1])], axis=-1)

    def out_map(d, pw_r, pw_i):
        w2r = c_re[d][None] * pw_r[:, :, None, :] - c_im[d][None] * pw_i[:, :, None, :]
        w2i = c_re[d][None] * pw_i[:, :, None, :] + c_im[d][None] * pw_r[:, :, None, :]
        cols = lambda w: jnp.transpose(w, (1, 3, 0, 2)).reshape(g, SSM_STATE, nrow)
        return [cols(w2r), -cols(w2i)]

    ct = jnp.concatenate(out_map(0, pwr[1:, 0], pwi[1:, 0]) + out_map(1, pwr[1:, 1][::-1], pwi[1:, 1][::-1]), axis=-2)
    return mt, bt, ct, pwr[CHUNK], pwi[CHUNK]


def _scan_powers(a16r, a16i, n_chunks):
    prs, pis = [], []
    r, i = a16r, a16i
    s = 1
    while s < n_chunks:
        prs.append(jnp.concatenate([r, r], axis=-1))
        pis.append(jnp.concatenate([-i, i], axis=-1))
        r, i = r * r - i * i, 2.0 * r * i
        s *= 2
    pad = 16 - len(prs)
    z = jnp.zeros_like(prs[0])
    return jnp.stack(prs + [z] * pad, axis=2), jnp.stack(pis + [z] * pad, axis=2)


def _swap_halves(x):
    return pltpu.roll(x, 64, 1)


def _chunk_scan(s, pr, pi, reverse, conj):
    n = s.shape[0]
    row = lax.broadcasted_iota(jnp.int32, s.shape, 0)

    def shift(x, k):
        if reverse:
            return jnp.where(row < n - k, pltpu.roll(x, n - k, 0), 0.0)
        return jnp.where(row >= k, pltpu.roll(x, k, 0), 0.0)

    h = shift(s, 1)
    k, j = 1, 0
    while k < n:
        hs = shift(h, k)
        pij = pi[j:j + 1, :]
        h = h + pr[j:j + 1, :] * hs + (-pij if conj else pij) * _swap_halves(hs)
        k, j = 2 * k, j + 1
    return h


def _lane_group():
    return jnp.right_shift(lax.broadcasted_iota(jnp.int32, (1, LANES), 1), 4)


def _blocks_from_rows(xs, gl):
    lg = _lane_group()
    halves = []
    for half in range(2):
        acc = None
        for tl in range(8):
            sft = ((tl - gl) * SSM_GROUP) % LANES
            v = xs[half * 8 + tl]
            v = pltpu.roll(v, sft, 1) if sft else v
            acc = v if acc is None else jnp.where(lg == tl, v, acc)
        halves.append(acc)
    return jnp.concatenate(halves, axis=1)


def _rows_from_blocks(ys, tok):
    lg = _lane_group()
    half, tl = tok // 8, tok % 8
    acc = None
    for gl in range(8):
        v = ys[gl][:, half * LANES:(half + 1) * LANES]
        sft = ((gl - tl) * SSM_GROUP) % LANES
        v = pltpu.roll(v, sft, 1) if sft else v
        acc = v if acc is None else jnp.where(lg == gl, v, acc)
    return acc


def _both_scans(s, prf, pif, prb, pib, ncc, adjoint):
    n = s.shape[0]
    hf = _chunk_scan(s[:, :LANES], prf, pif, adjoint, adjoint)
    rot = pltpu.roll(s[:, LANES:], n - ncc, 0)
    hb = pltpu.roll(_chunk_scan(rot, prb, pib, not adjoint, adjoint), ncc, 0)
    return hf, hb


def _s5_op_specs(idx):
    op = pl.BlockSpec((8, 256, 256), lambda a, b: (idx(a, b), 0, 0))
    pw = pl.BlockSpec((2, 8, 16, LANES), lambda a, b: (0, idx(a, b), 0, 0))
    return [op, op, op, pw, pw]


def _s5_fwd(z, col0, mt, bt, ct, pr, pi, lc, name):
    bsz, t, _ = z.shape
    nc, ncc = t // CHUNK, lc // CHUNK

    def body(u_ref, mt_ref, bt_ref, ct_ref, pr_ref, pi_ref, y_ref):
        xs = [u_ref[pl.ds(tok, nc, stride=CHUNK), :] for tok in range(CHUNK)]
        ys = []
        for gl in range(8):
            u = _bf(_blocks_from_rows(xs, gl))
            hf, hb = _both_scans(_nn(u, bt_ref[gl]), pr_ref[0, gl], pi_ref[0, gl], pr_ref[1, gl], pi_ref[1, gl],
                                 ncc, False)
            ys.append(_nn(u, mt_ref[gl]) + _nn(_bf(jnp.concatenate([hf, hb], axis=1)), ct_ref[gl]))
        for tok in range(CHUNK):
            y_ref[pl.ds(tok, nc, stride=CHUNK), :] = _rows_from_blocks(ys, tok)

    return pl.pallas_call(
        body, name=name, grid=(bsz, 4),
        in_specs=[pl.BlockSpec((None, t, LANES), lambda b, s: (b, 0, col0 + s))] + _s5_op_specs(lambda b, s: s),
        out_specs=pl.BlockSpec((None, t, LANES), lambda b, s: (b, 0, s)),
        out_shape=jax.ShapeDtypeStruct((bsz, t, 512), F32), compiler_params=_cp(VMEM_BIG),
    )(z, mt, bt, ct, pr, pi)


def _s5_bwd(z, col0, dy, mt, bt, ct, pr, pi, lc, name):
    bsz, t, _ = z.shape
    nc, ncc = t // CHUNK, lc // CHUNK

    def body(u_ref, dy_ref, mt_ref, bt_ref, ct_ref, pr_ref, pi_ref, du_ref, dmt_ref, dbt_ref, dct_ref, da_ref):
        b = pl.program_id(1)

        @pl.when(b == 0)
        def _():
            dmt_ref[...] = jnp.zeros_like(dmt_ref)
            dbt_ref[...] = jnp.zeros_like(dbt_ref)
            dct_ref[...] = jnp.zeros_like(dct_ref)
            da_ref[...] = jnp.zeros_like(da_ref)
        xs = [u_ref[pl.ds(tok, nc, stride=CHUNK), :] for tok in range(CHUNK)]
        dys = [dy_ref[pl.ds(tok, nc, stride=CHUNK), :] for tok in range(CHUNK)]
        row8 = lax.broadcasted_iota(jnp.int32, (8, LANES), 0)
        dus = []
        for gl in range(8):
            pw = (pr_ref[0, gl], pi_ref[0, gl], pr_ref[1, gl], pi_ref[1, gl])
            u = _bf(_blocks_from_rows(xs, gl))
            dyg = _bf(_blocks_from_rows(dys, gl))
            hf, hb = _both_scans(_nn(u, bt_ref[gl]), *pw, ncc, False)
            gf, gb = _both_scans(_nt(dyg, ct_ref[gl]), *pw, ncc, True)
            hcat, gcat = _bf(jnp.concatenate([hf, hb], axis=1)), _bf(jnp.concatenate([gf, gb], axis=1))
            dus.append(_nt(dyg, mt_ref[gl]) + _nt(gcat, bt_ref[gl]))
            dmt_ref[gl] += _tn(u, dyg)
            dct_ref[gl] += _tn(hcat, dyg)
            dbt_ref[gl] += _tn(u, gcat)
            vals = [jnp.sum(gf * hf, axis=0, keepdims=True), jnp.sum(gf * _swap_halves(hf), axis=0, keepdims=True),
                    jnp.sum(gb * hb, axis=0, keepdims=True), jnp.sum(gb * _swap_halves(hb), axis=0, keepdims=True)]
            da = jnp.zeros((8, LANES), F32)
            for q, v in enumerate(vals):
                da = da + jnp.where(row8 == q, v, 0.0)
            da_ref[gl] += da
        for tok in range(CHUNK):
            du_ref[pl.ds(tok, nc, stride=CHUNK), :] = _rows_from_blocks(dus, tok)

    op_out = pl.BlockSpec((8, 256, 256), lambda s, b: (s, 0, 0))
    op_shape = jax.ShapeDtypeStruct((SSM_GROUPS, 256, 256), F32)
    return pl.pallas_call(
        body, name=name, grid=(4, bsz),
        in_specs=[pl.BlockSpec((None, t, LANES), lambda s, b: (b, 0, col0 + s)),
                  pl.BlockSpec((None, t, LANES), lambda s, b: (b, 0, s))] + _s5_op_specs(lambda s, b: s),
        out_specs=[pl.BlockSpec((None, t, LANES), lambda s, b: (b, 0, s)), op_out, op_out, op_out,
                   pl.BlockSpec((8, 8, LANES), lambda s, b: (s, 0, 0))],
        out_shape=(jax.ShapeDtypeStruct((bsz, t, 512), F32), op_shape, op_shape, op_shape,
                   jax.ShapeDtypeStruct((SSM_GROUPS, 8, LANES), F32)),
        compiler_params=_cp(VMEM_BIG),
    )(z, dy, mt, bt, ct, pr, pi)


def _glu_in(y, u, dsk):
    return jax.nn.gelu(y + u * dsk)


def _even_mix(oa, ga, zg, tg, gs):
    return jnp.concatenate([oa * jax.nn.silu(ga), zg * jax.nn.sigmoid(tg) * jax.nn.silu(gs)], axis=1)


def _even_specs(d, i, jl, nct):
    return [_tile(d), _mod_spec(i, nct, d), _tile(512), _tile(512)] + [_tile(256, cb) for cb in range(3, 9)] + [
        _layer_spec(jl, 1, 512), _layer_spec(jl, 512, 512), _layer_spec(jl, 1, 512), _layer_spec(jl, 1024, d)]


def _cat2(a_ref, b_ref):
    return jnp.concatenate([a_ref[...], b_ref[...]], axis=1)


def _even_out_fwd(hs, mods, oa, y, z, dsk, gw, gb, wout, i, jl, nct, name):
    bsz, t, d = hs.shape

    def body(h_ref, mod_ref, oa_ref, y_ref, ga0, ga1, u0, u1, gs0, gs1, dsk_ref, gw_ref, gb_ref, wo_ref, o_ref):
        zg = _glu_in(y_ref[...], _cat2(u0, u1), dsk_ref[...])
        tg = _nn(_bf(zg), gw_ref[...]) + gb_ref[...]
        mix = _even_mix(oa_ref[...], _cat2(ga0, ga1), zg, tg, _cat2(gs0, gs1))
        o_ref[...] = h_ref[...] + mod_ref[2:3, :] * _nn(_bf(mix), wo_ref[...])

    return pl.pallas_call(
        body, name=name, grid=(bsz, t // ROWT), in_specs=_even_specs(d, i, jl, nct),
        out_specs=_tile(d), out_shape=jax.ShapeDtypeStruct((bsz, t, d), F32), compiler_params=_cp(VMEM_BIG),
    )(hs, mods, oa, y, z, z, z, z, z, z, dsk, gw, gb, wout)


def _even_out_bwd(dh, mods, oa, y, z, dsk, gw, gb, wout, i, jl, nct, name):
    bsz, t, d = dh.shape

    def body(dh_ref, mod_ref, oa_ref, y_ref, ga0, ga1, u0, u1, gs0, gs1, dsk_ref, gw_ref, gb_ref, wo_ref,
             doa_ref, dga_ref, dy_ref, dgs_ref, dmod_ref, ddsk_ref, dgw_ref, dgb_ref, dwo_ref):
        b, j = pl.program_id(0), pl.program_id(1)
        zg, vj1 = jax.vjp(_glu_in, y_ref[...], _cat2(u0, u1), dsk_ref[...])
        zgb = _bf(zg)
        tg = _nn(zgb, gw_ref[...]) + gb_ref[...]
        mix, vj2 = jax.vjp(_even_mix, oa_ref[...], _cat2(ga0, ga1), zg, tg, _cat2(gs0, gs1))
        mixb = _bf(mix)
        dhv = dh_ref[...]
        dyo = _bf(dhv * mod_ref[2:3, :])
        yout = _nn(mixb, wo_ref[...])
        doa, dga, dzg, dtg, dgs = vj2(_nt(dyo, wo_ref[...]))
        dtb = _bf(dtg)
        dzg = dzg + _nt(dtb, gw_ref[...])
        dy, _, ddsk = vj1(dzg)
        doa_ref[...], dga_ref[...], dy_ref[...], dgs_ref[...] = doa, dga, dy, dgs

        @pl.when((j == 0) | (j == nct))
        def _():
            dmod_ref[...] = jnp.zeros_like(dmod_ref)

        @pl.when((b == 0) & (j == 0))
        def _():
            ddsk_ref[...] = jnp.zeros_like(ddsk_ref)
            dgw_ref[...] = jnp.zeros_like(dgw_ref)
            dgb_ref[...] = jnp.zeros_like(dgb_ref)
            dwo_ref[...] = jnp.zeros_like(dwo_ref)
        dmod_ref[2:3, :] += jnp.sum(dhv * yout, axis=0, keepdims=True)
        ddsk_ref[...] += ddsk
        dgw_ref[...] += _tn(zgb, dtb)
        dgb_ref[...] += jnp.sum(dtg, axis=0, keepdims=True)
        dwo_ref[...] += _tn(mixb, dyo)

    full = lambda r, c: pl.BlockSpec((r, c), lambda b, j: (0, 0))
    f512 = jax.ShapeDtypeStruct((bsz, t, 512), F32)
    return pl.pallas_call(
        body, name=name, grid=(bsz, t // ROWT), in_specs=_even_specs(d, i, jl, nct),
        out_specs=[_tile(512), _tile(512), _tile(512), _tile(512), _dmod_spec(nct, d),
                   full(1, 512), full(512, 512), full(1, 512), full(1024, d)],
        out_shape=(f512, f512, f512, f512, jax.ShapeDtypeStruct((2 * bsz, 3, d), F32),
                   jax.ShapeDtypeStruct((1, 512), F32), jax.ShapeDtypeStruct((512, 512), F32),
                   jax.ShapeDtypeStruct((1, 512), F32), jax.ShapeDtypeStruct((1024, d), F32)),
        compiler_params=_cp(VMEM_BIG),
    )(dh, mods, oa, y, z, z, z, z, z, z, dsk, gw, gb, wout)


def _pool(src, col_blk, r, lc, transpose, name):
    bsz, t, _ = src.shape
    segs = [(0, lc, 16), (lc, t - lc, 32 + lc)]
    halo = [(0, 16), (16 + lc, 16), (32 + t, 16)]
    cw = 256

    def inv_count(t0, rows, ln):
        pos = t0 + lax.broadcasted_iota(jnp.int32, (rows, cw), 0)
        cnt = jnp.minimum(pos + r + 1, ln) - jnp.maximum(pos - r, 0)
        return 1.0 / cnt.astype(F32)

    def body(x_ref, o_ref, s_ref):
        for h0, hn in halo:
            s_ref[pl.ds(h0, hn), :] = jnp.zeros((hn, cw), F32)
        for r0, ln, s0 in segs:
            step = min(512, ln)
            for c0 in range(0, ln, step):
                v = x_ref[pl.ds(r0 + c0, step), :]
                s_ref[pl.ds(s0 + c0, step), :] = v * inv_count(c0, step, ln) if transpose else v
        for r0, ln, s0 in segs:
            step = min(512, ln)
            for c0 in range(0, ln, step):
                acc = s_ref[pl.ds(s0 + c0 - r, step), :]
                for dlt in range(-r + 1, r + 1):
                    acc = acc + s_ref[pl.ds(s0 + c0 + dlt, step), :]
                ctr = x_ref[pl.ds(r0 + c0, step), :]
                o_ref[pl.ds(r0 + c0, step), :] = (acc if transpose else acc * inv_count(c0, step, ln)) - ctr

    return pl.pallas_call(
        body, name=name, grid=(bsz,),
        in_specs=[pl.BlockSpec((None, t, cw), lambda b: (b, 0, col_blk))],
        out_specs=pl.BlockSpec((None, t, cw), lambda b: (b, 0, 0)),
        out_shape=jax.ShapeDtypeStruct((bsz, t, cw), F32),
        scratch_shapes=[pltpu.VMEM((t + 48, cw), F32)],
        compiler_params=_cp(VMEM_BIG),
    )(src)


def _odd_specs(d, i, jl, nct):
    return [_tile(d), _mod_spec(i, nct, d), _tile(256), _tile(256), _tile(256), _tile(256), _tile(1024, 1),
            _layer_spec(jl, 4, 256, 256), _layer_spec(jl, 1, 1024), _layer_spec(jl, 1024, d)]


def _odd_mix(pm, psc, gz):
    return pm * psc * jax.nn.silu(gz)


def _odd_out_fwd(hs, mods, ps, z, pw, psc, wout, i, jl, nct, name):
    bsz, t, d = hs.shape

    def body(h_ref, mod_ref, p0, p1, p2, p3, gz_ref, pw_ref, psc_ref, wo_ref, o_ref):
        pm = jnp.concatenate([_nn(_bf(p[...]), pw_ref[q]) for q, p in enumerate((p0, p1, p2, p3))], axis=1)
        mix = _odd_mix(pm, psc_ref[...], gz_ref[...])
        o_ref[...] = h_ref[...] + mod_ref[2:3, :] * _nn(_bf(mix), wo_ref[...])

    return pl.pallas_call(
        body, name=name, grid=(bsz, t // ROWT), in_specs=_odd_specs(d, i, jl, nct),
        out_specs=_tile(d), out_shape=jax.ShapeDtypeStruct((bsz, t, d), F32), compiler_params=_cp(VMEM_BIG),
    )(hs, mods, *ps, z, pw, psc, wout)


def _odd_out_bwd(dh, mods, ps, z, pw, psc, wout, i, jl, nct, name):
    bsz, t, d = dh.shape

    def body(dh_ref, mod_ref, p0, p1, p2, p3, gz_ref, pw_ref, psc_ref, wo_ref,
             dp_ref, dgz_ref, dmod_ref, dpw_ref, dpsc_ref, dwo_ref):
        b, j = pl.program_id(0), pl.program_id(1)
        pbs = [_bf(p[...]) for p in (p0, p1, p2, p3)]
        pm = jnp.concatenate([_nn(pb, pw_ref[q]) for q, pb in enumerate(pbs)], axis=1)
        mix, vj = jax.vjp(_odd_mix, pm, psc_ref[...], gz_ref[...])
        mixb = _bf(mix)
        dhv = dh_ref[...]
        dyo = _bf(dhv * mod_ref[2:3, :])
        yout = _nn(mixb, wo_ref[...])
        dpm, dpsc, dgz = vj(_nt(dyo, wo_ref[...]))
        dgz_ref[...] = dgz
        dpmb = _bf(dpm)
        dp_ref[...] = jnp.concatenate([_nt(dpmb[:, q * 256:(q + 1) * 256], pw_ref[q]) for q in range(4)], axis=1)

        @pl.when((j == 0) | (j == nct))
        def _():
            dmod_ref[...] = jnp.zeros_like(dmod_ref)

        @pl.when((b == 0) & (j == 0))
        def _():
            dpw_ref[...] = jnp.zeros_like(dpw_ref)
            dpsc_ref[...] = jnp.zeros_like(dpsc_ref)
            dwo_ref[...] = jnp.zeros_like(dwo_ref)
        dmod_ref[2:3, :] += jnp.sum(dhv * yout, axis=0, keepdims=True)
        for q in range(4):
            dpw_ref[q] += _tn(pbs[q], dpmb[:, q * 256:(q + 1) * 256])
        dpsc_ref[...] += dpsc
        dwo_ref[...] += _tn(mixb, dyo)

    full = lambda *s: pl.BlockSpec(s, lambda b, j: (0,) * len(s))
    return pl.pallas_call(
        body, name=name, grid=(bsz, t // ROWT), in_specs=_odd_specs(d, i, jl, nct),
        out_specs=[_tile(1024), _tile(1024), _dmod_spec(nct, d), full(4, 256, 256), full(1, 1024), full(1024, d)],
        out_shape=(jax.ShapeDtypeStruct((bsz, t, 1024), F32), jax.ShapeDtypeStruct((bsz, t, 1024), F32),
                   jax.ShapeDtypeStruct((2 * bsz, 3, d), F32), jax.ShapeDtypeStruct((4, 256, 256), F32),
                   jax.ShapeDtypeStruct((1, 1024), F32), jax.ShapeDtypeStruct((1024, d), F32)),
        compiler_params=_cp(VMEM_BIG),
    )(dh, mods, *ps, z, pw, psc, wout)


def _rms(h, g):
    return h * lax.rsqrt(jnp.mean(h * h, axis=-1, keepdims=True) + EPS) * g


def _final_loss(hs, g, target, nct, name):
    bsz, t, d = hs.shape

    def body(h_ref, g_ref, t_ref, dh_ref, loss_ref, dg_ref):
        b, j = pl.program_id(0), pl.program_id(1)
        y, vj = jax.vjp(_rms, h_ref[...], g_ref[...])
        err = jnp.where(j >= nct, y - t_ref[...], 0.0)
        dh, dg = vj(err * (1.0 / d))
        dh_ref[...] = dh

        @pl.when(j == 0)
        def _():
            loss_ref[...] = jnp.zeros_like(loss_ref)

        @pl.when((b == 0) & (j == 0))
        def _():
            dg_ref[...] = jnp.zeros_like(dg_ref)
        loss_ref[...] += 0.5 * jnp.sum(jnp.mean(err * err, axis=-1))
        dg_ref[...] += dg

    return pl.pallas_call(
        body, name=name, grid=(bsz, t // ROWT),
        in_specs=[_tile(d), pl.BlockSpec((1, d), lambda b, j: (0, 0)),
                  pl.BlockSpec((None, ROWT, d), lambda b, j: (b, jnp.maximum(j - nct, 0), 0))],
        out_specs=[_tile(d), pl.BlockSpec((None, 8, LANES), lambda b, j: (b, 0, 0)),
                   pl.BlockSpec((1, d), lambda b, j: (0, 0))],
        out_shape=(jax.ShapeDtypeStruct((bsz, t, d), F32), jax.ShapeDtypeStruct((bsz, 8, LANES), F32),
                   jax.ShapeDtypeStruct((1, d), F32)),
    )(hs, g, target)


def _adamw(w, g_slots, m, v, transpose_g, rows, name):
    nl, r, c = w.shape
    ns = g_slots.shape[1]
    rt = rows or r

    def body(w_ref, g_ref, m_ref, v_ref, go_ref, d_ref, mo_ref, vo_ref):
        g = g_ref[0]
        for s in range(1, ns):
            g = g + g_ref[s]
        if transpose_g:
            g = g.T
        mn = ADAM_B1 * m_ref[...] + (1.0 - ADAM_B1) * g
        vn = ADAM_B2 * v_ref[...] + (1.0 - ADAM_B2) * jnp.square(g)
        m_hat = mn / (1.0 - ADAM_B1 ** ADAM_STEP)
        v_hat = vn / (1.0 - ADAM_B2 ** ADAM_STEP)
        go_ref[...] = g
        d_ref[...] = -ADAM_LR * (m_hat / (jnp.sqrt(v_hat) + ADAM_EPS) + ADAM_WD * w_ref[...])
        mo_ref[...] = mn
        vo_ref[...] = vn

    blk = pl.BlockSpec((None, rt, c), lambda l, q: (l, q, 0))
    gshape = (None, ns, c, r) if transpose_g else (None, ns, rt, c)
    out = jax.ShapeDtypeStruct((nl, r, c), F32)
    return pl.pallas_call(
        body, name=name, grid=(nl, r // rt),
        in_specs=[blk, pl.BlockSpec(gshape, lambda l, q: (l, 0, 0, 0) if transpose_g else (l, 0, q, 0)), blk, blk],
        out_specs=[blk, blk, blk, blk], out_shape=(out, out, out, out), compiler_params=_cp(VMEM_BIG),
    )(w, g_slots, m, v)


def _pack(arrs):
    flat = jnp.concatenate([a.reshape(-1) for a in arrs])
    return jnp.pad(flat, (0, (-flat.shape[0]) % (PACK_ROWS * PACK_W))).reshape(-1, PACK_W)


def _unpack(buf, shapes):
    flat = buf.reshape(-1)
    out, off = [], 0
    for s in shapes:
        n = int(np.prod(s))
        out.append(flat[off:off + n].reshape(s))
        off += n
    return out


def kernel(x, c, ctx, c_ctx, ada_w, ada_b, norm_g, even_w_in, even_w_out, attn_sink, ssm_a_re, ssm_a_im, ssm_log_dt, ssm_b_re, ssm_b_im, ssm_c_re, ssm_c_im, ssm_d, glu_w, glu_b, odd_w_in, odd_w_out, pool_w, pool_scale, final_g, loss_target, m_c_ctx, m_ada_w, m_ada_b, m_norm_g, m_even_w_in, m_even_w_out, m_attn_sink, m_ssm_a_re, m_ssm_a_im, m_ssm_log_dt, m_ssm_b_re, m_ssm_b_im, m_ssm_c_re, m_ssm_c_im, m_ssm_d, m_glu_w, m_glu_b, m_odd_w_in, m_odd_w_out, m_pool_w, m_pool_scale, m_final_g, v_c_ctx, v_ada_w, v_ada_b, v_norm_g, v_even_w_in, v_even_w_out, v_attn_sink, v_ssm_a_re, v_ssm_a_im, v_ssm_log_dt, v_ssm_b_re, v_ssm_b_im, v_ssm_c_re, v_ssm_c_im, v_ssm_d, v_glu_w, v_glu_b, v_odd_w_in, v_odd_w_out, v_pool_w, v_pool_scale, v_final_g):
    args = dict(locals())
    bsz, l, d = x.shape
    lc = ctx.shape[1]
    t = lc + l
    nct = lc // ROWT
    depth = ada_w.shape[0]
    nl2 = even_w_in.shape[0]
    me = _my_index()
    assert lc % ROWT == 0 and l % ROWT == 0 and d == 1024 and bsz * N_DEV < 32

    both = lambda ii: [(ii, li) for li in range(nl2)]
    gin = [_bf(jnp.transpose(even_w_in, (0, 2, 1))), _bf(even_w_out), _bf(glu_w), _bf(jnp.transpose(odd_w_in, (0, 2, 1))),
           _bf(odd_w_out), _bf(pool_w.reshape(nl2, -1, pool_w.shape[-1])), c, pool_scale]
    got = _exchange(gin, [("gather", both(ii), 0) for ii in range(6)] + [("gather", [(6, None)], 0), ("gather", [(7, None)], 0)],
                    "gather_weights")
    join = lambda a: a.reshape(a.shape[0], N_DEV * a.shape[2], a.shape[3])
    wt_even, wo_even, gw_full, wt_odd, wo_odd = (join(a) for a in got[:5])
    npw = pool_w.shape[2]
    pw_full = jnp.transpose(got[5].reshape(nl2, N_DEV, 4, npw, 256), (0, 2, 1, 3, 4)).reshape(nl2, 4, 256, 256)
    c_all = got[6].reshape(N_DEV * bsz, d)
    psc_full = jnp.transpose(got[7][0], (1, 0, 2)).reshape(nl2, 1, d)

    n_rows = 32
    craw = jnp.concatenate([c_all, c_ctx[None], jnp.zeros((n_rows - N_DEV * bsz - 1, d), F32)], axis=0)
    ncol = ada_w.shape[2]
    ada_b_loc = lax.dynamic_slice_in_dim(ada_b, me * ncol, ncol, axis=1)[:, None, :]
    m_loc = _adaln_fwd(craw, ada_w, ada_b_loc)
    m_all = _exchange([m_loc], [("gather", [(0, None)], 0)], "gather_mod")[0][0]
    m_all = jnp.transpose(m_all, (1, 2, 0, 3)).reshape(depth, n_rows, 3 * d)
    lat = lax.dynamic_slice_in_dim(m_all, me * bsz, bsz, axis=1).reshape(depth, bsz, 1, 3, d)
    cmod = jnp.broadcast_to(m_all[:, N_DEV * bsz].reshape(depth, 1, 1, 3, d), (depth, bsz, 1, 3, d))
    mods = jnp.concatenate([cmod, lat], axis=2).reshape(depth, 2 * bsz, 3, d)

    cos, sin = _rope_tables(lc, l)
    hs = jnp.concatenate([ctx, x], axis=1)
    g3 = norm_g[:, None, :]
    dsk3, gb3 = ssm_d[:, None, :], glu_b[:, None, :]
    u_col = 1280 // LANES

    saved = []
    for i in range(depth):
        jl = i // 2
        if i % 2 == 0:
            z = _fused_in_fwd(hs, mods, g3, wt_even, i, jl, nct, f"in_fwd{i}")
            qp, kvp = _rope_fwd(z, cos, sin, 2, f"rope_fwd{i}")
            oa, lse = _attn_fwd(qp, kvp, attn_sink, jl, lc, f"attn_fwd{i}")
            ops = _s5_operators(ssm_a_re[jl], ssm_a_im[jl], ssm_log_dt[jl], ssm_b_re[jl], ssm_b_im[jl],
                                ssm_c_re[jl], ssm_c_im[jl])
            pr, pi = _scan_powers(ops[3], ops[4], t // CHUNK)
            opb = (_bf(ops[0]), _bf(ops[1]), _bf(ops[2]))
            y = _s5_fwd(z, u_col, *opb, pr, pi, lc, f"s5_fwd{i}")
            hn = _even_out_fwd(hs, mods, oa, y, z, dsk3, gw_full, gb3, wo_even, i, jl, nct, f"out_fwd{i}")
            saved.append(dict(hs=hs, z=z, qp=qp, kvp=kvp, oa=oa, lse=lse, y=y, opb=opb, pr=pr, pi=pi))
        else:
            z = _fused_in_fwd(hs, mods, g3, wt_odd, i, jl, nct, f"in_fwd{i}")
            ps = [_pool(z, gi, wd // 2, lc, False, f"pool_fwd{i}_{gi}") for gi, wd in enumerate(POOL_WINDOWS)]
            hn = _odd_out_fwd(hs, mods, ps, z, pw_full, psc_full, wo_odd, i, jl, nct, f"out_fwd{i}")
            saved.append(dict(hs=hs, z=z, ps=ps))
        hs = hn

    dh, loss_p, d_final_g = _final_loss(hs, final_g[None], loss_target, nct, "final_loss")
    loss = lax.psum(jnp.sum(loss_p[:, 0, 0]), ("x", "y", "c"))

    gbig = {n: [None] * nl2 for n in ("even_w_in", "even_w_out", "glu_w", "odd_w_in", "odd_w_out", "pool_w", "pool_scale")}
    gsmall = {n: [None] * nl2 for n in ("attn_sink", "ssm_a_re", "ssm_a_im", "ssm_log_dt", "ssm_b_re", "ssm_b_im",
                                        "ssm_c_re", "ssm_c_im", "ssm_d", "glu_b")}
    d_norm_g, d_mods = [None] * depth, [None] * depth
    for i in reversed(range(depth)):
        jl = i // 2
        sv = saved[i]
        if i % 2 == 0:
            doa, dga, dy, dgs, dmod_g, ddsk, dgw, dgb, dwo = _even_out_bwd(
                dh, mods, sv["oa"], sv["y"], sv["z"], dsk3, gw_full, gb3, wo_even, i, jl, nct, f"out_bwd{i}")
            dq_r, dkv_acc, dsink = _attn_bwd(sv["qp"], sv["kvp"], attn_sink, jl, sv["oa"], sv["lse"], doa, lc, f"attn_bwd{i}")
            dq, dkv = _rope_bwd(dq_r, dkv_acc, cos, sin, t, f"rope_bwd{i}")
            du, dmt, dbt, dct, dav = _s5_bwd(sv["z"], u_col, dy, *sv["opb"], sv["pr"], sv["pi"], lc, f"s5_bwd{i}")
            da16r = jnp.stack([dav[:, 0, :64] + dav[:, 0, 64:], dav[:, 2, :64] + dav[:, 2, 64:]])
            da16i = jnp.stack([dav[:, 1, 64:] - dav[:, 1, :64], dav[:, 3, 64:] - dav[:, 3, :64]])
            prm = (ssm_a_re[jl], ssm_a_im[jl], ssm_log_dt[jl], ssm_b_re[jl], ssm_b_im[jl], ssm_c_re[jl], ssm_c_im[jl])
            _, op_vjp = jax.vjp(_s5_operators, *prm)
            dprm = op_vjp((dmt, dbt, dct, da16r, da16i))
            for n, gval in zip(("ssm_a_re", "ssm_a_im", "ssm_log_dt", "ssm_b_re", "ssm_b_im", "ssm_c_re", "ssm_c_im"), dprm):
                gsmall[n][jl] = gval
            gsmall["attn_sink"][jl] = jnp.sum(dsink[:, :, 0], axis=0)
            gsmall["ssm_d"][jl] = ddsk[0]
            gsmall["glu_b"][jl] = dgb[0]
            gbig["glu_w"][jl] = dgw
            gbig["even_w_out"][jl] = dwo
            pieces = [(dq, 512, 0), (dkv, 256, 0), (dga, 512, 0), (du, 512, 0), (dy, 512, 0), (dgs, 512, 0)]
            asm = lambda p, s: jnp.concatenate([p[0], p[1], p[2], p[3] + p[4] * s, p[5]], axis=1)
            dh, dmod_in, dg, dw = _fused_in_bwd(sv["hs"], mods, g3, wt_even, dh, pieces, (dsk3, jl), asm, i, jl, nct, f"in_bwd{i}")
            gbig["even_w_in"][jl] = dw
        else:
            dp, dgz, dmod_g, dpw, dpsc, dwo = _odd_out_bwd(
                dh, mods, sv["ps"], sv["z"], pw_full, psc_full, wo_odd, i, jl, nct, f"out_bwd{i}")
            dus = [_pool(dp, gi, wd // 2, lc, True, f"pool_bwd{i}_{gi}") for gi, wd in enumerate(POOL_WINDOWS)]
            gbig["pool_w"][jl] = dpw
            gbig["pool_scale"][jl] = jnp.broadcast_to(dpsc.reshape(N_DEV, 1, -1), (N_DEV, 8, d // N_DEV)).reshape(N_DEV * 8, -1)
            gbig["odd_w_out"][jl] = dwo
            pieces = [(u_, 256, 0) for u_ in dus] + [(dgz, 1024, 0)]
            asm = lambda p, s: jnp.concatenate(p, axis=1)
            dh, dmod_in, dg, dw = _fused_in_bwd(sv["hs"], mods, g3, wt_odd, dh, pieces, None, asm, i, jl, nct, f"in_bwd{i}")
            gbig["odd_w_in"][jl] = dw
        d_norm_g[i] = dg[0]
        d_mods[i] = (dmod_g + dmod_in).reshape(bsz, 2, 3 * d)
    grad_x = dh[:, lc:]

    dmd = jnp.stack(d_mods)
    dm_loc = jnp.concatenate([dmd[:, :, 1], jnp.sum(dmd[:, :, 0], axis=1, keepdims=True)], axis=1)
    dm_all = _exchange([dm_loc], [("gather", [(0, None)], 0)], "gather_dmod")[0][0]
    dm_lat = jnp.transpose(dm_all[:, :, :bsz], (1, 0, 2, 3)).reshape(depth, N_DEV * bsz, 3 * d)
    dm_ctx = dm_all[0, :, bsz]
    for e in range(1, N_DEV):
        dm_ctx = dm_ctx + dm_all[e, :, bsz]
    dm_rows = jnp.concatenate([dm_lat, dm_ctx[:, None], jnp.zeros((depth, n_rows - N_DEV * bsz - 1, 3 * d), F32)], axis=1)
    dm_cols = lax.dynamic_slice_in_dim(dm_rows, me * ncol, ncol, axis=2)
    g_ada_w, dcraw = _adaln_bwd(craw, dm_cols, ada_w)

    small = {"c_ctx": dcraw[N_DEV * bsz], "ada_b": jnp.sum(dm_loc, axis=1), "norm_g": jnp.stack(d_norm_g),
             "final_g": d_final_g[0]}
    for n, v2 in gsmall.items():
        small[n] = jnp.stack(v2)
    snames = list(small)
    sshapes = [args[n].shape for n in snames]
    spack = _pack([small[n].reshape(args[n].shape) for n in snames])

    bnames = ["even_w_in", "even_w_out", "glu_w", "odd_w_in", "odd_w_out", "pool_w", "pool_scale"]
    ginp, gspecs = [], []
    for n in bnames:
        parts = []
        for jl in range(nl2):
            parts.append((len(ginp), None))
            ginp.append(gbig[n][jl])
        rows_per = gbig[n][0].shape[1 if n == "pool_w" else 0] // N_DEV
        gspecs.append(("rows2" if n == "pool_w" else "rows", parts, rows_per))
    gspecs.append(("gather", [(len(ginp), None)], 0))
    ginp.append(spack)
    red = _exchange(ginp, gspecs, "reduce_grads")

    out = {}

    def put(n, res, shape):
        for kind, buf in zip(("grad_", "delta_", "new_m_", "new_v_"), res):
            out[kind + n] = buf.reshape(shape)

    def as3(a):
        return a.reshape(a.shape[0], -1, a.shape[-1])

    for n, slots in zip(bnames, red[:-1]):
        w = args[n]
        if n == "pool_scale":
            slots = slots[:, :, :1]
        tr = n in ("even_w_in", "odd_w_in")
        g4 = slots.reshape(slots.shape[0], N_DEV, -1, slots.shape[-1])
        put(n, _adamw(as3(w), g4, as3(args["m_" + n]), as3(args["v_" + n]), tr, None, "adamw_" + n), w.shape)
    put("ada_w", _adamw(ada_w, g_ada_w[:, None], m_ada_w, v_ada_w, False, None, "adamw_ada_w"), ada_w.shape)
    res = _adamw(_pack([args[n] for n in snames])[None], red[-1], _pack([args["m_" + n] for n in snames])[None],
                 _pack([args["v_" + n] for n in snames])[None], False, PACK_ROWS, "adamw_small")
    for kind, buf in zip(("grad_", "delta_", "new_m_", "new_v_"), res):
        for n, a in zip(snames, _unpack(buf, sshapes)):
            out[kind + n] = a

    order = ["c_ctx", "ada_w", "ada_b", "norm_g", "even_w_in", "even_w_out", "attn_sink", "ssm_a_re", "ssm_a_im",
             "ssm_log_dt", "ssm_b_re", "ssm_b_im", "ssm_c_re", "ssm_c_im", "ssm_d", "glu_w", "glu_b", "odd_w_in",
             "odd_w_out", "pool_w", "pool_scale", "final_g"]
    return (loss, grad_x, *[out[k + n] for k in ("grad_", "delta_", "new_m_", "new_v_") for n in order])
```

```python
import functools

import numpy as np
import jax
import jax.numpy as jnp
from jax import lax
from jax.experimental import pallas as pl
from jax.experimental.pallas import tpu as pltpu

F32 = jnp.float32
BF16 = jnp.bfloat16
EPS = 1e-6
NEG_INF = -1e30
N_DEV = 8
GRID_W = 64
WINDOW = 128
QBLK = 128
ROWT = 256
CHUNK = 16
SSM_GROUPS = 32
SSM_GROUP = 16
SSM_STATE = 64
POOL_WINDOWS = (2, 4, 8, 16)
ROPE_BASE = 10000.0
ADAM_LR, ADAM_B1, ADAM_B2, ADAM_EPS, ADAM_WD, ADAM_STEP = 0.001, 0.9, 0.999, 1e-08, 0.01, 10
LANES = 128
PACK_W = 1024
PACK_ROWS = 64
VMEM_BIG = 56 << 20


def _cp(vmem=None):
    return pltpu.CompilerParams(vmem_limit_bytes=vmem) if vmem else pltpu.CompilerParams()


def _nt(a, b):
    return lax.dot_general(a, b, (((1,), (1,)), ((), ())), preferred_element_type=F32)


def _tn(a, b):
    return lax.dot_general(a, b, (((0,), (0,)), ((), ())), preferred_element_type=F32)


def _nn(a, b):
    return jnp.dot(a, b, preferred_element_type=F32)


def _bf(x):
    return x.astype(BF16)


def _my_index():
    return 4 * lax.axis_index("x") + 2 * lax.axis_index("y") + lax.axis_index("c")


def _xchg_shapes(inputs, specs):
    out_shapes = []
    for kind, parts, r in specs:
        ii, li = parts[0]
        shp = tuple(inputs[ii].shape if li is None else inputs[ii].shape[1:])
        piece = shp if kind == "gather" else ((r,) + shp[1:] if kind == "rows" else (shp[0], r) + shp[2:])
        out_shapes.append(jax.ShapeDtypeStruct((len(parts), N_DEV) + piece, inputs[ii].dtype))
    return out_shapes, sum(len(parts) for _, parts, _ in specs)


def _xchg_sems(n_parts):
    return [pltpu.SemaphoreType.DMA((n_parts * (N_DEV - 1),)), pltpu.SemaphoreType.DMA((n_parts * (N_DEV - 1),)),
            pltpu.SemaphoreType.DMA((n_parts,))]


def _xchg_copies(specs, in_refs, out_refs, send_sems, recv_sems, loc_sems):
    x, y, c = lax.axis_index("x"), lax.axis_index("y"), lax.axis_index("c")
    me = 4 * x + 2 * y + c

    def piece(ref, kind, r, p):
        if kind == "gather":
            return ref
        if kind == "rows":
            return ref.at[pl.ds(p * r, r)]
        return ref.at[:, pl.ds(p * r, r)]

    copies, q = [], 0
    for si, (kind, parts, r) in enumerate(specs):
        for pi, (ii, li) in enumerate(parts):
            src = in_refs[ii] if li is None else in_refs[ii].at[li]
            dst = out_refs[si].at[pi, me]
            copies.append(pltpu.make_async_copy(piece(src, kind, r, me), dst, loc_sems.at[q]))
            for k in range(1, N_DEV):
                px = 1 - x if k & 4 else x
                py = 1 - y if k & 2 else y
                pc = 1 - c if k & 1 else c
                copies.append(pltpu.make_async_remote_copy(
                    src_ref=piece(src, kind, r, 4 * px + 2 * py + pc), dst_ref=dst,
                    send_sem=send_sems.at[q * (N_DEV - 1) + k - 1], recv_sem=recv_sems.at[q * (N_DEV - 1) + k - 1],
                    device_id=(px, py, pc), device_id_type=pl.DeviceIdType.MESH))
            q += 1
    return copies


def _exchange(inputs, specs, name):
    out_shapes, n_parts = _xchg_shapes(inputs, specs)
    ni, ns = len(inputs), len(specs)

    def body(*refs):
        copies = _xchg_copies(specs, refs[:ni], refs[ni:ni + ns], *refs[ni + ns:])
        for cp in copies:
            cp.start()
        for cp in copies:
            cp.wait()

    return pl.pallas_call(
        body, name=name, out_shape=tuple(out_shapes),
        in_specs=[pl.BlockSpec(memory_space=pl.ANY)] * ni,
        out_specs=tuple(pl.BlockSpec(memory_space=pl.ANY) for _ in specs),
        scratch_shapes=_xchg_sems(n_parts),
        compiler_params=pltpu.CompilerParams(has_side_effects=True),
    )(*inputs)


def _call(body, *, name, grid, in_specs, out_specs, out_shape, operands, vmem=None, scratch=(), xchg=None):
    out_shape, out_specs = tuple(out_shape), list(out_specs)
    if xchg is None:
        res = pl.pallas_call(body, name=name, grid=grid, in_specs=in_specs, out_specs=out_specs, out_shape=out_shape,
                             scratch_shapes=list(scratch), compiler_params=_cp(vmem))(*operands)
        return list(res), None
    xin, xspecs = xchg
    xshapes, n_parts = _xchg_shapes(xin, xspecs)
    ni, no, nxi, nxo, nsc = len(operands), len(out_shape), len(xin), len(xspecs), len(scratch)
    anyspec = pl.BlockSpec(memory_space=pl.ANY)

    def hosted(*refs):
        ins, xins = refs[:ni], refs[ni:ni + nxi]
        outs, xouts = refs[ni + nxi:ni + nxi + no], refs[ni + nxi + no:ni + nxi + no + nxo]
        scr, sems = refs[ni + nxi + no + nxo:ni + nxi + no + nxo + nsc], refs[ni + nxi + no + nxo + nsc:]
        copies = _xchg_copies(xspecs, xins, xouts, *sems)
        first, last = True, True
        for ax, n in enumerate(grid):
            first = first & (pl.program_id(ax) == 0)
            last = last & (pl.program_id(ax) == n - 1)

        @pl.when(first)
        def _():
            for cp in copies:
                cp.start()
        body(*ins, *outs, *scr)

        @pl.when(last)
        def _():
            for cp in copies:
                cp.wait()

    res = pl.pallas_call(
        hosted, name=name, grid=grid, in_specs=list(in_specs) + [anyspec] * nxi,
        out_specs=out_specs + [anyspec] * nxo, out_shape=out_shape + tuple(xshapes),
        scratch_shapes=list(scratch) + _xchg_sems(n_parts), compiler_params=_cp(vmem))(*operands, *xin)
    return list(res[:no]), list(res[no:])


def _adaln_fwd(craw, ada_w, ada_b):
    nl, d, n = ada_w.shape
    r = craw.shape[0]

    def body(c_ref, w_ref, b_ref, o_ref):
        s = jax.nn.silu(c_ref[...])
        o_ref[...] = _nn(_bf(s), _bf(w_ref[...])) + b_ref[...]

    return pl.pallas_call(
        body, name="adaln_fwd", grid=(nl,),
        in_specs=[pl.BlockSpec((r, d), lambda i: (0, 0)), pl.BlockSpec((None, d, n), lambda i: (i, 0, 0)),
                  pl.BlockSpec((None, 1, n), lambda i: (i, 0, 0))],
        out_specs=pl.BlockSpec((None, r, n), lambda i: (i, 0, 0)),
        out_shape=jax.ShapeDtypeStruct((nl, r, n), F32),
    )(craw, ada_w, ada_b)


def _adaln_bwd(craw, dm, ada_w):
    nl, d, n = ada_w.shape
    r = craw.shape[0]

    def body(c_ref, dm_ref, w_ref, gw_ref, dc_ref, acc_ref):
        i = pl.program_id(0)
        s, vj = jax.vjp(jax.nn.silu, c_ref[...])
        dmb = _bf(dm_ref[...])
        gw_ref[...] = _tn(_bf(s), dmb)

        @pl.when(i == 0)
        def _():
            acc_ref[...] = jnp.zeros_like(acc_ref)
        acc_ref[...] += _nt(dmb, _bf(w_ref[...]))

        @pl.when(i == nl - 1)
        def _():
            dc_ref[...] = vj(acc_ref[...])[0]

    return pl.pallas_call(
        body, name="adaln_bwd", grid=(nl,),
        in_specs=[pl.BlockSpec((r, d), lambda i: (0, 0)), pl.BlockSpec((None, r, n), lambda i: (i, 0, 0)),
                  pl.BlockSpec((None, d, n), lambda i: (i, 0, 0))],
        out_specs=[pl.BlockSpec((None, d, n), lambda i: (i, 0, 0)), pl.BlockSpec((r, d), lambda i: (0, 0))],
        out_shape=(jax.ShapeDtypeStruct((nl, d, n), F32), jax.ShapeDtypeStruct((r, d), F32)),
        scratch_shapes=[pltpu.VMEM((r, d), F32)],
    )(craw, dm, ada_w)


def _norm_mod(h, mod, g):
    ms = jnp.mean(h * h, axis=-1, keepdims=True)
    y = h * lax.rsqrt(ms + EPS) * g
    return y * (1.0 + mod[1:2]) + mod[0:1]


def _mod_spec(i, nct, d):
    return pl.BlockSpec((None, None, 3, d), lambda b, j: (i, 2 * b + jnp.where(j >= nct, 1, 0), 0, 0))


def _dmod_spec(nct, d):
    return pl.BlockSpec((None, 3, d), lambda b, j: (2 * b + jnp.where(j >= nct, 1, 0), 0, 0))


def _layer_spec(li, *shape):
    return pl.BlockSpec((None,) + tuple(shape), lambda b, j: (li,) + (0,) * len(shape))


def _tile(width, col_blk=0):
    return pl.BlockSpec((None, ROWT, width), lambda b, j: (b, j, col_blk))


def _fused_in_fwd(hs, mods, g, wt, i, jl, nct, name):
    bsz, t, d = hs.shape
    n = wt.shape[1]

    def body(h_ref, mod_ref, g_ref, w_ref, z_ref):
        a = _norm_mod(h_ref[...], mod_ref[...], g_ref[...])
        z_ref[...] = _nt(_bf(a), w_ref[...])

    return pl.pallas_call(
        body, name=name, grid=(bsz, t // ROWT),
        in_specs=[_tile(d), _mod_spec(i, nct, d), _layer_spec(i, 1, d), _layer_spec(jl, n, d)],
        out_specs=_tile(n), out_shape=jax.ShapeDtypeStruct((bsz, t, n), F32),
        compiler_params=_cp(VMEM_BIG),
    )(hs, mods, g, wt)


def _fused_in_bwd(hs, mods, g, wt, dh_out, pieces, small, assemble, i, jl, nct, name, xchg=None):
    bsz, t, d = hs.shape
    n = wt.shape[1]
    npc, nsm = len(pieces), int(small is not None)

    def body(*refs):
        h_ref, mod_ref, g_ref, w_ref, dho_ref = refs[:5]
        p_refs = refs[5:5 + npc]
        s_val = refs[5 + npc][...] if nsm else None
        dhi_ref, dmod_ref, dg_ref, dw_ref = refs[5 + npc + nsm:]
        b, j = pl.program_id(0), pl.program_id(1)
        a, vj = jax.vjp(_norm_mod, h_ref[...], mod_ref[...], g_ref[...])
        dz = _bf(assemble([r[...] for r in p_refs], s_val))
        dh, dmod, dg = vj(_nn(dz, w_ref[...]))
        dhi_ref[...] = dho_ref[...] + dh

        @pl.when((j == 0) | (j == nct))
        def _():
            dmod_ref[...] = jnp.zeros_like(dmod_ref)

        @pl.when((b == 0) & (j == 0))
        def _():
            dg_ref[...] = jnp.zeros_like(dg_ref)
            dw_ref[...] = jnp.zeros_like(dw_ref)
        dmod_ref[...] += dmod
        dg_ref[...] += dg
        dw_ref[...] += _tn(dz, _bf(a))

    full = lambda r, c: pl.BlockSpec((r, c), lambda b, j: (0, 0))
    return _call(
        body, name=name, grid=(bsz, t // ROWT),
        in_specs=[_tile(d), _mod_spec(i, nct, d), _layer_spec(i, 1, d), _layer_spec(jl, n, d), _tile(d)]
                 + [_tile(wd, cb) for _, wd, cb in pieces]
                 + ([_layer_spec(small[1], 1, small[0].shape[2])] if nsm else []),
        out_specs=[_tile(d), _dmod_spec(nct, d), full(1, d), full(n, d)],
        out_shape=(jax.ShapeDtypeStruct((bsz, t, d), F32), jax.ShapeDtypeStruct((2 * bsz, 3, d), F32),
                   jax.ShapeDtypeStruct((1, d), F32), jax.ShapeDtypeStruct((n, d), F32)),
        vmem=VMEM_BIG, xchg=xchg,
        operands=[hs, mods, g, wt, dh_out, *[p for p, _, _ in pieces], *([small[0]] if nsm else [])])


def _rope_tables(lc, l):
    tpos = np.arange(l)
    row = (tpos // GRID_W).astype(np.float32)
    col = (tpos % GRID_W).astype(np.float32)
    nf = 16
    inv = (np.float32(ROPE_BASE) ** (-np.arange(nf, dtype=np.float32) / np.float32(nf))).astype(np.float32)
    ang_r = row[:, None] * inv[None]
    ang_c = col[:, None] * inv[None]
    cos64 = np.concatenate([np.cos(ang_r), np.cos(ang_r), np.cos(ang_c), np.cos(ang_c)], axis=1)
    sin64 = np.concatenate([-np.sin(ang_r), np.sin(ang_r), -np.sin(ang_c), np.sin(ang_c)], axis=1)
    cos = np.concatenate([np.ones((lc, 128), np.float32), np.tile(cos64, (1, 2)).astype(np.float32)], axis=0)
    sin = np.concatenate([np.zeros((lc, 128), np.float32), np.tile(sin64, (1, 2)).astype(np.float32)], axis=0)
    return jnp.asarray(cos), jnp.asarray(sin)


def _rot(x, cos, sin):
    lane = lax.broadcasted_iota(jnp.int32, x.shape, 1)
    first = jnp.bitwise_and(jnp.right_shift(lane, 4), 1) == 0
    partner = jnp.where(first, pltpu.roll(x, LANES - 16, 1), pltpu.roll(x, 16, 1))
    return x * cos + partner * sin


def _split_heads(x):
    low = lax.broadcasted_iota(jnp.int32, x.shape, 1) < 64
    return jnp.where(low, x, 0.0), jnp.where(low, pltpu.roll(x, 64, 1), 0.0)


def _merge_heads(a, b):
    return a + pltpu.roll(b, 64, 1)


def _rope_fwd(z, cos, sin, kv_blk, name):
    bsz, t, _ = z.shape

    def body(q_ref, kv_ref, cos_ref, sin_ref, qp_ref, kvp_ref):
        c, s = cos_ref[...], sin_ref[...]
        outs = []
        for sl in range(4):
            xr = _rot(q_ref[:, sl * LANES:(sl + 1) * LANES], c, s) * 0.125
            outs += list(_split_heads(xr))
        qp_ref[...] = _bf(jnp.concatenate(outs, axis=1))
        k0, k1 = _split_heads(_rot(kv_ref[:, 0:LANES], c, s))
        v0, v1 = _split_heads(kv_ref[:, LANES:2 * LANES])
        kvp_ref[...] = _bf(jnp.concatenate([k0, k1, v0, v1], axis=1))

    return pl.pallas_call(
        body, name=name, grid=(bsz, t // ROWT),
        in_specs=[_tile(512), _tile(256, kv_blk),
                  pl.BlockSpec((ROWT, LANES), lambda b, j: (j, 0)), pl.BlockSpec((ROWT, LANES), lambda b, j: (j, 0))],
        out_specs=[_tile(1024), _tile(512)],
        out_shape=(jax.ShapeDtypeStruct((bsz, t, 1024), BF16), jax.ShapeDtypeStruct((bsz, t, 512), BF16)),
    )(z, z, cos, sin)


def _rope_bwd(dq_r, dkv_acc, cos, sin, t, name):
    bsz = dq_r.shape[0]

    def body(dq_ref, acc_ref, cos_ref, sin_ref, dqo_ref, dkvo_ref):
        c, s = cos_ref[...], -sin_ref[...]
        dqo_ref[...] = jnp.concatenate(
            [_rot(dq_ref[:, sl * LANES:(sl + 1) * LANES], c, s) for sl in range(4)], axis=1)
        dk = _merge_heads(acc_ref[:, 0:LANES], acc_ref[:, LANES:2 * LANES])
        dv = _merge_heads(acc_ref[:, 2 * LANES:3 * LANES], acc_ref[:, 3 * LANES:4 * LANES])
        dkvo_ref[...] = jnp.concatenate([_rot(dk, c, s), dv], axis=1)

    return pl.pallas_call(
        body, name=name, grid=(bsz, t // ROWT),
        in_specs=[_tile(512), _tile(512),
                  pl.BlockSpec((ROWT, LANES), lambda b, j: (j, 0)), pl.BlockSpec((ROWT, LANES), lambda b, j: (j, 0))],
        out_specs=[_tile(512), _tile(256)],
        out_shape=(jax.ShapeDtypeStruct((bsz, t, 512), F32), jax.ShapeDtypeStruct((bsz, t, 256), F32)),
    )(dq_r, dkv_acc, cos, sin)


def _attn_bias(j, ncb, l):
    n = j - ncb
    r = lax.broadcasted_iota(jnp.int32, (QBLK, 3 * QBLK), 0)
    cidx = lax.broadcasted_iota(jnp.int32, (QBLK, 3 * QBLK), 1)
    kpos = (n - 1) * QBLK + cidx
    valid = (jnp.abs(r - cidx + QBLK) <= WINDOW) & (kpos >= 0) & (kpos < l) & (j >= ncb)
    return jnp.where(valid, 0.0, NEG_INF).astype(F32)


def _kv_specs(nb, lc):
    return [pl.BlockSpec((None, QBLK, 512), lambda b, j: (b, jnp.maximum(j - 1, 0), 0)),
            pl.BlockSpec((None, QBLK, 512), lambda b, j: (b, j, 0)),
            pl.BlockSpec((None, QBLK, 512), lambda b, j: (b, jnp.minimum(j + 1, nb - 1), 0)),
            pl.BlockSpec((None, lc, 512), lambda b, j: (b, 0, 0))]


def _lane_pick(x, h):
    lane = lax.broadcasted_iota(jnp.int32, x.shape, 1)
    return jnp.sum(jnp.where(lane == h, x, 0.0), axis=1, keepdims=True)


def _attn_fwd(qp, kvp, sinks, jl, lc, name, xchg=None):
    bsz, t, _ = qp.shape
    nb, ncb, l = t // QBLK, lc // QBLK, t - lc

    def body(sink_ref, q_ref, kp_ref, kc_ref, kn_ref, kx_ref, o_ref, lse_ref):
        j = pl.program_id(1)
        kall = jnp.concatenate([kp_ref[...], kc_ref[...], kn_ref[...], kx_ref[...]], axis=0)
        bias = jnp.concatenate([_attn_bias(j, ncb, l), jnp.zeros((QBLK, lc), F32)], axis=1)
        lane = lax.broadcasted_iota(jnp.int32, (QBLK, LANES), 1)
        lse_all = jnp.zeros((QBLK, LANES), F32)
        outs = []
        for h in range(8):
            hk = h // 4
            kh = kall[:, hk * LANES:(hk + 1) * LANES]
            vh = kall[:, (2 + hk) * LANES:(3 + hk) * LANES]
            s = _nt(q_ref[:, h * LANES:(h + 1) * LANES], kh) + bias
            sk = sink_ref[jl, h]
            m = jnp.maximum(jnp.max(s, axis=1, keepdims=True), sk)
            p = jnp.exp(s - m)
            den = jnp.sum(p, axis=1, keepdims=True) + jnp.exp(sk - m)
            outs.append(_nn(_bf(p), vh) / den)
            lse_all = lse_all + jnp.where(lane == h, m + jnp.log(den), 0.0)
        o_ref[...] = jnp.concatenate([_merge_heads(outs[2 * i], outs[2 * i + 1]) for i in range(4)], axis=1)
        lse_ref[...] = lse_all

    return _call(
        body, name=name, grid=(bsz, nb),
        in_specs=[pl.BlockSpec(memory_space=pltpu.SMEM),
                  pl.BlockSpec((None, QBLK, 1024), lambda b, j: (b, j, 0))] + _kv_specs(nb, lc),
        out_specs=[pl.BlockSpec((None, QBLK, 512), lambda b, j: (b, j, 0)),
                   pl.BlockSpec((None, QBLK, LANES), lambda b, j: (b, j, 0))],
        out_shape=(jax.ShapeDtypeStruct((bsz, t, 512), F32), jax.ShapeDtypeStruct((bsz, t, LANES), F32)),
        xchg=xchg, operands=[sinks, qp, kvp, kvp, kvp, kvp])


def _attn_bwd(qp, kvp, sinks, jl, o, lse, do, lc, name, xchg=None):
    bsz, t, _ = qp.shape
    nb, ncb, l = t // QBLK, lc // QBLK, t - lc
    nk = 3 * QBLK

    def body(sink_ref, q_ref, kp_ref, kc_ref, kn_ref, kx_ref, o_ref, lse_ref, do_ref, dq_ref, acc_ref, ds_ref):
        j = pl.program_id(1)

        @pl.when(j == 0)
        def _():
            acc_ref[...] = jnp.zeros_like(acc_ref)
            ds_ref[...] = jnp.zeros_like(ds_ref)
        kall = jnp.concatenate([kp_ref[...], kc_ref[...], kn_ref[...], kx_ref[...]], axis=0)
        bias = jnp.concatenate([_attn_bias(j, ncb, l), jnp.zeros((QBLK, lc), F32)], axis=1)
        lse = lse_ref[...]
        row8 = lax.broadcasted_iota(jnp.int32, (8, LANES), 0)
        dsink = jnp.zeros((8, LANES), F32)
        dqs = []
        dkv = [jnp.zeros((nk + lc, LANES), F32) for _ in range(4)]
        for s2 in range(4):
            do_pair = _split_heads(do_ref[:, s2 * LANES:(s2 + 1) * LANES])
            o_pair = _split_heads(o_ref[:, s2 * LANES:(s2 + 1) * LANES])
            for e in range(2):
                h = 2 * s2 + e
                hk = h // 4
                kh = kall[:, hk * LANES:(hk + 1) * LANES]
                vh = kall[:, (2 + hk) * LANES:(3 + hk) * LANES]
                qh = q_ref[:, h * LANES:(h + 1) * LANES]
                doh = do_pair[e]
                delta = jnp.sum(doh * o_pair[e], axis=1, keepdims=True)
                lse_h = _lane_pick(lse, h)
                p = jnp.exp(_nt(qh, kh) + bias - lse_h)
                dob = _bf(doh)
                dp = _nt(dob, vh)
                dsc = _bf(p * (dp - delta))
                dqs.append(_nn(dsc, kh) * 0.125)
                dkv[hk] = dkv[hk] + _tn(dsc, qh)
                dkv[2 + hk] = dkv[2 + hk] + _tn(_bf(p), dob)
                ps = jnp.exp(sink_ref[jl, h] - lse_h)
                dsink = dsink + jnp.where(row8 == h, -jnp.sum(ps * delta), 0.0)
        dq_ref[...] = jnp.concatenate([_merge_heads(dqs[2 * i], dqs[2 * i + 1]) for i in range(4)], axis=1)
        ds_ref[...] += dsink
        dall = jnp.concatenate(dkv, axis=1)
        acc_ref[pl.ds(0, lc), :] += dall[nk:, :]

        @pl.when(j >= ncb)
        def _():
            st = pl.multiple_of((j - 1) * QBLK, QBLK)
            acc_ref[pl.ds(st, nk), :] += dall[:nk, :]

    blk = lambda wd: pl.BlockSpec((None, QBLK, wd), lambda b, j: (b, j, 0))
    return _call(
        body, name=name, grid=(bsz, nb),
        in_specs=[pl.BlockSpec(memory_space=pltpu.SMEM), blk(1024)] + _kv_specs(nb, lc)
                 + [blk(512), blk(LANES), blk(512)],
        out_specs=[blk(512), pl.BlockSpec((None, t + QBLK, 512), lambda b, j: (b, 0, 0)),
                   pl.BlockSpec((None, 8, LANES), lambda b, j: (b, 0, 0))],
        out_shape=(jax.ShapeDtypeStruct((bsz, t, 512), F32), jax.ShapeDtypeStruct((bsz, t + QBLK, 512), F32),
                   jax.ShapeDtypeStruct((bsz, 8, LANES), F32)),
        vmem=VMEM_BIG, xchg=xchg, operands=[sinks, qp, kvp, kvp, kvp, kvp, o, lse, do])


def _s5_operators(a_re, a_im, log_dt, b_re, b_im, c_re, c_im):
    hi = lax.Precision.HIGHEST
    dt = jnp.exp(log_dt)[..., None]
    tau = jnp.arange(CHUNK + 1, dtype=F32)[:, None, None, None]
    mag = jnp.exp(tau * (a_re * dt))
    pwr, pwi = mag * jnp.cos(tau * (a_im * dt)), mag * jnp.sin(tau * (a_im * dt))
    ar, ai = pwr[1], pwi[1]
    den = a_re * a_re + a_im * a_im
    fr = ((ar - 1.0) * a_re + ai * a_im) / den
    fi = (ai * a_re - (ar - 1.0) * a_im) / den
    bbr = fr[..., None] * b_re - fi[..., None] * b_im
    bbi = fr[..., None] * b_im + fi[..., None] * b_re
    wr = pwr[:CHUNK, ..., None] * bbr - pwi[:CHUNK, ..., None] * bbi
    wi = pwr[:CHUNK, ..., None] * bbi + pwi[:CHUNK, ..., None] * bbr
    kern = (jnp.einsum("dgcp,tdgpk->dgtck", c_re, wr, precision=hi)
            - jnp.einsum("dgcp,tdgpk->dgtck", c_im, wi, precision=hi))
    g = a_re.shape[1]
    nrow = CHUNK * SSM_GROUP
    lag = np.arange(CHUNK)[:, None, None]
    tt, ss = np.arange(CHUNK)[None, :, None], np.arange(CHUNK)[None, None, :]
    sel_f = jnp.asarray((tt - ss == lag).astype(np.float32))
    sel_b = jnp.asarray((ss - tt == lag).astype(np.float32))
    mt = (jnp.einsum("lts,glck->gsktc", sel_f, kern[0], precision=hi)
          + jnp.einsum("lts,glck->gsktc", sel_b, kern[1], precision=hi)).reshape(g, nrow, nrow)
    to_rows = lambda w: jnp.transpose(w, (1, 0, 3, 2)).reshape(g, nrow, SSM_STATE)
    bt = jnp.concatenate([to_rows(wr[::-1, 0]), to_rows(wi[::-1, 0]), to_rows(wr[:, 1]), to_rows(wi[:, 1])], axis=-1)

    def out_map(d, pw_r, pw_i):
        w2r = c_re[d][None] * pw_r[:, :, None, :] - c_im[d][None] * pw_i[:, :, None, :]
        w2i = c_re[d][None] * pw_i[:, :, None, :] + c_im[d][None] * pw_r[:, :, None, :]
        cols = lambda w: jnp.transpose(w, (1, 3, 0, 2)).reshape(g, SSM_STATE, nrow)
        return [cols(w2r), -cols(w2i)]

    ct = jnp.concatenate(out_map(0, pwr[1:, 0], pwi[1:, 0]) + out_map(1, pwr[1:, 1][::-1], pwi[1:, 1][::-1]), axis=-2)
    return mt, bt, ct, pwr[CHUNK], pwi[CHUNK]


def _scan_powers(a16r, a16i, n_chunks):
    prs, pis = [], []
    r, i = a16r, a16i
    s = 1
    while s < n_chunks:
        prs.append(jnp.concatenate([r, r], axis=-1))
        pis.append(jnp.concatenate([-i, i], axis=-1))
        r, i = r * r - i * i, 2.0 * r * i
        s *= 2
    pad = 16 - len(prs)
    z = jnp.zeros_like(prs[0])
    return jnp.stack(prs + [z] * pad, axis=2), jnp.stack(pis + [z] * pad, axis=2)


def _swap_halves(x):
    return pltpu.roll(x, 64, 1)


def _chunk_scan(s, pr, pi, reverse, conj):
    n = s.shape[0]
    row = lax.broadcasted_iota(jnp.int32, s.shape, 0)

    def shift(x, k):
        if reverse:
            return jnp.where(row < n - k, pltpu.roll(x, n - k, 0), 0.0)
        return jnp.where(row >= k, pltpu.roll(x, k, 0), 0.0)

    h = shift(s, 1)
    k, j = 1, 0
    while k < n:
        hs = shift(h, k)
        pij = pi[j:j + 1, :]
        h = h + pr[j:j + 1, :] * hs + (-pij if conj else pij) * _swap_halves(hs)
        k, j = 2 * k, j + 1
    return h


def _lane_group():
    return jnp.right_shift(lax.broadcasted_iota(jnp.int32, (1, LANES), 1), 4)


def _blocks_from_rows(xs, gl):
    lg = _lane_group()
    halves = []
    for half in range(2):
        acc = None
        for tl in range(8):
            sft = ((tl - gl) * SSM_GROUP) % LANES
            v = xs[half * 8 + tl]
            v = pltpu.roll(v, sft, 1) if sft else v
            acc = v if acc is None else jnp.where(lg == tl, v, acc)
        halves.append(acc)
    return jnp.concatenate(halves, axis=1)


def _rows_from_blocks(ys, tok):
    lg = _lane_group()
    half, tl = tok // 8, tok % 8
    acc = None
    for gl in range(8):
        v = ys[gl][:, half * LANES:(half + 1) * LANES]
        sft = ((gl - tl) * SSM_GROUP) % LANES
        v = pltpu.roll(v, sft, 1) if sft else v
        acc = v if acc is None else jnp.where(lg == gl, v, acc)
    return acc


def _both_scans(s, prf, pif, prb, pib, ncc, adjoint):
    n = s.shape[0]
    hf = _chunk_scan(s[:, :LANES], prf, pif, adjoint, adjoint)
    rot = pltpu.roll(s[:, LANES:], n - ncc, 0)
    hb = pltpu.roll(_chunk_scan(rot, prb, pib, not adjoint, adjoint), ncc, 0)
    return hf, hb


def _s5_op_specs(idx, jl):
    op = pl.BlockSpec((None, 8, 256, 256), lambda a, b: (jl, idx(a, b), 0, 0))
    pw = pl.BlockSpec((None, 2, 8, 16, LANES), lambda a, b: (jl, 0, idx(a, b), 0, 0))
    return [op, op, op, pw, pw]


def _s5_fwd(z, col0, s5_ops, jl, lc, name, xchg=None):
    bsz, t, _ = z.shape
    nc, ncc = t // CHUNK, lc // CHUNK

    def body(u_ref, mt_ref, bt_ref, ct_ref, pr_ref, pi_ref, y_ref):
        xs = [u_ref[pl.ds(tok, nc, stride=CHUNK), :] for tok in range(CHUNK)]
        ys = []
        for gl in range(8):
            u = _bf(_blocks_from_rows(xs, gl))
            hf, hb = _both_scans(_nn(u, bt_ref[gl]), pr_ref[0, gl], pi_ref[0, gl], pr_ref[1, gl], pi_ref[1, gl],
                                 ncc, False)
            ys.append(_nn(u, mt_ref[gl]) + _nn(_bf(jnp.concatenate([hf, hb], axis=1)), ct_ref[gl]))
        for tok in range(CHUNK):
            y_ref[pl.ds(tok, nc, stride=CHUNK), :] = _rows_from_blocks(ys, tok)

    return _call(
        body, name=name, grid=(bsz, 4),
        in_specs=[pl.BlockSpec((None, t, LANES), lambda b, s: (b, 0, col0 + s))] + _s5_op_specs(lambda b, s: s, jl),
        out_specs=[pl.BlockSpec((None, t, LANES), lambda b, s: (b, 0, s))],
        out_shape=[jax.ShapeDtypeStruct((bsz, t, 512), F32)], vmem=VMEM_BIG, xchg=xchg,
        operands=[z, *s5_ops])


def _s5_bwd(z, col0, dy, s5_ops, jl, lc, name, xchg=None):
    bsz, t, _ = z.shape
    nc, ncc = t // CHUNK, lc // CHUNK

    def body(u_ref, dy_ref, mt_ref, bt_ref, ct_ref, pr_ref, pi_ref, du_ref, dmt_ref, dbt_ref, dct_ref, da_ref):
        b = pl.program_id(1)

        @pl.when(b == 0)
        def _():
            dmt_ref[...] = jnp.zeros_like(dmt_ref)
            dbt_ref[...] = jnp.zeros_like(dbt_ref)
            dct_ref[...] = jnp.zeros_like(dct_ref)
            da_ref[...] = jnp.zeros_like(da_ref)
        xs = [u_ref[pl.ds(tok, nc, stride=CHUNK), :] for tok in range(CHUNK)]
        dys = [dy_ref[pl.ds(tok, nc, stride=CHUNK), :] for tok in range(CHUNK)]
        row8 = lax.broadcasted_iota(jnp.int32, (8, LANES), 0)
        dus = []
        for gl in range(8):
            pw = (pr_ref[0, gl], pi_ref[0, gl], pr_ref[1, gl], pi_ref[1, gl])
            u = _bf(_blocks_from_rows(xs, gl))
            dyg = _bf(_blocks_from_rows(dys, gl))
            hf, hb = _both_scans(_nn(u, bt_ref[gl]), *pw, ncc, False)
            gf, gb = _both_scans(_nt(dyg, ct_ref[gl]), *pw, ncc, True)
            hcat, gcat = _bf(jnp.concatenate([hf, hb], axis=1)), _bf(jnp.concatenate([gf, gb], axis=1))
            dus.append(_nt(dyg, mt_ref[gl]) + _nt(gcat, bt_ref[gl]))
            dmt_ref[gl] += _tn(u, dyg)
            dct_ref[gl] += _tn(hcat, dyg)
            dbt_ref[gl] += _tn(u, gcat)
            vals = [jnp.sum(gf * hf, axis=0, keepdims=True), jnp.sum(gf * _swap_halves(hf), axis=0, keepdims=True),
                    jnp.sum(gb * hb, axis=0, keepdims=True), jnp.sum(gb * _swap_halves(hb), axis=0, keepdims=True)]
            da = jnp.zeros((8, LANES), F32)
            for q, v in enumerate(vals):
                da = da + jnp.where(row8 == q, v, 0.0)
            da_ref[gl] += da
        for tok in range(CHUNK):
            du_ref[pl.ds(tok, nc, stride=CHUNK), :] = _rows_from_blocks(dus, tok)

    op_out = pl.BlockSpec((8, 256, 256), lambda s, b: (s, 0, 0))
    op_shape = jax.ShapeDtypeStruct((SSM_GROUPS, 256, 256), F32)
    return _call(
        body, name=name, grid=(4, bsz),
        in_specs=[pl.BlockSpec((None, t, LANES), lambda s, b: (b, 0, col0 + s)),
                  pl.BlockSpec((None, t, LANES), lambda s, b: (b, 0, s))] + _s5_op_specs(lambda s, b: s, jl),
        out_specs=[pl.BlockSpec((None, t, LANES), lambda s, b: (b, 0, s)), op_out, op_out, op_out,
                   pl.BlockSpec((8, 8, LANES), lambda s, b: (s, 0, 0))],
        out_shape=(jax.ShapeDtypeStruct((bsz, t, 512), F32), op_shape, op_shape, op_shape,
                   jax.ShapeDtypeStruct((SSM_GROUPS, 8, LANES), F32)),
        vmem=VMEM_BIG, xchg=xchg, operands=[z, dy, *s5_ops])


def _glu_in(y, u, dsk):
    return jax.nn.gelu(y + u * dsk)


def _even_mix(oa, ga, zg, tg, gs):
    return jnp.concatenate([oa * jax.nn.silu(ga), zg * jax.nn.sigmoid(tg) * jax.nn.silu(gs)], axis=1)


def _even_specs(d, i, jl, nct):
    return [_tile(d), _mod_spec(i, nct, d), _tile(512), _tile(512)] + [_tile(256, cb) for cb in range(3, 9)] + [
        _layer_spec(jl, 1, 512), _layer_spec(0, 512, 512), _layer_spec(jl, 1, 512), _layer_spec(0, 1024, d)]


def _cat2(a_ref, b_ref):
    return jnp.concatenate([a_ref[...], b_ref[...]], axis=1)


def _even_out_fwd(hs, mods, oa, y, z, dsk, gw, gb, wout, i, jl, nct, name):
    bsz, t, d = hs.shape

    def body(h_ref, mod_ref, oa_ref, y_ref, ga0, ga1, u0, u1, gs0, gs1, dsk_ref, gw_ref, gb_ref, wo_ref, o_ref):
        zg = _glu_in(y_ref[...], _cat2(u0, u1), dsk_ref[...])
        tg = _nn(_bf(zg), gw_ref[...]) + gb_ref[...]
        mix = _even_mix(oa_ref[...], _cat2(ga0, ga1), zg, tg, _cat2(gs0, gs1))
        o_ref[...] = h_ref[...] + mod_ref[2:3, :] * _nn(_bf(mix), wo_ref[...])

    return pl.pallas_call(
        body, name=name, grid=(bsz, t // ROWT), in_specs=_even_specs(d, i, jl, nct),
        out_specs=_tile(d), out_shape=jax.ShapeDtypeStruct((bsz, t, d), F32), compiler_params=_cp(VMEM_BIG),
    )(hs, mods, oa, y, z, z, z, z, z, z, dsk, gw, gb, wout)


def _even_out_bwd(dh, mods, oa, y, z, dsk, gw, gb, wout, i, jl, nct, name):
    bsz, t, d = dh.shape

    def body(dh_ref, mod_ref, oa_ref, y_ref, ga0, ga1, u0, u1, gs0, gs1, dsk_ref, gw_ref, gb_ref, wo_ref,
             doa_ref, dga_ref, dy_ref, dgs_ref, dmod_ref, ddsk_ref, dgw_ref, dgb_ref, dwo_ref):
        b, j = pl.program_id(0), pl.program_id(1)
        zg, vj1 = jax.vjp(_glu_in, y_ref[...], _cat2(u0, u1), dsk_ref[...])
        zgb = _bf(zg)
        tg = _nn(zgb, gw_ref[...]) + gb_ref[...]
        mix, vj2 = jax.vjp(_even_mix, oa_ref[...], _cat2(ga0, ga1), zg, tg, _cat2(gs0, gs1))
        mixb = _bf(mix)
        dhv = dh_ref[...]
        dyo = _bf(dhv * mod_ref[2:3, :])
        yout = _nn(mixb, wo_ref[...])
        doa, dga, dzg, dtg, dgs = vj2(_nt(dyo, wo_ref[...]))
        dtb = _bf(dtg)
        dzg = dzg + _nt(dtb, gw_ref[...])
        dy, _, ddsk = vj1(dzg)
        doa_ref[...], dga_ref[...], dy_ref[...], dgs_ref[...] = doa, dga, dy, dgs

        @pl.when((j == 0) | (j == nct))
        def _():
            dmod_ref[...] = jnp.zeros_like(dmod_ref)

        @pl.when((b == 0) & (j == 0))
        def _():
            ddsk_ref[...] = jnp.zeros_like(ddsk_ref)
            dgw_ref[...] = jnp.zeros_like(dgw_ref)
            dgb_ref[...] = jnp.zeros_like(dgb_ref)
            dwo_ref[...] = jnp.zeros_like(dwo_ref)
        dmod_ref[2:3, :] += jnp.sum(dhv * yout, axis=0, keepdims=True)
        ddsk_ref[...] += ddsk
        dgw_ref[...] += _tn(zgb, dtb)
        dgb_ref[...] += jnp.sum(dtg, axis=0, keepdims=True)
        dwo_ref[...] += _tn(mixb, dyo)

    full = lambda r, c: pl.BlockSpec((r, c), lambda b, j: (0, 0))
    f512 = jax.ShapeDtypeStruct((bsz, t, 512), F32)
    return pl.pallas_call(
        body, name=name, grid=(bsz, t // ROWT), in_specs=_even_specs(d, i, jl, nct),
        out_specs=[_tile(512), _tile(512), _tile(512), _tile(512), _dmod_spec(nct, d),
                   full(1, 512), full(512, 512), full(1, 512), full(1024, d)],
        out_shape=(f512, f512, f512, f512, jax.ShapeDtypeStruct((2 * bsz, 3, d), F32),
                   jax.ShapeDtypeStruct((1, 512), F32), jax.ShapeDtypeStruct((512, 512), F32),
                   jax.ShapeDtypeStruct((1, 512), F32), jax.ShapeDtypeStruct((1024, d), F32)),
        compiler_params=_cp(VMEM_BIG),
    )(dh, mods, oa, y, z, z, z, z, z, z, dsk, gw, gb, wout)


def _pool(src, col_blk, r, lc, transpose, name):
    bsz, t, _ = src.shape
    segs = [(0, lc, 16), (lc, t - lc, 32 + lc)]
    halo = [(0, 16), (16 + lc, 16), (32 + t, 16)]
    cw = 256

    def inv_count(t0, rows, ln):
        pos = t0 + lax.broadcasted_iota(jnp.int32, (rows, cw), 0)
        cnt = jnp.minimum(pos + r + 1, ln) - jnp.maximum(pos - r, 0)
        return 1.0 / cnt.astype(F32)

    def body(x_ref, o_ref, s_ref):
        for h0, hn in halo:
            s_ref[pl.ds(h0, hn), :] = jnp.zeros((hn, cw), F32)
        for r0, ln, s0 in segs:
            step = min(512, ln)
            for c0 in range(0, ln, step):
                v = x_ref[pl.ds(r0 + c0, step), :]
                s_ref[pl.ds(s0 + c0, step), :] = v * inv_count(c0, step, ln) if transpose else v
        for r0, ln, s0 in segs:
            step = min(512, ln)
            for c0 in range(0, ln, step):
                acc = s_ref[pl.ds(s0 + c0 - r, step), :]
                for dlt in range(-r + 1, r + 1):
                    acc = acc + s_ref[pl.ds(s0 + c0 + dlt, step), :]
                ctr = x_ref[pl.ds(r0 + c0, step), :]
                o_ref[pl.ds(r0 + c0, step), :] = (acc if transpose else acc * inv_count(c0, step, ln)) - ctr

    return pl.pallas_call(
        body, name=name, grid=(bsz,),
        in_specs=[pl.BlockSpec((None, t, cw), lambda b: (b, 0, col_blk))],
        out_specs=pl.BlockSpec((None, t, cw), lambda b: (b, 0, 0)),
        out_shape=jax.ShapeDtypeStruct((bsz, t, cw), F32),
        scratch_shapes=[pltpu.VMEM((t + 48, cw), F32)],
        compiler_params=_cp(VMEM_BIG),
    )(src)


def _odd_specs(d, i, jl, nct):
    return [_tile(d), _mod_spec(i, nct, d), _tile(256), _tile(256), _tile(256), _tile(256), _tile(1024, 1),
            _layer_spec(0, 4, 256, 256), _layer_spec(jl, 1, 1024), _layer_spec(0, 1024, d)]


def _odd_mix(pm, psc, gz):
    return pm * psc * jax.nn.silu(gz)


def _odd_out_fwd(hs, mods, ps, z, pw, psc, wout, i, jl, nct, name):
    bsz, t, d = hs.shape

    def body(h_ref, mod_ref, p0, p1, p2, p3, gz_ref, pw_ref, psc_ref, wo_ref, o_ref):
        pm = jnp.concatenate([_nn(_bf(p[...]), pw_ref[q]) for q, p in enumerate((p0, p1, p2, p3))], axis=1)
        mix = _odd_mix(pm, psc_ref[...], gz_ref[...])
        o_ref[...] = h_ref[...] + mod_ref[2:3, :] * _nn(_bf(mix), wo_ref[...])

    return pl.pallas_call(
        body, name=name, grid=(bsz, t // ROWT), in_specs=_odd_specs(d, i, jl, nct),
        out_specs=_tile(d), out_shape=jax.ShapeDtypeStruct((bsz, t, d), F32), compiler_params=_cp(VMEM_BIG),
    )(hs, mods, *ps, z, pw, psc, wout)


def _odd_out_bwd(dh, mods, ps, z, pw, psc, wout, i, jl, nct, name):
    bsz, t, d = dh.shape

    def body(dh_ref, mod_ref, p0, p1, p2, p3, gz_ref, pw_ref, psc_ref, wo_ref,
             dp_ref, dgz_ref, dmod_ref, dpw_ref, dpsc_ref, dwo_ref):
        b, j = pl.program_id(0), pl.program_id(1)
        pbs = [_bf(p[...]) for p in (p0, p1, p2, p3)]
        pm = jnp.concatenate([_nn(pb, pw_ref[q]) for q, pb in enumerate(pbs)], axis=1)
        mix, vj = jax.vjp(_odd_mix, pm, psc_ref[...], gz_ref[...])
        mixb = _bf(mix)
        dhv = dh_ref[...]
        dyo = _bf(dhv * mod_ref[2:3, :])
        yout = _nn(mixb, wo_ref[...])
        dpm, dpsc, dgz = vj(_nt(dyo, wo_ref[...]))
        dgz_ref[...] = dgz
        dpmb = _bf(dpm)
        dp_ref[...] = jnp.concatenate([_nt(dpmb[:, q * 256:(q + 1) * 256], pw_ref[q]) for q in range(4)], axis=1)

        @pl.when((j == 0) | (j == nct))
        def _():
            dmod_ref[...] = jnp.zeros_like(dmod_ref)

        @pl.when((b == 0) & (j == 0))
        def _():
            dpw_ref[...] = jnp.zeros_like(dpw_ref)
            dpsc_ref[...] = jnp.zeros_like(dpsc_ref)
            dwo_ref[...] = jnp.zeros_like(dwo_ref)
        dmod_ref[2:3, :] += jnp.sum(dhv * yout, axis=0, keepdims=True)
        for q in range(4):
            dpw_ref[q] += _tn(pbs[q], dpmb[:, q * 256:(q + 1) * 256])
        dpsc_ref[...] += dpsc
        dwo_ref[...] += _tn(mixb, dyo)

    full = lambda *s: pl.BlockSpec(s, lambda b, j: (0,) * len(s))
    return pl.pallas_call(
        body, name=name, grid=(bsz, t // ROWT), in_specs=_odd_specs(d, i, jl, nct),
        out_specs=[_tile(1024), _tile(1024), _dmod_spec(nct, d), full(4, 256, 256), full(1, 1024), full(1024, d)],
        out_shape=(jax.ShapeDtypeStruct((bsz, t, 1024), F32), jax.ShapeDtypeStruct((bsz, t, 1024), F32),
                   jax.ShapeDtypeStruct((2 * bsz, 3, d), F32), jax.ShapeDtypeStruct((4, 256, 256), F32),
                   jax.ShapeDtypeStruct((1, 1024), F32), jax.ShapeDtypeStruct((1024, d), F32)),
        compiler_params=_cp(VMEM_BIG),
    )(dh, mods, *ps, z, pw, psc, wout)


def _rms(h, g):
    return h * lax.rsqrt(jnp.mean(h * h, axis=-1, keepdims=True) + EPS) * g


def _final_loss(hs, g, target, nct, name):
    bsz, t, d = hs.shape

    def body(h_ref, g_ref, t_ref, dh_ref, loss_ref, dg_ref):
        b, j = pl.program_id(0), pl.program_id(1)
        y, vj = jax.vjp(_rms, h_ref[...], g_ref[...])
        err = jnp.where(j >= nct, y - t_ref[...], 0.0)
        dh, dg = vj(err * (1.0 / d))
        dh_ref[...] = dh

        @pl.when(j == 0)
        def _():
            loss_ref[...] = jnp.zeros_like(loss_ref)

        @pl.when((b == 0) & (j == 0))
        def _():
            dg_ref[...] = jnp.zeros_like(dg_ref)
        loss_ref[...] += 0.5 * jnp.sum(jnp.mean(err * err, axis=-1))
        dg_ref[...] += dg

    return pl.pallas_call(
        body, name=name, grid=(bsz, t // ROWT),
        in_specs=[_tile(d), pl.BlockSpec((1, d), lambda b, j: (0, 0)),
                  pl.BlockSpec((None, ROWT, d), lambda b, j: (b, jnp.maximum(j - nct, 0), 0))],
        out_specs=[_tile(d), pl.BlockSpec((None, 8, LANES), lambda b, j: (b, 0, 0)),
                   pl.BlockSpec((1, d), lambda b, j: (0, 0))],
        out_shape=(jax.ShapeDtypeStruct((bsz, t, d), F32), jax.ShapeDtypeStruct((bsz, 8, LANES), F32),
                   jax.ShapeDtypeStruct((1, d), F32)),
    )(hs, g, target)


def _adamw(w, g_list, m, v, transpose_g, rows, name):
    nl, r, c = w.shape
    ns = g_list[0].shape[1]
    rt = rows or r
    offs = np.cumsum([0] + [g.shape[0] for g in g_list])
    ng = len(g_list)

    def body(*refs):
        w_ref, m_ref, v_ref = refs[0], refs[1 + ng], refs[2 + ng]
        go_ref, d_ref, mo_ref, vo_ref = refs[3 + ng:]
        layer = pl.program_id(0)
        for k in range(ng):
            g_ref = refs[1 + k]

            @pl.when((layer >= offs[k]) & (layer < offs[k + 1]))
            def _():
                g = g_ref[0]
                for s in range(1, ns):
                    g = g + g_ref[s]
                if transpose_g:
                    g = g.T
                mn = ADAM_B1 * m_ref[...] + (1.0 - ADAM_B1) * g
                vn = ADAM_B2 * v_ref[...] + (1.0 - ADAM_B2) * jnp.square(g)
                m_hat = mn / (1.0 - ADAM_B1 ** ADAM_STEP)
                v_hat = vn / (1.0 - ADAM_B2 ** ADAM_STEP)
                go_ref[...] = g
                d_ref[...] = -ADAM_LR * (m_hat / (jnp.sqrt(v_hat) + ADAM_EPS) + ADAM_WD * w_ref[...])
                mo_ref[...] = mn
                vo_ref[...] = vn

    def g_spec(k):
        lo, n_k = int(offs[k]), int(offs[k + 1] - offs[k])

        def idx(l, q):
            mine = (l >= lo) & (l < lo + n_k)
            q = jnp.where(mine, q, 0)
            return (jnp.clip(l - lo, 0, n_k - 1), 0, 0, q) if transpose_g else (jnp.clip(l - lo, 0, n_k - 1), 0, q, 0)
        return pl.BlockSpec((None, ns, c, rt) if transpose_g else (None, ns, rt, c), idx)

    blk = pl.BlockSpec((None, rt, c), lambda l, q: (l, q, 0))
    out = jax.ShapeDtypeStruct((nl, r, c), F32)
    return pl.pallas_call(
        body, name=name, grid=(nl, r // rt),
        in_specs=[blk] + [g_spec(k) for k in range(ng)] + [blk, blk],
        out_specs=[blk, blk, blk, blk], out_shape=(out, out, out, out), compiler_params=_cp(VMEM_BIG),
    )(w, *g_list, m, v)


def _sum_slots(slots, name):
    ns, r, c = slots.shape

    def body(s_ref, o_ref):
        g = s_ref[0]
        for s in range(1, ns):
            g = g + s_ref[s]
        o_ref[...] = g

    return pl.pallas_call(body, name=name, out_shape=jax.ShapeDtypeStruct((r, c), F32), compiler_params=_cp(VMEM_BIG))(slots)


def _pack(arrs):
    flat = jnp.concatenate([a.reshape(-1) for a in arrs])
    return jnp.pad(flat, (0, (-flat.shape[0]) % (PACK_ROWS * PACK_W))).reshape(-1, PACK_W)


def _unpack(buf, shapes):
    flat = buf.reshape(-1)
    out, off = [], 0
    for s in shapes:
        n = int(np.prod(s))
        out.append(flat[off:off + n].reshape(s))
        off += n
    return out


def kernel(x, c, ctx, c_ctx, ada_w, ada_b, norm_g, even_w_in, even_w_out, attn_sink, ssm_a_re, ssm_a_im, ssm_log_dt, ssm_b_re, ssm_b_im, ssm_c_re, ssm_c_im, ssm_d, glu_w, glu_b, odd_w_in, odd_w_out, pool_w, pool_scale, final_g, loss_target, m_c_ctx, m_ada_w, m_ada_b, m_norm_g, m_even_w_in, m_even_w_out, m_attn_sink, m_ssm_a_re, m_ssm_a_im, m_ssm_log_dt, m_ssm_b_re, m_ssm_b_im, m_ssm_c_re, m_ssm_c_im, m_ssm_d, m_glu_w, m_glu_b, m_odd_w_in, m_odd_w_out, m_pool_w, m_pool_scale, m_final_g, v_c_ctx, v_ada_w, v_ada_b, v_norm_g, v_even_w_in, v_even_w_out, v_attn_sink, v_ssm_a_re, v_ssm_a_im, v_ssm_log_dt, v_ssm_b_re, v_ssm_b_im, v_ssm_c_re, v_ssm_c_im, v_ssm_d, v_glu_w, v_glu_b, v_odd_w_in, v_odd_w_out, v_pool_w, v_pool_scale, v_final_g):
    args = dict(locals())
    bsz, l, d = x.shape
    lc = ctx.shape[1]
    t = lc + l
    nct = lc // ROWT
    depth = ada_w.shape[0]
    nl2 = even_w_in.shape[0]
    me = _my_index()
    assert lc % ROWT == 0 and l % ROWT == 0 and d == 1024 and bsz * N_DEV < 32

    npw = pool_w.shape[2]
    shards = {"even": [_bf(jnp.transpose(even_w_in, (0, 2, 1))), _bf(even_w_out), _bf(glu_w)],
              "odd": [_bf(jnp.transpose(odd_w_in, (0, 2, 1))), _bf(odd_w_out), _bf(pool_w.reshape(nl2, -1, pool_w.shape[-1]))]}

    def wgather(kind, jl):
        return shards[kind], [("gather", [(ii, jl)], 0) for ii in range(3)]

    def wjoin(kind, res):
        full = [a.reshape(1, N_DEV * a.shape[2], a.shape[3]) for a in res]
        if kind == "odd":
            full[2] = jnp.transpose(res[2].reshape(1, N_DEV, 4, npw, 256), (0, 2, 1, 3, 4)).reshape(1, 4, 256, 256)
        return full

    xin, xsp = wgather("even", 0)
    got = _exchange(xin + [c, pool_scale], xsp + [("gather", [(3, None)], 0), ("gather", [(4, None)], 0)], "gather_first")
    weights = {("even", 0): wjoin("even", got[:3])}
    c_all = got[3].reshape(N_DEV * bsz, d)
    psc_full = jnp.transpose(got[4][0], (1, 0, 2)).reshape(nl2, 1, d)

    n_rows = 32
    craw = jnp.concatenate([c_all, c_ctx[None], jnp.zeros((n_rows - N_DEV * bsz - 1, d), F32)], axis=0)
    ncol = ada_w.shape[2]
    ada_b_loc = lax.dynamic_slice_in_dim(ada_b, me * ncol, ncol, axis=1)[:, None, :]
    m_loc = _adaln_fwd(craw, ada_w, ada_b_loc)
    m_all = _exchange([m_loc], [("gather", [(0, None)], 0)], "gather_mod")[0][0]
    m_all = jnp.transpose(m_all, (1, 2, 0, 3)).reshape(depth, n_rows, 3 * d)
    lat = lax.dynamic_slice_in_dim(m_all, me * bsz, bsz, axis=1).reshape(depth, bsz, 1, 3, d)
    cmod = jnp.broadcast_to(m_all[:, N_DEV * bsz].reshape(depth, 1, 1, 3, d), (depth, bsz, 1, 3, d))
    mods = jnp.concatenate([cmod, lat], axis=2).reshape(depth, 2 * bsz, 3, d)

    cos, sin = _rope_tables(lc, l)
    hs = jnp.concatenate([ctx, x], axis=1)
    g3 = norm_g[:, None, :]
    dsk3, gb3 = ssm_d[:, None, :], glu_b[:, None, :]
    u_col = 1280 // LANES

    s5_prm = (ssm_a_re, ssm_a_im, ssm_log_dt, ssm_b_re, ssm_b_im, ssm_c_re, ssm_c_im)
    ops = jax.vmap(_s5_operators)(*s5_prm)
    pr, pi = jax.vmap(lambda r, q: _scan_powers(r, q, t // CHUNK))(ops[3], ops[4])
    s5_ops = (_bf(ops[0]), _bf(ops[1]), _bf(ops[2]), pr, pi)
    saved = []
    for i in range(depth):
        jl = i // 2
        if i % 2 == 0:
            wt, wo, gwf = weights[("even", jl)]
            z = _fused_in_fwd(hs, mods, g3, wt, i, 0, nct, f"in_fwd{i}")
            qp, kvp = _rope_fwd(z, cos, sin, 2, f"rope_fwd{i}")
            (oa, lse), got = _attn_fwd(qp, kvp, attn_sink, jl, lc, f"attn_fwd{i}", xchg=wgather("odd", jl))
            weights[("odd", jl)] = wjoin("odd", got)
            nxt = wgather("even", jl + 1) if jl + 1 < nl2 else None
            (y,), got = _s5_fwd(z, u_col, s5_ops, jl, lc, f"s5_fwd{i}", xchg=nxt)
            if nxt:
                weights[("even", jl + 1)] = wjoin("even", got)
            hn = _even_out_fwd(hs, mods, oa, y, z, dsk3, gwf, gb3, wo, i, jl, nct, f"out_fwd{i}")
            saved.append(dict(hs=hs, z=z, qp=qp, kvp=kvp, oa=oa, lse=lse, y=y))
        else:
            wt, wo, pwf = weights[("odd", jl)]
            z = _fused_in_fwd(hs, mods, g3, wt, i, 0, nct, f"in_fwd{i}")
            ps = [_pool(z, gi, wd // 2, lc, False, f"pool_fwd{i}_{gi}") for gi, wd in enumerate(POOL_WINDOWS)]
            hn = _odd_out_fwd(hs, mods, ps, z, pwf, psc_full, wo, i, jl, nct, f"out_fwd{i}")
            saved.append(dict(hs=hs, z=z, ps=ps))
        hs = hn

    dh, loss_p, d_final_g = _final_loss(hs, final_g[None], loss_target, nct, "final_loss")
    loss = lax.psum(jnp.sum(loss_p[:, 0, 0]), ("x", "y", "c"))

    slots = {n: [None] * nl2 for n in ("even_w_in", "even_w_out", "glu_w", "odd_w_in", "odd_w_out", "pool_w", "pool_scale")}
    gsmall = {n: [None] * nl2 for n in ("attn_sink", "ssm_d", "glu_b")}
    d_norm_g, d_mods = [None] * depth, [None] * depth
    d_ops = [None] * nl2

    def rows_xchg(named):
        arrs = [a for _, _, a in named]
        specs = [("rows2" if n == "pool_w" else "rows", [(k, None)], a.shape[1 if n == "pool_w" else 0] // N_DEV)
                 for k, (n, _, a) in enumerate(named)]
        return (arrs, specs)

    def keep(named, res):
        for (n, jl_, _), r_ in zip(named, res):
            slots[n][jl_] = r_

    pending = None
    for i in reversed(range(depth)):
        jl = i // 2
        sv = saved[i]
        if i % 2 == 0:
            wt, wo, gwf = weights[("even", jl)]
            doa, dga, dy, dgs, dmod_g, ddsk, dgw, dgb, dwo = _even_out_bwd(
                dh, mods, sv["oa"], sv["y"], sv["z"], dsk3, gwf, gb3, wo, i, jl, nct, f"out_bwd{i}")
            (dq_r, dkv_acc, dsink), got = _attn_bwd(sv["qp"], sv["kvp"], attn_sink, jl, sv["oa"], sv["lse"], doa, lc,
                                                     f"attn_bwd{i}", xchg=rows_xchg(pending) if pending else None)
            if pending:
                keep(pending, got)
            dq, dkv = _rope_bwd(dq_r, dkv_acc, cos, sin, t, f"rope_bwd{i}")
            mine = [("even_w_out", jl, dwo), ("glu_w", jl, dgw)]
            (du, dmt, dbt, dct, dav), got = _s5_bwd(sv["z"], u_col, dy, s5_ops, jl, lc, f"s5_bwd{i}", xchg=rows_xchg(mine))
            keep(mine, got)
            d_ops[jl] = (dmt, dbt, dct,
                         jnp.stack([dav[:, 0, :64] + dav[:, 0, 64:], dav[:, 2, :64] + dav[:, 2, 64:]]),
                         jnp.stack([dav[:, 1, 64:] - dav[:, 1, :64], dav[:, 3, 64:] - dav[:, 3, :64]]))
            gsmall["attn_sink"][jl] = jnp.sum(dsink[:, :, 0], axis=0)
            gsmall["ssm_d"][jl] = ddsk[0]
            gsmall["glu_b"][jl] = dgb[0]
            pieces = [(dq, 512, 0), (dkv, 256, 0), (dga, 512, 0), (du, 512, 0), (dy, 512, 0), (dgs, 512, 0)]
            asm = lambda p, s: jnp.concatenate([p[0], p[1], p[2], p[3] + p[4] * s, p[5]], axis=1)
            (dh, dmod_in, dg, dw), _ = _fused_in_bwd(sv["hs"], mods, g3, wt, dh, pieces, (dsk3, jl), asm, i, 0, nct, f"in_bwd{i}")
            pending = [("even_w_in", jl, dw)]
        else:
            wt, wo, pwf = weights[("odd", jl)]
            dp, dgz, dmod_g, dpw, dpsc, dwo = _odd_out_bwd(
                dh, mods, sv["ps"], sv["z"], pwf, psc_full, wo, i, jl, nct, f"out_bwd{i}")
            dus = [_pool(dp, gi, wd // 2, lc, True, f"pool_bwd{i}_{gi}") for gi, wd in enumerate(POOL_WINDOWS)]
            dpsc8 = jnp.broadcast_to(dpsc.reshape(N_DEV, 1, -1), (N_DEV, 8, d // N_DEV)).reshape(N_DEV * 8, -1)
            pieces = [(u_, 256, 0) for u_ in dus] + [(dgz, 1024, 0)]
            asm = lambda p, s: jnp.concatenate(p, axis=1)
            (dh, dmod_in, dg, dw), got = _fused_in_bwd(sv["hs"], mods, g3, wt, dh, pieces, None, asm, i, 0, nct, f"in_bwd{i}",
                                                       xchg=rows_xchg(pending) if pending else None)
            if pending:
                keep(pending, got)
            pending = [("odd_w_in", jl, dw), ("odd_w_out", jl, dwo), ("pool_w", jl, dpw), ("pool_scale", jl, dpsc8)]
        d_norm_g[i] = dg[0]
        d_mods[i] = (dmod_g + dmod_in).reshape(bsz, 2, 3 * d)
    grad_x = dh[:, lc:]

    _, op_vjp = jax.vjp(jax.vmap(_s5_operators), *s5_prm)
    dprm = op_vjp(tuple(jnp.stack([d_ops[jl][q] for jl in range(nl2)]) for q in range(5)))

    dmd = jnp.stack(d_mods)
    dm_loc = jnp.concatenate([dmd[:, :, 1], jnp.sum(dmd[:, :, 0], axis=1, keepdims=True)], axis=1)
    dm_all = _exchange([dm_loc], [("gather", [(0, None)], 0)], "gather_dmod")[0][0]
    dm_lat = jnp.transpose(dm_all[:, :, :bsz], (1, 0, 2, 3)).reshape(depth, N_DEV * bsz, 3 * d)
    dm_ctx = dm_all[0, :, bsz]
    for e in range(1, N_DEV):
        dm_ctx = dm_ctx + dm_all[e, :, bsz]
    dm_rows = jnp.concatenate([dm_lat, dm_ctx[:, None], jnp.zeros((depth, n_rows - N_DEV * bsz - 1, 3 * d), F32)], axis=1)
    dm_cols = lax.dynamic_slice_in_dim(dm_rows, me * ncol, ncol, axis=2)
    g_ada_w, dcraw = _adaln_bwd(craw, dm_cols, ada_w)

    small = {"c_ctx": dcraw[N_DEV * bsz], "ada_b": jnp.sum(dm_loc, axis=1), "norm_g": jnp.stack(d_norm_g),
             "final_g": d_final_g[0]}
    for n, v2 in gsmall.items():
        small[n] = jnp.stack(v2)
    for n, gval in zip(("ssm_a_re", "ssm_a_im", "ssm_log_dt", "ssm_b_re", "ssm_b_im", "ssm_c_re", "ssm_c_im"), dprm):
        small[n] = gval
    snames = list(small)
    sshapes = [args[n].shape for n in snames]
    spack = _pack([small[n].reshape(args[n].shape) for n in snames])

    last = pending + [("small", 0, spack)]
    red = _exchange(*rows_xchg(last), "reduce_last")
    keep(pending, red[:-1])
    s_sum = _sum_slots(red[-1][0], "sum_small")
    s_all = _exchange([s_sum], [("gather", [(0, None)], 0)], "gather_small")[0]
    s_all = s_all.reshape(1, 1, -1, PACK_W)

    out = {}

    def put(n, res, shape):
        for kind, buf in zip(("grad_", "delta_", "new_m_", "new_v_"), res):
            out[kind + n] = buf.reshape(shape)

    def as3(a):
        return a.reshape(a.shape[0], -1, a.shape[-1])

    for n in slots:
        w = args[n]
        g_list = [s_[:, :, :1] if n == "pool_scale" else s_ for s_ in slots[n]]
        g_list = [s_.reshape(1, N_DEV, -1, s_.shape[-1]) for s_ in g_list]
        tr = n in ("even_w_in", "odd_w_in")
        put(n, _adamw(as3(w), g_list, as3(args["m_" + n]), as3(args["v_" + n]), tr, 256 if tr else None, "adamw_" + n),
            w.shape)
    put("ada_w", _adamw(ada_w, [g_ada_w[:, None]], m_ada_w, v_ada_w, False, None, "adamw_ada_w"), ada_w.shape)
    res = _adamw(_pack([args[n] for n in snames])[None], [s_all], _pack([args["m_" + n] for n in snames])[None],
                 _pack([args["v_" + n] for n in snames])[None], False, PACK_ROWS, "adamw_small")
    for kind, buf in zip(("grad_", "delta_", "new_m_", "new_v_"), res):
        for n, a in zip(snames, _unpack(buf, sshapes)):
            out[kind + n] = a

    order = ["c_ctx", "ada_w", "ada_b", "norm_g", "even_w_in", "even_w_out", "attn_sink", "ssm_a_re", "ssm_a_im",
             "ssm_log_dt", "ssm_b_re", "ssm_b_im", "ssm_c_re", "ssm_c_im", "ssm_d", "glu_w", "glu_b", "odd_w_in",
             "odd_w_out", "pool_w", "pool_scale", "final_g"]
    return (loss, grad_x, *[out[k + n] for k in ("grad_", "delta_", "new_m_", "new_v_") for n in order])
```

```python
import functools

import numpy as np
import jax
import jax.numpy as jnp
from jax import lax
from jax.experimental import pallas as pl
from jax.experimental.pallas import tpu as pltpu

F32 = jnp.float32
BF16 = jnp.bfloat16
EPS = 1e-6
NEG_INF = -1e30
N_DEV = 8
GRID_W = 64
WINDOW = 128
QBLK = 128
ROWT = 256
CHUNK = 16
SSM_GROUPS = 32
SSM_GROUP = 16
SSM_STATE = 64
POOL_WINDOWS = (2, 4, 8, 16)
ROPE_BASE = 10000.0
ADAM_LR, ADAM_B1, ADAM_B2, ADAM_EPS, ADAM_WD, ADAM_STEP = 0.001, 0.9, 0.999, 1e-08, 0.01, 10
LANES = 128
PACK_W = 1024
PACK_ROWS = 64
VMEM_BIG = 56 << 20


def _cp(vmem=None):
    return pltpu.CompilerParams(vmem_limit_bytes=vmem) if vmem else pltpu.CompilerParams()


def _nt(a, b):
    return lax.dot_general(a, b, (((1,), (1,)), ((), ())), preferred_element_type=F32)


def _tn(a, b):
    return lax.dot_general(a, b, (((0,), (0,)), ((), ())), preferred_element_type=F32)


def _nn(a, b):
    return jnp.dot(a, b, preferred_element_type=F32)


def _bf(x):
    return x.astype(BF16)


def _my_index():
    return 4 * lax.axis_index("x") + 2 * lax.axis_index("y") + lax.axis_index("c")


def _xchg_shapes(inputs, specs):
    out_shapes = []
    for kind, parts, r in specs:
        ii, li = parts[0]
        shp = tuple(inputs[ii].shape if li is None else inputs[ii].shape[1:])
        piece = shp if kind == "gather" else ((r,) + shp[1:] if kind == "rows" else (shp[0], r) + shp[2:])
        out_shapes.append(jax.ShapeDtypeStruct((len(parts), N_DEV) + piece, inputs[ii].dtype))
    return out_shapes, sum(len(parts) for _, parts, _ in specs)


def _xchg_sems(n_parts):
    return [pltpu.SemaphoreType.DMA((n_parts * (N_DEV - 1),)), pltpu.SemaphoreType.DMA((n_parts * (N_DEV - 1),)),
            pltpu.SemaphoreType.DMA((n_parts,))]


def _xchg_copies(specs, in_refs, out_refs, send_sems, recv_sems, loc_sems):
    x, y, c = lax.axis_index("x"), lax.axis_index("y"), lax.axis_index("c")
    me = 4 * x + 2 * y + c

    def piece(ref, kind, r, p):
        if kind == "gather":
            return ref
        if kind == "rows":
            return ref.at[pl.ds(p * r, r)]
        return ref.at[:, pl.ds(p * r, r)]

    copies, q = [], 0
    for si, (kind, parts, r) in enumerate(specs):
        for pi, (ii, li) in enumerate(parts):
            src = in_refs[ii] if li is None else in_refs[ii].at[li]
            dst = out_refs[si].at[pi, me]
            copies.append(pltpu.make_async_copy(piece(src, kind, r, me), dst, loc_sems.at[q]))
            for k in range(1, N_DEV):
                px = 1 - x if k & 4 else x
                py = 1 - y if k & 2 else y
                pc = 1 - c if k & 1 else c
                copies.append(pltpu.make_async_remote_copy(
                    src_ref=piece(src, kind, r, 4 * px + 2 * py + pc), dst_ref=dst,
                    send_sem=send_sems.at[q * (N_DEV - 1) + k - 1], recv_sem=recv_sems.at[q * (N_DEV - 1) + k - 1],
                    device_id=(px, py, pc), device_id_type=pl.DeviceIdType.MESH))
            q += 1
    return copies


def _exchange(inputs, specs, name):
    out_shapes, n_parts = _xchg_shapes(inputs, specs)
    ni, ns = len(inputs), len(specs)

    def body(*refs):
        copies = _xchg_copies(specs, refs[:ni], refs[ni:ni + ns], *refs[ni + ns:])
        for cp in copies:
            cp.start()
        for cp in copies:
            cp.wait()

    return pl.pallas_call(
        body, name=name, out_shape=tuple(out_shapes),
        in_specs=[pl.BlockSpec(memory_space=pl.ANY)] * ni,
        out_specs=tuple(pl.BlockSpec(memory_space=pl.ANY) for _ in specs),
        scratch_shapes=_xchg_sems(n_parts),
        compiler_params=pltpu.CompilerParams(has_side_effects=True),
    )(*inputs)


def _call(body, *, name, grid, in_specs, out_specs, out_shape, operands, vmem=None, scratch=(), xchg=None):
    out_shape, out_specs = tuple(out_shape), list(out_specs)
    if xchg is None:
        res = pl.pallas_call(body, name=name, grid=grid, in_specs=in_specs, out_specs=out_specs, out_shape=out_shape,
                             scratch_shapes=list(scratch), compiler_params=_cp(vmem))(*operands)
        return list(res), None
    xin, xspecs = xchg
    xshapes, n_parts = _xchg_shapes(xin, xspecs)
    ni, no, nxi, nxo, nsc = len(operands), len(out_shape), len(xin), len(xspecs), len(scratch)
    anyspec = pl.BlockSpec(memory_space=pl.ANY)

    def hosted(*refs):
        ins, xins = refs[:ni], refs[ni:ni + nxi]
        outs, xouts = refs[ni + nxi:ni + nxi + no], refs[ni + nxi + no:ni + nxi + no + nxo]
        scr, sems = refs[ni + nxi + no + nxo:ni + nxi + no + nxo + nsc], refs[ni + nxi + no + nxo + nsc:]
        copies = _xchg_copies(xspecs, xins, xouts, *sems)
        first, last = True, True
        for ax, n in enumerate(grid):
            first = first & (pl.program_id(ax) == 0)
            last = last & (pl.program_id(ax) == n - 1)

        @pl.when(first)
        def _():
            for cp in copies:
                cp.start()
        body(*ins, *outs, *scr)

        @pl.when(last)
        def _():
            for cp in copies:
                cp.wait()

    res = pl.pallas_call(
        hosted, name=name, grid=grid, in_specs=list(in_specs) + [anyspec] * nxi,
        out_specs=out_specs + [anyspec] * nxo, out_shape=out_shape + tuple(xshapes),
        scratch_shapes=list(scratch) + _xchg_sems(n_parts), compiler_params=_cp(vmem))(*operands, *xin)
    return list(res[:no]), list(res[no:])


def _adaln_fwd(craw, ada_w, ada_b):
    nl, d, n = ada_w.shape
    r = craw.shape[0]

    def body(c_ref, w_ref, b_ref, o_ref):
        s = jax.nn.silu(c_ref[...])
        o_ref[...] = _nn(_bf(s), _bf(w_ref[...])) + b_ref[...]

    return pl.pallas_call(
        body, name="adaln_fwd", grid=(nl,),
        in_specs=[pl.BlockSpec((r, d), lambda i: (0, 0)), pl.BlockSpec((None, d, n), lambda i: (i, 0, 0)),
                  pl.BlockSpec((None, 1, n), lambda i: (i, 0, 0))],
        out_specs=pl.BlockSpec((None, r, n), lambda i: (i, 0, 0)),
        out_shape=jax.ShapeDtypeStruct((nl, r, n), F32),
    )(craw, ada_w, ada_b)


def _adaln_bwd(craw, dm, ada_w):
    nl, d, n = ada_w.shape
    r = craw.shape[0]

    def body(c_ref, dm_ref, w_ref, gw_ref, dc_ref, acc_ref):
        i = pl.program_id(0)
        s, vj = jax.vjp(jax.nn.silu, c_ref[...])
        dmb = _bf(dm_ref[...])
        gw_ref[...] = _tn(_bf(s), dmb)

        @pl.when(i == 0)
        def _():
            acc_ref[...] = jnp.zeros_like(acc_ref)
        acc_ref[...] += _nt(dmb, _bf(w_ref[...]))

        @pl.when(i == nl - 1)
        def _():
            dc_ref[...] = vj(acc_ref[...])[0]

    return pl.pallas_call(
        body, name="adaln_bwd", grid=(nl,),
        in_specs=[pl.BlockSpec((r, d), lambda i: (0, 0)), pl.BlockSpec((None, r, n), lambda i: (i, 0, 0)),
                  pl.BlockSpec((None, d, n), lambda i: (i, 0, 0))],
        out_specs=[pl.BlockSpec((None, d, n), lambda i: (i, 0, 0)), pl.BlockSpec((r, d), lambda i: (0, 0))],
        out_shape=(jax.ShapeDtypeStruct((nl, d, n), F32), jax.ShapeDtypeStruct((r, d), F32)),
        scratch_shapes=[pltpu.VMEM((r, d), F32)],
    )(craw, dm, ada_w)


def _norm_mod(h, mod, g):
    ms = jnp.mean(h * h, axis=-1, keepdims=True)
    y = h * lax.rsqrt(ms + EPS) * g
    return y * (1.0 + mod[1:2]) + mod[0:1]


def _mod_spec(i, nct, d):
    return pl.BlockSpec((None, None, 3, d), lambda b, j: (i, 2 * b + jnp.where(j >= nct, 1, 0), 0, 0))


def _dmod_spec(nct, d):
    return pl.BlockSpec((None, 3, d), lambda b, j: (2 * b + jnp.where(j >= nct, 1, 0), 0, 0))


def _layer_spec(li, *shape):
    return pl.BlockSpec((None,) + tuple(shape), lambda b, j: (li,) + (0,) * len(shape))


def _tile(width, col_blk=0):
    return pl.BlockSpec((None, ROWT, width), lambda b, j: (b, j, col_blk))


def _fused_in_fwd(hs, mods, g, wt, i, jl, nct, name):
    bsz, t, d = hs.shape
    n = wt.shape[1]

    def body(h_ref, mod_ref, g_ref, w_ref, z_ref):
        a = _norm_mod(h_ref[...], mod_ref[...], g_ref[...])
        z_ref[...] = _nt(_bf(a), w_ref[...])

    return pl.pallas_call(
        body, name=name, grid=(bsz, t // ROWT),
        in_specs=[_tile(d), _mod_spec(i, nct, d), _layer_spec(i, 1, d), _layer_spec(jl, n, d)],
        out_specs=_tile(n), out_shape=jax.ShapeDtypeStruct((bsz, t, n), F32),
        compiler_params=_cp(VMEM_BIG),
    )(hs, mods, g, wt)


def _fused_in_bwd(hs, mods, g, wt, dh_out, pieces, small, assemble, i, jl, nct, name, xchg=None):
    bsz, t, d = hs.shape
    n = wt.shape[1]
    npc, nsm = len(pieces), int(small is not None)

    def body(*refs):
        h_ref, mod_ref, g_ref, w_ref, dho_ref = refs[:5]
        p_refs = refs[5:5 + npc]
        s_val = refs[5 + npc][...] if nsm else None
        dhi_ref, dmod_ref, dg_ref, dw_ref = refs[5 + npc + nsm:]
        b, j = pl.program_id(0), pl.program_id(1)
        a, vj = jax.vjp(_norm_mod, h_ref[...], mod_ref[...], g_ref[...])
        dz = _bf(assemble([r[...] for r in p_refs], s_val))
        dh, dmod, dg = vj(_nn(dz, w_ref[...]))
        dhi_ref[...] = dho_ref[...] + dh

        @pl.when((j == 0) | (j == nct))
        def _():
            dmod_ref[...] = jnp.zeros_like(dmod_ref)

        @pl.when((b == 0) & (j == 0))
        def _():
            dg_ref[...] = jnp.zeros_like(dg_ref)
            dw_ref[...] = jnp.zeros_like(dw_ref)
        dmod_ref[...] += dmod
        dg_ref[...] += dg
        dw_ref[...] += _tn(dz, _bf(a))

    full = lambda r, c: pl.BlockSpec((r, c), lambda b, j: (0, 0))
    return _call(
        body, name=name, grid=(bsz, t // ROWT),
        in_specs=[_tile(d), _mod_spec(i, nct, d), _layer_spec(i, 1, d), _layer_spec(jl, n, d), _tile(d)]
                 + [_tile(wd, cb) for _, wd, cb in pieces]
                 + ([_layer_spec(small[1], 1, small[0].shape[2])] if nsm else []),
        out_specs=[_tile(d), _dmod_spec(nct, d), full(1, d), full(n, d)],
        out_shape=(jax.ShapeDtypeStruct((bsz, t, d), F32), jax.ShapeDtypeStruct((2 * bsz, 3, d), F32),
                   jax.ShapeDtypeStruct((1, d), F32), jax.ShapeDtypeStruct((n, d), F32)),
        vmem=VMEM_BIG, xchg=xchg,
        operands=[hs, mods, g, wt, dh_out, *[p for p, _, _ in pieces], *([small[0]] if nsm else [])])


def _rope_tables(lc, l):
    tpos = np.arange(l)
    row = (tpos // GRID_W).astype(np.float32)
    col = (tpos % GRID_W).astype(np.float32)
    nf = 16
    inv = (np.float32(ROPE_BASE) ** (-np.arange(nf, dtype=np.float32) / np.float32(nf))).astype(np.float32)
    ang_r = row[:, None] * inv[None]
    ang_c = col[:, None] * inv[None]
    cos64 = np.concatenate([np.cos(ang_r), np.cos(ang_r), np.cos(ang_c), np.cos(ang_c)], axis=1)
    sin64 = np.concatenate([-np.sin(ang_r), np.sin(ang_r), -np.sin(ang_c), np.sin(ang_c)], axis=1)
    cos = np.concatenate([np.ones((lc, 128), np.float32), np.tile(cos64, (1, 2)).astype(np.float32)], axis=0)
    sin = np.concatenate([np.zeros((lc, 128), np.float32), np.tile(sin64, (1, 2)).astype(np.float32)], axis=0)
    return jnp.asarray(cos), jnp.asarray(sin)


def _rot(x, cos, sin):
    lane = lax.broadcasted_iota(jnp.int32, x.shape, 1)
    first = jnp.bitwise_and(jnp.right_shift(lane, 4), 1) == 0
    partner = jnp.where(first, pltpu.roll(x, LANES - 16, 1), pltpu.roll(x, 16, 1))
    return x * cos + partner * sin


def _split_heads(x):
    low = lax.broadcasted_iota(jnp.int32, x.shape, 1) < 64
    return jnp.where(low, x, 0.0), jnp.where(low, pltpu.roll(x, 64, 1), 0.0)


def _merge_heads(a, b):
    return a + pltpu.roll(b, 64, 1)


def _rope_fwd(z, cos, sin, kv_blk, name):
    bsz, t, _ = z.shape

    def body(q_ref, kv_ref, cos_ref, sin_ref, qp_ref, kvp_ref):
        c, s = cos_ref[...], sin_ref[...]
        outs = []
        for sl in range(4):
            xr = _rot(q_ref[:, sl * LANES:(sl + 1) * LANES], c, s) * 0.125
            outs += list(_split_heads(xr))
        qp_ref[...] = _bf(jnp.concatenate(outs, axis=1))
        k0, k1 = _split_heads(_rot(kv_ref[:, 0:LANES], c, s))
        v0, v1 = _split_heads(kv_ref[:, LANES:2 * LANES])
        kvp_ref[...] = _bf(jnp.concatenate([k0, k1, v0, v1], axis=1))

    return pl.pallas_call(
        body, name=name, grid=(bsz, t // ROWT),
        in_specs=[_tile(512), _tile(256, kv_blk),
                  pl.BlockSpec((ROWT, LANES), lambda b, j: (j, 0)), pl.BlockSpec((ROWT, LANES), lambda b, j: (j, 0))],
        out_specs=[_tile(1024), _tile(512)],
        out_shape=(jax.ShapeDtypeStruct((bsz, t, 1024), BF16), jax.ShapeDtypeStruct((bsz, t, 512), BF16)),
    )(z, z, cos, sin)


def _rope_bwd(dq_r, dkv_acc, cos, sin, t, name):
    bsz = dq_r.shape[0]

    def body(dq_ref, acc_ref, cos_ref, sin_ref, dqo_ref, dkvo_ref):
        c, s = cos_ref[...], -sin_ref[...]
        dqo_ref[...] = jnp.concatenate(
            [_rot(dq_ref[:, sl * LANES:(sl + 1) * LANES], c, s) for sl in range(4)], axis=1)
        dk = _merge_heads(acc_ref[:, 0:LANES], acc_ref[:, LANES:2 * LANES])
        dv = _merge_heads(acc_ref[:, 2 * LANES:3 * LANES], acc_ref[:, 3 * LANES:4 * LANES])
        dkvo_ref[...] = jnp.concatenate([_rot(dk, c, s), dv], axis=1)

    return pl.pallas_call(
        body, name=name, grid=(bsz, t // ROWT),
        in_specs=[_tile(512), _tile(512),
                  pl.BlockSpec((ROWT, LANES), lambda b, j: (j, 0)), pl.BlockSpec((ROWT, LANES), lambda b, j: (j, 0))],
        out_specs=[_tile(512), _tile(256)],
        out_shape=(jax.ShapeDtypeStruct((bsz, t, 512), F32), jax.ShapeDtypeStruct((bsz, t, 256), F32)),
    )(dq_r, dkv_acc, cos, sin)


def _attn_bias(j, ncb, l):
    n = j - ncb
    r = lax.broadcasted_iota(jnp.int32, (QBLK, 3 * QBLK), 0)
    cidx = lax.broadcasted_iota(jnp.int32, (QBLK, 3 * QBLK), 1)
    kpos = (n - 1) * QBLK + cidx
    valid = (jnp.abs(r - cidx + QBLK) <= WINDOW) & (kpos >= 0) & (kpos < l) & (j >= ncb)
    return jnp.where(valid, 0.0, NEG_INF).astype(F32)


def _kv_specs(nb, lc):
    return [pl.BlockSpec((None, QBLK, 512), lambda b, j: (b, jnp.maximum(j - 1, 0), 0)),
            pl.BlockSpec((None, QBLK, 512), lambda b, j: (b, j, 0)),
            pl.BlockSpec((None, QBLK, 512), lambda b, j: (b, jnp.minimum(j + 1, nb - 1), 0)),
            pl.BlockSpec((None, lc, 512), lambda b, j: (b, 0, 0))]


def _lane_pick(x, h):
    lane = lax.broadcasted_iota(jnp.int32, x.shape, 1)
    return jnp.sum(jnp.where(lane == h, x, 0.0), axis=1, keepdims=True)


def _attn_fwd(qp, kvp, sinks, jl, lc, name, xchg=None):
    bsz, t, _ = qp.shape
    nb, ncb, l = t // QBLK, lc // QBLK, t - lc

    def body(sink_ref, q_ref, kp_ref, kc_ref, kn_ref, kx_ref, o_ref, lse_ref):
        j = pl.program_id(1)
        kall = jnp.concatenate([kp_ref[...], kc_ref[...], kn_ref[...], kx_ref[...]], axis=0)
        bias = jnp.concatenate([_attn_bias(j, ncb, l), jnp.zeros((QBLK, lc), F32)], axis=1)
        lane = lax.broadcasted_iota(jnp.int32, (QBLK, LANES), 1)
        lse_all = jnp.zeros((QBLK, LANES), F32)
        outs = []
        for h in range(8):
            hk = h // 4
            kh = kall[:, hk * LANES:(hk + 1) * LANES]
            vh = kall[:, (2 + hk) * LANES:(3 + hk) * LANES]
            s = _nt(q_ref[:, h * LANES:(h + 1) * LANES], kh) + bias
            sk = sink_ref[jl, h]
            m = jnp.maximum(jnp.max(s, axis=1, keepdims=True), sk)
            p = jnp.exp(s - m)
            den = jnp.sum(p, axis=1, keepdims=True) + jnp.exp(sk - m)
            outs.append(_nn(_bf(p), vh) / den)
            lse_all = lse_all + jnp.where(lane == h, m + jnp.log(den), 0.0)
        o_ref[...] = jnp.concatenate([_merge_heads(outs[2 * i], outs[2 * i + 1]) for i in range(4)], axis=1)
        lse_ref[...] = lse_all

    return _call(
        body, name=name, grid=(bsz, nb),
        in_specs=[pl.BlockSpec(memory_space=pltpu.SMEM),
                  pl.BlockSpec((None, QBLK, 1024), lambda b, j: (b, j, 0))] + _kv_specs(nb, lc),
        out_specs=[pl.BlockSpec((None, QBLK, 512), lambda b, j: (b, j, 0)),
                   pl.BlockSpec((None, QBLK, LANES), lambda b, j: (b, j, 0))],
        out_shape=(jax.ShapeDtypeStruct((bsz, t, 512), F32), jax.ShapeDtypeStruct((bsz, t, LANES), F32)),
        xchg=xchg, operands=[sinks, qp, kvp, kvp, kvp, kvp])


def _attn_bwd(qp, kvp, sinks, jl, o, lse, do, lc, name, xchg=None):
    bsz, t, _ = qp.shape
    nb, ncb, l = t // QBLK, lc // QBLK, t - lc
    nk = 3 * QBLK

    def body(sink_ref, q_ref, kp_ref, kc_ref, kn_ref, kx_ref, o_ref, lse_ref, do_ref, dq_ref, acc_ref, ds_ref):
        j = pl.program_id(1)

        @pl.when(j == 0)
        def _():
            acc_ref[...] = jnp.zeros_like(acc_ref)
            ds_ref[...] = jnp.zeros_like(ds_ref)
        kall = jnp.concatenate([kp_ref[...], kc_ref[...], kn_ref[...], kx_ref[...]], axis=0)
        bias = jnp.concatenate([_attn_bias(j, ncb, l), jnp.zeros((QBLK, lc), F32)], axis=1)
        lse = lse_ref[...]
        row8 = lax.broadcasted_iota(jnp.int32, (8, LANES), 0)
        dsink = jnp.zeros((8, LANES), F32)
        dqs = []
        dkv = [jnp.zeros((nk + lc, LANES), F32) for _ in range(4)]
        for s2 in range(4):
            do_pair = _split_heads(do_ref[:, s2 * LANES:(s2 + 1) * LANES])
            o_pair = _split_heads(o_ref[:, s2 * LANES:(s2 + 1) * LANES])
            for e in range(2):
                h = 2 * s2 + e
                hk = h // 4
                kh = kall[:, hk * LANES:(hk + 1) * LANES]
                vh = kall[:, (2 + hk) * LANES:(3 + hk) * LANES]
                qh = q_ref[:, h * LANES:(h + 1) * LANES]
                doh = do_pair[e]
                delta = jnp.sum(doh * o_pair[e], axis=1, keepdims=True)
                lse_h = _lane_pick(lse, h)
                p = jnp.exp(_nt(qh, kh) + bias - lse_h)
                dob = _bf(doh)
                dp = _nt(dob, vh)
                dsc = _bf(p * (dp - delta))
                dqs.append(_nn(dsc, kh) * 0.125)
                dkv[hk] = dkv[hk] + _tn(dsc, qh)
                dkv[2 + hk] = dkv[2 + hk] + _tn(_bf(p), dob)
                ps = jnp.exp(sink_ref[jl, h] - lse_h)
                dsink = dsink + jnp.where(row8 == h, -jnp.sum(ps * delta), 0.0)
        dq_ref[...] = jnp.concatenate([_merge_heads(dqs[2 * i], dqs[2 * i + 1]) for i in range(4)], axis=1)
        ds_ref[...] += dsink
        dall = jnp.concatenate(dkv, axis=1)
        acc_ref[pl.ds(0, lc), :] += dall[nk:, :]

        @pl.when(j >= ncb)
        def _():
            st = pl.multiple_of((j - 1) * QBLK, QBLK)
            acc_ref[pl.ds(st, nk), :] += dall[:nk, :]

    blk = lambda wd: pl.BlockSpec((None, QBLK, wd), lambda b, j: (b, j, 0))
    return _call(
        body, name=name, grid=(bsz, nb),
        in_specs=[pl.BlockSpec(memory_space=pltpu.SMEM), blk(1024)] + _kv_specs(nb, lc)
                 + [blk(512), blk(LANES), blk(512)],
        out_specs=[blk(512), pl.BlockSpec((None, t + QBLK, 512), lambda b, j: (b, 0, 0)),
                   pl.BlockSpec((None, 8, LANES), lambda b, j: (b, 0, 0))],
        out_shape=(jax.ShapeDtypeStruct((bsz, t, 512), F32), jax.ShapeDtypeStruct((bsz, t + QBLK, 512), F32),
                   jax.ShapeDtypeStruct((bsz, 8, LANES), F32)),
        vmem=VMEM_BIG, xchg=xchg, operands=[sinks, qp, kvp, kvp, kvp, kvp, o, lse, do])


def _s5_operators(a_re, a_im, log_dt, b_re, b_im, c_re, c_im):
    hi = lax.Precision.HIGHEST
    dt = jnp.exp(log_dt)[..., None]
    tau = jnp.arange(CHUNK + 1, dtype=F32)[:, None, None, None]
    mag = jnp.exp(tau * (a_re * dt))
    pwr, pwi = mag * jnp.cos(tau * (a_im * dt)), mag * jnp.sin(tau * (a_im * dt))
    ar, ai = pwr[1], pwi[1]
    den = a_re * a_re + a_im * a_im
    fr = ((ar - 1.0) * a_re + ai * a_im) / den
    fi = (ai * a_re - (ar - 1.0) * a_im) / den
    bbr = fr[..., None] * b_re - fi[..., None] * b_im
    bbi = fr[..., None] * b_im + fi[..., None] * b_re
    wr = pwr[:CHUNK, ..., None] * bbr - pwi[:CHUNK, ..., None] * bbi
    wi = pwr[:CHUNK, ..., None] * bbi + pwi[:CHUNK, ..., None] * bbr
    kern = (jnp.einsum("dgcp,tdgpk->dgtck", c_re, wr, precision=hi)
            - jnp.einsum("dgcp,tdgpk->dgtck", c_im, wi, precision=hi))
    g = a_re.shape[1]
    nrow = CHUNK * SSM_GROUP
    lag = np.arange(CHUNK)[:, None, None]
    tt, ss = np.arange(CHUNK)[None, :, None], np.arange(CHUNK)[None, None, :]
    sel_f = jnp.asarray((tt - ss == lag).astype(np.float32))
    sel_b = jnp.asarray((ss - tt == lag).astype(np.float32))
    mt = (jnp.einsum("lts,glck->gsktc", sel_f, kern[0], precision=hi)
          + jnp.einsum("lts,glck->gsktc", sel_b, kern[1], precision=hi)).reshape(g, nrow, nrow)
    to_rows = lambda w: jnp.transpose(w, (1, 0, 3, 2)).reshape(g, nrow, SSM_STATE)
    bt = jnp.concatenate([to_rows(wr[::-1, 0]), to_rows(wi[::-1, 0]), to_rows(wr[:, 1]), to_rows(wi[:, 1])], axis=-1)

    def out_map(d, pw_r, pw_i):
        w2r = c_re[d][None] * pw_r[:, :, None, :] - c_im[d][None] * pw_i[:, :, None, :]
        w2i = c_re[d][None] * pw_i[:, :, None, :] + c_im[d][None] * pw_r[:, :, None, :]
        cols = lambda w: jnp.transpose(w, (1, 3, 0, 2)).reshape(g, SSM_STATE, nrow)
        return [cols(w2r), -cols(w2i)]

    ct = jnp.concatenate(out_map(0, pwr[1:, 0], pwi[1:, 0]) + out_map(1, pwr[1:, 1][::-1], pwi[1:, 1][::-1]), axis=-2)
    return mt, bt, ct, pwr[CHUNK], pwi[CHUNK]


def _scan_powers(a16r, a16i, n_chunks):
    prs, pis = [], []
    r, i = a16r, a16i
    s = 1
    while s < n_chunks:
        prs.append(jnp.concatenate([r, r], axis=-1))
        pis.append(jnp.concatenate([-i, i], axis=-1))
        r, i = r * r - i * i, 2.0 * r * i
        s *= 2
    pad = 16 - len(prs)
    z = jnp.zeros_like(prs[0])
    return jnp.stack(prs + [z] * pad, axis=2), jnp.stack(pis + [z] * pad, axis=2)


def _low_half():
    return lax.broadcasted_iota(jnp.int32, (1, LANES), 1) < 64


def _to_pair(sa, sb):
    low = _low_half()
    return jnp.where(low, sa, pltpu.roll(sb, 64, 1)), jnp.where(low, pltpu.roll(sa, 64, 1), sb)


def _from_pair(hr, hi):
    low = _low_half()
    return jnp.where(low, hr, pltpu.roll(hi, 64, 1)), jnp.where(low, pltpu.roll(hr, 64, 1), hi)


def _pair_scan(sr, si, pr, pi, reverse, conj):
    n = sr.shape[0]
    row = lax.broadcasted_iota(jnp.int32, sr.shape, 0)

    def shift(x, k):
        if reverse:
            return jnp.where(row < n - k, pltpu.roll(x, n - k, 0), 0.0)
        return jnp.where(row >= k, pltpu.roll(x, k, 0), 0.0)

    hr, hi = shift(sr, 1), shift(si, 1)
    k, j = 1, 0
    while k < n:
        tr, ti = shift(hr, k), shift(hi, k)
        ar = pr[j:j + 1, :]
        ai = -pi[j:j + 1, :] if conj else pi[j:j + 1, :]
        hr, hi = hr + ar * tr - ai * ti, hi + ar * ti + ai * tr
        k, j = 2 * k, j + 1
    return hr, hi


def _transpose8(a):
    a = list(a)
    blk = jnp.right_shift(lax.broadcasted_iota(jnp.int32, (1, LANES), 1), 4)
    for dd in (4, 2, 1):
        upper = jnp.bitwise_and(blk, dd) != 0
        for i in range(8):
            if i & dd:
                continue
            lo, hi = a[i], a[i + dd]
            a[i] = jnp.where(upper, pltpu.roll(hi, dd * SSM_GROUP, 1), lo)
            a[i + dd] = jnp.where(upper, hi, pltpu.roll(lo, LANES - dd * SSM_GROUP, 1))
    return a


def _blocks_from_rows(xs):
    t0, t1 = _transpose8(xs[:8]), _transpose8(xs[8:])
    return [jnp.concatenate([t0[g], t1[g]], axis=1) for g in range(8)]


def _rows_from_blocks(ys):
    return _transpose8([y[:, :LANES] for y in ys]) + _transpose8([y[:, LANES:] for y in ys])


def _pair_states(sa, sb, pr_ref, pi_ref, ga, gb, ncc, adjoint):
    n = sa.shape[0]
    low = _low_half()
    out_a, out_b, pairs = [], [], []
    for dirn in range(2):
        xr, xi = _to_pair(sa[:, dirn * LANES:(dirn + 1) * LANES], sb[:, dirn * LANES:(dirn + 1) * LANES])
        pr = jnp.where(low, pr_ref[dirn, ga], pr_ref[dirn, gb])
        pi = jnp.where(low, -pi_ref[dirn, ga], pi_ref[dirn, gb])
        if dirn == 0:
            hr, hi = _pair_scan(xr, xi, pr, pi, adjoint, adjoint)
        else:
            hr, hi = _pair_scan(pltpu.roll(xr, n - ncc, 0), pltpu.roll(xi, n - ncc, 0), pr, pi, not adjoint, adjoint)
            hr, hi = pltpu.roll(hr, ncc, 0), pltpu.roll(hi, ncc, 0)
        pairs.append((hr, hi))
        ha, hb = _from_pair(hr, hi)
        out_a.append(ha)
        out_b.append(hb)
    return (jnp.concatenate(out_a, axis=1), jnp.concatenate(out_b, axis=1)), pairs


def _s5_op_specs(idx, jl):
    op = pl.BlockSpec((None, 8, 256, 256), lambda a, b: (jl, idx(a, b), 0, 0))
    pw = pl.BlockSpec((None, 2, 8, 16, LANES), lambda a, b: (jl, 0, idx(a, b), 0, 0))
    return [op, op, op, pw, pw]


def _s5_fwd(z, col0, s5_ops, jl, lc, name, xchg=None):
    bsz, t, _ = z.shape
    nc, ncc = t // CHUNK, lc // CHUNK

    def body(u_ref, mt_ref, bt_ref, ct_ref, pr_ref, pi_ref, y_ref):
        us = [_bf(u) for u in _blocks_from_rows([u_ref[pl.ds(tok, nc, stride=CHUNK), :] for tok in range(CHUNK)])]
        ys = []
        for ga in range(0, 8, 2):
            gb = ga + 1
            hs, _ = _pair_states(_nn(us[ga], bt_ref[ga]), _nn(us[gb], bt_ref[gb]), pr_ref, pi_ref, ga, gb, ncc, False)
            for g, h in zip((ga, gb), hs):
                ys.append(_nn(us[g], mt_ref[g]) + _nn(_bf(h), ct_ref[g]))
        for tok, rows in enumerate(_rows_from_blocks(ys)):
            y_ref[pl.ds(tok, nc, stride=CHUNK), :] = rows

    return _call(
        body, name=name, grid=(bsz, 4),
        in_specs=[pl.BlockSpec((None, t, LANES), lambda b, s: (b, 0, col0 + s))] + _s5_op_specs(lambda b, s: s, jl),
        out_specs=[pl.BlockSpec((None, t, LANES), lambda b, s: (b, 0, s))],
        out_shape=[jax.ShapeDtypeStruct((bsz, t, 512), F32)], vmem=VMEM_BIG, xchg=xchg,
        operands=[z, *s5_ops])


def _s5_bwd(z, col0, dy, s5_ops, jl, lc, name, xchg=None):
    bsz, t, _ = z.shape
    nc, ncc = t // CHUNK, lc // CHUNK

    def body(u_ref, dy_ref, mt_ref, bt_ref, ct_ref, pr_ref, pi_ref, du_ref, dmt_ref, dbt_ref, dct_ref, da_ref):
        b = pl.program_id(1)

        @pl.when(b == 0)
        def _():
            dmt_ref[...] = jnp.zeros_like(dmt_ref)
            dbt_ref[...] = jnp.zeros_like(dbt_ref)
            dct_ref[...] = jnp.zeros_like(dct_ref)
            da_ref[...] = jnp.zeros_like(da_ref)
        us = [_bf(u) for u in _blocks_from_rows([u_ref[pl.ds(tok, nc, stride=CHUNK), :] for tok in range(CHUNK)])]
        dys = [_bf(u) for u in _blocks_from_rows([dy_ref[pl.ds(tok, nc, stride=CHUNK), :] for tok in range(CHUNK)])]
        row8 = lax.broadcasted_iota(jnp.int32, (8, LANES), 0)
        dus = []
        for ga in range(0, 8, 2):
            gb = ga + 1
            hs, hp = _pair_states(_nn(us[ga], bt_ref[ga]), _nn(us[gb], bt_ref[gb]), pr_ref, pi_ref, ga, gb, ncc, False)
            gs, gp = _pair_states(_nt(dys[ga], ct_ref[ga]), _nt(dys[gb], ct_ref[gb]), pr_ref, pi_ref, ga, gb, ncc, True)
            for g, h, gg in zip((ga, gb), hs, gs):
                hcat, gcat = _bf(h), _bf(gg)
                dus.append(_nt(dys[g], mt_ref[g]) + _nt(gcat, bt_ref[g]))
                dmt_ref[g] += _tn(us[g], dys[g])
                dct_ref[g] += _tn(hcat, dys[g])
                dbt_ref[g] += _tn(us[g], gcat)
            da = jnp.zeros((8, LANES), F32)
            for dirn in range(2):
                (hr, hi), (gr, gi) = hp[dirn], gp[dirn]
                da = da + jnp.where(row8 == 2 * dirn, jnp.sum(gr * hr + gi * hi, axis=0, keepdims=True), 0.0)
                da = da + jnp.where(row8 == 2 * dirn + 1, jnp.sum(gi * hr - gr * hi, axis=0, keepdims=True), 0.0)
            da_ref[ga // 2] += da
        for tok, rows in enumerate(_rows_from_blocks(dus)):
            du_ref[pl.ds(tok, nc, stride=CHUNK), :] = rows

    op_out = pl.BlockSpec((8, 256, 256), lambda s, b: (s, 0, 0))
    op_shape = jax.ShapeDtypeStruct((SSM_GROUPS, 256, 256), F32)
    return _call(
        body, name=name, grid=(4, bsz),
        in_specs=[pl.BlockSpec((None, t, LANES), lambda s, b: (b, 0, col0 + s)),
                  pl.BlockSpec((None, t, LANES), lambda s, b: (b, 0, s))] + _s5_op_specs(lambda s, b: s, jl),
        out_specs=[pl.BlockSpec((None, t, LANES), lambda s, b: (b, 0, s)), op_out, op_out, op_out,
                   pl.BlockSpec((4, 8, LANES), lambda s, b: (s, 0, 0))],
        out_shape=(jax.ShapeDtypeStruct((bsz, t, 512), F32), op_shape, op_shape, op_shape,
                   jax.ShapeDtypeStruct((SSM_GROUPS // 2, 8, LANES), F32)),
        vmem=VMEM_BIG, xchg=xchg, operands=[z, dy, *s5_ops])


def _glu_in(y, u, dsk):
    return jax.nn.gelu(y + u * dsk)


def _even_mix(oa, ga, zg, tg, gs):
    return jnp.concatenate([oa * jax.nn.silu(ga), zg * jax.nn.sigmoid(tg) * jax.nn.silu(gs)], axis=1)


def _even_specs(d, i, jl, nct):
    return [_tile(d), _mod_spec(i, nct, d), _tile(512), _tile(512)] + [_tile(256, cb) for cb in range(3, 9)] + [
        _layer_spec(jl, 1, 512), _layer_spec(0, 512, 512), _layer_spec(jl, 1, 512), _layer_spec(0, 1024, d)]


def _cat2(a_ref, b_ref):
    return jnp.concatenate([a_ref[...], b_ref[...]], axis=1)


def _even_out_fwd(hs, mods, oa, y, z, dsk, gw, gb, wout, i, jl, nct, name):
    bsz, t, d = hs.shape

    def body(h_ref, mod_ref, oa_ref, y_ref, ga0, ga1, u0, u1, gs0, gs1, dsk_ref, gw_ref, gb_ref, wo_ref, o_ref):
        zg = _glu_in(y_ref[...], _cat2(u0, u1), dsk_ref[...])
        tg = _nn(_bf(zg), gw_ref[...]) + gb_ref[...]
        mix = _even_mix(oa_ref[...], _cat2(ga0, ga1), zg, tg, _cat2(gs0, gs1))
        o_ref[...] = h_ref[...] + mod_ref[2:3, :] * _nn(_bf(mix), wo_ref[...])

    return pl.pallas_call(
        body, name=name, grid=(bsz, t // ROWT), in_specs=_even_specs(d, i, jl, nct),
        out_specs=_tile(d), out_shape=jax.ShapeDtypeStruct((bsz, t, d), F32), compiler_params=_cp(VMEM_BIG),
    )(hs, mods, oa, y, z, z, z, z, z, z, dsk, gw, gb, wout)


def _even_out_bwd(dh, mods, oa, y, z, dsk, gw, gb, wout, i, jl, nct, name):
    bsz, t, d = dh.shape

    def body(dh_ref, mod_ref, oa_ref, y_ref, ga0, ga1, u0, u1, gs0, gs1, dsk_ref, gw_ref, gb_ref, wo_ref,
             doa_ref, dga_ref, dy_ref, dgs_ref, dmod_ref, ddsk_ref, dgw_ref, dgb_ref, dwo_ref):
        b, j = pl.program_id(0), pl.program_id(1)
        zg, vj1 = jax.vjp(_glu_in, y_ref[...], _cat2(u0, u1), dsk_ref[...])
        zgb = _bf(zg)
        tg = _nn(zgb, gw_ref[...]) + gb_ref[...]
        mix, vj2 = jax.vjp(_even_mix, oa_ref[...], _cat2(ga0, ga1), zg, tg, _cat2(gs0, gs1))
        mixb = _bf(mix)
        dhv = dh_ref[...]
        dyo = _bf(dhv * mod_ref[2:3, :])
        yout = _nn(mixb, wo_ref[...])
        doa, dga, dzg, dtg, dgs = vj2(_nt(dyo, wo_ref[...]))
        dtb = _bf(dtg)
        dzg = dzg + _nt(dtb, gw_ref[...])
        dy, _, ddsk = vj1(dzg)
        doa_ref[...], dga_ref[...], dy_ref[...], dgs_ref[...] = doa, dga, dy, dgs

        @pl.when((j == 0) | (j == nct))
        def _():
            dmod_ref[...] = jnp.zeros_like(dmod_ref)

        @pl.when((b == 0) & (j == 0))
        def _():
            ddsk_ref[...] = jnp.zeros_like(ddsk_ref)
            dgw_ref[...] = jnp.zeros_like(dgw_ref)
            dgb_ref[...] = jnp.zeros_like(dgb_ref)
            dwo_ref[...] = jnp.zeros_like(dwo_ref)
        dmod_ref[2:3, :] += jnp.sum(dhv * yout, axis=0, keepdims=True)
        ddsk_ref[...] += ddsk
        dgw_ref[...] += _tn(zgb, dtb)
        dgb_ref[...] += jnp.sum(dtg, axis=0, keepdims=True)
        dwo_ref[...] += _tn(mixb, dyo)

    full = lambda r, c: pl.BlockSpec((r, c), lambda b, j: (0, 0))
    f512 = jax.ShapeDtypeStruct((bsz, t, 512), F32)
    return pl.pallas_call(
        body, name=name, grid=(bsz, t // ROWT), in_specs=_even_specs(d, i, jl, nct),
        out_specs=[_tile(512), _tile(512), _tile(512), _tile(512), _dmod_spec(nct, d),
                   full(1, 512), full(512, 512), full(1, 512), full(1024, d)],
        out_shape=(f512, f512, f512, f512, jax.ShapeDtypeStruct((2 * bsz, 3, d), F32),
                   jax.ShapeDtypeStruct((1, 512), F32), jax.ShapeDtypeStruct((512, 512), F32),
                   jax.ShapeDtypeStruct((1, 512), F32), jax.ShapeDtypeStruct((1024, d), F32)),
        compiler_params=_cp(VMEM_BIG),
    )(dh, mods, oa, y, z, z, z, z, z, z, dsk, gw, gb, wout)


def _pool(src, col_blk, r, lc, transpose, name):
    bsz, t, _ = src.shape
    segs = [(0, lc, 16), (lc, t - lc, 32 + lc)]
    halo = [(0, 16), (16 + lc, 16), (32 + t, 16)]
    cw = 256

    def inv_count(t0, rows, ln):
        pos = t0 + lax.broadcasted_iota(jnp.int32, (rows, cw), 0)
        cnt = jnp.minimum(pos + r + 1, ln) - jnp.maximum(pos - r, 0)
        return 1.0 / cnt.astype(F32)

    def body(x_ref, o_ref, s_ref):
        for h0, hn in halo:
            s_ref[pl.ds(h0, hn), :] = jnp.zeros((hn, cw), F32)
        for r0, ln, s0 in segs:
            step = min(512, ln)
            for c0 in range(0, ln, step):
                v = x_ref[pl.ds(r0 + c0, step), :]
                s_ref[pl.ds(s0 + c0, step), :] = v * inv_count(c0, step, ln) if transpose else v
        for r0, ln, s0 in segs:
            step = min(512, ln)
            for c0 in range(0, ln, step):
                acc = s_ref[pl.ds(s0 + c0 - r, step), :]
                for dlt in range(-r + 1, r + 1):
                    acc = acc + s_ref[pl.ds(s0 + c0 + dlt, step), :]
                ctr = x_ref[pl.ds(r0 + c0, step), :]
                o_ref[pl.ds(r0 + c0, step), :] = (acc if transpose else acc * inv_count(c0, step, ln)) - ctr

    return pl.pallas_call(
        body, name=name, grid=(bsz,),
        in_specs=[pl.BlockSpec((None, t, cw), lambda b: (b, 0, col_blk))],
        out_specs=pl.BlockSpec((None, t, cw), lambda b: (b, 0, 0)),
        out_shape=jax.ShapeDtypeStruct((bsz, t, cw), F32),
        scratch_shapes=[pltpu.VMEM((t + 48, cw), F32)],
        compiler_params=_cp(VMEM_BIG),
    )(src)


def _odd_specs(d, i, jl, nct):
    return [_tile(d), _mod_spec(i, nct, d), _tile(256), _tile(256), _tile(256), _tile(256), _tile(1024, 1),
            _layer_spec(0, 4, 256, 256), _layer_spec(jl, 1, 1024), _layer_spec(0, 1024, d)]


def _odd_mix(pm, psc, gz):
    return pm * psc * jax.nn.silu(gz)


def _odd_out_fwd(hs, mods, ps, z, pw, psc, wout, i, jl, nct, name):
    bsz, t, d = hs.shape

    def body(h_ref, mod_ref, p0, p1, p2, p3, gz_ref, pw_ref, psc_ref, wo_ref, o_ref):
        pm = jnp.concatenate([_nn(_bf(p[...]), pw_ref[q]) for q, p in enumerate((p0, p1, p2, p3))], axis=1)
        mix = _odd_mix(pm, psc_ref[...], gz_ref[...])
        o_ref[...] = h_ref[...] + mod_ref[2:3, :] * _nn(_bf(mix), wo_ref[...])

    return pl.pallas_call(
        body, name=name, grid=(bsz, t // ROWT), in_specs=_odd_specs(d, i, jl, nct),
        out_specs=_tile(d), out_shape=jax.ShapeDtypeStruct((bsz, t, d), F32), compiler_params=_cp(VMEM_BIG),
    )(hs, mods, *ps, z, pw, psc, wout)


def _odd_out_bwd(dh, mods, ps, z, pw, psc, wout, i, jl, nct, name):
    bsz, t, d = dh.shape

    def body(dh_ref, mod_ref, p0, p1, p2, p3, gz_ref, pw_ref, psc_ref, wo_ref,
             dp_ref, dgz_ref, dmod_ref, dpw_ref, dpsc_ref, dwo_ref):
        b, j = pl.program_id(0), pl.program_id(1)
        pbs = [_bf(p[...]) for p in (p0, p1, p2, p3)]
        pm = jnp.concatenate([_nn(pb, pw_ref[q]) for q, pb in enumerate(pbs)], axis=1)
        mix, vj = jax.vjp(_odd_mix, pm, psc_ref[...], gz_ref[...])
        mixb = _bf(mix)
        dhv = dh_ref[...]
        dyo = _bf(dhv * mod_ref[2:3, :])
        yout = _nn(mixb, wo_ref[...])
        dpm, dpsc, dgz = vj(_nt(dyo, wo_ref[...]))
        dgz_ref[...] = dgz
        dpmb = _bf(dpm)
        dp_ref[...] = jnp.concatenate([_nt(dpmb[:, q * 256:(q + 1) * 256], pw_ref[q]) for q in range(4)], axis=1)

        @pl.when((j == 0) | (j == nct))
        def _():
            dmod_ref[...] = jnp.zeros_like(dmod_ref)

        @pl.when((b == 0) & (j == 0))
        def _():
            dpw_ref[...] = jnp.zeros_like(dpw_ref)
            dpsc_ref[...] = jnp.zeros_like(dpsc_ref)
            dwo_ref[...] = jnp.zeros_like(dwo_ref)
        dmod_ref[2:3, :] += jnp.sum(dhv * yout, axis=0, keepdims=True)
        for q in range(4):
            dpw_ref[q] += _tn(pbs[q], dpmb[:, q * 256:(q + 1) * 256])
        dpsc_ref[...] += dpsc
        dwo_ref[...] += _tn(mixb, dyo)

    full = lambda *s: pl.BlockSpec(s, lambda b, j: (0,) * len(s))
    return pl.pallas_call(
        body, name=name, grid=(bsz, t // ROWT), in_specs=_odd_specs(d, i, jl, nct),
        out_specs=[_tile(1024), _tile(1024), _dmod_spec(nct, d), full(4, 256, 256), full(1, 1024), full(1024, d)],
        out_shape=(jax.ShapeDtypeStruct((bsz, t, 1024), F32), jax.ShapeDtypeStruct((bsz, t, 1024), F32),
                   jax.ShapeDtypeStruct((2 * bsz, 3, d), F32), jax.ShapeDtypeStruct((4, 256, 256), F32),
                   jax.ShapeDtypeStruct((1, 1024), F32), jax.ShapeDtypeStruct((1024, d), F32)),
        compiler_params=_cp(VMEM_BIG),
    )(dh, mods, *ps, z, pw, psc, wout)


def _rms(h, g):
    return h * lax.rsqrt(jnp.mean(h * h, axis=-1, keepdims=True) + EPS) * g


def _final_loss(hs, g, target, nct, name):
    bsz, t, d = hs.shape

    def body(h_ref, g_ref, t_ref, dh_ref, loss_ref, dg_ref):
        b, j = pl.program_id(0), pl.program_id(1)
        y, vj = jax.vjp(_rms, h_ref[...], g_ref[...])
        err = jnp.where(j >= nct, y - t_ref[...], 0.0)
        dh, dg = vj(err * (1.0 / d))
        dh_ref[...] = dh

        @pl.when(j == 0)
        def _():
            loss_ref[...] = jnp.zeros_like(loss_ref)

        @pl.when((b == 0) & (j == 0))
        def _():
            dg_ref[...] = jnp.zeros_like(dg_ref)
        loss_ref[...] += 0.5 * jnp.sum(jnp.mean(err * err, axis=-1))
        dg_ref[...] += dg

    return pl.pallas_call(
        body, name=name, grid=(bsz, t // ROWT),
        in_specs=[_tile(d), pl.BlockSpec((1, d), lambda b, j: (0, 0)),
                  pl.BlockSpec((None, ROWT, d), lambda b, j: (b, jnp.maximum(j - nct, 0), 0))],
        out_specs=[_tile(d), pl.BlockSpec((None, 8, LANES), lambda b, j: (b, 0, 0)),
                   pl.BlockSpec((1, d), lambda b, j: (0, 0))],
        out_shape=(jax.ShapeDtypeStruct((bsz, t, d), F32), jax.ShapeDtypeStruct((bsz, 8, LANES), F32),
                   jax.ShapeDtypeStruct((1, d), F32)),
    )(hs, g, target)


def _adamw(w, g_list, m, v, transpose_g, rows, name):
    nl, r, c = w.shape
    ns = g_list[0].shape[1]
    rt = rows or r
    offs = np.cumsum([0] + [g.shape[0] for g in g_list])
    ng = len(g_list)

    def body(*refs):
        w_ref, m_ref, v_ref = refs[0], refs[1 + ng], refs[2 + ng]
        go_ref, d_ref, mo_ref, vo_ref = refs[3 + ng:]
        layer = pl.program_id(0)
        for k in range(ng):
            g_ref = refs[1 + k]

            @pl.when((layer >= offs[k]) & (layer < offs[k + 1]))
            def _():
                g = g_ref[0]
                for s in range(1, ns):
                    g = g + g_ref[s]
                if transpose_g:
                    g = g.T
                mn = ADAM_B1 * m_ref[...] + (1.0 - ADAM_B1) * g
                vn = ADAM_B2 * v_ref[...] + (1.0 - ADAM_B2) * jnp.square(g)
                m_hat = mn / (1.0 - ADAM_B1 ** ADAM_STEP)
                v_hat = vn / (1.0 - ADAM_B2 ** ADAM_STEP)
                go_ref[...] = g
                d_ref[...] = -ADAM_LR * (m_hat / (jnp.sqrt(v_hat) + ADAM_EPS) + ADAM_WD * w_ref[...])
                mo_ref[...] = mn
                vo_ref[...] = vn

    def g_spec(k):
        lo, n_k = int(offs[k]), int(offs[k + 1] - offs[k])

        def idx(l, q):
            mine = (l >= lo) & (l < lo + n_k)
            q = jnp.where(mine, q, 0)
            return (jnp.clip(l - lo, 0, n_k - 1), 0, 0, q) if transpose_g else (jnp.clip(l - lo, 0, n_k - 1), 0, q, 0)
        return pl.BlockSpec((None, ns, c, rt) if transpose_g else (None, ns, rt, c), idx)

    blk = pl.BlockSpec((None, rt, c), lambda l, q: (l, q, 0))
    out = jax.ShapeDtypeStruct((nl, r, c), F32)
    return pl.pallas_call(
        body, name=name, grid=(nl, r // rt),
        in_specs=[blk] + [g_spec(k) for k in range(ng)] + [blk, blk],
        out_specs=[blk, blk, blk, blk], out_shape=(out, out, out, out), compiler_params=_cp(VMEM_BIG),
    )(w, *g_list, m, v)


def _sum_slots(slots, name):
    ns, r, c = slots.shape

    def body(s_ref, o_ref):
        g = s_ref[0]
        for s in range(1, ns):
            g = g + s_ref[s]
        o_ref[...] = g

    return pl.pallas_call(body, name=name, out_shape=jax.ShapeDtypeStruct((r, c), F32), compiler_params=_cp(VMEM_BIG))(slots)


def _pack(arrs):
    flat = jnp.concatenate([a.reshape(-1) for a in arrs])
    return jnp.pad(flat, (0, (-flat.shape[0]) % (PACK_ROWS * PACK_W))).reshape(-1, PACK_W)


def _unpack(buf, shapes):
    flat = buf.reshape(-1)
    out, off = [], 0
    for s in shapes:
        n = int(np.prod(s))
        out.append(flat[off:off + n].reshape(s))
        off += n
    return out


def kernel(x, c, ctx, c_ctx, ada_w, ada_b, norm_g, even_w_in, even_w_out, attn_sink, ssm_a_re, ssm_a_im, ssm_log_dt, ssm_b_re, ssm_b_im, ssm_c_re, ssm_c_im, ssm_d, glu_w, glu_b, odd_w_in, odd_w_out, pool_w, pool_scale, final_g, loss_target, m_c_ctx, m_ada_w, m_ada_b, m_norm_g, m_even_w_in, m_even_w_out, m_attn_sink, m_ssm_a_re, m_ssm_a_im, m_ssm_log_dt, m_ssm_b_re, m_ssm_b_im, m_ssm_c_re, m_ssm_c_im, m_ssm_d, m_glu_w, m_glu_b, m_odd_w_in, m_odd_w_out, m_pool_w, m_pool_scale, m_final_g, v_c_ctx, v_ada_w, v_ada_b, v_norm_g, v_even_w_in, v_even_w_out, v_attn_sink, v_ssm_a_re, v_ssm_a_im, v_ssm_log_dt, v_ssm_b_re, v_ssm_b_im, v_ssm_c_re, v_ssm_c_im, v_ssm_d, v_glu_w, v_glu_b, v_odd_w_in, v_odd_w_out, v_pool_w, v_pool_scale, v_final_g):
    args = dict(locals())
    bsz, l, d = x.shape
    lc = ctx.shape[1]
    t = lc + l
    nct = lc // ROWT
    depth = ada_w.shape[0]
    nl2 = even_w_in.shape[0]
    me = _my_index()
    assert lc % ROWT == 0 and l % ROWT == 0 and d == 1024 and bsz * N_DEV < 32

    npw = pool_w.shape[2]
    shards = {"even": [_bf(jnp.transpose(even_w_in, (0, 2, 1))), _bf(even_w_out), _bf(glu_w)],
              "odd": [_bf(jnp.transpose(odd_w_in, (0, 2, 1))), _bf(odd_w_out), _bf(pool_w.reshape(nl2, -1, pool_w.shape[-1]))]}

    def wgather(kind, jl):
        return shards[kind], [("gather", [(ii, jl)], 0) for ii in range(3)]

    def wjoin(kind, res):
        full = [a.reshape(1, N_DEV * a.shape[2], a.shape[3]) for a in res]
        if kind == "odd":
            full[2] = jnp.transpose(res[2].reshape(1, N_DEV, 4, npw, 256), (0, 2, 1, 3, 4)).reshape(1, 4, 256, 256)
        return full

    xin, xsp = wgather("even", 0)
    got = _exchange(xin + [c, pool_scale], xsp + [("gather", [(3, None)], 0), ("gather", [(4, None)], 0)], "gather_first")
    weights = {("even", 0): wjoin("even", got[:3])}
    c_all = got[3].reshape(N_DEV * bsz, d)
    psc_full = jnp.transpose(got[4][0], (1, 0, 2)).reshape(nl2, 1, d)

    n_rows = 32
    craw = jnp.concatenate([c_all, c_ctx[None], jnp.zeros((n_rows - N_DEV * bsz - 1, d), F32)], axis=0)
    ncol = ada_w.shape[2]
    ada_b_loc = lax.dynamic_slice_in_dim(ada_b, me * ncol, ncol, axis=1)[:, None, :]
    m_loc = _adaln_fwd(craw, ada_w, ada_b_loc)
    m_all = _exchange([m_loc], [("gather", [(0, None)], 0)], "gather_mod")[0][0]
    m_all = jnp.transpose(m_all, (1, 2, 0, 3)).reshape(depth, n_rows, 3 * d)
    lat = lax.dynamic_slice_in_dim(m_all, me * bsz, bsz, axis=1).reshape(depth, bsz, 1, 3, d)
    cmod = jnp.broadcast_to(m_all[:, N_DEV * bsz].reshape(depth, 1, 1, 3, d), (depth, bsz, 1, 3, d))
    mods = jnp.concatenate([cmod, lat], axis=2).reshape(depth, 2 * bsz, 3, d)

    cos, sin = _rope_tables(lc, l)
    hs = jnp.concatenate([ctx, x], axis=1)
    g3 = norm_g[:, None, :]
    dsk3, gb3 = ssm_d[:, None, :], glu_b[:, None, :]
    u_col = 1280 // LANES

    s5_prm = (ssm_a_re, ssm_a_im, ssm_log_dt, ssm_b_re, ssm_b_im, ssm_c_re, ssm_c_im)
    ops = jax.vmap(_s5_operators)(*s5_prm)
    pr, pi = jax.vmap(lambda r, q: _scan_powers(r, q, t // CHUNK))(ops[3], ops[4])
    s5_ops = (_bf(ops[0]), _bf(ops[1]), _bf(ops[2]), pr, pi)
    saved = []
    for i in range(depth):
        jl = i // 2
        if i % 2 == 0:
            wt, wo, gwf = weights[("even", jl)]
            z = _fused_in_fwd(hs, mods, g3, wt, i, 0, nct, f"in_fwd{i}")
            qp, kvp = _rope_fwd(z, cos, sin, 2, f"rope_fwd{i}")
            (oa, lse), got = _attn_fwd(qp, kvp, attn_sink, jl, lc, f"attn_fwd{i}", xchg=wgather("odd", jl))
            weights[("odd", jl)] = wjoin("odd", got)
            nxt = wgather("even", jl + 1) if jl + 1 < nl2 else None
            (y,), got = _s5_fwd(z, u_col, s5_ops, jl, lc, f"s5_fwd{i}", xchg=nxt)
            if nxt:
                weights[("even", jl + 1)] = wjoin("even", got)
            hn = _even_out_fwd(hs, mods, oa, y, z, dsk3, gwf, gb3, wo, i, jl, nct, f"out_fwd{i}")
            saved.append(dict(hs=hs, z=z, qp=qp, kvp=kvp, oa=oa, lse=lse, y=y))
        else:
            wt, wo, pwf = weights[("odd", jl)]
            z = _fused_in_fwd(hs, mods, g3, wt, i, 0, nct, f"in_fwd{i}")
            ps = [_pool(z, gi, wd // 2, lc, False, f"pool_fwd{i}_{gi}") for gi, wd in enumerate(POOL_WINDOWS)]
            hn = _odd_out_fwd(hs, mods, ps, z, pwf, psc_full, wo, i, jl, nct, f"out_fwd{i}")
            saved.append(dict(hs=hs, z=z, ps=ps))
        hs = hn

    dh, loss_p, d_final_g = _final_loss(hs, final_g[None], loss_target, nct, "final_loss")
    loss = lax.psum(jnp.sum(loss_p[:, 0, 0]), ("x", "y", "c"))

    slots = {n: [None] * nl2 for n in ("even_w_in", "even_w_out", "glu_w", "odd_w_in", "odd_w_out", "pool_w", "pool_scale")}
    gsmall = {n: [None] * nl2 for n in ("attn_sink", "ssm_d", "glu_b")}
    d_norm_g, d_mods = [None] * depth, [None] * depth
    d_ops = [None] * nl2

    def rows_xchg(named):
        arrs = [a for _, _, a in named]
        specs = [("rows2" if n == "pool_w" else "rows", [(k, None)], a.shape[1 if n == "pool_w" else 0] // N_DEV)
                 for k, (n, _, a) in enumerate(named)]
        return (arrs, specs)

    def keep(named, res):
        for (n, jl_, _), r_ in zip(named, res):
            slots[n][jl_] = r_

    pending = None
    for i in reversed(range(depth)):
        jl = i // 2
        sv = saved[i]
        if i % 2 == 0:
            wt, wo, gwf = weights[("even", jl)]
            doa, dga, dy, dgs, dmod_g, ddsk, dgw, dgb, dwo = _even_out_bwd(
                dh, mods, sv["oa"], sv["y"], sv["z"], dsk3, gwf, gb3, wo, i, jl, nct, f"out_bwd{i}")
            (dq_r, dkv_acc, dsink), got = _attn_bwd(sv["qp"], sv["kvp"], attn_sink, jl, sv["oa"], sv["lse"], doa, lc,
                                                     f"attn_bwd{i}", xchg=rows_xchg(pending) if pending else None)
            if pending:
                keep(pending, got)
            dq, dkv = _rope_bwd(dq_r, dkv_acc, cos, sin, t, f"rope_bwd{i}")
            mine = [("even_w_out", jl, dwo), ("glu_w", jl, dgw)]
            (du, dmt, dbt, dct, dav), got = _s5_bwd(sv["z"], u_col, dy, s5_ops, jl, lc, f"s5_bwd{i}", xchg=rows_xchg(mine))
            keep(mine, got)
            per_group = lambda q: dav[:, q].reshape(SSM_GROUPS, SSM_STATE)
            d_ops[jl] = (dmt, dbt, dct, jnp.stack([per_group(0), per_group(2)]), jnp.stack([per_group(1), per_group(3)]))
            gsmall["attn_sink"][jl] = jnp.sum(dsink[:, :, 0], axis=0)
            gsmall["ssm_d"][jl] = ddsk[0]
            gsmall["glu_b"][jl] = dgb[0]
            pieces = [(dq, 512, 0), (dkv, 256, 0), (dga, 512, 0), (du, 512, 0), (dy, 512, 0), (dgs, 512, 0)]
            asm = lambda p, s: jnp.concatenate([p[0], p[1], p[2], p[3] + p[4] * s, p[5]], axis=1)
            (dh, dmod_in, dg, dw), _ = _fused_in_bwd(sv["hs"], mods, g3, wt, dh, pieces, (dsk3, jl), asm, i, 0, nct, f"in_bwd{i}")
            pending = [("even_w_in", jl, dw)]
        else:
            wt, wo, pwf = weights[("odd", jl)]
            dp, dgz, dmod_g, dpw, dpsc, dwo = _odd_out_bwd(
                dh, mods, sv["ps"], sv["z"], pwf, psc_full, wo, i, jl, nct, f"out_bwd{i}")
            dus = [_pool(dp, gi, wd // 2, lc, True, f"pool_bwd{i}_{gi}") for gi, wd in enumerate(POOL_WINDOWS)]
            dpsc8 = jnp.broadcast_to(dpsc.reshape(N_DEV, 1, -1), (N_DEV, 8, d // N_DEV)).reshape(N_DEV * 8, -1)
            pieces = [(u_, 256, 0) for u_ in dus] + [(dgz, 1024, 0)]
            asm = lambda p, s: jnp.concatenate(p, axis=1)
            (dh, dmod_in, dg, dw), got = _fused_in_bwd(sv["hs"], mods, g3, wt, dh, pieces, None, asm, i, 0, nct, f"in_bwd{i}",
                                                       xchg=rows_xchg(pending) if pending else None)
            if pending:
                keep(pending, got)
            pending = [("odd_w_in", jl, dw), ("odd_w_out", jl, dwo), ("pool_w", jl, dpw), ("pool_scale", jl, dpsc8)]
        d_norm_g[i] = dg[0]
        d_mods[i] = (dmod_g + dmod_in).reshape(bsz, 2, 3 * d)
    grad_x = dh[:, lc:]

    _, op_vjp = jax.vjp(jax.vmap(_s5_operators), *s5_prm)
    dprm = op_vjp(tuple(jnp.stack([d_ops[jl][q] for jl in range(nl2)]) for q in range(5)))

    dmd = jnp.stack(d_mods)
    dm_loc = jnp.concatenate([dmd[:, :, 1], jnp.sum(dmd[:, :, 0], axis=1, keepdims=True)], axis=1)
    dm_all = _exchange([dm_loc], [("gather", [(0, None)], 0)], "gather_dmod")[0][0]
    dm_lat = jnp.transpose(dm_all[:, :, :bsz], (1, 0, 2, 3)).reshape(depth, N_DEV * bsz, 3 * d)
    dm_ctx = dm_all[0, :, bsz]
    for e in range(1, N_DEV):
        dm_ctx = dm_ctx + dm_all[e, :, bsz]
    dm_rows = jnp.concatenate([dm_lat, dm_ctx[:, None], jnp.zeros((depth, n_rows - N_DEV * bsz - 1, 3 * d), F32)], axis=1)
    dm_cols = lax.dynamic_slice_in_dim(dm_rows, me * ncol, ncol, axis=2)
    g_ada_w, dcraw = _adaln_bwd(craw, dm_cols, ada_w)

    small = {"c_ctx": dcraw[N_DEV * bsz], "ada_b": jnp.sum(dm_loc, axis=1), "norm_g": jnp.stack(d_norm_g),
             "final_g": d_final_g[0]}
    for n, v2 in gsmall.items():
        small[n] = jnp.stack(v2)
    for n, gval in zip(("ssm_a_re", "ssm_a_im", "ssm_log_dt", "ssm_b_re", "ssm_b_im", "ssm_c_re", "ssm_c_im"), dprm):
        small[n] = gval
    snames = list(small)
    sshapes = [args[n].shape for n in snames]
    spack = _pack([small[n].reshape(args[n].shape) for n in snames])

    last = pending + [("small", 0, spack)]
    red = _exchange(*rows_xchg(last), "reduce_last")
    keep(pending, red[:-1])
    s_sum = _sum_slots(red[-1][0], "sum_small")
    s_all = _exchange([s_sum], [("gather", [(0, None)], 0)], "gather_small")[0]
    s_all = s_all.reshape(1, 1, -1, PACK_W)

    out = {}

    def put(n, res, shape):
        for kind, buf in zip(("grad_", "delta_", "new_m_", "new_v_"), res):
            out[kind + n] = buf.reshape(shape)

    def as3(a):
        return a.reshape(a.shape[0], -1, a.shape[-1])

    for n in slots:
        w = args[n]
        g_list = [s_[:, :, :1] if n == "pool_scale" else s_ for s_ in slots[n]]
        g_list = [s_.reshape(1, N_DEV, -1, s_.shape[-1]) for s_ in g_list]
        tr = n in ("even_w_in", "odd_w_in")
        put(n, _adamw(as3(w), g_list, as3(args["m_" + n]), as3(args["v_" + n]), tr, 256 if tr else None, "adamw_" + n),
            w.shape)
    put("ada_w", _adamw(ada_w, [g_ada_w[:, None]], m_ada_w, v_ada_w, False, None, "adamw_ada_w"), ada_w.shape)
    res = _adamw(_pack([args[n] for n in snames])[None], [s_all], _pack([args["m_" + n] for n in snames])[None],
                 _pack([args["v_" + n] for n in snames])[None], False, PACK_ROWS, "adamw_small")
    for kind, buf in zip(("grad_", "delta_", "new_m_", "new_v_"), res):
        for n, a in zip(snames, _unpack(buf, sshapes)):
            out[kind + n] = a

    order = ["c_ctx", "ada_w", "ada_b", "norm_g", "even_w_in", "even_w_out", "attn_sink", "ssm_a_re", "ssm_a_im",
             "ssm_log_dt", "ssm_b_re", "ssm_b_im", "ssm_c_re", "ssm_c_im", "ssm_d", "glu_w", "glu_b", "odd_w_in",
             "odd_w_out", "pool_w", "pool_scale", "final_g"]
    return (loss, grad_x, *[out[k + n] for k in ("grad_", "delta_", "new_m_", "new_v_") for n in order])
```

```python
import functools

import numpy as np
import jax
import jax.numpy as jnp
from jax import lax
from jax.experimental import pallas as pl
from jax.experimental.pallas import tpu as pltpu

F32 = jnp.float32
BF16 = jnp.bfloat16
EPS = 1e-6
NEG_INF = -1e30
N_DEV = 8
GRID_W = 64
WINDOW = 128
QBLK = 128
ROWT = 256
CHUNK = 16
SSM_GROUPS = 32
SSM_GROUP = 16
SSM_STATE = 64
POOL_WINDOWS = (2, 4, 8, 16)
ROPE_BASE = 10000.0
ADAM_LR, ADAM_B1, ADAM_B2, ADAM_EPS, ADAM_WD, ADAM_STEP = 0.001, 0.9, 0.999, 1e-08, 0.01, 10
LANES = 128
PACK_W = 1024
PACK_ROWS = 64
VMEM_BIG = 56 << 20


def _cp(vmem=None):
    return pltpu.CompilerParams(vmem_limit_bytes=vmem) if vmem else pltpu.CompilerParams()


def _nt(a, b):
    return lax.dot_general(a, b, (((1,), (1,)), ((), ())), preferred_element_type=F32)


def _tn(a, b):
    return lax.dot_general(a, b, (((0,), (0,)), ((), ())), preferred_element_type=F32)


def _nn(a, b):
    return jnp.dot(a, b, preferred_element_type=F32)


def _bf(x):
    return x.astype(BF16)


def _my_index():
    return 4 * lax.axis_index("x") + 2 * lax.axis_index("y") + lax.axis_index("c")


def _xchg_shapes(inputs, specs):
    out_shapes = []
    for kind, parts, r in specs:
        ii, li = parts[0]
        shp = tuple(inputs[ii].shape if li is None else inputs[ii].shape[1:])
        piece = shp if kind == "gather" else ((r,) + shp[1:] if kind == "rows" else (shp[0], r) + shp[2:])
        out_shapes.append(jax.ShapeDtypeStruct((len(parts), N_DEV) + piece, inputs[ii].dtype))
    return out_shapes, sum(len(parts) for _, parts, _ in specs)


def _xchg_sems(n_parts):
    return [pltpu.SemaphoreType.DMA((n_parts * (N_DEV - 1),)), pltpu.SemaphoreType.DMA((n_parts * (N_DEV - 1),)),
            pltpu.SemaphoreType.DMA((n_parts,))]


def _xchg_copies(specs, in_refs, out_refs, send_sems, recv_sems, loc_sems):
    x, y, c = lax.axis_index("x"), lax.axis_index("y"), lax.axis_index("c")
    me = 4 * x + 2 * y + c

    def piece(ref, kind, r, p):
        if kind == "gather":
            return ref
        if kind == "rows":
            return ref.at[pl.ds(p * r, r)]
        return ref.at[:, pl.ds(p * r, r)]

    copies, q = [], 0
    for si, (kind, parts, r) in enumerate(specs):
        for pi, (ii, li) in enumerate(parts):
            src = in_refs[ii] if li is None else in_refs[ii].at[li]
            dst = out_refs[si].at[pi, me]
            copies.append(pltpu.make_async_copy(piece(src, kind, r, me), dst, loc_sems.at[q]))
            for k in range(1, N_DEV):
                px = 1 - x if k & 4 else x
                py = 1 - y if k & 2 else y
                pc = 1 - c if k & 1 else c
                copies.append(pltpu.make_async_remote_copy(
                    src_ref=piece(src, kind, r, 4 * px + 2 * py + pc), dst_ref=dst,
                    send_sem=send_sems.at[q * (N_DEV - 1) + k - 1], recv_sem=recv_sems.at[q * (N_DEV - 1) + k - 1],
                    device_id=(px, py, pc), device_id_type=pl.DeviceIdType.MESH))
            q += 1
    return copies


def _exchange(inputs, specs, name):
    out_shapes, n_parts = _xchg_shapes(inputs, specs)
    ni, ns = len(inputs), len(specs)

    def body(*refs):
        copies = _xchg_copies(specs, refs[:ni], refs[ni:ni + ns], *refs[ni + ns:])
        for cp in copies:
            cp.start()
        for cp in copies:
            cp.wait()

    return pl.pallas_call(
        body, name=name, out_shape=tuple(out_shapes),
        in_specs=[pl.BlockSpec(memory_space=pl.ANY)] * ni,
        out_specs=tuple(pl.BlockSpec(memory_space=pl.ANY) for _ in specs),
        scratch_shapes=_xchg_sems(n_parts),
        compiler_params=pltpu.CompilerParams(has_side_effects=True),
    )(*inputs)


def _call(body, *, name, grid, in_specs, out_specs, out_shape, operands, vmem=None, scratch=(), xchg=None):
    out_shape, out_specs = tuple(out_shape), list(out_specs)
    if xchg is None:
        res = pl.pallas_call(body, name=name, grid=grid, in_specs=in_specs, out_specs=out_specs, out_shape=out_shape,
                             scratch_shapes=list(scratch), compiler_params=_cp(vmem))(*operands)
        return list(res), None
    xin, xspecs = xchg
    xshapes, n_parts = _xchg_shapes(xin, xspecs)
    ni, no, nxi, nxo, nsc = len(operands), len(out_shape), len(xin), len(xspecs), len(scratch)
    anyspec = pl.BlockSpec(memory_space=pl.ANY)

    def hosted(*refs):
        ins, xins = refs[:ni], refs[ni:ni + nxi]
        outs, xouts = refs[ni + nxi:ni + nxi + no], refs[ni + nxi + no:ni + nxi + no + nxo]
        scr, sems = refs[ni + nxi + no + nxo:ni + nxi + no + nxo + nsc], refs[ni + nxi + no + nxo + nsc:]
        copies = _xchg_copies(xspecs, xins, xouts, *sems)
        first, last = True, True
        for ax, n in enumerate(grid):
            first = first & (pl.program_id(ax) == 0)
            last = last & (pl.program_id(ax) == n - 1)

        @pl.when(first)
        def _():
            for cp in copies:
                cp.start()
        body(*ins, *outs, *scr)

        @pl.when(last)
        def _():
            for cp in copies:
                cp.wait()

    res = pl.pallas_call(
        hosted, name=name, grid=grid, in_specs=list(in_specs) + [anyspec] * nxi,
        out_specs=out_specs + [anyspec] * nxo, out_shape=out_shape + tuple(xshapes),
        scratch_shapes=list(scratch) + _xchg_sems(n_parts), compiler_params=_cp(vmem))(*operands, *xin)
    return list(res[:no]), list(res[no:])


def _adaln_fwd(craw, ada_w, ada_b):
    nl, d, n = ada_w.shape
    r = craw.shape[0]

    def body(c_ref, w_ref, b_ref, o_ref):
        s = jax.nn.silu(c_ref[...])
        o_ref[...] = _nn(_bf(s), _bf(w_ref[...])) + b_ref[...]

    return pl.pallas_call(
        body, name="adaln_fwd", grid=(nl,),
        in_specs=[pl.BlockSpec((r, d), lambda i: (0, 0)), pl.BlockSpec((None, d, n), lambda i: (i, 0, 0)),
                  pl.BlockSpec((None, 1, n), lambda i: (i, 0, 0))],
        out_specs=pl.BlockSpec((None, r, n), lambda i: (i, 0, 0)),
        out_shape=jax.ShapeDtypeStruct((nl, r, n), F32),
    )(craw, ada_w, ada_b)


def _adaln_bwd(craw, dm, ada_w):
    nl, d, n = ada_w.shape
    r = craw.shape[0]

    def body(c_ref, dm_ref, w_ref, gw_ref, dc_ref, acc_ref):
        i = pl.program_id(0)
        s, vj = jax.vjp(jax.nn.silu, c_ref[...])
        dmb = _bf(dm_ref[...])
        gw_ref[...] = _tn(_bf(s), dmb)

        @pl.when(i == 0)
        def _():
            acc_ref[...] = jnp.zeros_like(acc_ref)
        acc_ref[...] += _nt(dmb, _bf(w_ref[...]))

        @pl.when(i == nl - 1)
        def _():
            dc_ref[...] = vj(acc_ref[...])[0]

    return pl.pallas_call(
        body, name="adaln_bwd", grid=(nl,),
        in_specs=[pl.BlockSpec((r, d), lambda i: (0, 0)), pl.BlockSpec((None, r, n), lambda i: (i, 0, 0)),
                  pl.BlockSpec((None, d, n), lambda i: (i, 0, 0))],
        out_specs=[pl.BlockSpec((None, d, n), lambda i: (i, 0, 0)), pl.BlockSpec((r, d), lambda i: (0, 0))],
        out_shape=(jax.ShapeDtypeStruct((nl, d, n), F32), jax.ShapeDtypeStruct((r, d), F32)),
        scratch_shapes=[pltpu.VMEM((r, d), F32)],
    )(craw, dm, ada_w)


def _norm_mod(h, mod, g):
    ms = jnp.mean(h * h, axis=-1, keepdims=True)
    y = h * lax.rsqrt(ms + EPS) * g
    return y * (1.0 + mod[1:2]) + mod[0:1]


def _mod_spec(i, nct, d):
    return pl.BlockSpec((None, None, 3, d), lambda b, j: (i, 2 * b + jnp.where(j >= nct, 1, 0), 0, 0))


def _dmod_spec(nct, d):
    return pl.BlockSpec((None, 3, d), lambda b, j: (2 * b + jnp.where(j >= nct, 1, 0), 0, 0))


def _layer_spec(li, *shape):
    return pl.BlockSpec((None,) + tuple(shape), lambda b, j: (li,) + (0,) * len(shape))


def _tile(width, col_blk=0):
    return pl.BlockSpec((None, ROWT, width), lambda b, j: (b, j, col_blk))


def _fused_in_fwd(hs, mods, g, wt, i, jl, nct, name):
    bsz, t, d = hs.shape
    n = wt.shape[1]

    def body(h_ref, mod_ref, g_ref, w_ref, z_ref):
        a = _norm_mod(h_ref[...], mod_ref[...], g_ref[...])
        z_ref[...] = _nt(_bf(a), w_ref[...])

    return pl.pallas_call(
        body, name=name, grid=(bsz, t // ROWT),
        in_specs=[_tile(d), _mod_spec(i, nct, d), _layer_spec(i, 1, d), _layer_spec(jl, n, d)],
        out_specs=_tile(n), out_shape=jax.ShapeDtypeStruct((bsz, t, n), F32),
        compiler_params=_cp(VMEM_BIG),
    )(hs, mods, g, wt)


def _fused_in_bwd(hs, mods, g, wt, dh_out, pieces, small, assemble, i, jl, nct, name, xchg=None):
    bsz, t, d = hs.shape
    n = wt.shape[1]
    npc, nsm = len(pieces), int(small is not None)

    def body(*refs):
        h_ref, mod_ref, g_ref, w_ref, dho_ref = refs[:5]
        p_refs = refs[5:5 + npc]
        s_val = refs[5 + npc][...] if nsm else None
        dhi_ref, dmod_ref, dg_ref, dw_ref = refs[5 + npc + nsm:]
        b, j = pl.program_id(0), pl.program_id(1)
        a, vj = jax.vjp(_norm_mod, h_ref[...], mod_ref[...], g_ref[...])
        dz = _bf(assemble([r[...] for r in p_refs], s_val))
        dh, dmod, dg = vj(_nn(dz, w_ref[...]))
        dhi_ref[...] = dho_ref[...] + dh

        @pl.when((j == 0) | (j == nct))
        def _():
            dmod_ref[...] = jnp.zeros_like(dmod_ref)

        @pl.when((b == 0) & (j == 0))
        def _():
            dg_ref[...] = jnp.zeros_like(dg_ref)
            dw_ref[...] = jnp.zeros_like(dw_ref)
        dmod_ref[...] += dmod
        dg_ref[...] += dg
        dw_ref[...] += _tn(dz, _bf(a))

    full = lambda r, c: pl.BlockSpec((r, c), lambda b, j: (0, 0))
    return _call(
        body, name=name, grid=(bsz, t // ROWT),
        in_specs=[_tile(d), _mod_spec(i, nct, d), _layer_spec(i, 1, d), _layer_spec(jl, n, d), _tile(d)]
                 + [_tile(wd, cb) for _, wd, cb in pieces]
                 + ([_layer_spec(small[1], 1, small[0].shape[2])] if nsm else []),
        out_specs=[_tile(d), _dmod_spec(nct, d), full(1, d), full(n, d)],
        out_shape=(jax.ShapeDtypeStruct((bsz, t, d), F32), jax.ShapeDtypeStruct((2 * bsz, 3, d), F32),
                   jax.ShapeDtypeStruct((1, d), F32), jax.ShapeDtypeStruct((n, d), F32)),
        vmem=VMEM_BIG, xchg=xchg,
        operands=[hs, mods, g, wt, dh_out, *[p for p, _, _ in pieces], *([small[0]] if nsm else [])])


def _rope_tables(lc, l):
    tpos = np.arange(l)
    row = (tpos // GRID_W).astype(np.float32)
    col = (tpos % GRID_W).astype(np.float32)
    nf = 16
    inv = (np.float32(ROPE_BASE) ** (-np.arange(nf, dtype=np.float32) / np.float32(nf))).astype(np.float32)
    ang_r = row[:, None] * inv[None]
    ang_c = col[:, None] * inv[None]
    cos64 = np.concatenate([np.cos(ang_r), np.cos(ang_r), np.cos(ang_c), np.cos(ang_c)], axis=1)
    sin64 = np.concatenate([-np.sin(ang_r), np.sin(ang_r), -np.sin(ang_c), np.sin(ang_c)], axis=1)
    cos = np.concatenate([np.ones((lc, 128), np.float32), np.tile(cos64, (1, 2)).astype(np.float32)], axis=0)
    sin = np.concatenate([np.zeros((lc, 128), np.float32), np.tile(sin64, (1, 2)).astype(np.float32)], axis=0)
    return jnp.asarray(cos), jnp.asarray(sin)


def _rot(x, cos, sin):
    lane = lax.broadcasted_iota(jnp.int32, x.shape, 1)
    first = jnp.bitwise_and(jnp.right_shift(lane, 4), 1) == 0
    partner = jnp.where(first, pltpu.roll(x, LANES - 16, 1), pltpu.roll(x, 16, 1))
    return x * cos + partner * sin


def _split_heads(x):
    low = lax.broadcasted_iota(jnp.int32, x.shape, 1) < 64
    return jnp.where(low, x, 0.0), jnp.where(low, pltpu.roll(x, 64, 1), 0.0)


def _merge_heads(a, b):
    return a + pltpu.roll(b, 64, 1)


def _rope_fwd(z, cos, sin, kv_blk, name):
    bsz, t, _ = z.shape

    def body(q_ref, kv_ref, cos_ref, sin_ref, qp_ref, kvp_ref):
        c, s = cos_ref[...], sin_ref[...]
        outs = []
        for sl in range(4):
            xr = _rot(q_ref[:, sl * LANES:(sl + 1) * LANES], c, s) * 0.125
            outs += list(_split_heads(xr))
        qp_ref[...] = _bf(jnp.concatenate(outs, axis=1))
        k0, k1 = _split_heads(_rot(kv_ref[:, 0:LANES], c, s))
        v0, v1 = _split_heads(kv_ref[:, LANES:2 * LANES])
        kvp_ref[...] = _bf(jnp.concatenate([k0, k1, v0, v1], axis=1))

    return pl.pallas_call(
        body, name=name, grid=(bsz, t // ROWT),
        in_specs=[_tile(512), _tile(256, kv_blk),
                  pl.BlockSpec((ROWT, LANES), lambda b, j: (j, 0)), pl.BlockSpec((ROWT, LANES), lambda b, j: (j, 0))],
        out_specs=[_tile(1024), _tile(512)],
        out_shape=(jax.ShapeDtypeStruct((bsz, t, 1024), BF16), jax.ShapeDtypeStruct((bsz, t, 512), BF16)),
    )(z, z, cos, sin)


def _rope_bwd(dq_r, dkv_acc, cos, sin, t, name):
    bsz = dq_r.shape[0]

    def body(dq_ref, acc_ref, cos_ref, sin_ref, dqo_ref, dkvo_ref):
        c, s = cos_ref[...], -sin_ref[...]
        dqo_ref[...] = jnp.concatenate(
            [_rot(dq_ref[:, sl * LANES:(sl + 1) * LANES], c, s) for sl in range(4)], axis=1)
        dk = _merge_heads(acc_ref[:, 0:LANES], acc_ref[:, LANES:2 * LANES])
        dv = _merge_heads(acc_ref[:, 2 * LANES:3 * LANES], acc_ref[:, 3 * LANES:4 * LANES])
        dkvo_ref[...] = jnp.concatenate([_rot(dk, c, s), dv], axis=1)

    return pl.pallas_call(
        body, name=name, grid=(bsz, t // ROWT),
        in_specs=[_tile(512), _tile(512),
                  pl.BlockSpec((ROWT, LANES), lambda b, j: (j, 0)), pl.BlockSpec((ROWT, LANES), lambda b, j: (j, 0))],
        out_specs=[_tile(512), _tile(256)],
        out_shape=(jax.ShapeDtypeStruct((bsz, t, 512), F32), jax.ShapeDtypeStruct((bsz, t, 256), F32)),
    )(dq_r, dkv_acc, cos, sin)


def _attn_bias(j, ncb, l):
    n = j - ncb
    r = lax.broadcasted_iota(jnp.int32, (QBLK, 3 * QBLK), 0)
    cidx = lax.broadcasted_iota(jnp.int32, (QBLK, 3 * QBLK), 1)
    kpos = (n - 1) * QBLK + cidx
    valid = (jnp.abs(r - cidx + QBLK) <= WINDOW) & (kpos >= 0) & (kpos < l) & (j >= ncb)
    return jnp.where(valid, 0.0, NEG_INF).astype(F32)


def _attn_bias4(j, ncb, l):
    b1 = _attn_bias(j, ncb, l)
    return jnp.concatenate([b1, b1, b1, b1], axis=0)


def _sink_rows(sink_ref, jl, hk):
    head = jnp.right_shift(lax.broadcasted_iota(jnp.int32, (4 * QBLK, 1), 0), 7)
    sk = jnp.zeros((4 * QBLK, 1), F32)
    for g in range(4):
        sk = jnp.where(head == g, sink_ref[jl, 4 * hk + g], sk)
    return sk


def _kv_specs(nb, lc):
    return [pl.BlockSpec((None, QBLK, 512), lambda b, j: (b, jnp.maximum(j - 1, 0), 0)),
            pl.BlockSpec((None, QBLK, 512), lambda b, j: (b, j, 0)),
            pl.BlockSpec((None, QBLK, 512), lambda b, j: (b, jnp.minimum(j + 1, nb - 1), 0)),
            pl.BlockSpec((None, lc, 512), lambda b, j: (b, 0, 0))]


def _lane_pick(x, h):
    lane = lax.broadcasted_iota(jnp.int32, x.shape, 1)
    return jnp.sum(jnp.where(lane == h, x, 0.0), axis=1, keepdims=True)


def _attn_fwd(qp, kvp, sinks, jl, lc, name, xchg=None):
    bsz, t, _ = qp.shape
    nb, ncb, l = t // QBLK, lc // QBLK, t - lc

    def body(sink_ref, q_ref, kp_ref, kc_ref, kn_ref, kx_ref, o_ref, lse_ref):
        j = pl.program_id(1)
        kloc = jnp.concatenate([kp_ref[...], kc_ref[...], kn_ref[...]], axis=0)
        kctx = kx_ref[...]
        bias = _attn_bias4(j, ncb, l)
        lane = lax.broadcasted_iota(jnp.int32, (QBLK, LANES), 1)
        lse_all = jnp.zeros((QBLK, LANES), F32)
        outs = []
        for hk in range(2):
            ks, vs = slice(hk * LANES, (hk + 1) * LANES), slice((2 + hk) * LANES, (3 + hk) * LANES)
            q4 = jnp.concatenate([q_ref[:, (4 * hk + g) * LANES:(4 * hk + g + 1) * LANES] for g in range(4)], axis=0)
            s_loc = _nt(q4, kloc[:, ks]) + bias
            s_ctx = _nt(q4, kctx[:, ks])
            sk = _sink_rows(sink_ref, jl, hk)
            m = jnp.maximum(jnp.maximum(jnp.max(s_loc, axis=1, keepdims=True), jnp.max(s_ctx, axis=1, keepdims=True)), sk)
            p_loc, p_ctx = jnp.exp(s_loc - m), jnp.exp(s_ctx - m)
            den = jnp.sum(p_loc, axis=1, keepdims=True) + jnp.sum(p_ctx, axis=1, keepdims=True) + jnp.exp(sk - m)
            o4 = (_nn(_bf(p_loc), kloc[:, vs]) + _nn(_bf(p_ctx), kctx[:, vs])) / den
            lse4 = m + jnp.log(den)
            for g in range(4):
                outs.append(o4[g * QBLK:(g + 1) * QBLK])
                lse_all = lse_all + jnp.where(lane == 4 * hk + g, lse4[g * QBLK:(g + 1) * QBLK], 0.0)
        o_ref[...] = jnp.concatenate([_merge_heads(outs[2 * i], outs[2 * i + 1]) for i in range(4)], axis=1)
        lse_ref[...] = lse_all

    return _call(
        body, name=name, grid=(bsz, nb),
        in_specs=[pl.BlockSpec(memory_space=pltpu.SMEM),
                  pl.BlockSpec((None, QBLK, 1024), lambda b, j: (b, j, 0))] + _kv_specs(nb, lc),
        out_specs=[pl.BlockSpec((None, QBLK, 512), lambda b, j: (b, j, 0)),
                   pl.BlockSpec((None, QBLK, LANES), lambda b, j: (b, j, 0))],
        out_shape=(jax.ShapeDtypeStruct((bsz, t, 512), F32), jax.ShapeDtypeStruct((bsz, t, LANES), F32)),
        xchg=xchg, operands=[sinks, qp, kvp, kvp, kvp, kvp])


def _attn_bwd(qp, kvp, sinks, jl, o, lse, do, lc, name, xchg=None):
    bsz, t, _ = qp.shape
    nb, ncb, l = t // QBLK, lc // QBLK, t - lc
    nk = 3 * QBLK

    def body(sink_ref, q_ref, kp_ref, kc_ref, kn_ref, kx_ref, o_ref, lse_ref, do_ref, dq_ref, acc_ref, ds_ref):
        j = pl.program_id(1)

        @pl.when(j == 0)
        def _():
            acc_ref[...] = jnp.zeros_like(acc_ref)
            ds_ref[...] = jnp.zeros_like(ds_ref)
        kloc = jnp.concatenate([kp_ref[...], kc_ref[...], kn_ref[...]], axis=0)
        kctx = kx_ref[...]
        bias = _attn_bias4(j, ncb, l)
        lse = lse_ref[...]
        row8 = lax.broadcasted_iota(jnp.int32, (8, LANES), 0)
        dsink = jnp.zeros((8, LANES), F32)
        dqs = []
        d_loc, d_ctx = [None] * 4, [None] * 4
        for hk in range(2):
            ks, vs = slice(hk * LANES, (hk + 1) * LANES), slice((2 + hk) * LANES, (3 + hk) * LANES)
            q4 = jnp.concatenate([q_ref[:, (4 * hk + g) * LANES:(4 * hk + g + 1) * LANES] for g in range(4)], axis=0)
            do4 = jnp.concatenate([x_ for s2 in (2 * hk, 2 * hk + 1)
                                   for x_ in _split_heads(do_ref[:, s2 * LANES:(s2 + 1) * LANES])], axis=0)
            o4 = jnp.concatenate([x_ for s2 in (2 * hk, 2 * hk + 1)
                                  for x_ in _split_heads(o_ref[:, s2 * LANES:(s2 + 1) * LANES])], axis=0)
            delta = jnp.sum(do4 * o4, axis=1, keepdims=True)
            lse4 = jnp.concatenate([_lane_pick(lse, 4 * hk + g) for g in range(4)], axis=0)
            p_loc = jnp.exp(_nt(q4, kloc[:, ks]) + bias - lse4)
            p_ctx = jnp.exp(_nt(q4, kctx[:, ks]) - lse4)
            dob = _bf(do4)
            ds_loc = _bf(p_loc * (_nt(dob, kloc[:, vs]) - delta))
            ds_ctx = _bf(p_ctx * (_nt(dob, kctx[:, vs]) - delta))
            dq4 = (_nn(ds_loc, kloc[:, ks]) + _nn(ds_ctx, kctx[:, ks])) * 0.125
            d_loc[hk], d_ctx[hk] = _tn(ds_loc, q4), _tn(ds_ctx, q4)
            d_loc[2 + hk], d_ctx[2 + hk] = _tn(_bf(p_loc), dob), _tn(_bf(p_ctx), dob)
            dsk = -jnp.exp(_sink_rows(sink_ref, jl, hk) - lse4) * delta
            for g in range(4):
                dqs.append(dq4[g * QBLK:(g + 1) * QBLK])
                dsink = dsink + jnp.where(row8 == 4 * hk + g, jnp.sum(dsk[g * QBLK:(g + 1) * QBLK]), 0.0)
        dq_ref[...] = jnp.concatenate([_merge_heads(dqs[2 * i], dqs[2 * i + 1]) for i in range(4)], axis=1)
        ds_ref[...] += dsink
        acc_ref[pl.ds(0, lc), :] += jnp.concatenate(d_ctx, axis=1)

        @pl.when(j >= ncb)
        def _():
            st = pl.multiple_of((j - 1) * QBLK, QBLK)
            acc_ref[pl.ds(st, nk), :] += jnp.concatenate(d_loc, axis=1)

    blk = lambda wd: pl.BlockSpec((None, QBLK, wd), lambda b, j: (b, j, 0))
    return _call(
        body, name=name, grid=(bsz, nb),
        in_specs=[pl.BlockSpec(memory_space=pltpu.SMEM), blk(1024)] + _kv_specs(nb, lc)
                 + [blk(512), blk(LANES), blk(512)],
        out_specs=[blk(512), pl.BlockSpec((None, t + QBLK, 512), lambda b, j: (b, 0, 0)),
                   pl.BlockSpec((None, 8, LANES), lambda b, j: (b, 0, 0))],
        out_shape=(jax.ShapeDtypeStruct((bsz, t, 512), F32), jax.ShapeDtypeStruct((bsz, t + QBLK, 512), F32),
                   jax.ShapeDtypeStruct((bsz, 8, LANES), F32)),
        vmem=VMEM_BIG, xchg=xchg, operands=[sinks, qp, kvp, kvp, kvp, kvp, o, lse, do])


def _s5_operators(a_re, a_im, log_dt, b_re, b_im, c_re, c_im):
    hi = lax.Precision.HIGHEST
    dt = jnp.exp(log_dt)[..., None]
    tau = jnp.arange(CHUNK + 1, dtype=F32)[:, None, None, None]
    mag = jnp.exp(tau * (a_re * dt))
    pwr, pwi = mag * jnp.cos(tau * (a_im * dt)), mag * jnp.sin(tau * (a_im * dt))
    ar, ai = pwr[1], pwi[1]
    den = a_re * a_re + a_im * a_im
    fr = ((ar - 1.0) * a_re + ai * a_im) / den
    fi = (ai * a_re - (ar - 1.0) * a_im) / den
    bbr = fr[..., None] * b_re - fi[..., None] * b_im
    bbi = fr[..., None] * b_im + fi[..., None] * b_re
    wr = pwr[:CHUNK, ..., None] * bbr - pwi[:CHUNK, ..., None] * bbi
    wi = pwr[:CHUNK, ..., None] * bbi + pwi[:CHUNK, ..., None] * bbr
    kern = (jnp.einsum("dgcp,tdgpk->dgtck", c_re, wr, precision=hi)
            - jnp.einsum("dgcp,tdgpk->dgtck", c_im, wi, precision=hi))
    g = a_re.shape[1]
    nrow = CHUNK * SSM_GROUP
    lag = np.arange(CHUNK)[:, None, None]
    tt, ss = np.arange(CHUNK)[None, :, None], np.arange(CHUNK)[None, None, :]
    sel_f = jnp.asarray((tt - ss == lag).astype(np.float32))
    sel_b = jnp.asarray((ss - tt == lag).astype(np.float32))
    mt = (jnp.einsum("lts,glck->gsktc", sel_f, kern[0], precision=hi)
          + jnp.einsum("lts,glck->gsktc", sel_b, kern[1], precision=hi)).reshape(g, nrow, nrow)
    to_rows = lambda w: jnp.transpose(w, (1, 0, 3, 2)).reshape(g, nrow, SSM_STATE)
    bt = jnp.concatenate([to_rows(wr[::-1, 0]), to_rows(wi[::-1, 0]), to_rows(wr[:, 1]), to_rows(wi[:, 1])], axis=-1)

    def out_map(d, pw_r, pw_i):
        w2r = c_re[d][None] * pw_r[:, :, None, :] - c_im[d][None] * pw_i[:, :, None, :]
        w2i = c_re[d][None] * pw_i[:, :, None, :] + c_im[d][None] * pw_r[:, :, None, :]
        cols = lambda w: jnp.transpose(w, (1, 3, 0, 2)).reshape(g, SSM_STATE, nrow)
        return [cols(w2r), -cols(w2i)]

    ct = jnp.concatenate(out_map(0, pwr[1:, 0], pwi[1:, 0]) + out_map(1, pwr[1:, 1][::-1], pwi[1:, 1][::-1]), axis=-2)
    return mt, bt, ct, pwr[CHUNK], pwi[CHUNK]


def _scan_powers(a16r, a16i, n_chunks):
    prs, pis = [], []
    r, i = a16r, a16i
    s = 1
    while s < n_chunks:
        prs.append(jnp.concatenate([r, r], axis=-1))
        pis.append(jnp.concatenate([-i, i], axis=-1))
        r, i = r * r - i * i, 2.0 * r * i
        s *= 2
    pad = 16 - len(prs)
    z = jnp.zeros_like(prs[0])
    return jnp.stack(prs + [z] * pad, axis=2), jnp.stack(pis + [z] * pad, axis=2)


def _low_half():
    return lax.broadcasted_iota(jnp.int32, (1, LANES), 1) < 64


def _to_pair(sa, sb):
    low = _low_half()
    return jnp.where(low, sa, pltpu.roll(sb, 64, 1)), jnp.where(low, pltpu.roll(sa, 64, 1), sb)


def _from_pair(hr, hi):
    low = _low_half()
    return jnp.where(low, hr, pltpu.roll(hi, 64, 1)), jnp.where(low, pltpu.roll(hr, 64, 1), hi)


def _pair_scan(sr, si, pr, pi, reverse, conj):
    n = sr.shape[0]
    row = lax.broadcasted_iota(jnp.int32, sr.shape, 0)

    def shift(x, k):
        if reverse:
            return jnp.where(row < n - k, pltpu.roll(x, n - k, 0), 0.0)
        return jnp.where(row >= k, pltpu.roll(x, k, 0), 0.0)

    hr, hi = shift(sr, 1), shift(si, 1)
    k, j = 1, 0
    while k < n:
        tr, ti = shift(hr, k), shift(hi, k)
        ar = pr[j:j + 1, :]
        ai = -pi[j:j + 1, :] if conj else pi[j:j + 1, :]
        hr, hi = hr + ar * tr - ai * ti, hi + ar * ti + ai * tr
        k, j = 2 * k, j + 1
    return hr, hi


def _transpose8(a):
    a = list(a)
    blk = jnp.right_shift(lax.broadcasted_iota(jnp.int32, (1, LANES), 1), 4)
    for dd in (4, 2, 1):
        upper = jnp.bitwise_and(blk, dd) != 0
        for i in range(8):
            if i & dd:
                continue
            lo, hi = a[i], a[i + dd]
            a[i] = jnp.where(upper, pltpu.roll(hi, dd * SSM_GROUP, 1), lo)
            a[i + dd] = jnp.where(upper, hi, pltpu.roll(lo, LANES - dd * SSM_GROUP, 1))
    return a


def _blocks_from_rows(xs):
    t0, t1 = _transpose8(xs[:8]), _transpose8(xs[8:])
    return [jnp.concatenate([t0[g], t1[g]], axis=1) for g in range(8)]


def _rows_from_blocks(ys):
    return _transpose8([y[:, :LANES] for y in ys]) + _transpose8([y[:, LANES:] for y in ys])


def _pair_states(sa, sb, pr_ref, pi_ref, ga, gb, ncc, adjoint):
    n = sa.shape[0]
    low = _low_half()
    out_a, out_b, pairs = [], [], []
    for dirn in range(2):
        xr, xi = _to_pair(sa[:, dirn * LANES:(dirn + 1) * LANES], sb[:, dirn * LANES:(dirn + 1) * LANES])
        pr = jnp.where(low, pr_ref[dirn, ga], pr_ref[dirn, gb])
        pi = jnp.where(low, -pi_ref[dirn, ga], pi_ref[dirn, gb])
        if dirn == 0:
            hr, hi = _pair_scan(xr, xi, pr, pi, adjoint, adjoint)
        else:
            hr, hi = _pair_scan(pltpu.roll(xr, n - ncc, 0), pltpu.roll(xi, n - ncc, 0), pr, pi, not adjoint, adjoint)
            hr, hi = pltpu.roll(hr, ncc, 0), pltpu.roll(hi, ncc, 0)
        pairs.append((hr, hi))
        ha, hb = _from_pair(hr, hi)
        out_a.append(ha)
        out_b.append(hb)
    return (jnp.concatenate(out_a, axis=1), jnp.concatenate(out_b, axis=1)), pairs


def _s5_op_specs(idx, jl):
    op = pl.BlockSpec((None, 8, 256, 256), lambda a, b: (jl, idx(a, b), 0, 0))
    pw = pl.BlockSpec((None, 2, 8, 16, LANES), lambda a, b: (jl, 0, idx(a, b), 0, 0))
    return [op, op, op, pw, pw]


def _s5_fwd(z, col0, s5_ops, jl, lc, name, xchg=None):
    bsz, t, _ = z.shape
    nc, ncc = t // CHUNK, lc // CHUNK

    def body(u_ref, mt_ref, bt_ref, ct_ref, pr_ref, pi_ref, y_ref):
        us = [_bf(u) for u in _blocks_from_rows([u_ref[pl.ds(tok, nc, stride=CHUNK), :] for tok in range(CHUNK)])]
        ys = []
        for ga in range(0, 8, 2):
            gb = ga + 1
            hs, _ = _pair_states(_nn(us[ga], bt_ref[ga]), _nn(us[gb], bt_ref[gb]), pr_ref, pi_ref, ga, gb, ncc, False)
            for g, h in zip((ga, gb), hs):
                ys.append(_nn(us[g], mt_ref[g]) + _nn(_bf(h), ct_ref[g]))
        for tok, rows in enumerate(_rows_from_blocks(ys)):
            y_ref[pl.ds(tok, nc, stride=CHUNK), :] = rows

    return _call(
        body, name=name, grid=(bsz, 4),
        in_specs=[pl.BlockSpec((None, t, LANES), lambda b, s: (b, 0, col0 + s))] + _s5_op_specs(lambda b, s: s, jl),
        out_specs=[pl.BlockSpec((None, t, LANES), lambda b, s: (b, 0, s))],
        out_shape=[jax.ShapeDtypeStruct((bsz, t, 512), F32)], vmem=VMEM_BIG, xchg=xchg,
        operands=[z, *s5_ops])


def _s5_bwd(z, col0, dy, s5_ops, jl, lc, name, xchg=None):
    bsz, t, _ = z.shape
    nc, ncc = t // CHUNK, lc // CHUNK

    def body(u_ref, dy_ref, mt_ref, bt_ref, ct_ref, pr_ref, pi_ref, du_ref, dmt_ref, dbt_ref, dct_ref, da_ref):
        b = pl.program_id(1)

        @pl.when(b == 0)
        def _():
            dmt_ref[...] = jnp.zeros_like(dmt_ref)
            dbt_ref[...] = jnp.zeros_like(dbt_ref)
            dct_ref[...] = jnp.zeros_like(dct_ref)
            da_ref[...] = jnp.zeros_like(da_ref)
        us = [_bf(u) for u in _blocks_from_rows([u_ref[pl.ds(tok, nc, stride=CHUNK), :] for tok in range(CHUNK)])]
        dys = [_bf(u) for u in _blocks_from_rows([dy_ref[pl.ds(tok, nc, stride=CHUNK), :] for tok in range(CHUNK)])]
        row8 = lax.broadcasted_iota(jnp.int32, (8, LANES), 0)
        dus = []
        for ga in range(0, 8, 2):
            gb = ga + 1
            hs, hp = _pair_states(_nn(us[ga], bt_ref[ga]), _nn(us[gb], bt_ref[gb]), pr_ref, pi_ref, ga, gb, ncc, False)
            gs, gp = _pair_states(_nt(dys[ga], ct_ref[ga]), _nt(dys[gb], ct_ref[gb]), pr_ref, pi_ref, ga, gb, ncc, True)
            for g, h, gg in zip((ga, gb), hs, gs):
                hcat, gcat = _bf(h), _bf(gg)
                dus.append(_nt(dys[g], mt_ref[g]) + _nt(gcat, bt_ref[g]))
                dmt_ref[g] += _tn(us[g], dys[g])
                dct_ref[g] += _tn(hcat, dys[g])
                dbt_ref[g] += _tn(us[g], gcat)
            da = jnp.zeros((8, LANES), F32)
            for dirn in range(2):
                (hr, hi), (gr, gi) = hp[dirn], gp[dirn]
                da = da + jnp.where(row8 == 2 * dirn, jnp.sum(gr * hr + gi * hi, axis=0, keepdims=True), 0.0)
                da = da + jnp.where(row8 == 2 * dirn + 1, jnp.sum(gi * hr - gr * hi, axis=0, keepdims=True), 0.0)
            da_ref[ga // 2] += da
        for tok, rows in enumerate(_rows_from_blocks(dus)):
            du_ref[pl.ds(tok, nc, stride=CHUNK), :] = rows

    op_out = pl.BlockSpec((8, 256, 256), lambda s, b: (s, 0, 0))
    op_shape = jax.ShapeDtypeStruct((SSM_GROUPS, 256, 256), F32)
    return _call(
        body, name=name, grid=(4, bsz),
        in_specs=[pl.BlockSpec((None, t, LANES), lambda s, b: (b, 0, col0 + s)),
                  pl.BlockSpec((None, t, LANES), lambda s, b: (b, 0, s))] + _s5_op_specs(lambda s, b: s, jl),
        out_specs=[pl.BlockSpec((None, t, LANES), lambda s, b: (b, 0, s)), op_out, op_out, op_out,
                   pl.BlockSpec((4, 8, LANES), lambda s, b: (s, 0, 0))],
        out_shape=(jax.ShapeDtypeStruct((bsz, t, 512), F32), op_shape, op_shape, op_shape,
                   jax.ShapeDtypeStruct((SSM_GROUPS // 2, 8, LANES), F32)),
        vmem=VMEM_BIG, xchg=xchg, operands=[z, dy, *s5_ops])


def _glu_in(y, u, dsk):
    return jax.nn.gelu(y + u * dsk)


def _even_mix(oa, ga, zg, tg, gs):
    return jnp.concatenate([oa * jax.nn.silu(ga), zg * jax.nn.sigmoid(tg) * jax.nn.silu(gs)], axis=1)


def _even_specs(d, i, jl, nct):
    return [_tile(d), _mod_spec(i, nct, d), _tile(512), _tile(512)] + [_tile(256, cb) for cb in range(3, 9)] + [
        _layer_spec(jl, 1, 512), _layer_spec(0, 512, 512), _layer_spec(jl, 1, 512), _layer_spec(0, 1024, d)]


def _cat2(a_ref, b_ref):
    return jnp.concatenate([a_ref[...], b_ref[...]], axis=1)


def _even_out_fwd(hs, mods, oa, y, z, dsk, gw, gb, wout, i, jl, nct, name):
    bsz, t, d = hs.shape

    def body(h_ref, mod_ref, oa_ref, y_ref, ga0, ga1, u0, u1, gs0, gs1, dsk_ref, gw_ref, gb_ref, wo_ref, o_ref):
        zg = _glu_in(y_ref[...], _cat2(u0, u1), dsk_ref[...])
        tg = _nn(_bf(zg), gw_ref[...]) + gb_ref[...]
        mix = _even_mix(oa_ref[...], _cat2(ga0, ga1), zg, tg, _cat2(gs0, gs1))
        o_ref[...] = h_ref[...] + mod_ref[2:3, :] * _nn(_bf(mix), wo_ref[...])

    return pl.pallas_call(
        body, name=name, grid=(bsz, t // ROWT), in_specs=_even_specs(d, i, jl, nct),
        out_specs=_tile(d), out_shape=jax.ShapeDtypeStruct((bsz, t, d), F32), compiler_params=_cp(VMEM_BIG),
    )(hs, mods, oa, y, z, z, z, z, z, z, dsk, gw, gb, wout)


def _even_out_bwd(dh, mods, oa, y, z, dsk, gw, gb, wout, i, jl, nct, name):
    bsz, t, d = dh.shape

    def body(dh_ref, mod_ref, oa_ref, y_ref, ga0, ga1, u0, u1, gs0, gs1, dsk_ref, gw_ref, gb_ref, wo_ref,
             doa_ref, dga_ref, dy_ref, dgs_ref, dmod_ref, ddsk_ref, dgw_ref, dgb_ref, dwo_ref):
        b, j = pl.program_id(0), pl.program_id(1)
        zg, vj1 = jax.vjp(_glu_in, y_ref[...], _cat2(u0, u1), dsk_ref[...])
        zgb = _bf(zg)
        tg = _nn(zgb, gw_ref[...]) + gb_ref[...]
        mix, vj2 = jax.vjp(_even_mix, oa_ref[...], _cat2(ga0, ga1), zg, tg, _cat2(gs0, gs1))
        mixb = _bf(mix)
        dhv = dh_ref[...]
        dyo = _bf(dhv * mod_ref[2:3, :])
        yout = _nn(mixb, wo_ref[...])
        doa, dga, dzg, dtg, dgs = vj2(_nt(dyo, wo_ref[...]))
        dtb = _bf(dtg)
        dzg = dzg + _nt(dtb, gw_ref[...])
        dy, _, ddsk = vj1(dzg)
        doa_ref[...], dga_ref[...], dy_ref[...], dgs_ref[...] = doa, dga, dy, dgs

        @pl.when((j == 0) | (j == nct))
        def _():
            dmod_ref[...] = jnp.zeros_like(dmod_ref)

        @pl.when((b == 0) & (j == 0))
        def _():
            ddsk_ref[...] = jnp.zeros_like(ddsk_ref)
            dgw_ref[...] = jnp.zeros_like(dgw_ref)
            dgb_ref[...] = jnp.zeros_like(dgb_ref)
            dwo_ref[...] = jnp.zeros_like(dwo_ref)
        dmod_ref[2:3, :] += jnp.sum(dhv * yout, axis=0, keepdims=True)
        ddsk_ref[...] += ddsk
        dgw_ref[...] += _tn(zgb, dtb)
        dgb_ref[...] += jnp.sum(dtg, axis=0, keepdims=True)
        dwo_ref[...] += _tn(mixb, dyo)

    full = lambda r, c: pl.BlockSpec((r, c), lambda b, j: (0, 0))
    f512 = jax.ShapeDtypeStruct((bsz, t, 512), F32)
    return pl.pallas_call(
        body, name=name, grid=(bsz, t // ROWT), in_specs=_even_specs(d, i, jl, nct),
        out_specs=[_tile(512), _tile(512), _tile(512), _tile(512), _dmod_spec(nct, d),
                   full(1, 512), full(512, 512), full(1, 512), full(1024, d)],
        out_shape=(f512, f512, f512, f512, jax.ShapeDtypeStruct((2 * bsz, 3, d), F32),
                   jax.ShapeDtypeStruct((1, 512), F32), jax.ShapeDtypeStruct((512, 512), F32),
                   jax.ShapeDtypeStruct((1, 512), F32), jax.ShapeDtypeStruct((1024, d), F32)),
        compiler_params=_cp(VMEM_BIG),
    )(dh, mods, oa, y, z, z, z, z, z, z, dsk, gw, gb, wout)


def _pool(src, col_blk, r, lc, transpose, name):
    bsz, t, _ = src.shape
    segs = [(0, lc, 16), (lc, t - lc, 32 + lc)]
    halo = [(0, 16), (16 + lc, 16), (32 + t, 16)]
    cw = 256

    def inv_count(t0, rows, ln):
        pos = t0 + lax.broadcasted_iota(jnp.int32, (rows, cw), 0)
        cnt = jnp.minimum(pos + r + 1, ln) - jnp.maximum(pos - r, 0)
        return 1.0 / cnt.astype(F32)

    def body(x_ref, o_ref, s_ref):
        for h0, hn in halo:
            s_ref[pl.ds(h0, hn), :] = jnp.zeros((hn, cw), F32)
        for r0, ln, s0 in segs:
            step = min(512, ln)
            for c0 in range(0, ln, step):
                v = x_ref[pl.ds(r0 + c0, step), :]
                s_ref[pl.ds(s0 + c0, step), :] = v * inv_count(c0, step, ln) if transpose else v
        for r0, ln, s0 in segs:
            step = min(512, ln)
            for c0 in range(0, ln, step):
                acc = s_ref[pl.ds(s0 + c0 - r, step), :]
                for dlt in range(-r + 1, r + 1):
                    acc = acc + s_ref[pl.ds(s0 + c0 + dlt, step), :]
                ctr = x_ref[pl.ds(r0 + c0, step), :]
                o_ref[pl.ds(r0 + c0, step), :] = (acc if transpose else acc * inv_count(c0, step, ln)) - ctr

    return pl.pallas_call(
        body, name=name, grid=(bsz,),
        in_specs=[pl.BlockSpec((None, t, cw), lambda b: (b, 0, col_blk))],
        out_specs=pl.BlockSpec((None, t, cw), lambda b: (b, 0, 0)),
        out_shape=jax.ShapeDtypeStruct((bsz, t, cw), F32),
        scratch_shapes=[pltpu.VMEM((t + 48, cw), F32)],
        compiler_params=_cp(VMEM_BIG),
    )(src)


def _odd_specs(d, i, jl, nct):
    return [_tile(d), _mod_spec(i, nct, d), _tile(256), _tile(256), _tile(256), _tile(256), _tile(1024, 1),
            _layer_spec(0, 4, 256, 256), _layer_spec(jl, 1, 1024), _layer_spec(0, 1024, d)]


def _odd_mix(pm, psc, gz):
    return pm * psc * jax.nn.silu(gz)


def _odd_out_fwd(hs, mods, ps, z, pw, psc, wout, i, jl, nct, name):
    bsz, t, d = hs.shape

    def body(h_ref, mod_ref, p0, p1, p2, p3, gz_ref, pw_ref, psc_ref, wo_ref, o_ref):
        pm = jnp.concatenate([_nn(_bf(p[...]), pw_ref[q]) for q, p in enumerate((p0, p1, p2, p3))], axis=1)
        mix = _odd_mix(pm, psc_ref[...], gz_ref[...])
        o_ref[...] = h_ref[...] + mod_ref[2:3, :] * _nn(_bf(mix), wo_ref[...])

    return pl.pallas_call(
        body, name=name, grid=(bsz, t // ROWT), in_specs=_odd_specs(d, i, jl, nct),
        out_specs=_tile(d), out_shape=jax.ShapeDtypeStruct((bsz, t, d), F32), compiler_params=_cp(VMEM_BIG),
    )(hs, mods, *ps, z, pw, psc, wout)


def _odd_out_bwd(dh, mods, ps, z, pw, psc, wout, i, jl, nct, name):
    bsz, t, d = dh.shape

    def body(dh_ref, mod_ref, p0, p1, p2, p3, gz_ref, pw_ref, psc_ref, wo_ref,
             dp_ref, dgz_ref, dmod_ref, dpw_ref, dpsc_ref, dwo_ref):
        b, j = pl.program_id(0), pl.program_id(1)
        pbs = [_bf(p[...]) for p in (p0, p1, p2, p3)]
        pm = jnp.concatenate([_nn(pb, pw_ref[q]) for q, pb in enumerate(pbs)], axis=1)
        mix, vj = jax.vjp(_odd_mix, pm, psc_ref[...], gz_ref[...])
        mixb = _bf(mix)
        dhv = dh_ref[...]
        dyo = _bf(dhv * mod_ref[2:3, :])
        yout = _nn(mixb, wo_ref[...])
        dpm, dpsc, dgz = vj(_nt(dyo, wo_ref[...]))
        dgz_ref[...] = dgz
        dpmb = _bf(dpm)
        dp_ref[...] = jnp.concatenate([_nt(dpmb[:, q * 256:(q + 1) * 256], pw_ref[q]) for q in range(4)], axis=1)

        @pl.when((j == 0) | (j == nct))
        def _():
            dmod_ref[...] = jnp.zeros_like(dmod_ref)

        @pl.when((b == 0) & (j == 0))
        def _():
            dpw_ref[...] = jnp.zeros_like(dpw_ref)
            dpsc_ref[...] = jnp.zeros_like(dpsc_ref)
            dwo_ref[...] = jnp.zeros_like(dwo_ref)
        dmod_ref[2:3, :] += jnp.sum(dhv * yout, axis=0, keepdims=True)
        for q in range(4):
            dpw_ref[q] += _tn(pbs[q], dpmb[:, q * 256:(q + 1) * 256])
        dpsc_ref[...] += dpsc
        dwo_ref[...] += _tn(mixb, dyo)

    full = lambda *s: pl.BlockSpec(s, lambda b, j: (0,) * len(s))
    return pl.pallas_call(
        body, name=name, grid=(bsz, t // ROWT), in_specs=_odd_specs(d, i, jl, nct),
        out_specs=[_tile(1024), _tile(1024), _dmod_spec(nct, d), full(4, 256, 256), full(1, 1024), full(1024, d)],
        out_shape=(jax.ShapeDtypeStruct((bsz, t, 1024), F32), jax.ShapeDtypeStruct((bsz, t, 1024), F32),
                   jax.ShapeDtypeStruct((2 * bsz, 3, d), F32), jax.ShapeDtypeStruct((4, 256, 256), F32),
                   jax.ShapeDtypeStruct((1, 1024), F32), jax.ShapeDtypeStruct((1024, d), F32)),
        compiler_params=_cp(VMEM_BIG),
    )(dh, mods, *ps, z, pw, psc, wout)


def _rms(h, g):
    return h * lax.rsqrt(jnp.mean(h * h, axis=-1, keepdims=True) + EPS) * g


def _final_loss(hs, g, target, nct, name):
    bsz, t, d = hs.shape

    def body(h_ref, g_ref, t_ref, dh_ref, loss_ref, dg_ref):
        b, j = pl.program_id(0), pl.program_id(1)
        y, vj = jax.vjp(_rms, h_ref[...], g_ref[...])
        err = jnp.where(j >= nct, y - t_ref[...], 0.0)
        dh, dg = vj(err * (1.0 / d))
        dh_ref[...] = dh

        @pl.when(j == 0)
        def _():
            loss_ref[...] = jnp.zeros_like(loss_ref)

        @pl.when((b == 0) & (j == 0))
        def _():
            dg_ref[...] = jnp.zeros_like(dg_ref)
        loss_ref[...] += 0.5 * jnp.sum(jnp.mean(err * err, axis=-1))
        dg_ref[...] += dg

    return pl.pallas_call(
        body, name=name, grid=(bsz, t // ROWT),
        in_specs=[_tile(d), pl.BlockSpec((1, d), lambda b, j: (0, 0)),
                  pl.BlockSpec((None, ROWT, d), lambda b, j: (b, jnp.maximum(j - nct, 0), 0))],
        out_specs=[_tile(d), pl.BlockSpec((None, 8, LANES), lambda b, j: (b, 0, 0)),
                   pl.BlockSpec((1, d), lambda b, j: (0, 0))],
        out_shape=(jax.ShapeDtypeStruct((bsz, t, d), F32), jax.ShapeDtypeStruct((bsz, 8, LANES), F32),
                   jax.ShapeDtypeStruct((1, d), F32)),
    )(hs, g, target)


def _adamw(w, g_list, m, v, transpose_g, rows, name):
    nl, r, c = w.shape
    ns = g_list[0].shape[1]
    rt = rows or r
    offs = np.cumsum([0] + [g.shape[0] for g in g_list])
    ng = len(g_list)

    def body(*refs):
        w_ref, m_ref, v_ref = refs[0], refs[1 + ng], refs[2 + ng]
        go_ref, d_ref, mo_ref, vo_ref = refs[3 + ng:]
        layer = pl.program_id(0)
        for k in range(ng):
            g_ref = refs[1 + k]

            @pl.when((layer >= offs[k]) & (layer < offs[k + 1]))
            def _():
                g = g_ref[0]
                for s in range(1, ns):
                    g = g + g_ref[s]
                if transpose_g:
                    g = g.T
                mn = ADAM_B1 * m_ref[...] + (1.0 - ADAM_B1) * g
                vn = ADAM_B2 * v_ref[...] + (1.0 - ADAM_B2) * jnp.square(g)
                m_hat = mn / (1.0 - ADAM_B1 ** ADAM_STEP)
                v_hat = vn / (1.0 - ADAM_B2 ** ADAM_STEP)
                go_ref[...] = g
                d_ref[...] = -ADAM_LR * (m_hat / (jnp.sqrt(v_hat) + ADAM_EPS) + ADAM_WD * w_ref[...])
                mo_ref[...] = mn
                vo_ref[...] = vn

    def g_spec(k):
        lo, n_k = int(offs[k]), int(offs[k + 1] - offs[k])

        def idx(l, q):
            mine = (l >= lo) & (l < lo + n_k)
            q = jnp.where(mine, q, 0)
            return (jnp.clip(l - lo, 0, n_k - 1), 0, 0, q) if transpose_g else (jnp.clip(l - lo, 0, n_k - 1), 0, q, 0)
        return pl.BlockSpec((None, ns, c, rt) if transpose_g else (None, ns, rt, c), idx)

    blk = pl.BlockSpec((None, rt, c), lambda l, q: (l, q, 0))
    out = jax.ShapeDtypeStruct((nl, r, c), F32)
    return pl.pallas_call(
        body, name=name, grid=(nl, r // rt),
        in_specs=[blk] + [g_spec(k) for k in range(ng)] + [blk, blk],
        out_specs=[blk, blk, blk, blk], out_shape=(out, out, out, out), compiler_params=_cp(VMEM_BIG),
    )(w, *g_list, m, v)


def _sum_slots(slots, name):
    ns, r, c = slots.shape

    def body(s_ref, o_ref):
        g = s_ref[0]
        for s in range(1, ns):
            g = g + s_ref[s]
        o_ref[...] = g

    return pl.pallas_call(body, name=name, out_shape=jax.ShapeDtypeStruct((r, c), F32), compiler_params=_cp(VMEM_BIG))(slots)


def _pack(arrs):
    flat = jnp.concatenate([a.reshape(-1) for a in arrs])
    return jnp.pad(flat, (0, (-flat.shape[0]) % (PACK_ROWS * PACK_W))).reshape(-1, PACK_W)


def _unpack(buf, shapes):
    flat = buf.reshape(-1)
    out, off = [], 0
    for s in shapes:
        n = int(np.prod(s))
        out.append(flat[off:off + n].reshape(s))
        off += n
    return out


def kernel(x, c, ctx, c_ctx, ada_w, ada_b, norm_g, even_w_in, even_w_out, attn_sink, ssm_a_re, ssm_a_im, ssm_log_dt, ssm_b_re, ssm_b_im, ssm_c_re, ssm_c_im, ssm_d, glu_w, glu_b, odd_w_in, odd_w_out, pool_w, pool_scale, final_g, loss_target, m_c_ctx, m_ada_w, m_ada_b, m_norm_g, m_even_w_in, m_even_w_out, m_attn_sink, m_ssm_a_re, m_ssm_a_im, m_ssm_log_dt, m_ssm_b_re, m_ssm_b_im, m_ssm_c_re, m_ssm_c_im, m_ssm_d, m_glu_w, m_glu_b, m_odd_w_in, m_odd_w_out, m_pool_w, m_pool_scale, m_final_g, v_c_ctx, v_ada_w, v_ada_b, v_norm_g, v_even_w_in, v_even_w_out, v_attn_sink, v_ssm_a_re, v_ssm_a_im, v_ssm_log_dt, v_ssm_b_re, v_ssm_b_im, v_ssm_c_re, v_ssm_c_im, v_ssm_d, v_glu_w, v_glu_b, v_odd_w_in, v_odd_w_out, v_pool_w, v_pool_scale, v_final_g):
    args = dict(locals())
    bsz, l, d = x.shape
    lc = ctx.shape[1]
    t = lc + l
    nct = lc // ROWT
    depth = ada_w.shape[0]
    nl2 = even_w_in.shape[0]
    me = _my_index()
    assert lc % ROWT == 0 and l % ROWT == 0 and d == 1024 and bsz * N_DEV < 32

    npw = pool_w.shape[2]
    shards = {"even": [_bf(jnp.transpose(even_w_in, (0, 2, 1))), _bf(even_w_out), _bf(glu_w)],
              "odd": [_bf(jnp.transpose(odd_w_in, (0, 2, 1))), _bf(odd_w_out), _bf(pool_w.reshape(nl2, -1, pool_w.shape[-1]))]}

    def wgather(kind, jl):
        return shards[kind], [("gather", [(ii, jl)], 0) for ii in range(3)]

    def wjoin(kind, res):
        full = [a.reshape(1, N_DEV * a.shape[2], a.shape[3]) for a in res]
        if kind == "odd":
            full[2] = jnp.transpose(res[2].reshape(1, N_DEV, 4, npw, 256), (0, 2, 1, 3, 4)).reshape(1, 4, 256, 256)
        return full

    xin, xsp = wgather("even", 0)
    got = _exchange(xin + [c, pool_scale], xsp + [("gather", [(3, None)], 0), ("gather", [(4, None)], 0)], "gather_first")
    weights = {("even", 0): wjoin("even", got[:3])}
    c_all = got[3].reshape(N_DEV * bsz, d)
    psc_full = jnp.transpose(got[4][0], (1, 0, 2)).reshape(nl2, 1, d)

    n_rows = 32
    craw = jnp.concatenate([c_all, c_ctx[None], jnp.zeros((n_rows - N_DEV * bsz - 1, d), F32)], axis=0)
    ncol = ada_w.shape[2]
    ada_b_loc = lax.dynamic_slice_in_dim(ada_b, me * ncol, ncol, axis=1)[:, None, :]
    m_loc = _adaln_fwd(craw, ada_w, ada_b_loc)
    m_all = _exchange([m_loc], [("gather", [(0, None)], 0)], "gather_mod")[0][0]
    m_all = jnp.transpose(m_all, (1, 2, 0, 3)).reshape(depth, n_rows, 3 * d)
    lat = lax.dynamic_slice_in_dim(m_all, me * bsz, bsz, axis=1).reshape(depth, bsz, 1, 3, d)
    cmod = jnp.broadcast_to(m_all[:, N_DEV * bsz].reshape(depth, 1, 1, 3, d), (depth, bsz, 1, 3, d))
    mods = jnp.concatenate([cmod, lat], axis=2).reshape(depth, 2 * bsz, 3, d)

    cos, sin = _rope_tables(lc, l)
    hs = jnp.concatenate([ctx, x], axis=1)
    g3 = norm_g[:, None, :]
    dsk3, gb3 = ssm_d[:, None, :], glu_b[:, None, :]
    u_col = 1280 // LANES

    s5_prm = (ssm_a_re, ssm_a_im, ssm_log_dt, ssm_b_re, ssm_b_im, ssm_c_re, ssm_c_im)
    ops = jax.vmap(_s5_operators)(*s5_prm)
    pr, pi = jax.vmap(lambda r, q: _scan_powers(r, q, t // CHUNK))(ops[3], ops[4])
    s5_ops = (_bf(ops[0]), _bf(ops[1]), _bf(ops[2]), pr, pi)
    saved = []
    for i in range(depth):
        jl = i // 2
        if i % 2 == 0:
            wt, wo, gwf = weights[("even", jl)]
            z = _fused_in_fwd(hs, mods, g3, wt, i, 0, nct, f"in_fwd{i}")
            qp, kvp = _rope_fwd(z, cos, sin, 2, f"rope_fwd{i}")
            (oa, lse), got = _attn_fwd(qp, kvp, attn_sink, jl, lc, f"attn_fwd{i}", xchg=wgather("odd", jl))
            weights[("odd", jl)] = wjoin("odd", got)
            nxt = wgather("even", jl + 1) if jl + 1 < nl2 else None
            (y,), got = _s5_fwd(z, u_col, s5_ops, jl, lc, f"s5_fwd{i}", xchg=nxt)
            if nxt:
                weights[("even", jl + 1)] = wjoin("even", got)
            hn = _even_out_fwd(hs, mods, oa, y, z, dsk3, gwf, gb3, wo, i, jl, nct, f"out_fwd{i}")
            saved.append(dict(hs=hs, z=z, qp=qp, kvp=kvp, oa=oa, lse=lse, y=y))
        else:
            wt, wo, pwf = weights[("odd", jl)]
            z = _fused_in_fwd(hs, mods, g3, wt, i, 0, nct, f"in_fwd{i}")
            ps = [_pool(z, gi, wd // 2, lc, False, f"pool_fwd{i}_{gi}") for gi, wd in enumerate(POOL_WINDOWS)]
            hn = _odd_out_fwd(hs, mods, ps, z, pwf, psc_full, wo, i, jl, nct, f"out_fwd{i}")
            saved.append(dict(hs=hs, z=z, ps=ps))
        hs = hn

    dh, loss_p, d_final_g = _final_loss(hs, final_g[None], loss_target, nct, "final_loss")
    loss = lax.psum(jnp.sum(loss_p[:, 0, 0]), ("x", "y", "c"))

    slots = {n: [None] * nl2 for n in ("even_w_in", "even_w_out", "glu_w", "odd_w_in", "odd_w_out", "pool_w", "pool_scale")}
    gsmall = {n: [None] * nl2 for n in ("attn_sink", "ssm_d", "glu_b")}
    d_norm_g, d_mods = [None] * depth, [None] * depth
    d_ops = [None] * nl2

    def rows_xchg(named):
        arrs = [a for _, _, a in named]
        specs = [("rows2" if n == "pool_w" else "rows", [(k, None)], a.shape[1 if n == "pool_w" else 0] // N_DEV)
                 for k, (n, _, a) in enumerate(named)]
        return (arrs, specs)

    def keep(named, res):
        for (n, jl_, _), r_ in zip(named, res):
            slots[n][jl_] = r_

    pending = None
    for i in reversed(range(depth)):
        jl = i // 2
        sv = saved[i]
        if i % 2 == 0:
            wt, wo, gwf = weights[("even", jl)]
            doa, dga, dy, dgs, dmod_g, ddsk, dgw, dgb, dwo = _even_out_bwd(
                dh, mods, sv["oa"], sv["y"], sv["z"], dsk3, gwf, gb3, wo, i, jl, nct, f"out_bwd{i}")
            (dq_r, dkv_acc, dsink), got = _attn_bwd(sv["qp"], sv["kvp"], attn_sink, jl, sv["oa"], sv["lse"], doa, lc,
                                                     f"attn_bwd{i}", xchg=rows_xchg(pending) if pending else None)
            if pending:
                keep(pending, got)
            dq, dkv = _rope_bwd(dq_r, dkv_acc, cos, sin, t, f"rope_bwd{i}")
            mine = [("even_w_out", jl, dwo), ("glu_w", jl, dgw)]
            (du, dmt, dbt, dct, dav), got = _s5_bwd(sv["z"], u_col, dy, s5_ops, jl, lc, f"s5_bwd{i}", xchg=rows_xchg(mine))
            keep(mine, got)
            per_group = lambda q: dav[:, q].reshape(SSM_GROUPS, SSM_STATE)
            d_ops[jl] = (dmt, dbt, dct, jnp.stack([per_group(0), per_group(2)]), jnp.stack([per_group(1), per_group(3)]))
            gsmall["attn_sink"][jl] = jnp.sum(dsink[:, :, 0], axis=0)
            gsmall["ssm_d"][jl] = ddsk[0]
            gsmall["glu_b"][jl] = dgb[0]
            pieces = [(dq, 512, 0), (dkv, 256, 0), (dga, 512, 0), (du, 512, 0), (dy, 512, 0), (dgs, 512, 0)]
            asm = lambda p, s: jnp.concatenate([p[0], p[1], p[2], p[3] + p[4] * s, p[5]], axis=1)
            (dh, dmod_in, dg, dw), _ = _fused_in_bwd(sv["hs"], mods, g3, wt, dh, pieces, (dsk3, jl), asm, i, 0, nct, f"in_bwd{i}")
            pending = [("even_w_in", jl, dw)]
        else:
            wt, wo, pwf = weights[("odd", jl)]
            dp, dgz, dmod_g, dpw, dpsc, dwo = _odd_out_bwd(
                dh, mods, sv["ps"], sv["z"], pwf, psc_full, wo, i, jl, nct, f"out_bwd{i}")
            dus = [_pool(dp, gi, wd // 2, lc, True, f"pool_bwd{i}_{gi}") for gi, wd in enumerate(POOL_WINDOWS)]
            dpsc8 = jnp.broadcast_to(dpsc.reshape(N_DEV, 1, -1), (N_DEV, 8, d // N_DEV)).reshape(N_DEV * 8, -1)
            pieces = [(u_, 256, 0) for u_ in dus] + [(dgz, 1024, 0)]
            asm = lambda p, s: jnp.concatenate(p, axis=1)
            (dh, dmod_in, dg, dw), got = _fused_in_bwd(sv["hs"], mods, g3, wt, dh, pieces, None, asm, i, 0, nct, f"in_bwd{i}",
                                                       xchg=rows_xchg(pending) if pending else None)
            if pending:
                keep(pending, got)
            pending = [("odd_w_in", jl, dw), ("odd_w_out", jl, dwo), ("pool_w", jl, dpw), ("pool_scale", jl, dpsc8)]
        d_norm_g[i] = dg[0]
        d_mods[i] = (dmod_g + dmod_in).reshape(bsz, 2, 3 * d)
    grad_x = dh[:, lc:]

    _, op_vjp = jax.vjp(jax.vmap(_s5_operators), *s5_prm)
    dprm = op_vjp(tuple(jnp.stack([d_ops[jl][q] for jl in range(nl2)]) for q in range(5)))

    dmd = jnp.stack(d_mods)
    dm_loc = jnp.concatenate([dmd[:, :, 1], jnp.sum(dmd[:, :, 0], axis=1, keepdims=True)], axis=1)
    dm_all = _exchange([dm_loc], [("gather", [(0, None)], 0)], "gather_dmod")[0][0]
    dm_lat = jnp.transpose(dm_all[:, :, :bsz], (1, 0, 2, 3)).reshape(depth, N_DEV * bsz, 3 * d)
    dm_ctx = dm_all[0, :, bsz]
    for e in range(1, N_DEV):
        dm_ctx = dm_ctx + dm_all[e, :, bsz]
    dm_rows = jnp.concatenate([dm_lat, dm_ctx[:, None], jnp.zeros((depth, n_rows - N_DEV * bsz - 1, 3 * d), F32)], axis=1)
    dm_cols = lax.dynamic_slice_in_dim(dm_rows, me * ncol, ncol, axis=2)
    g_ada_w, dcraw = _adaln_bwd(craw, dm_cols, ada_w)

    small = {"c_ctx": dcraw[N_DEV * bsz], "ada_b": jnp.sum(dm_loc, axis=1), "norm_g": jnp.stack(d_norm_g),
             "final_g": d_final_g[0]}
    for n, v2 in gsmall.items():
        small[n] = jnp.stack(v2)
    for n, gval in zip(("ssm_a_re", "ssm_a_im", "ssm_log_dt", "ssm_b_re", "ssm_b_im", "ssm_c_re", "ssm_c_im"), dprm):
        small[n] = gval
    snames = list(small)
    sshapes = [args[n].shape for n in snames]
    spack = _pack([small[n].reshape(args[n].shape) for n in snames])

    last = pending + [("small", 0, spack)]
    red = _exchange(*rows_xchg(last), "reduce_last")
    keep(pending, red[:-1])
    s_sum = _sum_slots(red[-1][0], "sum_small")
    s_all = _exchange([s_sum], [("gather", [(0, None)], 0)], "gather_small")[0]
    s_all = s_all.reshape(1, 1, -1, PACK_W)

    out = {}

    def put(n, res, shape):
        for kind, buf in zip(("grad_", "delta_", "new_m_", "new_v_"), res):
            out[kind + n] = buf.reshape(shape)

    def as3(a):
        return a.reshape(a.shape[0], -1, a.shape[-1])

    for n in slots:
        w = args[n]
        g_list = [s_[:, :, :1] if n == "pool_scale" else s_ for s_ in slots[n]]
        g_list = [s_.reshape(1, N_DEV, -1, s_.shape[-1]) for s_ in g_list]
        tr = n in ("even_w_in", "odd_w_in")
        put(n, _adamw(as3(w), g_list, as3(args["m_" + n]), as3(args["v_" + n]), tr, 256 if tr else None, "adamw_" + n),
            w.shape)
    put("ada_w", _adamw(ada_w, [g_ada_w[:, None]], m_ada_w, v_ada_w, False, None, "adamw_ada_w"), ada_w.shape)
    res = _adamw(_pack([args[n] for n in snames])[None], [s_all], _pack([args["m_" + n] for n in snames])[None],
                 _pack([args["v_" + n] for n in snames])[None], False, PACK_ROWS, "adamw_small")
    for kind, buf in zip(("grad_", "delta_", "new_m_", "new_v_"), res):
        for n, a in zip(snames, _unpack(buf, sshapes)):
            out[kind + n] = a

    order = ["c_ctx", "ada_w", "ada_b", "norm_g", "even_w_in", "even_w_out", "attn_sink", "ssm_a_re", "ssm_a_im",
             "ssm_log_dt", "ssm_b_re", "ssm_b_im", "ssm_c_re", "ssm_c_im", "ssm_d", "glu_w", "glu_b", "odd_w_in",
             "odd_w_out", "pool_w", "pool_scale", "final_g"]
    return (loss, grad_x, *[out[k + n] for k in ("grad_", "delta_", "new_m_", "new_v_") for n in order])
```

```python
import functools

import numpy as np
import jax
import jax.numpy as jnp
from jax import lax
from jax.experimental import pallas as pl
from jax.experimental.pallas import tpu as pltpu

F32 = jnp.float32
BF16 = jnp.bfloat16
EPS = 1e-6
NEG_INF = -1e30
N_DEV = 8
GRID_W = 64
WINDOW = 128
QBLK = 128
ROWT = 256
CHUNK = 16
SSM_GROUPS = 32
SSM_GROUP = 16
SSM_STATE = 64
POOL_WINDOWS = (2, 4, 8, 16)
ROPE_BASE = 10000.0
ADAM_LR, ADAM_B1, ADAM_B2, ADAM_EPS, ADAM_WD, ADAM_STEP = 0.001, 0.9, 0.999, 1e-08, 0.01, 10
LANES = 128
PACK_W = 1024
PACK_ROWS = 64
VMEM_BIG = 56 << 20


def _cp(vmem=None):
    return pltpu.CompilerParams(vmem_limit_bytes=vmem) if vmem else pltpu.CompilerParams()


def _nt(a, b):
    return lax.dot_general(a, b, (((1,), (1,)), ((), ())), preferred_element_type=F32)


def _tn(a, b):
    return lax.dot_general(a, b, (((0,), (0,)), ((), ())), preferred_element_type=F32)


def _nn(a, b):
    return jnp.dot(a, b, preferred_element_type=F32)


def _bf(x):
    return x.astype(BF16)


def _my_index():
    return 4 * lax.axis_index("x") + 2 * lax.axis_index("y") + lax.axis_index("c")


def _xchg_shapes(inputs, specs):
    out_shapes = []
    for kind, parts, r in specs:
        ii, li = parts[0]
        shp = tuple(inputs[ii].shape if li is None else inputs[ii].shape[1:])
        piece = shp if kind == "gather" else ((r,) + shp[1:] if kind == "rows" else (shp[0], r) + shp[2:])
        out_shapes.append(jax.ShapeDtypeStruct((len(parts), N_DEV) + piece, inputs[ii].dtype))
    return out_shapes, sum(len(parts) for _, parts, _ in specs)


def _xchg_sems(n_parts):
    return [pltpu.SemaphoreType.DMA((n_parts * (N_DEV - 1),)), pltpu.SemaphoreType.DMA((n_parts * (N_DEV - 1),)),
            pltpu.SemaphoreType.DMA((n_parts,))]


def _xchg_copies(specs, in_refs, out_refs, send_sems, recv_sems, loc_sems):
    x, y, c = lax.axis_index("x"), lax.axis_index("y"), lax.axis_index("c")
    me = 4 * x + 2 * y + c

    def piece(ref, kind, r, p):
        if kind == "gather":
            return ref
        if kind == "rows":
            return ref.at[pl.ds(p * r, r)]
        return ref.at[:, pl.ds(p * r, r)]

    copies, q = [], 0
    for si, (kind, parts, r) in enumerate(specs):
        for pi, (ii, li) in enumerate(parts):
            src = in_refs[ii] if li is None else in_refs[ii].at[li]
            dst = out_refs[si].at[pi, me]
            copies.append(pltpu.make_async_copy(piece(src, kind, r, me), dst, loc_sems.at[q]))
            for k in range(1, N_DEV):
                px = 1 - x if k & 4 else x
                py = 1 - y if k & 2 else y
                pc = 1 - c if k & 1 else c
                copies.append(pltpu.make_async_remote_copy(
                    src_ref=piece(src, kind, r, 4 * px + 2 * py + pc), dst_ref=dst,
                    send_sem=send_sems.at[q * (N_DEV - 1) + k - 1], recv_sem=recv_sems.at[q * (N_DEV - 1) + k - 1],
                    device_id=(px, py, pc), device_id_type=pl.DeviceIdType.MESH))
            q += 1
    return copies


def _exchange(inputs, specs, name):
    out_shapes, n_parts = _xchg_shapes(inputs, specs)
    ni, ns = len(inputs), len(specs)

    def body(*refs):
        copies = _xchg_copies(specs, refs[:ni], refs[ni:ni + ns], *refs[ni + ns:])
        for cp in copies:
            cp.start()
        for cp in copies:
            cp.wait()

    return pl.pallas_call(
        body, name=name, out_shape=tuple(out_shapes),
        in_specs=[pl.BlockSpec(memory_space=pl.ANY)] * ni,
        out_specs=tuple(pl.BlockSpec(memory_space=pl.ANY) for _ in specs),
        scratch_shapes=_xchg_sems(n_parts),
        compiler_params=pltpu.CompilerParams(has_side_effects=True),
    )(*inputs)


def _call(body, *, name, grid, in_specs, out_specs, out_shape, operands, vmem=None, scratch=(), xchg=None):
    out_shape, out_specs = tuple(out_shape), list(out_specs)
    if xchg is None:
        res = pl.pallas_call(body, name=name, grid=grid, in_specs=in_specs, out_specs=out_specs, out_shape=out_shape,
                             scratch_shapes=list(scratch), compiler_params=_cp(vmem))(*operands)
        return list(res), None
    xin, xspecs = xchg
    xshapes, n_parts = _xchg_shapes(xin, xspecs)
    ni, no, nxi, nxo, nsc = len(operands), len(out_shape), len(xin), len(xspecs), len(scratch)
    anyspec = pl.BlockSpec(memory_space=pl.ANY)

    def hosted(*refs):
        ins, xins = refs[:ni], refs[ni:ni + nxi]
        outs, xouts = refs[ni + nxi:ni + nxi + no], refs[ni + nxi + no:ni + nxi + no + nxo]
        scr, sems = refs[ni + nxi + no + nxo:ni + nxi + no + nxo + nsc], refs[ni + nxi + no + nxo + nsc:]
        copies = _xchg_copies(xspecs, xins, xouts, *sems)
        first, last = True, True
        for ax, n in enumerate(grid):
            first = first & (pl.program_id(ax) == 0)
            last = last & (pl.program_id(ax) == n - 1)

        @pl.when(first)
        def _():
            for cp in copies:
                cp.start()
        body(*ins, *outs, *scr)

        @pl.when(last)
        def _():
            for cp in copies:
                cp.wait()

    res = pl.pallas_call(
        hosted, name=name, grid=grid, in_specs=list(in_specs) + [anyspec] * nxi,
        out_specs=out_specs + [anyspec] * nxo, out_shape=out_shape + tuple(xshapes),
        scratch_shapes=list(scratch) + _xchg_sems(n_parts), compiler_params=_cp(vmem))(*operands, *xin)
    return list(res[:no]), list(res[no:])


def _adaln_fwd(craw, ada_w, ada_b):
    nl, d, n = ada_w.shape
    r = craw.shape[0]

    def body(c_ref, w_ref, b_ref, o_ref):
        s = jax.nn.silu(c_ref[...])
        o_ref[...] = _nn(_bf(s), _bf(w_ref[...])) + b_ref[...]

    return pl.pallas_call(
        body, name="adaln_fwd", grid=(nl,),
        in_specs=[pl.BlockSpec((r, d), lambda i: (0, 0)), pl.BlockSpec((None, d, n), lambda i: (i, 0, 0)),
                  pl.BlockSpec((None, 1, n), lambda i: (i, 0, 0))],
        out_specs=pl.BlockSpec((None, r, n), lambda i: (i, 0, 0)),
        out_shape=jax.ShapeDtypeStruct((nl, r, n), F32),
    )(craw, ada_w, ada_b)


def _adaln_bwd(craw, dm, ada_w):
    nl, d, n = ada_w.shape
    r = craw.shape[0]

    def body(c_ref, dm_ref, w_ref, gw_ref, dc_ref, acc_ref):
        i = pl.program_id(0)
        s, vj = jax.vjp(jax.nn.silu, c_ref[...])
        dmb = _bf(dm_ref[...])
        gw_ref[...] = _tn(_bf(s), dmb)

        @pl.when(i == 0)
        def _():
            acc_ref[...] = jnp.zeros_like(acc_ref)
        acc_ref[...] += _nt(dmb, _bf(w_ref[...]))

        @pl.when(i == nl - 1)
        def _():
            dc_ref[...] = vj(acc_ref[...])[0]

    return pl.pallas_call(
        body, name="adaln_bwd", grid=(nl,),
        in_specs=[pl.BlockSpec((r, d), lambda i: (0, 0)), pl.BlockSpec((None, r, n), lambda i: (i, 0, 0)),
                  pl.BlockSpec((None, d, n), lambda i: (i, 0, 0))],
        out_specs=[pl.BlockSpec((None, d, n), lambda i: (i, 0, 0)), pl.BlockSpec((r, d), lambda i: (0, 0))],
        out_shape=(jax.ShapeDtypeStruct((nl, d, n), F32), jax.ShapeDtypeStruct((r, d), F32)),
        scratch_shapes=[pltpu.VMEM((r, d), F32)],
    )(craw, dm, ada_w)


def _norm_mod(h, mod, g):
    ms = jnp.mean(h * h, axis=-1, keepdims=True)
    y = h * lax.rsqrt(ms + EPS) * g
    return y * (1.0 + mod[1:2]) + mod[0:1]


def _mod_spec(i, nct, d):
    return pl.BlockSpec((None, None, 3, d), lambda b, j: (i, 2 * b + jnp.where(j >= nct, 1, 0), 0, 0))


def _dmod_spec(nct, d):
    return pl.BlockSpec((None, 3, d), lambda b, j: (2 * b + jnp.where(j >= nct, 1, 0), 0, 0))


def _layer_spec(li, *shape):
    return pl.BlockSpec((None,) + tuple(shape), lambda b, j: (li,) + (0,) * len(shape))


def _tile(width, col_blk=0):
    return pl.BlockSpec((None, ROWT, width), lambda b, j: (b, j, col_blk))


def _fused_in_fwd(hs, mods, g, wt, i, jl, nct, name):
    bsz, t, d = hs.shape
    n = wt.shape[1]

    def body(h_ref, mod_ref, g_ref, w_ref, z_ref):
        a = _norm_mod(h_ref[...], mod_ref[...], g_ref[...])
        z_ref[...] = _nt(_bf(a), w_ref[...])

    return pl.pallas_call(
        body, name=name, grid=(bsz, t // ROWT),
        in_specs=[_tile(d), _mod_spec(i, nct, d), _layer_spec(i, 1, d), _layer_spec(jl, n, d)],
        out_specs=_tile(n), out_shape=jax.ShapeDtypeStruct((bsz, t, n), F32),
        compiler_params=_cp(VMEM_BIG),
    )(hs, mods, g, wt)


def _pair_accumulate(acc_ref, lhs, rhs, lhs_prev, rhs_prev, step, n_steps):
    odd = jnp.bitwise_and(step, 1) == 1

    @pl.when(odd)
    def _():
        acc_ref[...] += _tn(jnp.concatenate([lhs_prev[...], lhs], axis=0), jnp.concatenate([rhs_prev[...], rhs], axis=0))

    @pl.when(jnp.logical_not(odd))
    def _():
        lhs_prev[...] = lhs
        rhs_prev[...] = rhs

    if n_steps % 2:
        @pl.when(step == n_steps - 1)
        def _():
            acc_ref[...] += _tn(lhs, rhs)


def _fused_in_bwd(hs, mods, g, wt, dh_out, pieces, small, assemble, i, jl, nct, name, xchg=None):
    bsz, t, d = hs.shape
    n = wt.shape[1]
    nt = t // ROWT
    npc, nsm = len(pieces), int(small is not None)

    def body(*refs):
        h_ref, mod_ref, g_ref, w_ref, dho_ref = refs[:5]
        p_refs = refs[5:5 + npc]
        s_val = refs[5 + npc][...] if nsm else None
        dhi_ref, dmod_ref, dg_ref, dw_ref, dz_prev, a_prev = refs[5 + npc + nsm:]
        b, j = pl.program_id(0), pl.program_id(1)
        a, vj = jax.vjp(_norm_mod, h_ref[...], mod_ref[...], g_ref[...])
        dz = jnp.concatenate([_bf(p) for p in assemble([r[...] for r in p_refs], s_val)], axis=1)
        dh, dmod, dg = vj(_nn(dz, w_ref[...]))
        dhi_ref[...] = dho_ref[...] + dh

        @pl.when((j == 0) | (j == nct))
        def _():
            dmod_ref[...] = jnp.zeros_like(dmod_ref)

        @pl.when((b == 0) & (j == 0))
        def _():
            dg_ref[...] = jnp.zeros_like(dg_ref)
            dw_ref[...] = jnp.zeros_like(dw_ref)
        dmod_ref[...] += dmod
        dg_ref[...] += dg
        _pair_accumulate(dw_ref, dz, _bf(a), dz_prev, a_prev, b * nt + j, bsz * nt)

    full = lambda r, c: pl.BlockSpec((r, c), lambda b, j: (0, 0))
    return _call(
        body, name=name, grid=(bsz, t // ROWT),
        in_specs=[_tile(d), _mod_spec(i, nct, d), _layer_spec(i, 1, d), _layer_spec(jl, n, d), _tile(d)]
                 + [_tile(wd, cb) for _, wd, cb in pieces]
                 + ([_layer_spec(small[1], 1, small[0].shape[2])] if nsm else []),
        out_specs=[_tile(d), _dmod_spec(nct, d), full(1, d), full(n, d)],
        out_shape=(jax.ShapeDtypeStruct((bsz, t, d), F32), jax.ShapeDtypeStruct((2 * bsz, 3, d), F32),
                   jax.ShapeDtypeStruct((1, d), F32), jax.ShapeDtypeStruct((n, d), F32)),
        vmem=VMEM_BIG, xchg=xchg, scratch=[pltpu.VMEM((ROWT, n), BF16), pltpu.VMEM((ROWT, d), BF16)],
        operands=[hs, mods, g, wt, dh_out, *[p for p, _, _ in pieces], *([small[0]] if nsm else [])])


def _rope_tables(lc, l):
    tpos = np.arange(l)
    row = (tpos // GRID_W).astype(np.float32)
    col = (tpos % GRID_W).astype(np.float32)
    nf = 16
    inv = (np.float32(ROPE_BASE) ** (-np.arange(nf, dtype=np.float32) / np.float32(nf))).astype(np.float32)
    ang_r = row[:, None] * inv[None]
    ang_c = col[:, None] * inv[None]
    cos64 = np.concatenate([np.cos(ang_r), np.cos(ang_r), np.cos(ang_c), np.cos(ang_c)], axis=1)
    sin64 = np.concatenate([-np.sin(ang_r), np.sin(ang_r), -np.sin(ang_c), np.sin(ang_c)], axis=1)
    cos = np.concatenate([np.ones((lc, 128), np.float32), np.tile(cos64, (1, 2)).astype(np.float32)], axis=0)
    sin = np.concatenate([np.zeros((lc, 128), np.float32), np.tile(sin64, (1, 2)).astype(np.float32)], axis=0)
    return jnp.asarray(cos), jnp.asarray(sin)


def _rot(x, cos, sin):
    lane = lax.broadcasted_iota(jnp.int32, x.shape, 1)
    first = jnp.bitwise_and(jnp.right_shift(lane, 4), 1) == 0
    partner = jnp.where(first, pltpu.roll(x, LANES - 16, 1), pltpu.roll(x, 16, 1))
    return x * cos + partner * sin


def _split_heads(x):
    low = lax.broadcasted_iota(jnp.int32, x.shape, 1) < 64
    return jnp.where(low, x, 0.0), jnp.where(low, pltpu.roll(x, 64, 1), 0.0)


def _merge_heads(a, b):
    return a + pltpu.roll(b, 64, 1)


def _rope_fwd(z, cos, sin, kv_blk, name):
    bsz, t, _ = z.shape

    def body(q_ref, kv_ref, cos_ref, sin_ref, qp_ref, kvp_ref):
        c, s = cos_ref[...], sin_ref[...]
        outs = []
        for sl in range(4):
            xr = _rot(q_ref[:, sl * LANES:(sl + 1) * LANES], c, s) * 0.125
            outs += list(_split_heads(xr))
        qp_ref[...] = _bf(jnp.concatenate(outs, axis=1))
        k0, k1 = _split_heads(_rot(kv_ref[:, 0:LANES], c, s))
        v0, v1 = _split_heads(kv_ref[:, LANES:2 * LANES])
        kvp_ref[...] = _bf(jnp.concatenate([k0, k1, v0, v1], axis=1))

    return pl.pallas_call(
        body, name=name, grid=(bsz, t // ROWT),
        in_specs=[_tile(512), _tile(256, kv_blk),
                  pl.BlockSpec((ROWT, LANES), lambda b, j: (j, 0)), pl.BlockSpec((ROWT, LANES), lambda b, j: (j, 0))],
        out_specs=[_tile(1024), _tile(512)],
        out_shape=(jax.ShapeDtypeStruct((bsz, t, 1024), BF16), jax.ShapeDtypeStruct((bsz, t, 512), BF16)),
    )(z, z, cos, sin)


def _rope_bwd(dq_r, dkv_acc, cos, sin, t, name):
    bsz = dq_r.shape[0]

    def body(dq_ref, acc_ref, cos_ref, sin_ref, dqo_ref, dkvo_ref):
        c, s = cos_ref[...], -sin_ref[...]
        dqo_ref[...] = jnp.concatenate(
            [_rot(dq_ref[:, sl * LANES:(sl + 1) * LANES], c, s) for sl in range(4)], axis=1)
        dk = _merge_heads(acc_ref[:, 0:LANES], acc_ref[:, LANES:2 * LANES])
        dv = _merge_heads(acc_ref[:, 2 * LANES:3 * LANES], acc_ref[:, 3 * LANES:4 * LANES])
        dkvo_ref[...] = jnp.concatenate([_rot(dk, c, s), dv], axis=1)

    return pl.pallas_call(
        body, name=name, grid=(bsz, t // ROWT),
        in_specs=[_tile(512), _tile(512),
                  pl.BlockSpec((ROWT, LANES), lambda b, j: (j, 0)), pl.BlockSpec((ROWT, LANES), lambda b, j: (j, 0))],
        out_specs=[_tile(512), _tile(256)],
        out_shape=(jax.ShapeDtypeStruct((bsz, t, 512), F32), jax.ShapeDtypeStruct((bsz, t, 256), F32)),
    )(dq_r, dkv_acc, cos, sin)


def _attn_bias(j, ncb, l):
    n = j - ncb
    r = lax.broadcasted_iota(jnp.int32, (QBLK, 3 * QBLK), 0)
    cidx = lax.broadcasted_iota(jnp.int32, (QBLK, 3 * QBLK), 1)
    kpos = (n - 1) * QBLK + cidx
    valid = (jnp.abs(r - cidx + QBLK) <= WINDOW) & (kpos >= 0) & (kpos < l) & (j >= ncb)
    return jnp.where(valid, 0.0, NEG_INF).astype(F32)


def _attn_bias4(j, ncb, l):
    b1 = _attn_bias(j, ncb, l)
    return jnp.concatenate([b1, b1, b1, b1], axis=0)


def _sink_rows(sink_ref, jl, hk):
    head = jnp.right_shift(lax.broadcasted_iota(jnp.int32, (4 * QBLK, 1), 0), 7)
    sk = jnp.zeros((4 * QBLK, 1), F32)
    for g in range(4):
        sk = jnp.where(head == g, sink_ref[jl, 4 * hk + g], sk)
    return sk


def _kv_specs(nb, lc):
    return [pl.BlockSpec((None, QBLK, 512), lambda b, j: (b, jnp.maximum(j - 1, 0), 0)),
            pl.BlockSpec((None, QBLK, 512), lambda b, j: (b, j, 0)),
            pl.BlockSpec((None, QBLK, 512), lambda b, j: (b, jnp.minimum(j + 1, nb - 1), 0)),
            pl.BlockSpec((None, lc, 512), lambda b, j: (b, 0, 0))]


def _lane_pick(x, h):
    lane = lax.broadcasted_iota(jnp.int32, x.shape, 1)
    return jnp.sum(jnp.where(lane == h, x, 0.0), axis=1, keepdims=True)


def _attn_fwd(qp, kvp, sinks, jl, lc, name, xchg=None):
    bsz, t, _ = qp.shape
    nb, ncb, l = t // QBLK, lc // QBLK, t - lc

    def body(sink_ref, q_ref, kp_ref, kc_ref, kn_ref, kx_ref, o_ref, lse_ref):
        j = pl.program_id(1)
        kloc = jnp.concatenate([kp_ref[...], kc_ref[...], kn_ref[...]], axis=0)
        kctx = kx_ref[...]
        bias = _attn_bias4(j, ncb, l)
        lane = lax.broadcasted_iota(jnp.int32, (QBLK, LANES), 1)
        lse_all = jnp.zeros((QBLK, LANES), F32)
        outs = []
        for hk in range(2):
            ks, vs = slice(hk * LANES, (hk + 1) * LANES), slice((2 + hk) * LANES, (3 + hk) * LANES)
            q4 = jnp.concatenate([q_ref[:, (4 * hk + g) * LANES:(4 * hk + g + 1) * LANES] for g in range(4)], axis=0)
            s_loc = _nt(q4, kloc[:, ks]) + bias
            s_ctx = _nt(q4, kctx[:, ks])
            sk = _sink_rows(sink_ref, jl, hk)
            m = jnp.maximum(jnp.maximum(jnp.max(s_loc, axis=1, keepdims=True), jnp.max(s_ctx, axis=1, keepdims=True)), sk)
            p_loc, p_ctx = jnp.exp(s_loc - m), jnp.exp(s_ctx - m)
            den = jnp.sum(p_loc, axis=1, keepdims=True) + jnp.sum(p_ctx, axis=1, keepdims=True) + jnp.exp(sk - m)
            o4 = (_nn(_bf(p_loc), kloc[:, vs]) + _nn(_bf(p_ctx), kctx[:, vs])) / den
            lse4 = m + jnp.log(den)
            for g in range(4):
                outs.append(o4[g * QBLK:(g + 1) * QBLK])
                lse_all = lse_all + jnp.where(lane == 4 * hk + g, lse4[g * QBLK:(g + 1) * QBLK], 0.0)
        o_ref[...] = jnp.concatenate([_merge_heads(outs[2 * i], outs[2 * i + 1]) for i in range(4)], axis=1)
        lse_ref[...] = lse_all

    return _call(
        body, name=name, grid=(bsz, nb),
        in_specs=[pl.BlockSpec(memory_space=pltpu.SMEM),
                  pl.BlockSpec((None, QBLK, 1024), lambda b, j: (b, j, 0))] + _kv_specs(nb, lc),
        out_specs=[pl.BlockSpec((None, QBLK, 512), lambda b, j: (b, j, 0)),
                   pl.BlockSpec((None, QBLK, LANES), lambda b, j: (b, j, 0))],
        out_shape=(jax.ShapeDtypeStruct((bsz, t, 512), F32), jax.ShapeDtypeStruct((bsz, t, LANES), F32)),
        xchg=xchg, operands=[sinks, qp, kvp, kvp, kvp, kvp])


def _attn_bwd(qp, kvp, sinks, jl, o, lse, do, lc, name, xchg=None):
    bsz, t, _ = qp.shape
    nb, ncb, l = t // QBLK, lc // QBLK, t - lc
    nk = 3 * QBLK

    def body(sink_ref, q_ref, kp_ref, kc_ref, kn_ref, kx_ref, o_ref, lse_ref, do_ref, dq_ref, acc_ref, ds_ref):
        j = pl.program_id(1)

        @pl.when(j == 0)
        def _():
            acc_ref[...] = jnp.zeros_like(acc_ref)
            ds_ref[...] = jnp.zeros_like(ds_ref)
        kloc = jnp.concatenate([kp_ref[...], kc_ref[...], kn_ref[...]], axis=0)
        kctx = kx_ref[...]
        bias = _attn_bias4(j, ncb, l)
        lse = lse_ref[...]
        row8 = lax.broadcasted_iota(jnp.int32, (8, LANES), 0)
        dsink = jnp.zeros((8, LANES), F32)
        dqs = []
        d_loc, d_ctx = [None] * 4, [None] * 4
        for hk in range(2):
            ks, vs = slice(hk * LANES, (hk + 1) * LANES), slice((2 + hk) * LANES, (3 + hk) * LANES)
            q4 = jnp.concatenate([q_ref[:, (4 * hk + g) * LANES:(4 * hk + g + 1) * LANES] for g in range(4)], axis=0)
            do4 = jnp.concatenate([x_ for s2 in (2 * hk, 2 * hk + 1)
                                   for x_ in _split_heads(do_ref[:, s2 * LANES:(s2 + 1) * LANES])], axis=0)
            o4 = jnp.concatenate([x_ for s2 in (2 * hk, 2 * hk + 1)
                                  for x_ in _split_heads(o_ref[:, s2 * LANES:(s2 + 1) * LANES])], axis=0)
            delta = jnp.sum(do4 * o4, axis=1, keepdims=True)
            lse4 = jnp.concatenate([_lane_pick(lse, 4 * hk + g) for g in range(4)], axis=0)
            p_loc = jnp.exp(_nt(q4, kloc[:, ks]) + bias - lse4)
            p_ctx = jnp.exp(_nt(q4, kctx[:, ks]) - lse4)
            dob = _bf(do4)
            ds_loc = _bf(p_loc * (_nt(dob, kloc[:, vs]) - delta))
            ds_ctx = _bf(p_ctx * (_nt(dob, kctx[:, vs]) - delta))
            dq4 = (_nn(ds_loc, kloc[:, ks]) + _nn(ds_ctx, kctx[:, ks])) * 0.125
            d_loc[hk], d_ctx[hk] = _tn(ds_loc, q4), _tn(ds_ctx, q4)
            d_loc[2 + hk], d_ctx[2 + hk] = _tn(_bf(p_loc), dob), _tn(_bf(p_ctx), dob)
            dsk = -jnp.exp(_sink_rows(sink_ref, jl, hk) - lse4) * delta
            for g in range(4):
                dqs.append(dq4[g * QBLK:(g + 1) * QBLK])
                dsink = dsink + jnp.where(row8 == 4 * hk + g, jnp.sum(dsk[g * QBLK:(g + 1) * QBLK]), 0.0)
        dq_ref[...] = jnp.concatenate([_merge_heads(dqs[2 * i], dqs[2 * i + 1]) for i in range(4)], axis=1)
        ds_ref[...] += dsink
        acc_ref[pl.ds(0, lc), :] += jnp.concatenate(d_ctx, axis=1)

        @pl.when(j >= ncb)
        def _():
            st = pl.multiple_of((j - 1) * QBLK, QBLK)
            acc_ref[pl.ds(st, nk), :] += jnp.concatenate(d_loc, axis=1)

    blk = lambda wd: pl.BlockSpec((None, QBLK, wd), lambda b, j: (b, j, 0))
    return _call(
        body, name=name, grid=(bsz, nb),
        in_specs=[pl.BlockSpec(memory_space=pltpu.SMEM), blk(1024)] + _kv_specs(nb, lc)
                 + [blk(512), blk(LANES), blk(512)],
        out_specs=[blk(512), pl.BlockSpec((None, t + QBLK, 512), lambda b, j: (b, 0, 0)),
                   pl.BlockSpec((None, 8, LANES), lambda b, j: (b, 0, 0))],
        out_shape=(jax.ShapeDtypeStruct((bsz, t, 512), F32), jax.ShapeDtypeStruct((bsz, t + QBLK, 512), F32),
                   jax.ShapeDtypeStruct((bsz, 8, LANES), F32)),
        vmem=VMEM_BIG, xchg=xchg, operands=[sinks, qp, kvp, kvp, kvp, kvp, o, lse, do])


def _s5_operators(a_re, a_im, log_dt, b_re, b_im, c_re, c_im):
    hi = lax.Precision.HIGHEST
    dt = jnp.exp(log_dt)[..., None]
    tau = jnp.arange(CHUNK + 1, dtype=F32)[:, None, None, None]
    mag = jnp.exp(tau * (a_re * dt))
    pwr, pwi = mag * jnp.cos(tau * (a_im * dt)), mag * jnp.sin(tau * (a_im * dt))
    ar, ai = pwr[1], pwi[1]
    den = a_re * a_re + a_im * a_im
    fr = ((ar - 1.0) * a_re + ai * a_im) / den
    fi = (ai * a_re - (ar - 1.0) * a_im) / den
    bbr = fr[..., None] * b_re - fi[..., None] * b_im
    bbi = fr[..., None] * b_im + fi[..., None] * b_re
    wr = pwr[:CHUNK, ..., None] * bbr - pwi[:CHUNK, ..., None] * bbi
    wi = pwr[:CHUNK, ..., None] * bbi + pwi[:CHUNK, ..., None] * bbr
    kern = (jnp.einsum("dgcp,tdgpk->dgtck", c_re, wr, precision=hi)
            - jnp.einsum("dgcp,tdgpk->dgtck", c_im, wi, precision=hi))
    g = a_re.shape[1]
    nrow = CHUNK * SSM_GROUP
    lag = np.arange(CHUNK)[:, None, None]
    tt, ss = np.arange(CHUNK)[None, :, None], np.arange(CHUNK)[None, None, :]
    sel_f = jnp.asarray((tt - ss == lag).astype(np.float32))
    sel_b = jnp.asarray((ss - tt == lag).astype(np.float32))
    mt = (jnp.einsum("lts,glck->gsktc", sel_f, kern[0], precision=hi)
          + jnp.einsum("lts,glck->gsktc", sel_b, kern[1], precision=hi)).reshape(g, nrow, nrow)
    to_rows = lambda w: jnp.transpose(w, (1, 0, 3, 2)).reshape(g, nrow, SSM_STATE)
    bt = jnp.concatenate([to_rows(wr[::-1, 0]), to_rows(wi[::-1, 0]), to_rows(wr[:, 1]), to_rows(wi[:, 1])], axis=-1)

    def out_map(d, pw_r, pw_i):
        w2r = c_re[d][None] * pw_r[:, :, None, :] - c_im[d][None] * pw_i[:, :, None, :]
        w2i = c_re[d][None] * pw_i[:, :, None, :] + c_im[d][None] * pw_r[:, :, None, :]
        cols = lambda w: jnp.transpose(w, (1, 3, 0, 2)).reshape(g, SSM_STATE, nrow)
        return [cols(w2r), -cols(w2i)]

    ct = jnp.concatenate(out_map(0, pwr[1:, 0], pwi[1:, 0]) + out_map(1, pwr[1:, 1][::-1], pwi[1:, 1][::-1]), axis=-2)
    return mt, bt, ct, pwr[CHUNK], pwi[CHUNK]


def _scan_powers(a16r, a16i, n_chunks):
    prs, pis = [], []
    r, i = a16r, a16i
    s = 1
    while s < n_chunks:
        prs.append(jnp.concatenate([r, r], axis=-1))
        pis.append(jnp.concatenate([-i, i], axis=-1))
        r, i = r * r - i * i, 2.0 * r * i
        s *= 2
    pad = 16 - len(prs)
    z = jnp.zeros_like(prs[0])
    return jnp.stack(prs + [z] * pad, axis=2), jnp.stack(pis + [z] * pad, axis=2)


def _low_half():
    return lax.broadcasted_iota(jnp.int32, (1, LANES), 1) < 64


def _to_pair(sa, sb):
    low = _low_half()
    return jnp.where(low, sa, pltpu.roll(sb, 64, 1)), jnp.where(low, pltpu.roll(sa, 64, 1), sb)


def _from_pair(hr, hi):
    low = _low_half()
    return jnp.where(low, hr, pltpu.roll(hi, 64, 1)), jnp.where(low, pltpu.roll(hr, 64, 1), hi)


def _pair_scan(sr, si, pr, pi, reverse, conj):
    n = sr.shape[0]
    row = lax.broadcasted_iota(jnp.int32, sr.shape, 0)

    def shift(x, k):
        if reverse:
            return jnp.where(row < n - k, pltpu.roll(x, n - k, 0), 0.0)
        return jnp.where(row >= k, pltpu.roll(x, k, 0), 0.0)

    hr, hi = shift(sr, 1), shift(si, 1)
    k, j = 1, 0
    while k < n:
        tr, ti = shift(hr, k), shift(hi, k)
        ar = pr[j:j + 1, :]
        ai = -pi[j:j + 1, :] if conj else pi[j:j + 1, :]
        hr, hi = hr + ar * tr - ai * ti, hi + ar * ti + ai * tr
        k, j = 2 * k, j + 1
    return hr, hi


def _transpose8(a):
    a = list(a)
    blk = jnp.right_shift(lax.broadcasted_iota(jnp.int32, (1, LANES), 1), 4)
    for dd in (4, 2, 1):
        upper = jnp.bitwise_and(blk, dd) != 0
        for i in range(8):
            if i & dd:
                continue
            lo, hi = a[i], a[i + dd]
            a[i] = jnp.where(upper, pltpu.roll(hi, dd * SSM_GROUP, 1), lo)
            a[i + dd] = jnp.where(upper, hi, pltpu.roll(lo, LANES - dd * SSM_GROUP, 1))
    return a


def _blocks_from_rows(xs):
    t0, t1 = _transpose8(xs[:8]), _transpose8(xs[8:])
    return [jnp.concatenate([t0[g], t1[g]], axis=1) for g in range(8)]


def _rows_from_blocks(ys):
    return _transpose8([y[:, :LANES] for y in ys]) + _transpose8([y[:, LANES:] for y in ys])


def _pair_states(sa, sb, pr_ref, pi_ref, ga, gb, ncc, adjoint):
    n = sa.shape[0]
    low = _low_half()
    out_a, out_b, pairs = [], [], []
    for dirn in range(2):
        xr, xi = _to_pair(sa[:, dirn * LANES:(dirn + 1) * LANES], sb[:, dirn * LANES:(dirn + 1) * LANES])
        pr = jnp.where(low, pr_ref[dirn, ga], pr_ref[dirn, gb])
        pi = jnp.where(low, -pi_ref[dirn, ga], pi_ref[dirn, gb])
        if dirn == 0:
            hr, hi = _pair_scan(xr, xi, pr, pi, adjoint, adjoint)
        else:
            hr, hi = _pair_scan(pltpu.roll(xr, n - ncc, 0), pltpu.roll(xi, n - ncc, 0), pr, pi, not adjoint, adjoint)
            hr, hi = pltpu.roll(hr, ncc, 0), pltpu.roll(hi, ncc, 0)
        pairs.append((hr, hi))
        ha, hb = _from_pair(hr, hi)
        out_a.append(ha)
        out_b.append(hb)
    return (jnp.concatenate(out_a, axis=1), jnp.concatenate(out_b, axis=1)), pairs


def _s5_op_specs(idx, jl):
    op = pl.BlockSpec((None, 8, 256, 256), lambda a, b: (jl, idx(a, b), 0, 0))
    pw = pl.BlockSpec((None, 2, 8, 16, LANES), lambda a, b: (jl, 0, idx(a, b), 0, 0))
    return [op, op, op, pw, pw]


def _s5_fwd(z, col0, s5_ops, jl, lc, name, xchg=None):
    bsz, t, _ = z.shape
    nc, ncc = t // CHUNK, lc // CHUNK

    def body(u_ref, mt_ref, bt_ref, ct_ref, pr_ref, pi_ref, y_ref):
        us = [_bf(u) for u in _blocks_from_rows([u_ref[pl.ds(tok, nc, stride=CHUNK), :] for tok in range(CHUNK)])]
        ys = []
        for ga in range(0, 8, 2):
            gb = ga + 1
            hs, _ = _pair_states(_nn(us[ga], bt_ref[ga]), _nn(us[gb], bt_ref[gb]), pr_ref, pi_ref, ga, gb, ncc, False)
            for g, h in zip((ga, gb), hs):
                ys.append(_nn(us[g], mt_ref[g]) + _nn(_bf(h), ct_ref[g]))
        for tok, rows in enumerate(_rows_from_blocks(ys)):
            y_ref[pl.ds(tok, nc, stride=CHUNK), :] = rows

    return _call(
        body, name=name, grid=(bsz, 4),
        in_specs=[pl.BlockSpec((None, t, LANES), lambda b, s: (b, 0, col0 + s))] + _s5_op_specs(lambda b, s: s, jl),
        out_specs=[pl.BlockSpec((None, t, LANES), lambda b, s: (b, 0, s))],
        out_shape=[jax.ShapeDtypeStruct((bsz, t, 512), F32)], vmem=VMEM_BIG, xchg=xchg,
        operands=[z, *s5_ops])


def _s5_bwd(z, col0, dy, s5_ops, jl, lc, name, xchg=None):
    bsz, t, _ = z.shape
    nc, ncc = t // CHUNK, lc // CHUNK

    def body(u_ref, dy_ref, mt_ref, bt_ref, ct_ref, pr_ref, pi_ref, du_ref, dmt_ref, dbt_ref, dct_ref, da_ref):
        b = pl.program_id(1)

        @pl.when(b == 0)
        def _():
            dmt_ref[...] = jnp.zeros_like(dmt_ref)
            dbt_ref[...] = jnp.zeros_like(dbt_ref)
            dct_ref[...] = jnp.zeros_like(dct_ref)
            da_ref[...] = jnp.zeros_like(da_ref)
        us = [_bf(u) for u in _blocks_from_rows([u_ref[pl.ds(tok, nc, stride=CHUNK), :] for tok in range(CHUNK)])]
        dys = [_bf(u) for u in _blocks_from_rows([dy_ref[pl.ds(tok, nc, stride=CHUNK), :] for tok in range(CHUNK)])]
        row8 = lax.broadcasted_iota(jnp.int32, (8, LANES), 0)
        dus = []
        for ga in range(0, 8, 2):
            gb = ga + 1
            hs, hp = _pair_states(_nn(us[ga], bt_ref[ga]), _nn(us[gb], bt_ref[gb]), pr_ref, pi_ref, ga, gb, ncc, False)
            gs, gp = _pair_states(_nt(dys[ga], ct_ref[ga]), _nt(dys[gb], ct_ref[gb]), pr_ref, pi_ref, ga, gb, ncc, True)
            for g, h, gg in zip((ga, gb), hs, gs):
                hcat, gcat = _bf(h), _bf(gg)
                dus.append(_nt(dys[g], mt_ref[g]) + _nt(gcat, bt_ref[g]))
                dmt_ref[g] += _tn(us[g], dys[g])
                dct_ref[g] += _tn(hcat, dys[g])
                dbt_ref[g] += _tn(us[g], gcat)
            da = jnp.zeros((8, LANES), F32)
            for dirn in range(2):
                (hr, hi), (gr, gi) = hp[dirn], gp[dirn]
                da = da + jnp.where(row8 == 2 * dirn, jnp.sum(gr * hr + gi * hi, axis=0, keepdims=True), 0.0)
                da = da + jnp.where(row8 == 2 * dirn + 1, jnp.sum(gi * hr - gr * hi, axis=0, keepdims=True), 0.0)
            da_ref[ga // 2] += da
        for tok, rows in enumerate(_rows_from_blocks(dus)):
            du_ref[pl.ds(tok, nc, stride=CHUNK), :] = rows

    op_out = pl.BlockSpec((8, 256, 256), lambda s, b: (s, 0, 0))
    op_shape = jax.ShapeDtypeStruct((SSM_GROUPS, 256, 256), F32)
    return _call(
        body, name=name, grid=(4, bsz),
        in_specs=[pl.BlockSpec((None, t, LANES), lambda s, b: (b, 0, col0 + s)),
                  pl.BlockSpec((None, t, LANES), lambda s, b: (b, 0, s))] + _s5_op_specs(lambda s, b: s, jl),
        out_specs=[pl.BlockSpec((None, t, LANES), lambda s, b: (b, 0, s)), op_out, op_out, op_out,
                   pl.BlockSpec((4, 8, LANES), lambda s, b: (s, 0, 0))],
        out_shape=(jax.ShapeDtypeStruct((bsz, t, 512), F32), op_shape, op_shape, op_shape,
                   jax.ShapeDtypeStruct((SSM_GROUPS // 2, 8, LANES), F32)),
        vmem=VMEM_BIG, xchg=xchg, operands=[z, dy, *s5_ops])


def _glu_in(y, u, dsk):
    return jax.nn.gelu(y + u * dsk)


def _even_mix(oa, ga, zg, tg, gs):
    return jnp.concatenate([oa * jax.nn.silu(ga), zg * jax.nn.sigmoid(tg) * jax.nn.silu(gs)], axis=1)


def _even_specs(d, i, jl, nct):
    return [_tile(d), _mod_spec(i, nct, d), _tile(512), _tile(512)] + [_tile(256, cb) for cb in range(3, 9)] + [
        _layer_spec(jl, 1, 512), _layer_spec(0, 512, 512), _layer_spec(jl, 1, 512), _layer_spec(0, 1024, d)]


def _cat2(a_ref, b_ref):
    return jnp.concatenate([a_ref[...], b_ref[...]], axis=1)


def _even_out_fwd(hs, mods, oa, y, z, dsk, gw, gb, wout, i, jl, nct, name):
    bsz, t, d = hs.shape

    def body(h_ref, mod_ref, oa_ref, y_ref, ga0, ga1, u0, u1, gs0, gs1, dsk_ref, gw_ref, gb_ref, wo_ref, o_ref):
        zg = _glu_in(y_ref[...], _cat2(u0, u1), dsk_ref[...])
        tg = _nn(_bf(zg), gw_ref[...]) + gb_ref[...]
        mix = _even_mix(oa_ref[...], _cat2(ga0, ga1), zg, tg, _cat2(gs0, gs1))
        o_ref[...] = h_ref[...] + mod_ref[2:3, :] * _nn(_bf(mix), wo_ref[...])

    return pl.pallas_call(
        body, name=name, grid=(bsz, t // ROWT), in_specs=_even_specs(d, i, jl, nct),
        out_specs=_tile(d), out_shape=jax.ShapeDtypeStruct((bsz, t, d), F32), compiler_params=_cp(VMEM_BIG),
    )(hs, mods, oa, y, z, z, z, z, z, z, dsk, gw, gb, wout)


def _even_out_bwd(dh, mods, oa, y, z, dsk, gw, gb, wout, i, jl, nct, name):
    bsz, t, d = dh.shape

    def body(dh_ref, mod_ref, oa_ref, y_ref, ga0, ga1, u0, u1, gs0, gs1, dsk_ref, gw_ref, gb_ref, wo_ref,
             doa_ref, dga_ref, dy_ref, dgs_ref, dmod_ref, ddsk_ref, dgw_ref, dgb_ref, dwo_ref, mix_prev, dyo_prev):
        b, j = pl.program_id(0), pl.program_id(1)
        zg, vj1 = jax.vjp(_glu_in, y_ref[...], _cat2(u0, u1), dsk_ref[...])
        zgb = _bf(zg)
        tg = _nn(zgb, gw_ref[...]) + gb_ref[...]
        mix, vj2 = jax.vjp(_even_mix, oa_ref[...], _cat2(ga0, ga1), zg, tg, _cat2(gs0, gs1))
        mixb = _bf(mix)
        dhv = dh_ref[...]
        dyo = _bf(dhv * mod_ref[2:3, :])
        yout = _nn(mixb, wo_ref[...])
        doa, dga, dzg, dtg, dgs = vj2(_nt(dyo, wo_ref[...]))
        dtb = _bf(dtg)
        dzg = dzg + _nt(dtb, gw_ref[...])
        dy, _, ddsk = vj1(dzg)
        doa_ref[...], dga_ref[...], dy_ref[...], dgs_ref[...] = doa, dga, dy, dgs

        @pl.when((j == 0) | (j == nct))
        def _():
            dmod_ref[...] = jnp.zeros_like(dmod_ref)

        @pl.when((b == 0) & (j == 0))
        def _():
            ddsk_ref[...] = jnp.zeros_like(ddsk_ref)
            dgw_ref[...] = jnp.zeros_like(dgw_ref)
            dgb_ref[...] = jnp.zeros_like(dgb_ref)
            dwo_ref[...] = jnp.zeros_like(dwo_ref)
        dmod_ref[2:3, :] += jnp.sum(dhv * yout, axis=0, keepdims=True)
        ddsk_ref[...] += ddsk
        dgw_ref[...] += _tn(zgb, dtb)
        dgb_ref[...] += jnp.sum(dtg, axis=0, keepdims=True)
        _pair_accumulate(dwo_ref, mixb, dyo, mix_prev, dyo_prev, b * (t // ROWT) + j, bsz * (t // ROWT))

    full = lambda r, c: pl.BlockSpec((r, c), lambda b, j: (0, 0))
    f512 = jax.ShapeDtypeStruct((bsz, t, 512), F32)
    return pl.pallas_call(
        body, name=name, grid=(bsz, t // ROWT), in_specs=_even_specs(d, i, jl, nct),
        out_specs=[_tile(512), _tile(512), _tile(512), _tile(512), _dmod_spec(nct, d),
                   full(1, 512), full(512, 512), full(1, 512), full(1024, d)],
        out_shape=(f512, f512, f512, f512, jax.ShapeDtypeStruct((2 * bsz, 3, d), F32),
                   jax.ShapeDtypeStruct((1, 512), F32), jax.ShapeDtypeStruct((512, 512), F32),
                   jax.ShapeDtypeStruct((1, 512), F32), jax.ShapeDtypeStruct((1024, d), F32)),
        scratch_shapes=[pltpu.VMEM((ROWT, 1024), BF16), pltpu.VMEM((ROWT, d), BF16)],
        compiler_params=_cp(VMEM_BIG),
    )(dh, mods, oa, y, z, z, z, z, z, z, dsk, gw, gb, wout)


def _pool(src, col_blk, r, lc, transpose, name):
    bsz, t, _ = src.shape
    segs = [(0, lc, 16), (lc, t - lc, 32 + lc)]
    halo = [(0, 16), (16 + lc, 16), (32 + t, 16)]
    cw = 256

    def inv_count(t0, rows, ln):
        pos = t0 + lax.broadcasted_iota(jnp.int32, (rows, cw), 0)
        cnt = jnp.minimum(pos + r + 1, ln) - jnp.maximum(pos - r, 0)
        return 1.0 / cnt.astype(F32)

    def body(x_ref, o_ref, s_ref):
        for h0, hn in halo:
            s_ref[pl.ds(h0, hn), :] = jnp.zeros((hn, cw), F32)
        for r0, ln, s0 in segs:
            step = min(512, ln)
            for c0 in range(0, ln, step):
                v = x_ref[pl.ds(r0 + c0, step), :]
                s_ref[pl.ds(s0 + c0, step), :] = v * inv_count(c0, step, ln) if transpose else v
        for r0, ln, s0 in segs:
            step = min(512, ln)
            for c0 in range(0, ln, step):
                acc = s_ref[pl.ds(s0 + c0 - r, step), :]
                for dlt in range(-r + 1, r + 1):
                    acc = acc + s_ref[pl.ds(s0 + c0 + dlt, step), :]
                ctr = x_ref[pl.ds(r0 + c0, step), :]
                o_ref[pl.ds(r0 + c0, step), :] = (acc if transpose else acc * inv_count(c0, step, ln)) - ctr

    return pl.pallas_call(
        body, name=name, grid=(bsz,),
        in_specs=[pl.BlockSpec((None, t, cw), lambda b: (b, 0, col_blk))],
        out_specs=pl.BlockSpec((None, t, cw), lambda b: (b, 0, 0)),
        out_shape=jax.ShapeDtypeStruct((bsz, t, cw), F32),
        scratch_shapes=[pltpu.VMEM((t + 48, cw), F32)],
        compiler_params=_cp(VMEM_BIG),
    )(src)


def _odd_specs(d, i, jl, nct):
    return [_tile(d), _mod_spec(i, nct, d), _tile(256), _tile(256), _tile(256), _tile(256), _tile(1024, 1),
            _layer_spec(0, 4, 256, 256), _layer_spec(jl, 1, 1024), _layer_spec(0, 1024, d)]


def _odd_mix(pm, psc, gz):
    return pm * psc * jax.nn.silu(gz)


def _odd_out_fwd(hs, mods, ps, z, pw, psc, wout, i, jl, nct, name):
    bsz, t, d = hs.shape

    def body(h_ref, mod_ref, p0, p1, p2, p3, gz_ref, pw_ref, psc_ref, wo_ref, o_ref):
        pm = jnp.concatenate([_nn(_bf(p[...]), pw_ref[q]) for q, p in enumerate((p0, p1, p2, p3))], axis=1)
        mix = _odd_mix(pm, psc_ref[...], gz_ref[...])
        o_ref[...] = h_ref[...] + mod_ref[2:3, :] * _nn(_bf(mix), wo_ref[...])

    return pl.pallas_call(
        body, name=name, grid=(bsz, t // ROWT), in_specs=_odd_specs(d, i, jl, nct),
        out_specs=_tile(d), out_shape=jax.ShapeDtypeStruct((bsz, t, d), F32), compiler_params=_cp(VMEM_BIG),
    )(hs, mods, *ps, z, pw, psc, wout)


def _odd_out_bwd(dh, mods, ps, z, pw, psc, wout, i, jl, nct, name):
    bsz, t, d = dh.shape

    def body(dh_ref, mod_ref, p0, p1, p2, p3, gz_ref, pw_ref, psc_ref, wo_ref,
             dp_ref, dgz_ref, dmod_ref, dpw_ref, dpsc_ref, dwo_ref, mix_prev, dyo_prev):
        b, j = pl.program_id(0), pl.program_id(1)
        pbs = [_bf(p[...]) for p in (p0, p1, p2, p3)]
        pm = jnp.concatenate([_nn(pb, pw_ref[q]) for q, pb in enumerate(pbs)], axis=1)
        mix, vj = jax.vjp(_odd_mix, pm, psc_ref[...], gz_ref[...])
        mixb = _bf(mix)
        dhv = dh_ref[...]
        dyo = _bf(dhv * mod_ref[2:3, :])
        yout = _nn(mixb, wo_ref[...])
        dpm, dpsc, dgz = vj(_nt(dyo, wo_ref[...]))
        dgz_ref[...] = dgz
        dpmb = _bf(dpm)
        dp_ref[...] = jnp.concatenate([_nt(dpmb[:, q * 256:(q + 1) * 256], pw_ref[q]) for q in range(4)], axis=1)

        @pl.when((j == 0) | (j == nct))
        def _():
            dmod_ref[...] = jnp.zeros_like(dmod_ref)

        @pl.when((b == 0) & (j == 0))
        def _():
            dpw_ref[...] = jnp.zeros_like(dpw_ref)
            dpsc_ref[...] = jnp.zeros_like(dpsc_ref)
            dwo_ref[...] = jnp.zeros_like(dwo_ref)
        dmod_ref[2:3, :] += jnp.sum(dhv * yout, axis=0, keepdims=True)
        for q in range(4):
            dpw_ref[q] += _tn(pbs[q], dpmb[:, q * 256:(q + 1) * 256])
        dpsc_ref[...] += dpsc
        _pair_accumulate(dwo_ref, mixb, dyo, mix_prev, dyo_prev, b * (t // ROWT) + j, bsz * (t // ROWT))

    full = lambda *s: pl.BlockSpec(s, lambda b, j: (0,) * len(s))
    return pl.pallas_call(
        body, name=name, grid=(bsz, t // ROWT), in_specs=_odd_specs(d, i, jl, nct),
        out_specs=[_tile(1024), _tile(1024), _dmod_spec(nct, d), full(4, 256, 256), full(1, 1024), full(1024, d)],
        out_shape=(jax.ShapeDtypeStruct((bsz, t, 1024), F32), jax.ShapeDtypeStruct((bsz, t, 1024), F32),
                   jax.ShapeDtypeStruct((2 * bsz, 3, d), F32), jax.ShapeDtypeStruct((4, 256, 256), F32),
                   jax.ShapeDtypeStruct((1, 1024), F32), jax.ShapeDtypeStruct((1024, d), F32)),
        scratch_shapes=[pltpu.VMEM((ROWT, 1024), BF16), pltpu.VMEM((ROWT, d), BF16)],
        compiler_params=_cp(VMEM_BIG),
    )(dh, mods, *ps, z, pw, psc, wout)


def _rms(h, g):
    return h * lax.rsqrt(jnp.mean(h * h, axis=-1, keepdims=True) + EPS) * g


def _final_loss(hs, g, target, nct, name):
    bsz, t, d = hs.shape

    def body(h_ref, g_ref, t_ref, dh_ref, loss_ref, dg_ref):
        b, j = pl.program_id(0), pl.program_id(1)
        y, vj = jax.vjp(_rms, h_ref[...], g_ref[...])
        err = jnp.where(j >= nct, y - t_ref[...], 0.0)
        dh, dg = vj(err * (1.0 / d))
        dh_ref[...] = dh

        @pl.when(j == 0)
        def _():
            loss_ref[...] = jnp.zeros_like(loss_ref)

        @pl.when((b == 0) & (j == 0))
        def _():
            dg_ref[...] = jnp.zeros_like(dg_ref)
        loss_ref[...] += 0.5 * jnp.sum(jnp.mean(err * err, axis=-1))
        dg_ref[...] += dg

    return pl.pallas_call(
        body, name=name, grid=(bsz, t // ROWT),
        in_specs=[_tile(d), pl.BlockSpec((1, d), lambda b, j: (0, 0)),
                  pl.BlockSpec((None, ROWT, d), lambda b, j: (b, jnp.maximum(j - nct, 0), 0))],
        out_specs=[_tile(d), pl.BlockSpec((None, 8, LANES), lambda b, j: (b, 0, 0)),
                   pl.BlockSpec((1, d), lambda b, j: (0, 0))],
        out_shape=(jax.ShapeDtypeStruct((bsz, t, d), F32), jax.ShapeDtypeStruct((bsz, 8, LANES), F32),
                   jax.ShapeDtypeStruct((1, d), F32)),
    )(hs, g, target)


def _adamw(w, g_list, m, v, transpose_g, rows, name):
    nl, r, c = w.shape
    ns = g_list[0].shape[1]
    rt = rows or r
    offs = np.cumsum([0] + [g.shape[0] for g in g_list])
    ng = len(g_list)

    def body(*refs):
        w_ref, m_ref, v_ref = refs[0], refs[1 + ng], refs[2 + ng]
        go_ref, d_ref, mo_ref, vo_ref = refs[3 + ng:]
        layer = pl.program_id(0)
        for k in range(ng):
            g_ref = refs[1 + k]

            @pl.when((layer >= offs[k]) & (layer < offs[k + 1]))
            def _():
                g = g_ref[0]
                for s in range(1, ns):
                    g = g + g_ref[s]
                if transpose_g:
                    g = g.T
                mn = ADAM_B1 * m_ref[...] + (1.0 - ADAM_B1) * g
                vn = ADAM_B2 * v_ref[...] + (1.0 - ADAM_B2) * jnp.square(g)
                m_hat = mn / (1.0 - ADAM_B1 ** ADAM_STEP)
                v_hat = vn / (1.0 - ADAM_B2 ** ADAM_STEP)
                go_ref[...] = g
                d_ref[...] = -ADAM_LR * (m_hat / (jnp.sqrt(v_hat) + ADAM_EPS) + ADAM_WD * w_ref[...])
                mo_ref[...] = mn
                vo_ref[...] = vn

    def g_spec(k):
        lo, n_k = int(offs[k]), int(offs[k + 1] - offs[k])

        def idx(l, q):
            mine = (l >= lo) & (l < lo + n_k)
            q = jnp.where(mine, q, 0)
            return (jnp.clip(l - lo, 0, n_k - 1), 0, 0, q) if transpose_g else (jnp.clip(l - lo, 0, n_k - 1), 0, q, 0)
        return pl.BlockSpec((None, ns, c, rt) if transpose_g else (None, ns, rt, c), idx)

    blk = pl.BlockSpec((None, rt, c), lambda l, q: (l, q, 0))
    out = jax.ShapeDtypeStruct((nl, r, c), F32)
    return pl.pallas_call(
        body, name=name, grid=(nl, r // rt),
        in_specs=[blk] + [g_spec(k) for k in range(ng)] + [blk, blk],
        out_specs=[blk, blk, blk, blk], out_shape=(out, out, out, out), compiler_params=_cp(VMEM_BIG),
    )(w, *g_list, m, v)


def _sum_slots(slots, name):
    ns, r, c = slots.shape

    def body(s_ref, o_ref):
        g = s_ref[0]
        for s in range(1, ns):
            g = g + s_ref[s]
        o_ref[...] = g

    return pl.pallas_call(body, name=name, out_shape=jax.ShapeDtypeStruct((r, c), F32), compiler_params=_cp(VMEM_BIG))(slots)


def _pack(arrs):
    flat = jnp.concatenate([a.reshape(-1) for a in arrs])
    return jnp.pad(flat, (0, (-flat.shape[0]) % (PACK_ROWS * PACK_W))).reshape(-1, PACK_W)


def _unpack(buf, shapes):
    flat = buf.reshape(-1)
    out, off = [], 0
    for s in shapes:
        n = int(np.prod(s))
        out.append(flat[off:off + n].reshape(s))
        off += n
    return out


def kernel(x, c, ctx, c_ctx, ada_w, ada_b, norm_g, even_w_in, even_w_out, attn_sink, ssm_a_re, ssm_a_im, ssm_log_dt, ssm_b_re, ssm_b_im, ssm_c_re, ssm_c_im, ssm_d, glu_w, glu_b, odd_w_in, odd_w_out, pool_w, pool_scale, final_g, loss_target, m_c_ctx, m_ada_w, m_ada_b, m_norm_g, m_even_w_in, m_even_w_out, m_attn_sink, m_ssm_a_re, m_ssm_a_im, m_ssm_log_dt, m_ssm_b_re, m_ssm_b_im, m_ssm_c_re, m_ssm_c_im, m_ssm_d, m_glu_w, m_glu_b, m_odd_w_in, m_odd_w_out, m_pool_w, m_pool_scale, m_final_g, v_c_ctx, v_ada_w, v_ada_b, v_norm_g, v_even_w_in, v_even_w_out, v_attn_sink, v_ssm_a_re, v_ssm_a_im, v_ssm_log_dt, v_ssm_b_re, v_ssm_b_im, v_ssm_c_re, v_ssm_c_im, v_ssm_d, v_glu_w, v_glu_b, v_odd_w_in, v_odd_w_out, v_pool_w, v_pool_scale, v_final_g):
    args = dict(locals())
    bsz, l, d = x.shape
    lc = ctx.shape[1]
    t = lc + l
    nct = lc // ROWT
    depth = ada_w.shape[0]
    nl2 = even_w_in.shape[0]
    me = _my_index()
    assert lc % ROWT == 0 and l % ROWT == 0 and d == 1024 and bsz * N_DEV < 32

    npw = pool_w.shape[2]
    shards = {"even": [_bf(jnp.transpose(even_w_in, (0, 2, 1))), _bf(even_w_out), _bf(glu_w)],
              "odd": [_bf(jnp.transpose(odd_w_in, (0, 2, 1))), _bf(odd_w_out), _bf(pool_w.reshape(nl2, -1, pool_w.shape[-1]))]}

    def wgather(kind, jl):
        return shards[kind], [("gather", [(ii, jl)], 0) for ii in range(3)]

    def wjoin(kind, res):
        full = [a.reshape(1, N_DEV * a.shape[2], a.shape[3]) for a in res]
        if kind == "odd":
            full[2] = jnp.transpose(res[2].reshape(1, N_DEV, 4, npw, 256), (0, 2, 1, 3, 4)).reshape(1, 4, 256, 256)
        return full

    xin, xsp = wgather("even", 0)
    got = _exchange(xin + [c, pool_scale], xsp + [("gather", [(3, None)], 0), ("gather", [(4, None)], 0)], "gather_first")
    weights = {("even", 0): wjoin("even", got[:3])}
    c_all = got[3].reshape(N_DEV * bsz, d)
    psc_full = jnp.transpose(got[4][0], (1, 0, 2)).reshape(nl2, 1, d)

    n_rows = 32
    craw = jnp.concatenate([c_all, c_ctx[None], jnp.zeros((n_rows - N_DEV * bsz - 1, d), F32)], axis=0)
    ncol = ada_w.shape[2]
    ada_b_loc = lax.dynamic_slice_in_dim(ada_b, me * ncol, ncol, axis=1)[:, None, :]
    m_loc = _adaln_fwd(craw, ada_w, ada_b_loc)
    m_all = _exchange([m_loc], [("gather", [(0, None)], 0)], "gather_mod")[0][0]
    m_all = jnp.transpose(m_all, (1, 2, 0, 3)).reshape(depth, n_rows, 3 * d)
    lat = lax.dynamic_slice_in_dim(m_all, me * bsz, bsz, axis=1).reshape(depth, bsz, 1, 3, d)
    cmod = jnp.broadcast_to(m_all[:, N_DEV * bsz].reshape(depth, 1, 1, 3, d), (depth, bsz, 1, 3, d))
    mods = jnp.concatenate([cmod, lat], axis=2).reshape(depth, 2 * bsz, 3, d)

    cos, sin = _rope_tables(lc, l)
    hs = jnp.concatenate([ctx, x], axis=1)
    g3 = norm_g[:, None, :]
    dsk3, gb3 = ssm_d[:, None, :], glu_b[:, None, :]
    u_col = 1280 // LANES

    s5_prm = (ssm_a_re, ssm_a_im, ssm_log_dt, ssm_b_re, ssm_b_im, ssm_c_re, ssm_c_im)
    ops = jax.vmap(_s5_operators)(*s5_prm)
    pr, pi = jax.vmap(lambda r, q: _scan_powers(r, q, t // CHUNK))(ops[3], ops[4])
    s5_ops = (_bf(ops[0]), _bf(ops[1]), _bf(ops[2]), pr, pi)
    saved = []
    for i in range(depth):
        jl = i // 2
        if i % 2 == 0:
            wt, wo, gwf = weights[("even", jl)]
            z = _fused_in_fwd(hs, mods, g3, wt, i, 0, nct, f"in_fwd{i}")
            qp, kvp = _rope_fwd(z, cos, sin, 2, f"rope_fwd{i}")
            (oa, lse), got = _attn_fwd(qp, kvp, attn_sink, jl, lc, f"attn_fwd{i}", xchg=wgather("odd", jl))
            weights[("odd", jl)] = wjoin("odd", got)
            nxt = wgather("even", jl + 1) if jl + 1 < nl2 else None
            (y,), got = _s5_fwd(z, u_col, s5_ops, jl, lc, f"s5_fwd{i}", xchg=nxt)
            if nxt:
                weights[("even", jl + 1)] = wjoin("even", got)
            hn = _even_out_fwd(hs, mods, oa, y, z, dsk3, gwf, gb3, wo, i, jl, nct, f"out_fwd{i}")
            saved.append(dict(hs=hs, z=z, qp=qp, kvp=kvp, oa=oa, lse=lse, y=y))
        else:
            wt, wo, pwf = weights[("odd", jl)]
            z = _fused_in_fwd(hs, mods, g3, wt, i, 0, nct, f"in_fwd{i}")
            ps = [_pool(z, gi, wd // 2, lc, False, f"pool_fwd{i}_{gi}") for gi, wd in enumerate(POOL_WINDOWS)]
            hn = _odd_out_fwd(hs, mods, ps, z, pwf, psc_full, wo, i, jl, nct, f"out_fwd{i}")
            saved.append(dict(hs=hs, z=z, ps=ps))
        hs = hn

    dh, loss_p, d_final_g = _final_loss(hs, final_g[None], loss_target, nct, "final_loss")
    loss = lax.psum(jnp.sum(loss_p[:, 0, 0]), ("x", "y", "c"))

    slots = {n: [None] * nl2 for n in ("even_w_in", "even_w_out", "glu_w", "odd_w_in", "odd_w_out", "pool_w", "pool_scale")}
    gsmall = {n: [None] * nl2 for n in ("attn_sink", "ssm_d", "glu_b")}
    d_norm_g, d_mods = [None] * depth, [None] * depth
    d_ops = [None] * nl2

    def rows_xchg(named):
        arrs = [a for _, _, a in named]
        specs = [("rows2" if n == "pool_w" else "rows", [(k, None)], a.shape[1 if n == "pool_w" else 0] // N_DEV)
                 for k, (n, _, a) in enumerate(named)]
        return (arrs, specs)

    def keep(named, res):
        for (n, jl_, _), r_ in zip(named, res):
            slots[n][jl_] = r_

    pending = None
    for i in reversed(range(depth)):
        jl = i // 2
        sv = saved[i]
        if i % 2 == 0:
            wt, wo, gwf = weights[("even", jl)]
            doa, dga, dy, dgs, dmod_g, ddsk, dgw, dgb, dwo = _even_out_bwd(
                dh, mods, sv["oa"], sv["y"], sv["z"], dsk3, gwf, gb3, wo, i, jl, nct, f"out_bwd{i}")
            (dq_r, dkv_acc, dsink), got = _attn_bwd(sv["qp"], sv["kvp"], attn_sink, jl, sv["oa"], sv["lse"], doa, lc,
                                                     f"attn_bwd{i}", xchg=rows_xchg(pending) if pending else None)
            if pending:
                keep(pending, got)
            dq, dkv = _rope_bwd(dq_r, dkv_acc, cos, sin, t, f"rope_bwd{i}")
            mine = [("even_w_out", jl, dwo), ("glu_w", jl, dgw)]
            (du, dmt, dbt, dct, dav), got = _s5_bwd(sv["z"], u_col, dy, s5_ops, jl, lc, f"s5_bwd{i}", xchg=rows_xchg(mine))
            keep(mine, got)
            per_group = lambda q: dav[:, q].reshape(SSM_GROUPS, SSM_STATE)
            d_ops[jl] = (dmt, dbt, dct, jnp.stack([per_group(0), per_group(2)]), jnp.stack([per_group(1), per_group(3)]))
            gsmall["attn_sink"][jl] = jnp.sum(dsink[:, :, 0], axis=0)
            gsmall["ssm_d"][jl] = ddsk[0]
            gsmall["glu_b"][jl] = dgb[0]
            pieces = [(dq, 512, 0), (dkv, 256, 0), (dga, 512, 0), (du, 512, 0), (dy, 512, 0), (dgs, 512, 0)]
            asm = lambda p, s: [p[0], p[1], p[2], p[3] + p[4] * s, p[5]]
            (dh, dmod_in, dg, dw), _ = _fused_in_bwd(sv["hs"], mods, g3, wt, dh, pieces, (dsk3, jl), asm, i, 0, nct, f"in_bwd{i}")
            pending = [("even_w_in", jl, dw)]
        else:
            wt, wo, pwf = weights[("odd", jl)]
            dp, dgz, dmod_g, dpw, dpsc, dwo = _odd_out_bwd(
                dh, mods, sv["ps"], sv["z"], pwf, psc_full, wo, i, jl, nct, f"out_bwd{i}")
            dus = [_pool(dp, gi, wd // 2, lc, True, f"pool_bwd{i}_{gi}") for gi, wd in enumerate(POOL_WINDOWS)]
            dpsc8 = jnp.broadcast_to(dpsc.reshape(N_DEV, 1, -1), (N_DEV, 8, d // N_DEV)).reshape(N_DEV * 8, -1)
            pieces = [(u_, 256, 0) for u_ in dus] + [(dgz, 1024, 0)]
            asm = lambda p, s: p
            (dh, dmod_in, dg, dw), got = _fused_in_bwd(sv["hs"], mods, g3, wt, dh, pieces, None, asm, i, 0, nct, f"in_bwd{i}",
                                                       xchg=rows_xchg(pending) if pending else None)
            if pending:
                keep(pending, got)
            pending = [("odd_w_in", jl, dw), ("odd_w_out", jl, dwo), ("pool_w", jl, dpw), ("pool_scale", jl, dpsc8)]
        d_norm_g[i] = dg[0]
        d_mods[i] = (dmod_g + dmod_in).reshape(bsz, 2, 3 * d)
    grad_x = dh[:, lc:]

    _, op_vjp = jax.vjp(jax.vmap(_s5_operators), *s5_prm)
    dprm = op_vjp(tuple(jnp.stack([d_ops[jl][q] for jl in range(nl2)]) for q in range(5)))

    dmd = jnp.stack(d_mods)
    dm_loc = jnp.concatenate([dmd[:, :, 1], jnp.sum(dmd[:, :, 0], axis=1, keepdims=True)], axis=1)
    dm_all = _exchange([dm_loc], [("gather", [(0, None)], 0)], "gather_dmod")[0][0]
    dm_lat = jnp.transpose(dm_all[:, :, :bsz], (1, 0, 2, 3)).reshape(depth, N_DEV * bsz, 3 * d)
    dm_ctx = dm_all[0, :, bsz]
    for e in range(1, N_DEV):
        dm_ctx = dm_ctx + dm_all[e, :, bsz]
    dm_rows = jnp.concatenate([dm_lat, dm_ctx[:, None], jnp.zeros((depth, n_rows - N_DEV * bsz - 1, 3 * d), F32)], axis=1)
    dm_cols = lax.dynamic_slice_in_dim(dm_rows, me * ncol, ncol, axis=2)
    g_ada_w, dcraw = _adaln_bwd(craw, dm_cols, ada_w)

    small = {"c_ctx": dcraw[N_DEV * bsz], "ada_b": jnp.sum(dm_loc, axis=1), "norm_g": jnp.stack(d_norm_g),
             "final_g": d_final_g[0]}
    for n, v2 in gsmall.items():
        small[n] = jnp.stack(v2)
    for n, gval in zip(("ssm_a_re", "ssm_a_im", "ssm_log_dt", "ssm_b_re", "ssm_b_im", "ssm_c_re", "ssm_c_im"), dprm):
        small[n] = gval
    snames = list(small)
    sshapes = [args[n].shape for n in snames]
    spack = _pack([small[n].reshape(args[n].shape) for n in snames])

    last = pending + [("small", 0, spack)]
    red = _exchange(*rows_xchg(last), "reduce_last")
    keep(pending, red[:-1])
    s_sum = _sum_slots(red[-1][0], "sum_small")
    s_all = _exchange([s_sum], [("gather", [(0, None)], 0)], "gather_small")[0]
    s_all = s_all.reshape(1, 1, -1, PACK_W)

    out = {}

    def put(n, res, shape):
        for kind, buf in zip(("grad_", "delta_", "new_m_", "new_v_"), res):
            out[kind + n] = buf.reshape(shape)

    def as3(a):
        return a.reshape(a.shape[0], -1, a.shape[-1])

    for n in slots:
        w = args[n]
        g_list = [s_[:, :, :1] if n == "pool_scale" else s_ for s_ in slots[n]]
        g_list = [s_.reshape(1, N_DEV, -1, s_.shape[-1]) for s_ in g_list]
        tr = n in ("even_w_in", "odd_w_in")
        put(n, _adamw(as3(w), g_list, as3(args["m_" + n]), as3(args["v_" + n]), tr, 256 if tr else None, "adamw_" + n),
            w.shape)
    put("ada_w", _adamw(ada_w, [g_ada_w[:, None]], m_ada_w, v_ada_w, False, None, "adamw_ada_w"), ada_w.shape)
    res = _adamw(_pack([args[n] for n in snames])[None], [s_all], _pack([args["m_" + n] for n in snames])[None],
                 _pack([args["v_" + n] for n in snames])[None], False, PACK_ROWS, "adamw_small")
    for kind, buf in zip(("grad_", "delta_", "new_m_", "new_v_"), res):
        for n, a in zip(snames, _unpack(buf, sshapes)):
            out[kind + n] = a

    order = ["c_ctx", "ada_w", "ada_b", "norm_g", "even_w_in", "even_w_out", "attn_sink", "ssm_a_re", "ssm_a_im",
             "ssm_log_dt", "ssm_b_re", "ssm_b_im", "ssm_c_re", "ssm_c_im", "ssm_d", "glu_w", "glu_b", "odd_w_in",
             "odd_w_out", "pool_w", "pool_scale", "final_g"]
    return (loss, grad_x, *[out[k + n] for k in ("grad_", "delta_", "new_m_", "new_v_") for n in order])
```

```python
import functools

import numpy as np
import jax
import jax.numpy as jnp
from jax import lax
from jax.experimental import pallas as pl
from jax.experimental.pallas import tpu as pltpu

F32 = jnp.float32
BF16 = jnp.bfloat16
EPS = 1e-6
NEG_INF = -1e30
N_DEV = 8
GRID_W = 64
WINDOW = 128
QBLK = 128
ROWT = 256
CHUNK = 16
SSM_GROUPS = 32
SSM_GROUP = 16
SSM_STATE = 64
POOL_WINDOWS = (2, 4, 8, 16)
ROPE_BASE = 10000.0
ADAM_LR, ADAM_B1, ADAM_B2, ADAM_EPS, ADAM_WD, ADAM_STEP = 0.001, 0.9, 0.999, 1e-08, 0.01, 10
LANES = 128
PACK_W = 1024
PACK_ROWS = 64
VMEM_BIG = 56 << 20


def _cp(vmem=None):
    return pltpu.CompilerParams(vmem_limit_bytes=vmem) if vmem else pltpu.CompilerParams()


def _nt(a, b):
    return lax.dot_general(a, b, (((1,), (1,)), ((), ())), preferred_element_type=F32)


def _tn(a, b):
    return lax.dot_general(a, b, (((0,), (0,)), ((), ())), preferred_element_type=F32)


def _nn(a, b):
    return jnp.dot(a, b, preferred_element_type=F32)


def _bf(x):
    return x.astype(BF16)


def _my_index():
    return 4 * lax.axis_index("x") + 2 * lax.axis_index("y") + lax.axis_index("c")


def _xchg_shapes(inputs, specs):
    out_shapes = []
    for kind, parts, r in specs:
        ii, li = parts[0]
        shp = tuple(inputs[ii].shape if li is None else inputs[ii].shape[1:])
        piece = shp if kind == "gather" else ((r,) + shp[1:] if kind == "rows" else (shp[0], r) + shp[2:])
        out_shapes.append(jax.ShapeDtypeStruct((len(parts), N_DEV) + piece, inputs[ii].dtype))
    return out_shapes, sum(len(parts) for _, parts, _ in specs)


def _xchg_sems(n_parts):
    return [pltpu.SemaphoreType.DMA((n_parts * (N_DEV - 1),)), pltpu.SemaphoreType.DMA((n_parts * (N_DEV - 1),)),
            pltpu.SemaphoreType.DMA((n_parts,))]


def _xchg_copies(specs, in_refs, out_refs, send_sems, recv_sems, loc_sems):
    x, y, c = lax.axis_index("x"), lax.axis_index("y"), lax.axis_index("c")
    me = 4 * x + 2 * y + c

    def piece(ref, kind, r, p):
        if kind == "gather":
            return ref
        if kind == "rows":
            return ref.at[pl.ds(p * r, r)]
        return ref.at[:, pl.ds(p * r, r)]

    copies, q = [], 0
    for si, (kind, parts, r) in enumerate(specs):
        for pi, (ii, li) in enumerate(parts):
            src = in_refs[ii] if li is None else in_refs[ii].at[li]
            dst = out_refs[si].at[pi, me]
            copies.append(pltpu.make_async_copy(piece(src, kind, r, me), dst, loc_sems.at[q]))
            for k in range(1, N_DEV):
                px = 1 - x if k & 4 else x
                py = 1 - y if k & 2 else y
                pc = 1 - c if k & 1 else c
                copies.append(pltpu.make_async_remote_copy(
                    src_ref=piece(src, kind, r, 4 * px + 2 * py + pc), dst_ref=dst,
                    send_sem=send_sems.at[q * (N_DEV - 1) + k - 1], recv_sem=recv_sems.at[q * (N_DEV - 1) + k - 1],
                    device_id=(px, py, pc), device_id_type=pl.DeviceIdType.MESH))
            q += 1
    return copies


def _exchange(inputs, specs, name):
    out_shapes, n_parts = _xchg_shapes(inputs, specs)
    ni, ns = len(inputs), len(specs)

    def body(*refs):
        copies = _xchg_copies(specs, refs[:ni], refs[ni:ni + ns], *refs[ni + ns:])
        for cp in copies:
            cp.start()
        for cp in copies:
            cp.wait()

    return pl.pallas_call(
        body, name=name, out_shape=tuple(out_shapes),
        in_specs=[pl.BlockSpec(memory_space=pl.ANY)] * ni,
        out_specs=tuple(pl.BlockSpec(memory_space=pl.ANY) for _ in specs),
        scratch_shapes=_xchg_sems(n_parts),
        compiler_params=pltpu.CompilerParams(has_side_effects=True),
    )(*inputs)


def _call(body, *, name, grid, in_specs, out_specs, out_shape, operands, vmem=None, scratch=(), xchg=None):
    out_shape, out_specs = tuple(out_shape), list(out_specs)
    if xchg is None:
        res = pl.pallas_call(body, name=name, grid=grid, in_specs=in_specs, out_specs=out_specs, out_shape=out_shape,
                             scratch_shapes=list(scratch), compiler_params=_cp(vmem))(*operands)
        return list(res), None
    xin, xspecs = xchg
    xshapes, n_parts = _xchg_shapes(xin, xspecs)
    ni, no, nxi, nxo, nsc = len(operands), len(out_shape), len(xin), len(xspecs), len(scratch)
    anyspec = pl.BlockSpec(memory_space=pl.ANY)

    def hosted(*refs):
        ins, xins = refs[:ni], refs[ni:ni + nxi]
        outs, xouts = refs[ni + nxi:ni + nxi + no], refs[ni + nxi + no:ni + nxi + no + nxo]
        scr, sems = refs[ni + nxi + no + nxo:ni + nxi + no + nxo + nsc], refs[ni + nxi + no + nxo + nsc:]
        copies = _xchg_copies(xspecs, xins, xouts, *sems)
        first, last = True, True
        for ax, n in enumerate(grid):
            first = first & (pl.program_id(ax) == 0)
            last = last & (pl.program_id(ax) == n - 1)

        @pl.when(first)
        def _():
            for cp in copies:
                cp.start()
        body(*ins, *outs, *scr)

        @pl.when(last)
        def _():
            for cp in copies:
                cp.wait()

    res = pl.pallas_call(
        hosted, name=name, grid=grid, in_specs=list(in_specs) + [anyspec] * nxi,
        out_specs=out_specs + [anyspec] * nxo, out_shape=out_shape + tuple(xshapes),
        scratch_shapes=list(scratch) + _xchg_sems(n_parts), compiler_params=_cp(vmem))(*operands, *xin)
    return list(res[:no]), list(res[no:])


def _adaln_fwd(craw, ada_w, ada_b):
    nl, d, n = ada_w.shape
    r = craw.shape[0]

    def body(c_ref, w_ref, b_ref, o_ref):
        s = jax.nn.silu(c_ref[...])
        o_ref[...] = _nn(_bf(s), _bf(w_ref[...])) + b_ref[...]

    return pl.pallas_call(
        body, name="adaln_fwd", grid=(nl,),
        in_specs=[pl.BlockSpec((r, d), lambda i: (0, 0)), pl.BlockSpec((None, d, n), lambda i: (i, 0, 0)),
                  pl.BlockSpec((None, 1, n), lambda i: (i, 0, 0))],
        out_specs=pl.BlockSpec((None, r, n), lambda i: (i, 0, 0)),
        out_shape=jax.ShapeDtypeStruct((nl, r, n), F32),
    )(craw, ada_w, ada_b)


def _adaln_bwd(craw, dm, ada_w):
    nl, d, n = ada_w.shape
    r = craw.shape[0]

    def body(c_ref, dm_ref, w_ref, gw_ref, dc_ref, acc_ref):
        i = pl.program_id(0)
        s, vj = jax.vjp(jax.nn.silu, c_ref[...])
        dmb = _bf(dm_ref[...])
        gw_ref[...] = _tn(_bf(s), dmb)

        @pl.when(i == 0)
        def _():
            acc_ref[...] = jnp.zeros_like(acc_ref)
        acc_ref[...] += _nt(dmb, _bf(w_ref[...]))

        @pl.when(i == nl - 1)
        def _():
            dc_ref[...] = vj(acc_ref[...])[0]

    return pl.pallas_call(
        body, name="adaln_bwd", grid=(nl,),
        in_specs=[pl.BlockSpec((r, d), lambda i: (0, 0)), pl.BlockSpec((None, r, n), lambda i: (i, 0, 0)),
                  pl.BlockSpec((None, d, n), lambda i: (i, 0, 0))],
        out_specs=[pl.BlockSpec((None, d, n), lambda i: (i, 0, 0)), pl.BlockSpec((r, d), lambda i: (0, 0))],
        out_shape=(jax.ShapeDtypeStruct((nl, d, n), F32), jax.ShapeDtypeStruct((r, d), F32)),
        scratch_shapes=[pltpu.VMEM((r, d), F32)],
    )(craw, dm, ada_w)


def _norm_mod(h, mod, g):
    ms = jnp.mean(h * h, axis=-1, keepdims=True)
    y = h * lax.rsqrt(ms + EPS) * g
    return y * (1.0 + mod[1:2]) + mod[0:1]


def _mod_spec(i, nct, d):
    return pl.BlockSpec((None, None, 3, d), lambda b, j: (i, 2 * b + jnp.where(j >= nct, 1, 0), 0, 0))


def _dmod_spec(nct, d):
    return pl.BlockSpec((None, 3, d), lambda b, j: (2 * b + jnp.where(j >= nct, 1, 0), 0, 0))


def _layer_spec(li, *shape):
    return pl.BlockSpec((None,) + tuple(shape), lambda b, j: (li,) + (0,) * len(shape))


def _tile(width, col_blk=0):
    return pl.BlockSpec((None, ROWT, width), lambda b, j: (b, j, col_blk))


def _fused_in_fwd(hs, mods, g, wt, i, jl, nct, name):
    bsz, t, d = hs.shape
    n = wt.shape[1]

    def body(h_ref, mod_ref, g_ref, w_ref, z_ref):
        a = _norm_mod(h_ref[...], mod_ref[...], g_ref[...])
        z_ref[...] = _nt(_bf(a), w_ref[...])

    return pl.pallas_call(
        body, name=name, grid=(bsz, t // ROWT),
        in_specs=[_tile(d), _mod_spec(i, nct, d), _layer_spec(i, 1, d), _layer_spec(jl, n, d)],
        out_specs=_tile(n), out_shape=jax.ShapeDtypeStruct((bsz, t, n), F32),
        compiler_params=_cp(VMEM_BIG),
    )(hs, mods, g, wt)


def _fused_in_bwd(hs, mods, g, wt, dh_out, pieces, small, assemble, i, jl, nct, name, xchg=None):
    bsz, t, d = hs.shape
    n = wt.shape[1]
    npc, nsm = len(pieces), int(small is not None)

    def body(*refs):
        h_ref, mod_ref, g_ref, w_ref, dho_ref = refs[:5]
        p_refs = refs[5:5 + npc]
        s_val = refs[5 + npc][...] if nsm else None
        dhi_ref, dmod_ref, dg_ref, dw_ref = refs[5 + npc + nsm:]
        b, j = pl.program_id(0), pl.program_id(1)
        a, vj = jax.vjp(_norm_mod, h_ref[...], mod_ref[...], g_ref[...])
        dz = jnp.concatenate([_bf(p) for p in assemble([r[...] for r in p_refs], s_val)], axis=1)
        dh, dmod, dg = vj(_nn(dz, w_ref[...]))
        dhi_ref[...] = dho_ref[...] + dh

        @pl.when((j == 0) | (j == nct))
        def _():
            dmod_ref[...] = jnp.zeros_like(dmod_ref)

        @pl.when((b == 0) & (j == 0))
        def _():
            dg_ref[...] = jnp.zeros_like(dg_ref)
            dw_ref[...] = jnp.zeros_like(dw_ref)
        dmod_ref[...] += dmod
        dg_ref[...] += dg
        dw_ref[...] += _tn(dz, _bf(a))

    full = lambda r, c: pl.BlockSpec((r, c), lambda b, j: (0, 0))
    return _call(
        body, name=name, grid=(bsz, t // ROWT),
        in_specs=[_tile(d), _mod_spec(i, nct, d), _layer_spec(i, 1, d), _layer_spec(jl, n, d), _tile(d)]
                 + [_tile(wd, cb) for _, wd, cb in pieces]
                 + ([_layer_spec(small[1], 1, small[0].shape[2])] if nsm else []),
        out_specs=[_tile(d), _dmod_spec(nct, d), full(1, d), full(n, d)],
        out_shape=(jax.ShapeDtypeStruct((bsz, t, d), F32), jax.ShapeDtypeStruct((2 * bsz, 3, d), F32),
                   jax.ShapeDtypeStruct((1, d), F32), jax.ShapeDtypeStruct((n, d), F32)),
        vmem=VMEM_BIG, xchg=xchg,
        operands=[hs, mods, g, wt, dh_out, *[p for p, _, _ in pieces], *([small[0]] if nsm else [])])


def _rope_tables(lc, l):
    tpos = np.arange(l)
    row = (tpos // GRID_W).astype(np.float32)
    col = (tpos % GRID_W).astype(np.float32)
    nf = 16
    inv = (np.float32(ROPE_BASE) ** (-np.arange(nf, dtype=np.float32) / np.float32(nf))).astype(np.float32)
    ang_r = row[:, None] * inv[None]
    ang_c = col[:, None] * inv[None]
    cos64 = np.concatenate([np.cos(ang_r), np.cos(ang_r), np.cos(ang_c), np.cos(ang_c)], axis=1)
    sin64 = np.concatenate([-np.sin(ang_r), np.sin(ang_r), -np.sin(ang_c), np.sin(ang_c)], axis=1)
    cos = np.concatenate([np.ones((lc, 128), np.float32), np.tile(cos64, (1, 2)).astype(np.float32)], axis=0)
    sin = np.concatenate([np.zeros((lc, 128), np.float32), np.tile(sin64, (1, 2)).astype(np.float32)], axis=0)
    return jnp.asarray(cos), jnp.asarray(sin)


def _rot(x, cos, sin):
    lane = lax.broadcasted_iota(jnp.int32, x.shape, 1)
    first = jnp.bitwise_and(jnp.right_shift(lane, 4), 1) == 0
    partner = jnp.where(first, pltpu.roll(x, LANES - 16, 1), pltpu.roll(x, 16, 1))
    return x * cos + partner * sin


def _split_heads(x):
    low = lax.broadcasted_iota(jnp.int32, x.shape, 1) < 64
    return jnp.where(low, x, 0.0), jnp.where(low, pltpu.roll(x, 64, 1), 0.0)


def _merge_heads(a, b):
    return a + pltpu.roll(b, 64, 1)


def _rope_fwd(z, cos, sin, kv_blk, name):
    bsz, t, _ = z.shape

    def body(q_ref, kv_ref, cos_ref, sin_ref, qp_ref, kvp_ref):
        c, s = cos_ref[...], sin_ref[...]
        outs = []
        for sl in range(4):
            xr = _rot(q_ref[:, sl * LANES:(sl + 1) * LANES], c, s) * 0.125
            outs += list(_split_heads(xr))
        qp_ref[...] = _bf(jnp.concatenate(outs, axis=1))
        k0, k1 = _split_heads(_rot(kv_ref[:, 0:LANES], c, s))
        v0, v1 = _split_heads(kv_ref[:, LANES:2 * LANES])
        kvp_ref[...] = _bf(jnp.concatenate([k0, k1, v0, v1], axis=1))

    return pl.pallas_call(
        body, name=name, grid=(bsz, t // ROWT),
        in_specs=[_tile(512), _tile(256, kv_blk),
                  pl.BlockSpec((ROWT, LANES), lambda b, j: (j, 0)), pl.BlockSpec((ROWT, LANES), lambda b, j: (j, 0))],
        out_specs=[_tile(1024), _tile(512)],
        out_shape=(jax.ShapeDtypeStruct((bsz, t, 1024), BF16), jax.ShapeDtypeStruct((bsz, t, 512), BF16)),
    )(z, z, cos, sin)


def _rope_bwd(dq_r, dkv_acc, cos, sin, t, name):
    bsz = dq_r.shape[0]

    def body(dq_ref, acc_ref, cos_ref, sin_ref, dqo_ref, dkvo_ref):
        c, s = cos_ref[...], -sin_ref[...]
        dqo_ref[...] = jnp.concatenate(
            [_rot(dq_ref[:, sl * LANES:(sl + 1) * LANES], c, s) for sl in range(4)], axis=1)
        dk = _merge_heads(acc_ref[:, 0:LANES], acc_ref[:, LANES:2 * LANES])
        dv = _merge_heads(acc_ref[:, 2 * LANES:3 * LANES], acc_ref[:, 3 * LANES:4 * LANES])
        dkvo_ref[...] = jnp.concatenate([_rot(dk, c, s), dv], axis=1)

    return pl.pallas_call(
        body, name=name, grid=(bsz, t // ROWT),
        in_specs=[_tile(512), _tile(512),
                  pl.BlockSpec((ROWT, LANES), lambda b, j: (j, 0)), pl.BlockSpec((ROWT, LANES), lambda b, j: (j, 0))],
        out_specs=[_tile(512), _tile(256)],
        out_shape=(jax.ShapeDtypeStruct((bsz, t, 512), F32), jax.ShapeDtypeStruct((bsz, t, 256), F32)),
    )(dq_r, dkv_acc, cos, sin)


def _attn_bias(j, ncb, l):
    n = j - ncb
    r = lax.broadcasted_iota(jnp.int32, (QBLK, 3 * QBLK), 0)
    cidx = lax.broadcasted_iota(jnp.int32, (QBLK, 3 * QBLK), 1)
    kpos = (n - 1) * QBLK + cidx
    valid = (jnp.abs(r - cidx + QBLK) <= WINDOW) & (kpos >= 0) & (kpos < l) & (j >= ncb)
    return jnp.where(valid, 0.0, NEG_INF).astype(F32)


def _attn_bias4(j, ncb, l):
    b1 = _attn_bias(j, ncb, l)
    return jnp.concatenate([b1, b1, b1, b1], axis=0)


def _sink_rows(sink_ref, jl, hk):
    head = jnp.right_shift(lax.broadcasted_iota(jnp.int32, (4 * QBLK, 1), 0), 7)
    sk = jnp.zeros((4 * QBLK, 1), F32)
    for g in range(4):
        sk = jnp.where(head == g, sink_ref[jl, 4 * hk + g], sk)
    return sk


def _kv_specs(nb, lc):
    return [pl.BlockSpec((None, QBLK, 512), lambda b, j: (b, jnp.maximum(j - 1, 0), 0)),
            pl.BlockSpec((None, QBLK, 512), lambda b, j: (b, j, 0)),
            pl.BlockSpec((None, QBLK, 512), lambda b, j: (b, jnp.minimum(j + 1, nb - 1), 0)),
            pl.BlockSpec((None, lc, 512), lambda b, j: (b, 0, 0))]


def _lane_pick(x, h):
    lane = lax.broadcasted_iota(jnp.int32, x.shape, 1)
    return jnp.sum(jnp.where(lane == h, x, 0.0), axis=1, keepdims=True)


def _attn_fwd(qp, kvp, sinks, jl, lc, name, xchg=None):
    bsz, t, _ = qp.shape
    nb, ncb, l = t // QBLK, lc // QBLK, t - lc

    def body(sink_ref, q_ref, kp_ref, kc_ref, kn_ref, kx_ref, o_ref, lse_ref):
        j = pl.program_id(1)
        kloc = jnp.concatenate([kp_ref[...], kc_ref[...], kn_ref[...]], axis=0)
        kctx = kx_ref[...]
        bias = _attn_bias4(j, ncb, l)
        lane = lax.broadcasted_iota(jnp.int32, (QBLK, LANES), 1)
        lse_all = jnp.zeros((QBLK, LANES), F32)
        outs = []
        for hk in range(2):
            ks, vs = slice(hk * LANES, (hk + 1) * LANES), slice((2 + hk) * LANES, (3 + hk) * LANES)
            q4 = jnp.concatenate([q_ref[:, (4 * hk + g) * LANES:(4 * hk + g + 1) * LANES] for g in range(4)], axis=0)
            s_loc = _nt(q4, kloc[:, ks]) + bias
            s_ctx = _nt(q4, kctx[:, ks])
            sk = _sink_rows(sink_ref, jl, hk)
            m = jnp.maximum(jnp.maximum(jnp.max(s_loc, axis=1, keepdims=True), jnp.max(s_ctx, axis=1, keepdims=True)), sk)
            p_loc, p_ctx = jnp.exp(s_loc - m), jnp.exp(s_ctx - m)
            den = jnp.sum(p_loc, axis=1, keepdims=True) + jnp.sum(p_ctx, axis=1, keepdims=True) + jnp.exp(sk - m)
            o4 = (_nn(_bf(p_loc), kloc[:, vs]) + _nn(_bf(p_ctx), kctx[:, vs])) / den
            lse4 = m + jnp.log(den)
            for g in range(4):
                outs.append(o4[g * QBLK:(g + 1) * QBLK])
                lse_all = lse_all + jnp.where(lane == 4 * hk + g, lse4[g * QBLK:(g + 1) * QBLK], 0.0)
        o_ref[...] = jnp.concatenate([_merge_heads(outs[2 * i], outs[2 * i + 1]) for i in range(4)], axis=1)
        lse_ref[...] = lse_all

    return _call(
        body, name=name, grid=(bsz, nb),
        in_specs=[pl.BlockSpec(memory_space=pltpu.SMEM),
                  pl.BlockSpec((None, QBLK, 1024), lambda b, j: (b, j, 0))] + _kv_specs(nb, lc),
        out_specs=[pl.BlockSpec((None, QBLK, 512), lambda b, j: (b, j, 0)),
                   pl.BlockSpec((None, QBLK, LANES), lambda b, j: (b, j, 0))],
        out_shape=(jax.ShapeDtypeStruct((bsz, t, 512), F32), jax.ShapeDtypeStruct((bsz, t, LANES), F32)),
        xchg=xchg, operands=[sinks, qp, kvp, kvp, kvp, kvp])


def _attn_bwd(qp, kvp, sinks, jl, o, lse, do, lc, name, xchg=None):
    bsz, t, _ = qp.shape
    nb, ncb, l = t // QBLK, lc // QBLK, t - lc
    nk = 3 * QBLK

    def body(sink_ref, q_ref, kp_ref, kc_ref, kn_ref, kx_ref, o_ref, lse_ref, do_ref, dq_ref, acc_ref, ds_ref):
        j = pl.program_id(1)

        @pl.when(j == 0)
        def _():
            acc_ref[...] = jnp.zeros_like(acc_ref)
            ds_ref[...] = jnp.zeros_like(ds_ref)
        kloc = jnp.concatenate([kp_ref[...], kc_ref[...], kn_ref[...]], axis=0)
        kctx = kx_ref[...]
        bias = _attn_bias4(j, ncb, l)
        lse = lse_ref[...]
        row8 = lax.broadcasted_iota(jnp.int32, (8, LANES), 0)
        dsink = jnp.zeros((8, LANES), F32)
        dqs = []
        d_loc, d_ctx = [None] * 4, [None] * 4
        for hk in range(2):
            ks, vs = slice(hk * LANES, (hk + 1) * LANES), slice((2 + hk) * LANES, (3 + hk) * LANES)
            q4 = jnp.concatenate([q_ref[:, (4 * hk + g) * LANES:(4 * hk + g + 1) * LANES] for g in range(4)], axis=0)
            do4 = jnp.concatenate([x_ for s2 in (2 * hk, 2 * hk + 1)
                                   for x_ in _split_heads(do_ref[:, s2 * LANES:(s2 + 1) * LANES])], axis=0)
            o4 = jnp.concatenate([x_ for s2 in (2 * hk, 2 * hk + 1)
                                  for x_ in _split_heads(o_ref[:, s2 * LANES:(s2 + 1) * LANES])], axis=0)
            delta = jnp.sum(do4 * o4, axis=1, keepdims=True)
            lse4 = jnp.concatenate([_lane_pick(lse, 4 * hk + g) for g in range(4)], axis=0)
            p_loc = jnp.exp(_nt(q4, kloc[:, ks]) + bias - lse4)
            p_ctx = jnp.exp(_nt(q4, kctx[:, ks]) - lse4)
            dob = _bf(do4)
            ds_loc = _bf(p_loc * (_nt(dob, kloc[:, vs]) - delta))
            ds_ctx = _bf(p_ctx * (_nt(dob, kctx[:, vs]) - delta))
            dq4 = (_nn(ds_loc, kloc[:, ks]) + _nn(ds_ctx, kctx[:, ks])) * 0.125
            d_loc[hk], d_ctx[hk] = _tn(ds_loc, q4), _tn(ds_ctx, q4)
            d_loc[2 + hk], d_ctx[2 + hk] = _tn(_bf(p_loc), dob), _tn(_bf(p_ctx), dob)
            dsk = -jnp.exp(_sink_rows(sink_ref, jl, hk) - lse4) * delta
            for g in range(4):
                dqs.append(dq4[g * QBLK:(g + 1) * QBLK])
                dsink = dsink + jnp.where(row8 == 4 * hk + g, jnp.sum(dsk[g * QBLK:(g + 1) * QBLK]), 0.0)
        dq_ref[...] = jnp.concatenate([_merge_heads(dqs[2 * i], dqs[2 * i + 1]) for i in range(4)], axis=1)
        ds_ref[...] += dsink
        acc_ref[pl.ds(0, lc), :] += jnp.concatenate(d_ctx, axis=1)

        @pl.when(j >= ncb)
        def _():
            st = pl.multiple_of((j - 1) * QBLK, QBLK)
            acc_ref[pl.ds(st, nk), :] += jnp.concatenate(d_loc, axis=1)

    blk = lambda wd: pl.BlockSpec((None, QBLK, wd), lambda b, j: (b, j, 0))
    return _call(
        body, name=name, grid=(bsz, nb),
        in_specs=[pl.BlockSpec(memory_space=pltpu.SMEM), blk(1024)] + _kv_specs(nb, lc)
                 + [blk(512), blk(LANES), blk(512)],
        out_specs=[blk(512), pl.BlockSpec((None, t + QBLK, 512), lambda b, j: (b, 0, 0)),
                   pl.BlockSpec((None, 8, LANES), lambda b, j: (b, 0, 0))],
        out_shape=(jax.ShapeDtypeStruct((bsz, t, 512), F32), jax.ShapeDtypeStruct((bsz, t + QBLK, 512), F32),
                   jax.ShapeDtypeStruct((bsz, 8, LANES), F32)),
        vmem=VMEM_BIG, xchg=xchg, operands=[sinks, qp, kvp, kvp, kvp, kvp, o, lse, do])


def _s5_operators(a_re, a_im, log_dt, b_re, b_im, c_re, c_im):
    hi = lax.Precision.HIGHEST
    dt = jnp.exp(log_dt)[..., None]
    tau = jnp.arange(CHUNK + 1, dtype=F32)[:, None, None, None]
    mag = jnp.exp(tau * (a_re * dt))
    pwr, pwi = mag * jnp.cos(tau * (a_im * dt)), mag * jnp.sin(tau * (a_im * dt))
    ar, ai = pwr[1], pwi[1]
    den = a_re * a_re + a_im * a_im
    fr = ((ar - 1.0) * a_re + ai * a_im) / den
    fi = (ai * a_re - (ar - 1.0) * a_im) / den
    bbr = fr[..., None] * b_re - fi[..., None] * b_im
    bbi = fr[..., None] * b_im + fi[..., None] * b_re
    wr = pwr[:CHUNK, ..., None] * bbr - pwi[:CHUNK, ..., None] * bbi
    wi = pwr[:CHUNK, ..., None] * bbi + pwi[:CHUNK, ..., None] * bbr
    kern = (jnp.einsum("dgcp,tdgpk->dgtck", c_re, wr, precision=hi)
            - jnp.einsum("dgcp,tdgpk->dgtck", c_im, wi, precision=hi))
    g = a_re.shape[1]
    nrow = CHUNK * SSM_GROUP
    qf = jnp.transpose(kern[0], (0, 3, 1, 2)).reshape(g, SSM_GROUP, nrow)
    qb = jnp.transpose(kern[1][:, ::-1], (0, 3, 1, 2)).reshape(g, SSM_GROUP, nrow)
    to_rows = lambda w: jnp.transpose(w, (1, 0, 3, 2)).reshape(g, nrow, SSM_STATE)
    bt = jnp.concatenate([to_rows(wr[::-1, 0]), to_rows(wi[::-1, 0]), to_rows(wr[:, 1]), to_rows(wi[:, 1])], axis=-1)

    def out_map(d, pw_r, pw_i):
        w2r = c_re[d][None] * pw_r[:, :, None, :] - c_im[d][None] * pw_i[:, :, None, :]
        w2i = c_re[d][None] * pw_i[:, :, None, :] + c_im[d][None] * pw_r[:, :, None, :]
        cols = lambda w: jnp.transpose(w, (1, 3, 0, 2)).reshape(g, SSM_STATE, nrow)
        return [cols(w2r), -cols(w2i)]

    ct = jnp.concatenate(out_map(0, pwr[1:, 0], pwi[1:, 0]) + out_map(1, pwr[1:, 1][::-1], pwi[1:, 1][::-1]), axis=-2)
    return qf, qb, bt, ct, pwr[CHUNK], pwi[CHUNK]


def _scan_powers(a16r, a16i, n_chunks):
    prs, pis = [], []
    r, i = a16r, a16i
    s = 1
    while s < n_chunks:
        prs.append(jnp.concatenate([r, r], axis=-1))
        pis.append(jnp.concatenate([-i, i], axis=-1))
        r, i = r * r - i * i, 2.0 * r * i
        s *= 2
    pad = 16 - len(prs)
    z = jnp.zeros_like(prs[0])
    return jnp.stack(prs + [z] * pad, axis=2), jnp.stack(pis + [z] * pad, axis=2)


def _low_half():
    return lax.broadcasted_iota(jnp.int32, (1, LANES), 1) < 64


def _to_pair(sa, sb):
    low = _low_half()
    return jnp.where(low, sa, pltpu.roll(sb, 64, 1)), jnp.where(low, pltpu.roll(sa, 64, 1), sb)


def _from_pair(hr, hi):
    low = _low_half()
    return jnp.where(low, hr, pltpu.roll(hi, 64, 1)), jnp.where(low, pltpu.roll(hr, 64, 1), hi)


def _pair_scan(sr, si, pr, pi, reverse, conj):
    n = sr.shape[0]
    row = lax.broadcasted_iota(jnp.int32, sr.shape, 0)

    def shift(x, k):
        if reverse:
            return jnp.where(row < n - k, pltpu.roll(x, n - k, 0), 0.0)
        return jnp.where(row >= k, pltpu.roll(x, k, 0), 0.0)

    hr, hi = shift(sr, 1), shift(si, 1)
    k, j = 1, 0
    while k < n:
        tr, ti = shift(hr, k), shift(hi, k)
        ar = pr[j:j + 1, :]
        ai = -pi[j:j + 1, :] if conj else pi[j:j + 1, :]
        hr, hi = hr + ar * tr - ai * ti, hi + ar * ti + ai * tr
        k, j = 2 * k, j + 1
    return hr, hi


def _transpose8(a):
    a = list(a)
    blk = jnp.right_shift(lax.broadcasted_iota(jnp.int32, (1, LANES), 1), 4)
    for dd in (4, 2, 1):
        upper = jnp.bitwise_and(blk, dd) != 0
        for i in range(8):
            if i & dd:
                continue
            lo, hi = a[i], a[i + dd]
            a[i] = jnp.where(upper, pltpu.roll(hi, dd * SSM_GROUP, 1), lo)
            a[i + dd] = jnp.where(upper, hi, pltpu.roll(lo, LANES - dd * SSM_GROUP, 1))
    return a


def _blocks_from_rows(xs):
    t0, t1 = _transpose8(xs[:8]), _transpose8(xs[8:])
    return [jnp.concatenate([t0[g], t1[g]], axis=1) for g in range(8)]


def _rows_from_blocks(ys):
    return _transpose8([y[:, :LANES] for y in ys]) + _transpose8([y[:, LANES:] for y in ys])


def _pair_states(sa, sb, pr_ref, pi_ref, ga, gb, ncc, adjoint):
    n = sa.shape[0]
    low = _low_half()
    out_a, out_b, pairs = [], [], []
    for dirn in range(2):
        xr, xi = _to_pair(sa[:, dirn * LANES:(dirn + 1) * LANES], sb[:, dirn * LANES:(dirn + 1) * LANES])
        pr = jnp.where(low, pr_ref[dirn, ga], pr_ref[dirn, gb])
        pi = jnp.where(low, -pi_ref[dirn, ga], pi_ref[dirn, gb])
        if dirn == 0:
            hr, hi = _pair_scan(xr, xi, pr, pi, adjoint, adjoint)
        else:
            hr, hi = _pair_scan(pltpu.roll(xr, n - ncc, 0), pltpu.roll(xi, n - ncc, 0), pr, pi, not adjoint, adjoint)
            hr, hi = pltpu.roll(hr, ncc, 0), pltpu.roll(hi, ncc, 0)
        pairs.append((hr, hi))
        ha, hb = _from_pair(hr, hi)
        out_a.append(ha)
        out_b.append(hb)
    return (jnp.concatenate(out_a, axis=1), jnp.concatenate(out_b, axis=1)), pairs


def _shift_lanes(x, n, left):
    if n == 0:
        return x
    lo, hi = x[:, :LANES], x[:, LANES:]
    lane = lax.broadcasted_iota(jnp.int32, lo.shape, 1)
    zero = jnp.zeros_like(lo)
    m = n % LANES
    if not left:
        if n < LANES:
            r_lo, r_hi = pltpu.roll(lo, n, 1), pltpu.roll(hi, n, 1)
            return jnp.concatenate([jnp.where(lane >= n, r_lo, 0.0), jnp.where(lane >= n, r_hi, r_lo)], axis=1)
        r_lo = pltpu.roll(lo, m, 1) if m else lo
        return jnp.concatenate([zero, jnp.where(lane >= m, r_lo, 0.0)], axis=1)
    if n < LANES:
        r_lo, r_hi = pltpu.roll(lo, LANES - n, 1), pltpu.roll(hi, LANES - n, 1)
        return jnp.concatenate([jnp.where(lane < LANES - n, r_lo, r_hi), jnp.where(lane < LANES - n, r_hi, 0.0)], axis=1)
    r_hi = pltpu.roll(hi, LANES - m, 1) if m else hi
    return jnp.concatenate([jnp.where(lane < LANES - m, r_hi, 0.0), zero], axis=1)


def _toeplitz(qf, qb):
    return jnp.concatenate([_shift_lanes(qf, SSM_GROUP * s, False) + _shift_lanes(qb, SSM_GROUP * (CHUNK - 1 - s), True)
                            for s in range(CHUNK)], axis=0)


def _toeplitz_t(dmt):
    dqf, dqb = 0.0, 0.0
    for s in range(CHUNK):
        rows = dmt[s * SSM_GROUP:(s + 1) * SSM_GROUP]
        dqf = dqf + _shift_lanes(rows, SSM_GROUP * s, True)
        dqb = dqb + _shift_lanes(rows, SSM_GROUP * (CHUNK - 1 - s), False)
    return dqf, dqb


def _s5_op_specs(idx, jl):
    q = pl.BlockSpec((None, 8, SSM_GROUP, 256), lambda a, b: (jl, idx(a, b), 0, 0))
    op = pl.BlockSpec((None, 8, 256, 256), lambda a, b: (jl, idx(a, b), 0, 0))
    pw = pl.BlockSpec((None, 2, 8, 16, LANES), lambda a, b: (jl, 0, idx(a, b), 0, 0))
    return [q, q, op, op, pw, pw]


def _s5_fwd(z, col0, s5_ops, jl, lc, name, xchg=None):
    bsz, t, _ = z.shape
    nc, ncc = t // CHUNK, lc // CHUNK

    def body(u_ref, qf_ref, qb_ref, bt_ref, ct_ref, pr_ref, pi_ref, y_ref):
        us = [_bf(u) for u in _blocks_from_rows([u_ref[pl.ds(tok, nc, stride=CHUNK), :] for tok in range(CHUNK)])]
        ys = []
        for ga in range(0, 8, 2):
            gb = ga + 1
            hs, _ = _pair_states(_nn(us[ga], bt_ref[ga]), _nn(us[gb], bt_ref[gb]), pr_ref, pi_ref, ga, gb, ncc, False)
            for g, h in zip((ga, gb), hs):
                ys.append(_nn(us[g], _bf(_toeplitz(qf_ref[g], qb_ref[g]))) + _nn(_bf(h), ct_ref[g]))
        for tok, rows in enumerate(_rows_from_blocks(ys)):
            y_ref[pl.ds(tok, nc, stride=CHUNK), :] = rows

    return _call(
        body, name=name, grid=(bsz, 4),
        in_specs=[pl.BlockSpec((None, t, LANES), lambda b, s: (b, 0, col0 + s))] + _s5_op_specs(lambda b, s: s, jl),
        out_specs=[pl.BlockSpec((None, t, LANES), lambda b, s: (b, 0, s))],
        out_shape=[jax.ShapeDtypeStruct((bsz, t, 512), F32)], vmem=VMEM_BIG, xchg=xchg,
        operands=[z, *s5_ops])


def _s5_bwd(z, col0, dy, s5_ops, jl, lc, name, xchg=None):
    bsz, t, _ = z.shape
    nc, ncc = t // CHUNK, lc // CHUNK

    def body(u_ref, dy_ref, qf_ref, qb_ref, bt_ref, ct_ref, pr_ref, pi_ref,
             du_ref, dqf_ref, dqb_ref, dbt_ref, dct_ref, da_ref):
        b = pl.program_id(1)

        @pl.when(b == 0)
        def _():
            dqf_ref[...] = jnp.zeros_like(dqf_ref)
            dqb_ref[...] = jnp.zeros_like(dqb_ref)
            dbt_ref[...] = jnp.zeros_like(dbt_ref)
            dct_ref[...] = jnp.zeros_like(dct_ref)
            da_ref[...] = jnp.zeros_like(da_ref)
        us = [_bf(u) for u in _blocks_from_rows([u_ref[pl.ds(tok, nc, stride=CHUNK), :] for tok in range(CHUNK)])]
        dys = [_bf(u) for u in _blocks_from_rows([dy_ref[pl.ds(tok, nc, stride=CHUNK), :] for tok in range(CHUNK)])]
        row8 = lax.broadcasted_iota(jnp.int32, (8, LANES), 0)
        dus = []
        for ga in range(0, 8, 2):
            gb = ga + 1
            hs, hp = _pair_states(_nn(us[ga], bt_ref[ga]), _nn(us[gb], bt_ref[gb]), pr_ref, pi_ref, ga, gb, ncc, False)
            gs, gp = _pair_states(_nt(dys[ga], ct_ref[ga]), _nt(dys[gb], ct_ref[gb]), pr_ref, pi_ref, ga, gb, ncc, True)
            for g, h, gg in zip((ga, gb), hs, gs):
                hcat, gcat = _bf(h), _bf(gg)
                dus.append(_nt(dys[g], _bf(_toeplitz(qf_ref[g], qb_ref[g]))) + _nt(gcat, bt_ref[g]))
                dqf, dqb = _toeplitz_t(_tn(us[g], dys[g]))
                dqf_ref[g] += dqf
                dqb_ref[g] += dqb
                dct_ref[g] += _tn(hcat, dys[g])
                dbt_ref[g] += _tn(us[g], gcat)
            da = jnp.zeros((8, LANES), F32)
            for dirn in range(2):
                (hr, hi), (gr, gi) = hp[dirn], gp[dirn]
                da = da + jnp.where(row8 == 2 * dirn, jnp.sum(gr * hr + gi * hi, axis=0, keepdims=True), 0.0)
                da = da + jnp.where(row8 == 2 * dirn + 1, jnp.sum(gi * hr - gr * hi, axis=0, keepdims=True), 0.0)
            da_ref[ga // 2] += da
        for tok, rows in enumerate(_rows_from_blocks(dus)):
            du_ref[pl.ds(tok, nc, stride=CHUNK), :] = rows

    op_out = pl.BlockSpec((8, 256, 256), lambda s, b: (s, 0, 0))
    op_shape = jax.ShapeDtypeStruct((SSM_GROUPS, 256, 256), F32)
    q_out = pl.BlockSpec((8, SSM_GROUP, 256), lambda s, b: (s, 0, 0))
    q_shape = jax.ShapeDtypeStruct((SSM_GROUPS, SSM_GROUP, 256), F32)
    return _call(
        body, name=name, grid=(4, bsz),
        in_specs=[pl.BlockSpec((None, t, LANES), lambda s, b: (b, 0, col0 + s)),
                  pl.BlockSpec((None, t, LANES), lambda s, b: (b, 0, s))] + _s5_op_specs(lambda s, b: s, jl),
        out_specs=[pl.BlockSpec((None, t, LANES), lambda s, b: (b, 0, s)), q_out, q_out, op_out, op_out,
                   pl.BlockSpec((4, 8, LANES), lambda s, b: (s, 0, 0))],
        out_shape=(jax.ShapeDtypeStruct((bsz, t, 512), F32), q_shape, q_shape, op_shape, op_shape,
                   jax.ShapeDtypeStruct((SSM_GROUPS // 2, 8, LANES), F32)),
        vmem=VMEM_BIG, xchg=xchg, operands=[z, dy, *s5_ops])


def _glu_in(y, u, dsk):
    return jax.nn.gelu(y + u * dsk)


def _even_mix(oa, ga, zg, tg, gs):
    return jnp.concatenate([oa * jax.nn.silu(ga), zg * jax.nn.sigmoid(tg) * jax.nn.silu(gs)], axis=1)


def _even_specs(d, i, jl, nct):
    return [_tile(d), _mod_spec(i, nct, d), _tile(512), _tile(512)] + [_tile(256, cb) for cb in range(3, 9)] + [
        _layer_spec(jl, 1, 512), _layer_spec(0, 512, 512), _layer_spec(jl, 1, 512), _layer_spec(0, 1024, d)]


def _cat2(a_ref, b_ref):
    return jnp.concatenate([a_ref[...], b_ref[...]], axis=1)


def _even_out_fwd(hs, mods, oa, y, z, dsk, gw, gb, wout, i, jl, nct, name):
    bsz, t, d = hs.shape

    def body(h_ref, mod_ref, oa_ref, y_ref, ga0, ga1, u0, u1, gs0, gs1, dsk_ref, gw_ref, gb_ref, wo_ref, o_ref):
        zg = _glu_in(y_ref[...], _cat2(u0, u1), dsk_ref[...])
        tg = _nn(_bf(zg), gw_ref[...]) + gb_ref[...]
        mix = _even_mix(oa_ref[...], _cat2(ga0, ga1), zg, tg, _cat2(gs0, gs1))
        o_ref[...] = h_ref[...] + mod_ref[2:3, :] * _nn(_bf(mix), wo_ref[...])

    return pl.pallas_call(
        body, name=name, grid=(bsz, t // ROWT), in_specs=_even_specs(d, i, jl, nct),
        out_specs=_tile(d), out_shape=jax.ShapeDtypeStruct((bsz, t, d), F32), compiler_params=_cp(VMEM_BIG),
    )(hs, mods, oa, y, z, z, z, z, z, z, dsk, gw, gb, wout)


def _even_out_bwd(dh, mods, oa, y, z, dsk, gw, gb, wout, i, jl, nct, name):
    bsz, t, d = dh.shape

    def body(dh_ref, mod_ref, oa_ref, y_ref, ga0, ga1, u0, u1, gs0, gs1, dsk_ref, gw_ref, gb_ref, wo_ref,
             doa_ref, dga_ref, dy_ref, dgs_ref, dmod_ref, ddsk_ref, dgw_ref, dgb_ref, dwo_ref):
        b, j = pl.program_id(0), pl.program_id(1)
        zg, vj1 = jax.vjp(_glu_in, y_ref[...], _cat2(u0, u1), dsk_ref[...])
        zgb = _bf(zg)
        tg = _nn(zgb, gw_ref[...]) + gb_ref[...]
        mix, vj2 = jax.vjp(_even_mix, oa_ref[...], _cat2(ga0, ga1), zg, tg, _cat2(gs0, gs1))
        mixb = _bf(mix)
        dhv = dh_ref[...]
        dyo = _bf(dhv * mod_ref[2:3, :])
        yout = _nn(mixb, wo_ref[...])
        doa, dga, dzg, dtg, dgs = vj2(_nt(dyo, wo_ref[...]))
        dtb = _bf(dtg)
        dzg = dzg + _nt(dtb, gw_ref[...])
        dy, _, ddsk = vj1(dzg)
        doa_ref[...], dga_ref[...], dy_ref[...], dgs_ref[...] = doa, dga, dy, dgs

        @pl.when((j == 0) | (j == nct))
        def _():
            dmod_ref[...] = jnp.zeros_like(dmod_ref)

        @pl.when((b == 0) & (j == 0))
        def _():
            ddsk_ref[...] = jnp.zeros_like(ddsk_ref)
            dgw_ref[...] = jnp.zeros_like(dgw_ref)
            dgb_ref[...] = jnp.zeros_like(dgb_ref)
            dwo_ref[...] = jnp.zeros_like(dwo_ref)
        dmod_ref[2:3, :] += jnp.sum(dhv * yout, axis=0, keepdims=True)
        ddsk_ref[...] += ddsk
        dgw_ref[...] += _tn(zgb, dtb)
        dgb_ref[...] += jnp.sum(dtg, axis=0, keepdims=True)
        dwo_ref[...] += _tn(mixb, dyo)

    full = lambda r, c: pl.BlockSpec((r, c), lambda b, j: (0, 0))
    f512 = jax.ShapeDtypeStruct((bsz, t, 512), F32)
    return pl.pallas_call(
        body, name=name, grid=(bsz, t // ROWT), in_specs=_even_specs(d, i, jl, nct),
        out_specs=[_tile(512), _tile(512), _tile(512), _tile(512), _dmod_spec(nct, d),
                   full(1, 512), full(512, 512), full(1, 512), full(1024, d)],
        out_shape=(f512, f512, f512, f512, jax.ShapeDtypeStruct((2 * bsz, 3, d), F32),
                   jax.ShapeDtypeStruct((1, 512), F32), jax.ShapeDtypeStruct((512, 512), F32),
                   jax.ShapeDtypeStruct((1, 512), F32), jax.ShapeDtypeStruct((1024, d), F32)),
        compiler_params=_cp(VMEM_BIG),
    )(dh, mods, oa, y, z, z, z, z, z, z, dsk, gw, gb, wout)


def _pool(src, col_blk, r, lc, transpose, name):
    bsz, t, _ = src.shape
    segs = [(0, lc, 16), (lc, t - lc, 32 + lc)]
    halo = [(0, 16), (16 + lc, 16), (32 + t, 16)]
    cw = 256

    def inv_count(t0, rows, ln):
        pos = t0 + lax.broadcasted_iota(jnp.int32, (rows, cw), 0)
        cnt = jnp.minimum(pos + r + 1, ln) - jnp.maximum(pos - r, 0)
        return 1.0 / cnt.astype(F32)

    def body(x_ref, o_ref, s_ref):
        for h0, hn in halo:
            s_ref[pl.ds(h0, hn), :] = jnp.zeros((hn, cw), F32)
        for r0, ln, s0 in segs:
            step = min(512, ln)
            for c0 in range(0, ln, step):
                v = x_ref[pl.ds(r0 + c0, step), :]
                s_ref[pl.ds(s0 + c0, step), :] = v * inv_count(c0, step, ln) if transpose else v
        for r0, ln, s0 in segs:
            step = min(512, ln)
            for c0 in range(0, ln, step):
                acc = s_ref[pl.ds(s0 + c0 - r, step), :]
                for dlt in range(-r + 1, r + 1):
                    acc = acc + s_ref[pl.ds(s0 + c0 + dlt, step), :]
                ctr = x_ref[pl.ds(r0 + c0, step), :]
                o_ref[pl.ds(r0 + c0, step), :] = (acc if transpose else acc * inv_count(c0, step, ln)) - ctr

    return pl.pallas_call(
        body, name=name, grid=(bsz,),
        in_specs=[pl.BlockSpec((None, t, cw), lambda b: (b, 0, col_blk))],
        out_specs=pl.BlockSpec((None, t, cw), lambda b: (b, 0, 0)),
        out_shape=jax.ShapeDtypeStruct((bsz, t, cw), F32),
        scratch_shapes=[pltpu.VMEM((t + 48, cw), F32)],
        compiler_params=_cp(VMEM_BIG),
    )(src)


def _odd_specs(d, i, jl, nct):
    return [_tile(d), _mod_spec(i, nct, d), _tile(256), _tile(256), _tile(256), _tile(256), _tile(1024, 1),
            _layer_spec(0, 4, 256, 256), _layer_spec(jl, 1, 1024), _layer_spec(0, 1024, d)]


def _odd_mix(pm, psc, gz):
    return pm * psc * jax.nn.silu(gz)


def _odd_out_fwd(hs, mods, ps, z, pw, psc, wout, i, jl, nct, name):
    bsz, t, d = hs.shape

    def body(h_ref, mod_ref, p0, p1, p2, p3, gz_ref, pw_ref, psc_ref, wo_ref, o_ref):
        pm = jnp.concatenate([_nn(_bf(p[...]), pw_ref[q]) for q, p in enumerate((p0, p1, p2, p3))], axis=1)
        mix = _odd_mix(pm, psc_ref[...], gz_ref[...])
        o_ref[...] = h_ref[...] + mod_ref[2:3, :] * _nn(_bf(mix), wo_ref[...])

    return pl.pallas_call(
        body, name=name, grid=(bsz, t // ROWT), in_specs=_odd_specs(d, i, jl, nct),
        out_specs=_tile(d), out_shape=jax.ShapeDtypeStruct((bsz, t, d), F32), compiler_params=_cp(VMEM_BIG),
    )(hs, mods, *ps, z, pw, psc, wout)


def _odd_out_bwd(dh, mods, ps, z, pw, psc, wout, i, jl, nct, name):
    bsz, t, d = dh.shape

    def body(dh_ref, mod_ref, p0, p1, p2, p3, gz_ref, pw_ref, psc_ref, wo_ref,
             dp_ref, dgz_ref, dmod_ref, dpw_ref, dpsc_ref, dwo_ref):
        b, j = pl.program_id(0), pl.program_id(1)
        pbs = [_bf(p[...]) for p in (p0, p1, p2, p3)]
        pm = jnp.concatenate([_nn(pb, pw_ref[q]) for q, pb in enumerate(pbs)], axis=1)
        mix, vj = jax.vjp(_odd_mix, pm, psc_ref[...], gz_ref[...])
        mixb = _bf(mix)
        dhv = dh_ref[...]
        dyo = _bf(dhv * mod_ref[2:3, :])
        yout = _nn(mixb, wo_ref[...])
        dpm, dpsc, dgz = vj(_nt(dyo, wo_ref[...]))
        dgz_ref[...] = dgz
        dpmb = _bf(dpm)
        dp_ref[...] = jnp.concatenate([_nt(dpmb[:, q * 256:(q + 1) * 256], pw_ref[q]) for q in range(4)], axis=1)

        @pl.when((j == 0) | (j == nct))
        def _():
            dmod_ref[...] = jnp.zeros_like(dmod_ref)

        @pl.when((b == 0) & (j == 0))
        def _():
            dpw_ref[...] = jnp.zeros_like(dpw_ref)
            dpsc_ref[...] = jnp.zeros_like(dpsc_ref)
            dwo_ref[...] = jnp.zeros_like(dwo_ref)
        dmod_ref[2:3, :] += jnp.sum(dhv * yout, axis=0, keepdims=True)
        for q in range(4):
            dpw_ref[q] += _tn(pbs[q], dpmb[:, q * 256:(q + 1) * 256])
        dpsc_ref[...] += dpsc
        dwo_ref[...] += _tn(mixb, dyo)

    full = lambda *s: pl.BlockSpec(s, lambda b, j: (0,) * len(s))
    return pl.pallas_call(
        body, name=name, grid=(bsz, t // ROWT), in_specs=_odd_specs(d, i, jl, nct),
        out_specs=[_tile(1024), _tile(1024), _dmod_spec(nct, d), full(4, 256, 256), full(1, 1024), full(1024, d)],
        out_shape=(jax.ShapeDtypeStruct((bsz, t, 1024), F32), jax.ShapeDtypeStruct((bsz, t, 1024), F32),
                   jax.ShapeDtypeStruct((2 * bsz, 3, d), F32), jax.ShapeDtypeStruct((4, 256, 256), F32),
                   jax.ShapeDtypeStruct((1, 1024), F32), jax.ShapeDtypeStruct((1024, d), F32)),
        compiler_params=_cp(VMEM_BIG),
    )(dh, mods, *ps, z, pw, psc, wout)


def _rms(h, g):
    return h * lax.rsqrt(jnp.mean(h * h, axis=-1, keepdims=True) + EPS) * g


def _final_loss(hs, g, target, nct, name):
    bsz, t, d = hs.shape

    def body(h_ref, g_ref, t_ref, dh_ref, loss_ref, dg_ref):
        b, j = pl.program_id(0), pl.program_id(1)
        y, vj = jax.vjp(_rms, h_ref[...], g_ref[...])
        err = jnp.where(j >= nct, y - t_ref[...], 0.0)
        dh, dg = vj(err * (1.0 / d))
        dh_ref[...] = dh

        @pl.when(j == 0)
        def _():
            loss_ref[...] = jnp.zeros_like(loss_ref)

        @pl.when((b == 0) & (j == 0))
        def _():
            dg_ref[...] = jnp.zeros_like(dg_ref)
        loss_ref[...] += 0.5 * jnp.sum(jnp.mean(err * err, axis=-1))
        dg_ref[...] += dg

    return pl.pallas_call(
        body, name=name, grid=(bsz, t // ROWT),
        in_specs=[_tile(d), pl.BlockSpec((1, d), lambda b, j: (0, 0)),
                  pl.BlockSpec((None, ROWT, d), lambda b, j: (b, jnp.maximum(j - nct, 0), 0))],
        out_specs=[_tile(d), pl.BlockSpec((None, 8, LANES), lambda b, j: (b, 0, 0)),
                   pl.BlockSpec((1, d), lambda b, j: (0, 0))],
        out_shape=(jax.ShapeDtypeStruct((bsz, t, d), F32), jax.ShapeDtypeStruct((bsz, 8, LANES), F32),
                   jax.ShapeDtypeStruct((1, d), F32)),
    )(hs, g, target)


def _adamw(w, g_list, m, v, transpose_g, rows, name):
    nl, r, c = w.shape
    ns = g_list[0].shape[1]
    rt = rows or r
    offs = np.cumsum([0] + [g.shape[0] for g in g_list])
    ng = len(g_list)

    def body(*refs):
        w_ref, m_ref, v_ref = refs[0], refs[1 + ng], refs[2 + ng]
        go_ref, d_ref, mo_ref, vo_ref = refs[3 + ng:]
        layer = pl.program_id(0)
        for k in range(ng):
            g_ref = refs[1 + k]

            @pl.when((layer >= offs[k]) & (layer < offs[k + 1]))
            def _():
                g = g_ref[0]
                for s in range(1, ns):
                    g = g + g_ref[s]
                if transpose_g:
                    g = g.T
                mn = ADAM_B1 * m_ref[...] + (1.0 - ADAM_B1) * g
                vn = ADAM_B2 * v_ref[...] + (1.0 - ADAM_B2) * jnp.square(g)
                m_hat = mn / (1.0 - ADAM_B1 ** ADAM_STEP)
                v_hat = vn / (1.0 - ADAM_B2 ** ADAM_STEP)
                go_ref[...] = g
                d_ref[...] = -ADAM_LR * (m_hat / (jnp.sqrt(v_hat) + ADAM_EPS) + ADAM_WD * w_ref[...])
                mo_ref[...] = mn
                vo_ref[...] = vn

    def g_spec(k):
        lo, n_k = int(offs[k]), int(offs[k + 1] - offs[k])

        def idx(l, q):
            mine = (l >= lo) & (l < lo + n_k)
            q = jnp.where(mine, q, 0)
            return (jnp.clip(l - lo, 0, n_k - 1), 0, 0, q) if transpose_g else (jnp.clip(l - lo, 0, n_k - 1), 0, q, 0)
        return pl.BlockSpec((None, ns, c, rt) if transpose_g else (None, ns, rt, c), idx)

    blk = pl.BlockSpec((None, rt, c), lambda l, q: (l, q, 0))
    out = jax.ShapeDtypeStruct((nl, r, c), F32)
    return pl.pallas_call(
        body, name=name, grid=(nl, r // rt),
        in_specs=[blk] + [g_spec(k) for k in range(ng)] + [blk, blk],
        out_specs=[blk, blk, blk, blk], out_shape=(out, out, out, out), compiler_params=_cp(VMEM_BIG),
    )(w, *g_list, m, v)


def _sum_slots(slots, name):
    ns, r, c = slots.shape

    def body(s_ref, o_ref):
        g = s_ref[0]
        for s in range(1, ns):
            g = g + s_ref[s]
        o_ref[...] = g

    return pl.pallas_call(body, name=name, out_shape=jax.ShapeDtypeStruct((r, c), F32), compiler_params=_cp(VMEM_BIG))(slots)


def _pack(arrs):
    flat = jnp.concatenate([a.reshape(-1) for a in arrs])
    return jnp.pad(flat, (0, (-flat.shape[0]) % (PACK_ROWS * PACK_W))).reshape(-1, PACK_W)


def _unpack(buf, shapes):
    flat = buf.reshape(-1)
    out, off = [], 0
    for s in shapes:
        n = int(np.prod(s))
        out.append(flat[off:off + n].reshape(s))
        off += n
    return out


def kernel(x, c, ctx, c_ctx, ada_w, ada_b, norm_g, even_w_in, even_w_out, attn_sink, ssm_a_re, ssm_a_im, ssm_log_dt, ssm_b_re, ssm_b_im, ssm_c_re, ssm_c_im, ssm_d, glu_w, glu_b, odd_w_in, odd_w_out, pool_w, pool_scale, final_g, loss_target, m_c_ctx, m_ada_w, m_ada_b, m_norm_g, m_even_w_in, m_even_w_out, m_attn_sink, m_ssm_a_re, m_ssm_a_im, m_ssm_log_dt, m_ssm_b_re, m_ssm_b_im, m_ssm_c_re, m_ssm_c_im, m_ssm_d, m_glu_w, m_glu_b, m_odd_w_in, m_odd_w_out, m_pool_w, m_pool_scale, m_final_g, v_c_ctx, v_ada_w, v_ada_b, v_norm_g, v_even_w_in, v_even_w_out, v_attn_sink, v_ssm_a_re, v_ssm_a_im, v_ssm_log_dt, v_ssm_b_re, v_ssm_b_im, v_ssm_c_re, v_ssm_c_im, v_ssm_d, v_glu_w, v_glu_b, v_odd_w_in, v_odd_w_out, v_pool_w, v_pool_scale, v_final_g):
    args = dict(locals())
    bsz, l, d = x.shape
    lc = ctx.shape[1]
    t = lc + l
    nct = lc // ROWT
    depth = ada_w.shape[0]
    nl2 = even_w_in.shape[0]
    me = _my_index()
    assert lc % ROWT == 0 and l % ROWT == 0 and d == 1024 and bsz * N_DEV < 32

    npw = pool_w.shape[2]
    shards = {"even": [_bf(jnp.transpose(even_w_in, (0, 2, 1))), _bf(even_w_out), _bf(glu_w)],
              "odd": [_bf(jnp.transpose(odd_w_in, (0, 2, 1))), _bf(odd_w_out), _bf(pool_w.reshape(nl2, -1, pool_w.shape[-1]))]}

    def wgather(kind, jl):
        return shards[kind], [("gather", [(ii, jl)], 0) for ii in range(3)]

    def wjoin(kind, res):
        full = [a.reshape(1, N_DEV * a.shape[2], a.shape[3]) for a in res]
        if kind == "odd":
            full[2] = jnp.transpose(res[2].reshape(1, N_DEV, 4, npw, 256), (0, 2, 1, 3, 4)).reshape(1, 4, 256, 256)
        return full

    xin, xsp = wgather("even", 0)
    got = _exchange(xin + [c, pool_scale], xsp + [("gather", [(3, None)], 0), ("gather", [(4, None)], 0)], "gather_first")
    weights = {("even", 0): wjoin("even", got[:3])}
    c_all = got[3].reshape(N_DEV * bsz, d)
    psc_full = jnp.transpose(got[4][0], (1, 0, 2)).reshape(nl2, 1, d)

    n_rows = 32
    craw = jnp.concatenate([c_all, c_ctx[None], jnp.zeros((n_rows - N_DEV * bsz - 1, d), F32)], axis=0)
    ncol = ada_w.shape[2]
    ada_b_loc = lax.dynamic_slice_in_dim(ada_b, me * ncol, ncol, axis=1)[:, None, :]
    m_loc = _adaln_fwd(craw, ada_w, ada_b_loc)
    m_all = _exchange([m_loc], [("gather", [(0, None)], 0)], "gather_mod")[0][0]
    m_all = jnp.transpose(m_all, (1, 2, 0, 3)).reshape(depth, n_rows, 3 * d)
    lat = lax.dynamic_slice_in_dim(m_all, me * bsz, bsz, axis=1).reshape(depth, bsz, 1, 3, d)
    cmod = jnp.broadcast_to(m_all[:, N_DEV * bsz].reshape(depth, 1, 1, 3, d), (depth, bsz, 1, 3, d))
    mods = jnp.concatenate([cmod, lat], axis=2).reshape(depth, 2 * bsz, 3, d)

    cos, sin = _rope_tables(lc, l)
    hs = jnp.concatenate([ctx, x], axis=1)
    g3 = norm_g[:, None, :]
    dsk3, gb3 = ssm_d[:, None, :], glu_b[:, None, :]
    u_col = 1280 // LANES

    s5_prm = (ssm_a_re, ssm_a_im, ssm_log_dt, ssm_b_re, ssm_b_im, ssm_c_re, ssm_c_im)
    ops = jax.vmap(_s5_operators)(*s5_prm)
    pr, pi = jax.vmap(lambda r, q: _scan_powers(r, q, t // CHUNK))(ops[4], ops[5])
    s5_ops = (ops[0], ops[1], _bf(ops[2]), _bf(ops[3]), pr, pi)
    saved = []
    for i in range(depth):
        jl = i // 2
        if i % 2 == 0:
            wt, wo, gwf = weights[("even", jl)]
            z = _fused_in_fwd(hs, mods, g3, wt, i, 0, nct, f"in_fwd{i}")
            qp, kvp = _rope_fwd(z, cos, sin, 2, f"rope_fwd{i}")
            (oa, lse), got = _attn_fwd(qp, kvp, attn_sink, jl, lc, f"attn_fwd{i}", xchg=wgather("odd", jl))
            weights[("odd", jl)] = wjoin("odd", got)
            nxt = wgather("even", jl + 1) if jl + 1 < nl2 else None
            (y,), got = _s5_fwd(z, u_col, s5_ops, jl, lc, f"s5_fwd{i}", xchg=nxt)
            if nxt:
                weights[("even", jl + 1)] = wjoin("even", got)
            hn = _even_out_fwd(hs, mods, oa, y, z, dsk3, gwf, gb3, wo, i, jl, nct, f"out_fwd{i}")
            saved.append(dict(hs=hs, z=z, qp=qp, kvp=kvp, oa=oa, lse=lse, y=y))
        else:
            wt, wo, pwf = weights[("odd", jl)]
            z = _fused_in_fwd(hs, mods, g3, wt, i, 0, nct, f"in_fwd{i}")
            ps = [_pool(z, gi, wd // 2, lc, False, f"pool_fwd{i}_{gi}") for gi, wd in enumerate(POOL_WINDOWS)]
            hn = _odd_out_fwd(hs, mods, ps, z, pwf, psc_full, wo, i, jl, nct, f"out_fwd{i}")
            saved.append(dict(hs=hs, z=z, ps=ps))
        hs = hn

    dh, loss_p, d_final_g = _final_loss(hs, final_g[None], loss_target, nct, "final_loss")
    loss = lax.psum(jnp.sum(loss_p[:, 0, 0]), ("x", "y", "c"))

    slots = {n: [None] * nl2 for n in ("even_w_in", "even_w_out", "glu_w", "odd_w_in", "odd_w_out", "pool_w", "pool_scale")}
    gsmall = {n: [None] * nl2 for n in ("attn_sink", "ssm_d", "glu_b")}
    d_norm_g, d_mods = [None] * depth, [None] * depth
    d_ops = [None] * nl2

    def rows_xchg(named):
        arrs = [a for _, _, a in named]
        specs = [("rows2" if n == "pool_w" else "rows", [(k, None)], a.shape[1 if n == "pool_w" else 0] // N_DEV)
                 for k, (n, _, a) in enumerate(named)]
        return (arrs, specs)

    def keep(named, res):
        for (n, jl_, _), r_ in zip(named, res):
            slots[n][jl_] = r_

    pending = None
    for i in reversed(range(depth)):
        jl = i // 2
        sv = saved[i]
        if i % 2 == 0:
            wt, wo, gwf = weights[("even", jl)]
            doa, dga, dy, dgs, dmod_g, ddsk, dgw, dgb, dwo = _even_out_bwd(
                dh, mods, sv["oa"], sv["y"], sv["z"], dsk3, gwf, gb3, wo, i, jl, nct, f"out_bwd{i}")
            (dq_r, dkv_acc, dsink), got = _attn_bwd(sv["qp"], sv["kvp"], attn_sink, jl, sv["oa"], sv["lse"], doa, lc,
                                                     f"attn_bwd{i}", xchg=rows_xchg(pending) if pending else None)
            if pending:
                keep(pending, got)
            dq, dkv = _rope_bwd(dq_r, dkv_acc, cos, sin, t, f"rope_bwd{i}")
            mine = [("even_w_out", jl, dwo), ("glu_w", jl, dgw)]
            (du, dqf, dqb, dbt, dct, dav), got = _s5_bwd(sv["z"], u_col, dy, s5_ops, jl, lc, f"s5_bwd{i}", xchg=rows_xchg(mine))
            keep(mine, got)
            per_group = lambda q: dav[:, q].reshape(SSM_GROUPS, SSM_STATE)
            d_ops[jl] = (dqf, dqb, dbt, dct, jnp.stack([per_group(0), per_group(2)]), jnp.stack([per_group(1), per_group(3)]))
            gsmall["attn_sink"][jl] = jnp.sum(dsink[:, :, 0], axis=0)
            gsmall["ssm_d"][jl] = ddsk[0]
            gsmall["glu_b"][jl] = dgb[0]
            pieces = [(dq, 512, 0), (dkv, 256, 0), (dga, 512, 0), (du, 512, 0), (dy, 512, 0), (dgs, 512, 0)]
            asm = lambda p, s: [p[0], p[1], p[2], p[3] + p[4] * s, p[5]]
            (dh, dmod_in, dg, dw), _ = _fused_in_bwd(sv["hs"], mods, g3, wt, dh, pieces, (dsk3, jl), asm, i, 0, nct, f"in_bwd{i}")
            pending = [("even_w_in", jl, dw)]
        else:
            wt, wo, pwf = weights[("odd", jl)]
            dp, dgz, dmod_g, dpw, dpsc, dwo = _odd_out_bwd(
                dh, mods, sv["ps"], sv["z"], pwf, psc_full, wo, i, jl, nct, f"out_bwd{i}")
            dus = [_pool(dp, gi, wd // 2, lc, True, f"pool_bwd{i}_{gi}") for gi, wd in enumerate(POOL_WINDOWS)]
            dpsc8 = jnp.broadcast_to(dpsc.reshape(N_DEV, 1, -1), (N_DEV, 8, d // N_DEV)).reshape(N_DEV * 8, -1)
            pieces = [(u_, 256, 0) for u_ in dus] + [(dgz, 1024, 0)]
            asm = lambda p, s: p
            (dh, dmod_in, dg, dw), got = _fused_in_bwd(sv["hs"], mods, g3, wt, dh, pieces, None, asm, i, 0, nct, f"in_bwd{i}",
                                                       xchg=rows_xchg(pending) if pending else None)
            if pending:
                keep(pending, got)
            pending = [("odd_w_in", jl, dw), ("odd_w_out", jl, dwo), ("pool_w", jl, dpw), ("pool_scale", jl, dpsc8)]
        d_norm_g[i] = dg[0]
        d_mods[i] = (dmod_g + dmod_in).reshape(bsz, 2, 3 * d)
    grad_x = dh[:, lc:]

    _, op_vjp = jax.vjp(jax.vmap(_s5_operators), *s5_prm)
    dprm = op_vjp(tuple(jnp.stack([d_ops[jl][q] for jl in range(nl2)]) for q in range(6)))

    dmd = jnp.stack(d_mods)
    dm_loc = jnp.concatenate([dmd[:, :, 1], jnp.sum(dmd[:, :, 0], axis=1, keepdims=True)], axis=1)
    dm_all = _exchange([dm_loc], [("gather", [(0, None)], 0)], "gather_dmod")[0][0]
    dm_lat = jnp.transpose(dm_all[:, :, :bsz], (1, 0, 2, 3)).reshape(depth, N_DEV * bsz, 3 * d)
    dm_ctx = dm_all[0, :, bsz]
    for e in range(1, N_DEV):
        dm_ctx = dm_ctx + dm_all[e, :, bsz]
    dm_rows = jnp.concatenate([dm_lat, dm_ctx[:, None], jnp.zeros((depth, n_rows - N_DEV * bsz - 1, 3 * d), F32)], axis=1)
    dm_cols = lax.dynamic_slice_in_dim(dm_rows, me * ncol, ncol, axis=2)
    g_ada_w, dcraw = _adaln_bwd(craw, dm_cols, ada_w)

    small = {"c_ctx": dcraw[N_DEV * bsz], "ada_b": jnp.sum(dm_loc, axis=1), "norm_g": jnp.stack(d_norm_g),
             "final_g": d_final_g[0]}
    for n, v2 in gsmall.items():
        small[n] = jnp.stack(v2)
    for n, gval in zip(("ssm_a_re", "ssm_a_im", "ssm_log_dt", "ssm_b_re", "ssm_b_im", "ssm_c_re", "ssm_c_im"), dprm):
        small[n] = gval
    snames = list(small)
    sshapes = [args[n].shape for n in snames]
    spack = _pack([small[n].reshape(args[n].shape) for n in snames])

    last = pending + [("small", 0, spack)]
    red = _exchange(*rows_xchg(last), "reduce_last")
    keep(pending, red[:-1])
    s_sum = _sum_slots(red[-1][0], "sum_small")
    s_all = _exchange([s_sum], [("gather", [(0, None)], 0)], "gather_small")[0]
    s_all = s_all.reshape(1, 1, -1, PACK_W)

    out = {}

    def put(n, res, shape):
        for kind, buf in zip(("grad_", "delta_", "new_m_", "new_v_"), res):
            out[kind + n] = buf.reshape(shape)

    def as3(a):
        return a.reshape(a.shape[0], -1, a.shape[-1])

    for n in slots:
        w = args[n]
        g_list = [s_[:, :, :1] if n == "pool_scale" else s_ for s_ in slots[n]]
        g_list = [s_.reshape(1, N_DEV, -1, s_.shape[-1]) for s_ in g_list]
        tr = n in ("even_w_in", "odd_w_in")
        put(n, _adamw(as3(w), g_list, as3(args["m_" + n]), as3(args["v_" + n]), tr, 256 if tr else None, "adamw_" + n),
            w.shape)
    put("ada_w", _adamw(ada_w, [g_ada_w[:, None]], m_ada_w, v_ada_w, False, None, "adamw_ada_w"), ada_w.shape)
    res = _adamw(_pack([args[n] for n in snames])[None], [s_all], _pack([args["m_" + n] for n in snames])[None],
                 _pack([args["v_" + n] for n in snames])[None], False, PACK_ROWS, "adamw_small")
    for kind, buf in zip(("grad_", "delta_", "new_m_", "new_v_"), res):
        for n, a in zip(snames, _unpack(buf, sshapes)):
            out[kind + n] = a

    order = ["c_ctx", "ada_w", "ada_b", "norm_g", "even_w_in", "even_w_out", "attn_sink", "ssm_a_re", "ssm_a_im",
             "ssm_log_dt", "ssm_b_re", "ssm_b_im", "ssm_c_re", "ssm_c_im", "ssm_d", "glu_w", "glu_b", "odd_w_in",
             "odd_w_out", "pool_w", "pool_scale", "final_g"]
    return (loss, grad_x, *[out[k + n] for k in ("grad_", "delta_", "new_m_", "new_v_") for n in order])
```

```python
import functools

import numpy as np
import jax
import jax.numpy as jnp
from jax import lax
from jax.experimental import pallas as pl
from jax.experimental.pallas import tpu as pltpu

F32 = jnp.float32
BF16 = jnp.bfloat16
EPS = 1e-6
NEG_INF = -1e30
N_DEV = 8
GRID_W = 64
WINDOW = 128
QBLK = 128
ROWT = 256
CHUNK = 16
SSM_GROUPS = 32
SSM_GROUP = 16
SSM_STATE = 64
POOL_WINDOWS = (2, 4, 8, 16)
ROPE_BASE = 10000.0
ADAM_LR, ADAM_B1, ADAM_B2, ADAM_EPS, ADAM_WD, ADAM_STEP = 0.001, 0.9, 0.999, 1e-08, 0.01, 10
LANES = 128
PACK_W = 1024
PACK_ROWS = 64
VMEM_BIG = 56 << 20


def _cp(vmem=None):
    return pltpu.CompilerParams(vmem_limit_bytes=vmem) if vmem else pltpu.CompilerParams()


def _nt(a, b):
    return lax.dot_general(a, b, (((1,), (1,)), ((), ())), preferred_element_type=F32)


def _tn(a, b):
    return lax.dot_general(a, b, (((0,), (0,)), ((), ())), preferred_element_type=F32)


def _nn(a, b):
    return jnp.dot(a, b, preferred_element_type=F32)


def _bf(x):
    return x.astype(BF16)


def _my_index():
    return 4 * lax.axis_index("x") + 2 * lax.axis_index("y") + lax.axis_index("c")


def _xchg_shapes(inputs, specs):
    out_shapes = []
    for kind, parts, r in specs:
        ii, li = parts[0]
        shp = tuple(inputs[ii].shape if li is None else inputs[ii].shape[1:])
        piece = shp if kind == "gather" else ((r,) + shp[1:] if kind == "rows" else (shp[0], r) + shp[2:])
        out_shapes.append(jax.ShapeDtypeStruct((len(parts), N_DEV) + piece, inputs[ii].dtype))
    return out_shapes, sum(len(parts) for _, parts, _ in specs)


def _xchg_sems(n_parts):
    return [pltpu.SemaphoreType.DMA((n_parts * (N_DEV - 1),)), pltpu.SemaphoreType.DMA((n_parts * (N_DEV - 1),)),
            pltpu.SemaphoreType.DMA((n_parts,))]


def _xchg_copies(specs, in_refs, out_refs, send_sems, recv_sems, loc_sems):
    x, y, c = lax.axis_index("x"), lax.axis_index("y"), lax.axis_index("c")
    me = 4 * x + 2 * y + c

    def piece(ref, kind, r, p):
        if kind == "gather":
            return ref
        if kind == "rows":
            return ref.at[pl.ds(p * r, r)]
        return ref.at[:, pl.ds(p * r, r)]

    copies, q = [], 0
    for si, (kind, parts, r) in enumerate(specs):
        for pi, (ii, li) in enumerate(parts):
            src = in_refs[ii] if li is None else in_refs[ii].at[li]
            dst = out_refs[si].at[pi, me]
            copies.append(pltpu.make_async_copy(piece(src, kind, r, me), dst, loc_sems.at[q]))
            for k in range(1, N_DEV):
                px = 1 - x if k & 4 else x
                py = 1 - y if k & 2 else y
                pc = 1 - c if k & 1 else c
                copies.append(pltpu.make_async_remote_copy(
                    src_ref=piece(src, kind, r, 4 * px + 2 * py + pc), dst_ref=dst,
                    send_sem=send_sems.at[q * (N_DEV - 1) + k - 1], recv_sem=recv_sems.at[q * (N_DEV - 1) + k - 1],
                    device_id=(px, py, pc), device_id_type=pl.DeviceIdType.MESH))
            q += 1
    return copies


def _exchange(inputs, specs, name):
    out_shapes, n_parts = _xchg_shapes(inputs, specs)
    ni, ns = len(inputs), len(specs)

    def body(*refs):
        copies = _xchg_copies(specs, refs[:ni], refs[ni:ni + ns], *refs[ni + ns:])
        for cp in copies:
            cp.start()
        for cp in copies:
            cp.wait()

    return pl.pallas_call(
        body, name=name, out_shape=tuple(out_shapes),
        in_specs=[pl.BlockSpec(memory_space=pl.ANY)] * ni,
        out_specs=tuple(pl.BlockSpec(memory_space=pl.ANY) for _ in specs),
        scratch_shapes=_xchg_sems(n_parts),
        compiler_params=pltpu.CompilerParams(has_side_effects=True),
    )(*inputs)


def _call(body, *, name, grid, in_specs, out_specs, out_shape, operands, vmem=None, scratch=(), xchg=None):
    out_shape, out_specs = tuple(out_shape), list(out_specs)
    if xchg is None:
        res = pl.pallas_call(body, name=name, grid=grid, in_specs=in_specs, out_specs=out_specs, out_shape=out_shape,
                             scratch_shapes=list(scratch), compiler_params=_cp(vmem))(*operands)
        return list(res), None
    xin, xspecs = xchg
    xshapes, n_parts = _xchg_shapes(xin, xspecs)
    ni, no, nxi, nxo, nsc = len(operands), len(out_shape), len(xin), len(xspecs), len(scratch)
    anyspec = pl.BlockSpec(memory_space=pl.ANY)

    def hosted(*refs):
        ins, xins = refs[:ni], refs[ni:ni + nxi]
        outs, xouts = refs[ni + nxi:ni + nxi + no], refs[ni + nxi + no:ni + nxi + no + nxo]
        scr, sems = refs[ni + nxi + no + nxo:ni + nxi + no + nxo + nsc], refs[ni + nxi + no + nxo + nsc:]
        copies = _xchg_copies(xspecs, xins, xouts, *sems)
        first, last = True, True
        for ax, n in enumerate(grid):
            first = first & (pl.program_id(ax) == 0)
            last = last & (pl.program_id(ax) == n - 1)

        @pl.when(first)
        def _():
            for cp in copies:
                cp.start()
        body(*ins, *outs, *scr)

        @pl.when(last)
        def _():
            for cp in copies:
                cp.wait()

    res = pl.pallas_call(
        hosted, name=name, grid=grid, in_specs=list(in_specs) + [anyspec] * nxi,
        out_specs=out_specs + [anyspec] * nxo, out_shape=out_shape + tuple(xshapes),
        scratch_shapes=list(scratch) + _xchg_sems(n_parts), compiler_params=_cp(vmem))(*operands, *xin)
    return list(res[:no]), list(res[no:])


def _adaln_fwd(craw, ada_w, ada_b):
    nl, d, n = ada_w.shape
    r = craw.shape[0]

    def body(c_ref, w_ref, b_ref, o_ref):
        s = jax.nn.silu(c_ref[...])
        o_ref[...] = _nn(_bf(s), _bf(w_ref[...])) + b_ref[...]

    return pl.pallas_call(
        body, name="adaln_fwd", grid=(nl,),
        in_specs=[pl.BlockSpec((r, d), lambda i: (0, 0)), pl.BlockSpec((None, d, n), lambda i: (i, 0, 0)),
                  pl.BlockSpec((None, 1, n), lambda i: (i, 0, 0))],
        out_specs=pl.BlockSpec((None, r, n), lambda i: (i, 0, 0)),
        out_shape=jax.ShapeDtypeStruct((nl, r, n), F32),
    )(craw, ada_w, ada_b)


def _adaln_bwd(craw, dm, ada_w):
    nl, d, n = ada_w.shape
    r = craw.shape[0]

    def body(c_ref, dm_ref, w_ref, gw_ref, dc_ref, acc_ref):
        i = pl.program_id(0)
        s, vj = jax.vjp(jax.nn.silu, c_ref[...])
        dmb = _bf(dm_ref[...])
        gw_ref[...] = _tn(_bf(s), dmb)

        @pl.when(i == 0)
        def _():
            acc_ref[...] = jnp.zeros_like(acc_ref)
        acc_ref[...] += _nt(dmb, _bf(w_ref[...]))

        @pl.when(i == nl - 1)
        def _():
            dc_ref[...] = vj(acc_ref[...])[0]

    return pl.pallas_call(
        body, name="adaln_bwd", grid=(nl,),
        in_specs=[pl.BlockSpec((r, d), lambda i: (0, 0)), pl.BlockSpec((None, r, n), lambda i: (i, 0, 0)),
                  pl.BlockSpec((None, d, n), lambda i: (i, 0, 0))],
        out_specs=[pl.BlockSpec((None, d, n), lambda i: (i, 0, 0)), pl.BlockSpec((r, d), lambda i: (0, 0))],
        out_shape=(jax.ShapeDtypeStruct((nl, d, n), F32), jax.ShapeDtypeStruct((r, d), F32)),
        scratch_shapes=[pltpu.VMEM((r, d), F32)],
    )(craw, dm, ada_w)


def _norm_mod(h, mod, g):
    ms = jnp.mean(h * h, axis=-1, keepdims=True)
    y = h * lax.rsqrt(ms + EPS) * g
    return y * (1.0 + mod[1:2]) + mod[0:1]


def _mod_spec(i, nct, d):
    return pl.BlockSpec((None, None, 3, d), lambda b, j: (i, 2 * b + jnp.where(j >= nct, 1, 0), 0, 0))


def _dmod_spec(nct, d):
    return pl.BlockSpec((None, 3, d), lambda b, j: (2 * b + jnp.where(j >= nct, 1, 0), 0, 0))


def _layer_spec(li, *shape):
    return pl.BlockSpec((None,) + tuple(shape), lambda b, j: (li,) + (0,) * len(shape))


def _tile(width, col_blk=0):
    return pl.BlockSpec((None, ROWT, width), lambda b, j: (b, j, col_blk))


def _fused_in_fwd(hs, mods, g, wt, i, jl, nct, name):
    bsz, t, d = hs.shape
    n = wt.shape[1]

    def body(h_ref, mod_ref, g_ref, w_ref, z_ref):
        a = _norm_mod(h_ref[...], mod_ref[...], g_ref[...])
        z_ref[...] = _nt(_bf(a), w_ref[...])

    return pl.pallas_call(
        body, name=name, grid=(bsz, t // ROWT),
        in_specs=[_tile(d), _mod_spec(i, nct, d), _layer_spec(i, 1, d), _layer_spec(jl, n, d)],
        out_specs=_tile(n), out_shape=jax.ShapeDtypeStruct((bsz, t, n), F32),
        compiler_params=_cp(VMEM_BIG),
    )(hs, mods, g, wt)


def _fused_in_bwd(hs, mods, g, wt, dh_out, pieces, small, assemble, i, jl, nct, name, xchg=None, latent_only=False):
    bsz, t, d = hs.shape
    n = wt.shape[1]
    npc, nsm = len(pieces), int(small is not None)

    def body(*refs):
        h_ref, mod_ref, g_ref, w_ref, dho_ref = refs[:5]
        p_refs = refs[5:5 + npc]
        s_val = refs[5 + npc][...] if nsm else None
        dhi_ref, dmod_ref, dg_ref, dw_ref = refs[5 + npc + nsm:]
        b, j = pl.program_id(0), pl.program_id(1)
        a, vj = jax.vjp(_norm_mod, h_ref[...], mod_ref[...], g_ref[...])
        dz = jnp.concatenate([_bf(p) for p in assemble([r[...] for r in p_refs], s_val)], axis=1)
        dh, dmod, dg = vj(_nn(dz, w_ref[...]))
        dhi_ref[...] = dho_ref[...] + dh

        @pl.when((j == 0) | (j == nct))
        def _():
            dmod_ref[...] = jnp.zeros_like(dmod_ref)

        @pl.when((b == 0) & (j == 0))
        def _():
            dg_ref[...] = jnp.zeros_like(dg_ref)
            dw_ref[...] = jnp.zeros_like(dw_ref)
        dmod_ref[...] += dmod
        dg_ref[...] += dg
        dw_ref[...] += _tn(dz, _bf(a))

    full = lambda r, c: pl.BlockSpec((r, c), lambda b, j: (0, 0))
    dh_rows = t - nct * ROWT if latent_only else t
    dh_spec = pl.BlockSpec((None, ROWT, d), lambda b, j: (b, jnp.maximum(j - nct, 0), 0)) if latent_only else _tile(d)
    return _call(
        body, name=name, grid=(bsz, t // ROWT),
        in_specs=[_tile(d), _mod_spec(i, nct, d), _layer_spec(i, 1, d), _layer_spec(jl, n, d), _tile(d)]
                 + [_tile(wd, cb) for _, wd, cb in pieces]
                 + ([_layer_spec(small[1], 1, small[0].shape[2])] if nsm else []),
        out_specs=[dh_spec, _dmod_spec(nct, d), full(1, d), full(n, d)],
        out_shape=(jax.ShapeDtypeStruct((bsz, dh_rows, d), F32), jax.ShapeDtypeStruct((2 * bsz, 3, d), F32),
                   jax.ShapeDtypeStruct((1, d), F32), jax.ShapeDtypeStruct((n, d), F32)),
        vmem=VMEM_BIG, xchg=xchg,
        operands=[hs, mods, g, wt, dh_out, *[p for p, _, _ in pieces], *([small[0]] if nsm else [])])


def _rope_tables(lc, l):
    tpos = np.arange(l)
    row = (tpos // GRID_W).astype(np.float32)
    col = (tpos % GRID_W).astype(np.float32)
    nf = 16
    inv = (np.float32(ROPE_BASE) ** (-np.arange(nf, dtype=np.float32) / np.float32(nf))).astype(np.float32)
    ang_r = row[:, None] * inv[None]
    ang_c = col[:, None] * inv[None]
    cos64 = np.concatenate([np.cos(ang_r), np.cos(ang_r), np.cos(ang_c), np.cos(ang_c)], axis=1)
    sin64 = np.concatenate([-np.sin(ang_r), np.sin(ang_r), -np.sin(ang_c), np.sin(ang_c)], axis=1)
    cos = np.concatenate([np.ones((lc, 128), np.float32), np.tile(cos64, (1, 2)).astype(np.float32)], axis=0)
    sin = np.concatenate([np.zeros((lc, 128), np.float32), np.tile(sin64, (1, 2)).astype(np.float32)], axis=0)
    return jnp.asarray(cos), jnp.asarray(sin)


def _rot(x, cos, sin):
    lane = lax.broadcasted_iota(jnp.int32, x.shape, 1)
    first = jnp.bitwise_and(jnp.right_shift(lane, 4), 1) == 0
    partner = jnp.where(first, pltpu.roll(x, LANES - 16, 1), pltpu.roll(x, 16, 1))
    return x * cos + partner * sin


def _split_heads(x):
    low = lax.broadcasted_iota(jnp.int32, x.shape, 1) < 64
    return jnp.where(low, x, 0.0), jnp.where(low, pltpu.roll(x, 64, 1), 0.0)


def _merge_heads(a, b):
    return a + pltpu.roll(b, 64, 1)


def _rope_fwd(z, cos, sin, kv_blk, name):
    bsz, t, _ = z.shape

    def body(q_ref, kv_ref, cos_ref, sin_ref, qp_ref, kvp_ref):
        c, s = cos_ref[...], sin_ref[...]
        outs = []
        for sl in range(4):
            xr = _rot(q_ref[:, sl * LANES:(sl + 1) * LANES], c, s) * 0.125
            outs += list(_split_heads(xr))
        qp_ref[...] = _bf(jnp.concatenate(outs, axis=1))
        k0, k1 = _split_heads(_rot(kv_ref[:, 0:LANES], c, s))
        v0, v1 = _split_heads(kv_ref[:, LANES:2 * LANES])
        kvp_ref[...] = _bf(jnp.concatenate([k0, k1, v0, v1], axis=1))

    return pl.pallas_call(
        body, name=name, grid=(bsz, t // ROWT),
        in_specs=[_tile(512), _tile(256, kv_blk),
                  pl.BlockSpec((ROWT, LANES), lambda b, j: (j, 0)), pl.BlockSpec((ROWT, LANES), lambda b, j: (j, 0))],
        out_specs=[_tile(1024), _tile(512)],
        out_shape=(jax.ShapeDtypeStruct((bsz, t, 1024), BF16), jax.ShapeDtypeStruct((bsz, t, 512), BF16)),
    )(z, z, cos, sin)


def _rope_bwd(dq_r, dkv_acc, cos, sin, t, name):
    bsz = dq_r.shape[0]

    def body(dq_ref, acc_ref, cos_ref, sin_ref, dqo_ref, dkvo_ref):
        c, s = cos_ref[...], -sin_ref[...]
        dqo_ref[...] = jnp.concatenate(
            [_rot(dq_ref[:, sl * LANES:(sl + 1) * LANES], c, s) for sl in range(4)], axis=1)
        dk = _merge_heads(acc_ref[:, 0:LANES], acc_ref[:, LANES:2 * LANES])
        dv = _merge_heads(acc_ref[:, 2 * LANES:3 * LANES], acc_ref[:, 3 * LANES:4 * LANES])
        dkvo_ref[...] = jnp.concatenate([_rot(dk, c, s), dv], axis=1)

    return pl.pallas_call(
        body, name=name, grid=(bsz, t // ROWT),
        in_specs=[_tile(512), _tile(512),
                  pl.BlockSpec((ROWT, LANES), lambda b, j: (j, 0)), pl.BlockSpec((ROWT, LANES), lambda b, j: (j, 0))],
        out_specs=[_tile(512), _tile(256)],
        out_shape=(jax.ShapeDtypeStruct((bsz, t, 512), F32), jax.ShapeDtypeStruct((bsz, t, 256), F32)),
    )(dq_r, dkv_acc, cos, sin)


def _attn_bias(j, ncb, l):
    n = j - ncb
    r = lax.broadcasted_iota(jnp.int32, (QBLK, 3 * QBLK), 0)
    cidx = lax.broadcasted_iota(jnp.int32, (QBLK, 3 * QBLK), 1)
    kpos = (n - 1) * QBLK + cidx
    valid = (jnp.abs(r - cidx + QBLK) <= WINDOW) & (kpos >= 0) & (kpos < l) & (j >= ncb)
    return jnp.where(valid, 0.0, NEG_INF).astype(F32)


def _attn_bias4(j, ncb, l):
    b1 = _attn_bias(j, ncb, l)
    return jnp.concatenate([b1, b1, b1, b1], axis=0)


def _sink_rows(sink_ref, jl, hk):
    head = jnp.right_shift(lax.broadcasted_iota(jnp.int32, (4 * QBLK, 1), 0), 7)
    sk = jnp.zeros((4 * QBLK, 1), F32)
    for g in range(4):
        sk = jnp.where(head == g, sink_ref[jl, 4 * hk + g], sk)
    return sk


def _kv_specs(nb, lc):
    return [pl.BlockSpec((None, QBLK, 512), lambda b, j: (b, jnp.maximum(j - 1, 0), 0)),
            pl.BlockSpec((None, QBLK, 512), lambda b, j: (b, j, 0)),
            pl.BlockSpec((None, QBLK, 512), lambda b, j: (b, jnp.minimum(j + 1, nb - 1), 0)),
            pl.BlockSpec((None, lc, 512), lambda b, j: (b, 0, 0))]


def _lane_pick(x, h):
    lane = lax.broadcasted_iota(jnp.int32, x.shape, 1)
    return jnp.sum(jnp.where(lane == h, x, 0.0), axis=1, keepdims=True)


def _attn_fwd(qp, kvp, sinks, jl, lc, name, xchg=None):
    bsz, t, _ = qp.shape
    nb, ncb, l = t // QBLK, lc // QBLK, t - lc

    def body(sink_ref, q_ref, kp_ref, kc_ref, kn_ref, kx_ref, o_ref, lse_ref):
        j = pl.program_id(1)
        kloc = jnp.concatenate([kp_ref[...], kc_ref[...], kn_ref[...]], axis=0)
        kctx = kx_ref[...]
        bias = _attn_bias4(j, ncb, l)
        lane = lax.broadcasted_iota(jnp.int32, (QBLK, LANES), 1)
        lse_all = jnp.zeros((QBLK, LANES), F32)
        outs = []
        for hk in range(2):
            ks, vs = slice(hk * LANES, (hk + 1) * LANES), slice((2 + hk) * LANES, (3 + hk) * LANES)
            q4 = jnp.concatenate([q_ref[:, (4 * hk + g) * LANES:(4 * hk + g + 1) * LANES] for g in range(4)], axis=0)
            s_loc = _nt(q4, kloc[:, ks]) + bias
            s_ctx = _nt(q4, kctx[:, ks])
            sk = _sink_rows(sink_ref, jl, hk)
            m = jnp.maximum(jnp.maximum(jnp.max(s_loc, axis=1, keepdims=True), jnp.max(s_ctx, axis=1, keepdims=True)), sk)
            p_loc, p_ctx = jnp.exp(s_loc - m), jnp.exp(s_ctx - m)
            den = jnp.sum(p_loc, axis=1, keepdims=True) + jnp.sum(p_ctx, axis=1, keepdims=True) + jnp.exp(sk - m)
            o4 = (_nn(_bf(p_loc), kloc[:, vs]) + _nn(_bf(p_ctx), kctx[:, vs])) / den
            lse4 = m + jnp.log(den)
            for g in range(4):
                outs.append(o4[g * QBLK:(g + 1) * QBLK])
                lse_all = lse_all + jnp.where(lane == 4 * hk + g, lse4[g * QBLK:(g + 1) * QBLK], 0.0)
        o_ref[...] = jnp.concatenate([_merge_heads(outs[2 * i], outs[2 * i + 1]) for i in range(4)], axis=1)
        lse_ref[...] = lse_all

    return _call(
        body, name=name, grid=(bsz, nb),
        in_specs=[pl.BlockSpec(memory_space=pltpu.SMEM),
                  pl.BlockSpec((None, QBLK, 1024), lambda b, j: (b, j, 0))] + _kv_specs(nb, lc),
        out_specs=[pl.BlockSpec((None, QBLK, 512), lambda b, j: (b, j, 0)),
                   pl.BlockSpec((None, QBLK, LANES), lambda b, j: (b, j, 0))],
        out_shape=(jax.ShapeDtypeStruct((bsz, t, 512), F32), jax.ShapeDtypeStruct((bsz, t, LANES), F32)),
        xchg=xchg, operands=[sinks, qp, kvp, kvp, kvp, kvp])


def _attn_bwd(qp, kvp, sinks, jl, o, lse, do, lc, name, xchg=None):
    bsz, t, _ = qp.shape
    nb, ncb, l = t // QBLK, lc // QBLK, t - lc
    nk = 3 * QBLK

    def body(sink_ref, q_ref, kp_ref, kc_ref, kn_ref, kx_ref, o_ref, lse_ref, do_ref, dq_ref, acc_ref, ds_ref):
        j = pl.program_id(1)

        @pl.when(j == 0)
        def _():
            acc_ref[...] = jnp.zeros_like(acc_ref)
            ds_ref[...] = jnp.zeros_like(ds_ref)
        kloc = jnp.concatenate([kp_ref[...], kc_ref[...], kn_ref[...]], axis=0)
        kctx = kx_ref[...]
        bias = _attn_bias4(j, ncb, l)
        lse = lse_ref[...]
        row8 = lax.broadcasted_iota(jnp.int32, (8, LANES), 0)
        dsink = jnp.zeros((8, LANES), F32)
        dqs = []
        d_loc, d_ctx = [None] * 4, [None] * 4
        for hk in range(2):
            ks, vs = slice(hk * LANES, (hk + 1) * LANES), slice((2 + hk) * LANES, (3 + hk) * LANES)
            q4 = jnp.concatenate([q_ref[:, (4 * hk + g) * LANES:(4 * hk + g + 1) * LANES] for g in range(4)], axis=0)
            do4 = jnp.concatenate([x_ for s2 in (2 * hk, 2 * hk + 1)
                                   for x_ in _split_heads(do_ref[:, s2 * LANES:(s2 + 1) * LANES])], axis=0)
            o4 = jnp.concatenate([x_ for s2 in (2 * hk, 2 * hk + 1)
                                  for x_ in _split_heads(o_ref[:, s2 * LANES:(s2 + 1) * LANES])], axis=0)
            delta = jnp.sum(do4 * o4, axis=1, keepdims=True)
            lse4 = jnp.concatenate([_lane_pick(lse, 4 * hk + g) for g in range(4)], axis=0)
            p_loc = jnp.exp(_nt(q4, kloc[:, ks]) + bias - lse4)
            p_ctx = jnp.exp(_nt(q4, kctx[:, ks]) - lse4)
            dob = _bf(do4)
            ds_loc = _bf(p_loc * (_nt(dob, kloc[:, vs]) - delta))
            ds_ctx = _bf(p_ctx * (_nt(dob, kctx[:, vs]) - delta))
            dq4 = (_nn(ds_loc, kloc[:, ks]) + _nn(ds_ctx, kctx[:, ks])) * 0.125
            d_loc[hk], d_ctx[hk] = _tn(ds_loc, q4), _tn(ds_ctx, q4)
            d_loc[2 + hk], d_ctx[2 + hk] = _tn(_bf(p_loc), dob), _tn(_bf(p_ctx), dob)
            dsk = -jnp.exp(_sink_rows(sink_ref, jl, hk) - lse4) * delta
            for g in range(4):
                dqs.append(dq4[g * QBLK:(g + 1) * QBLK])
                dsink = dsink + jnp.where(row8 == 4 * hk + g, jnp.sum(dsk[g * QBLK:(g + 1) * QBLK]), 0.0)
        dq_ref[...] = jnp.concatenate([_merge_heads(dqs[2 * i], dqs[2 * i + 1]) for i in range(4)], axis=1)
        ds_ref[...] += dsink
        acc_ref[pl.ds(0, lc), :] += jnp.concatenate(d_ctx, axis=1)

        @pl.when(j >= ncb)
        def _():
            st = pl.multiple_of((j - 1) * QBLK, QBLK)
            acc_ref[pl.ds(st, nk), :] += jnp.concatenate(d_loc, axis=1)

    blk = lambda wd: pl.BlockSpec((None, QBLK, wd), lambda b, j: (b, j, 0))
    return _call(
        body, name=name, grid=(bsz, nb),
        in_specs=[pl.BlockSpec(memory_space=pltpu.SMEM), blk(1024)] + _kv_specs(nb, lc)
                 + [blk(512), blk(LANES), blk(512)],
        out_specs=[blk(512), pl.BlockSpec((None, t + QBLK, 512), lambda b, j: (b, 0, 0)),
                   pl.BlockSpec((None, 8, LANES), lambda b, j: (b, 0, 0))],
        out_shape=(jax.ShapeDtypeStruct((bsz, t, 512), F32), jax.ShapeDtypeStruct((bsz, t + QBLK, 512), F32),
                   jax.ShapeDtypeStruct((bsz, 8, LANES), F32)),
        vmem=VMEM_BIG, xchg=xchg, operands=[sinks, qp, kvp, kvp, kvp, kvp, o, lse, do])


def _s5_operators(a_re, a_im, log_dt, b_re, b_im, c_re, c_im):
    hi = lax.Precision.HIGHEST
    dt = jnp.exp(log_dt)[..., None]
    tau = jnp.arange(CHUNK + 1, dtype=F32)[:, None, None, None]
    mag = jnp.exp(tau * (a_re * dt))
    pwr, pwi = mag * jnp.cos(tau * (a_im * dt)), mag * jnp.sin(tau * (a_im * dt))
    ar, ai = pwr[1], pwi[1]
    den = a_re * a_re + a_im * a_im
    fr = ((ar - 1.0) * a_re + ai * a_im) / den
    fi = (ai * a_re - (ar - 1.0) * a_im) / den
    bbr = fr[..., None] * b_re - fi[..., None] * b_im
    bbi = fr[..., None] * b_im + fi[..., None] * b_re
    wr = pwr[:CHUNK, ..., None] * bbr - pwi[:CHUNK, ..., None] * bbi
    wi = pwr[:CHUNK, ..., None] * bbi + pwi[:CHUNK, ..., None] * bbr
    kern = (jnp.einsum("dgcp,tdgpk->dgtck", c_re, wr, precision=hi)
            - jnp.einsum("dgcp,tdgpk->dgtck", c_im, wi, precision=hi))
    g = a_re.shape[1]
    nrow = CHUNK * SSM_GROUP
    qf = jnp.transpose(kern[0], (0, 3, 1, 2)).reshape(g, SSM_GROUP, nrow)
    qb = jnp.transpose(kern[1][:, ::-1], (0, 3, 1, 2)).reshape(g, SSM_GROUP, nrow)
    to_rows = lambda w: jnp.transpose(w, (1, 0, 3, 2)).reshape(g, nrow, SSM_STATE)
    bt = jnp.concatenate([to_rows(wr[::-1, 0]), to_rows(wi[::-1, 0]), to_rows(wr[:, 1]), to_rows(wi[:, 1])], axis=-1)

    def out_map(d, pw_r, pw_i):
        w2r = c_re[d][None] * pw_r[:, :, None, :] - c_im[d][None] * pw_i[:, :, None, :]
        w2i = c_re[d][None] * pw_i[:, :, None, :] + c_im[d][None] * pw_r[:, :, None, :]
        cols = lambda w: jnp.transpose(w, (1, 3, 0, 2)).reshape(g, SSM_STATE, nrow)
        return [cols(w2r), -cols(w2i)]

    ct = jnp.concatenate(out_map(0, pwr[1:, 0], pwi[1:, 0]) + out_map(1, pwr[1:, 1][::-1], pwi[1:, 1][::-1]), axis=-2)
    return qf, qb, bt, ct, pwr[CHUNK], pwi[CHUNK]


def _scan_powers(a16r, a16i, n_chunks):
    prs, pis = [], []
    r, i = a16r, a16i
    s = 1
    while s < n_chunks:
        prs.append(jnp.concatenate([r, r], axis=-1))
        pis.append(jnp.concatenate([-i, i], axis=-1))
        r, i = r * r - i * i, 2.0 * r * i
        s *= 2
    pad = 16 - len(prs)
    z = jnp.zeros_like(prs[0])
    return jnp.stack(prs + [z] * pad, axis=2), jnp.stack(pis + [z] * pad, axis=2)


def _low_half():
    return lax.broadcasted_iota(jnp.int32, (1, LANES), 1) < 64


def _to_pair(sa, sb):
    low = _low_half()
    return jnp.where(low, sa, pltpu.roll(sb, 64, 1)), jnp.where(low, pltpu.roll(sa, 64, 1), sb)


def _from_pair(hr, hi):
    low = _low_half()
    return jnp.where(low, hr, pltpu.roll(hi, 64, 1)), jnp.where(low, pltpu.roll(hr, 64, 1), hi)


def _pair_scan(sr, si, pr, pi, reverse, conj):
    n = sr.shape[0]
    row = lax.broadcasted_iota(jnp.int32, sr.shape, 0)

    def shift(x, k):
        if reverse:
            return jnp.where(row < n - k, pltpu.roll(x, n - k, 0), 0.0)
        return jnp.where(row >= k, pltpu.roll(x, k, 0), 0.0)

    hr, hi = shift(sr, 1), shift(si, 1)
    k, j = 1, 0
    while k < n:
        tr, ti = shift(hr, k), shift(hi, k)
        ar = pr[j:j + 1, :]
        ai = -pi[j:j + 1, :] if conj else pi[j:j + 1, :]
        hr, hi = hr + ar * tr - ai * ti, hi + ar * ti + ai * tr
        k, j = 2 * k, j + 1
    return hr, hi


def _transpose8(a):
    a = list(a)
    blk = jnp.right_shift(lax.broadcasted_iota(jnp.int32, (1, LANES), 1), 4)
    for dd in (4, 2, 1):
        upper = jnp.bitwise_and(blk, dd) != 0
        for i in range(8):
            if i & dd:
                continue
            lo, hi = a[i], a[i + dd]
            a[i] = jnp.where(upper, pltpu.roll(hi, dd * SSM_GROUP, 1), lo)
            a[i + dd] = jnp.where(upper, hi, pltpu.roll(lo, LANES - dd * SSM_GROUP, 1))
    return a


def _blocks_from_rows(xs):
    t0, t1 = _transpose8(xs[:8]), _transpose8(xs[8:])
    return [jnp.concatenate([t0[g], t1[g]], axis=1) for g in range(8)]


def _rows_from_blocks(ys):
    return _transpose8([y[:, :LANES] for y in ys]) + _transpose8([y[:, LANES:] for y in ys])


def _pair_states(sa, sb, pr_ref, pi_ref, ga, gb, ncc, adjoint):
    n = sa.shape[0]
    low = _low_half()
    out_a, out_b, pairs = [], [], []
    for dirn in range(2):
        xr, xi = _to_pair(sa[:, dirn * LANES:(dirn + 1) * LANES], sb[:, dirn * LANES:(dirn + 1) * LANES])
        pr = jnp.where(low, pr_ref[dirn, ga], pr_ref[dirn, gb])
        pi = jnp.where(low, -pi_ref[dirn, ga], pi_ref[dirn, gb])
        if dirn == 0:
            hr, hi = _pair_scan(xr, xi, pr, pi, adjoint, adjoint)
        else:
            hr, hi = _pair_scan(pltpu.roll(xr, n - ncc, 0), pltpu.roll(xi, n - ncc, 0), pr, pi, not adjoint, adjoint)
            hr, hi = pltpu.roll(hr, ncc, 0), pltpu.roll(hi, ncc, 0)
        pairs.append((hr, hi))
        ha, hb = _from_pair(hr, hi)
        out_a.append(ha)
        out_b.append(hb)
    return (jnp.concatenate(out_a, axis=1), jnp.concatenate(out_b, axis=1)), pairs


def _shift_lanes(x, n, left):
    if n == 0:
        return x
    lo, hi = x[:, :LANES], x[:, LANES:]
    lane = lax.broadcasted_iota(jnp.int32, lo.shape, 1)
    zero = jnp.zeros_like(lo)
    m = n % LANES
    if not left:
        if n < LANES:
            r_lo, r_hi = pltpu.roll(lo, n, 1), pltpu.roll(hi, n, 1)
            return jnp.concatenate([jnp.where(lane >= n, r_lo, 0.0), jnp.where(lane >= n, r_hi, r_lo)], axis=1)
        r_lo = pltpu.roll(lo, m, 1) if m else lo
        return jnp.concatenate([zero, jnp.where(lane >= m, r_lo, 0.0)], axis=1)
    if n < LANES:
        r_lo, r_hi = pltpu.roll(lo, LANES - n, 1), pltpu.roll(hi, LANES - n, 1)
        return jnp.concatenate([jnp.where(lane < LANES - n, r_lo, r_hi), jnp.where(lane < LANES - n, r_hi, 0.0)], axis=1)
    r_hi = pltpu.roll(hi, LANES - m, 1) if m else hi
    return jnp.concatenate([jnp.where(lane < LANES - m, r_hi, 0.0), zero], axis=1)


def _toeplitz(qf, qb):
    return jnp.concatenate([_shift_lanes(qf, SSM_GROUP * s, False) + _shift_lanes(qb, SSM_GROUP * (CHUNK - 1 - s), True)
                            for s in range(CHUNK)], axis=0)


def _toeplitz_t(dmt):
    dqf, dqb = 0.0, 0.0
    for s in range(CHUNK):
        rows = dmt[s * SSM_GROUP:(s + 1) * SSM_GROUP]
        dqf = dqf + _shift_lanes(rows, SSM_GROUP * s, True)
        dqb = dqb + _shift_lanes(rows, SSM_GROUP * (CHUNK - 1 - s), False)
    return dqf, dqb


def _s5_op_specs(idx, jl):
    q = pl.BlockSpec((None, 8, SSM_GROUP, 256), lambda a, b: (jl, idx(a, b), 0, 0))
    op = pl.BlockSpec((None, 8, 256, 256), lambda a, b: (jl, idx(a, b), 0, 0))
    pw = pl.BlockSpec((None, 2, 8, 16, LANES), lambda a, b: (jl, 0, idx(a, b), 0, 0))
    return [q, q, op, op, pw, pw]


def _s5_fwd(z, col0, s5_ops, jl, lc, name, xchg=None):
    bsz, t, _ = z.shape
    nc, ncc = t // CHUNK, lc // CHUNK

    def body(u_ref, qf_ref, qb_ref, bt_ref, ct_ref, pr_ref, pi_ref, y_ref):
        us = [_bf(u) for u in _blocks_from_rows([u_ref[pl.ds(tok, nc, stride=CHUNK), :] for tok in range(CHUNK)])]
        ys = []
        for ga in range(0, 8, 2):
            gb = ga + 1
            hs, _ = _pair_states(_nn(us[ga], bt_ref[ga]), _nn(us[gb], bt_ref[gb]), pr_ref, pi_ref, ga, gb, ncc, False)
            for g, h in zip((ga, gb), hs):
                ys.append(_nn(us[g], _bf(_toeplitz(qf_ref[g], qb_ref[g]))) + _nn(_bf(h), ct_ref[g]))
        for tok, rows in enumerate(_rows_from_blocks(ys)):
            y_ref[pl.ds(tok, nc, stride=CHUNK), :] = rows

    return _call(
        body, name=name, grid=(bsz, 4),
        in_specs=[pl.BlockSpec((None, t, LANES), lambda b, s: (b, 0, col0 + s))] + _s5_op_specs(lambda b, s: s, jl),
        out_specs=[pl.BlockSpec((None, t, LANES), lambda b, s: (b, 0, s))],
        out_shape=[jax.ShapeDtypeStruct((bsz, t, 512), F32)], vmem=VMEM_BIG, xchg=xchg,
        operands=[z, *s5_ops])


def _s5_bwd(z, col0, dy, s5_ops, jl, lc, name, xchg=None):
    bsz, t, _ = z.shape
    nc, ncc = t // CHUNK, lc // CHUNK

    def body(u_ref, dy_ref, qf_ref, qb_ref, bt_ref, ct_ref, pr_ref, pi_ref,
             du_ref, dqf_ref, dqb_ref, dbt_ref, dct_ref, da_ref):
        b = pl.program_id(1)

        @pl.when(b == 0)
        def _():
            dqf_ref[...] = jnp.zeros_like(dqf_ref)
            dqb_ref[...] = jnp.zeros_like(dqb_ref)
            dbt_ref[...] = jnp.zeros_like(dbt_ref)
            dct_ref[...] = jnp.zeros_like(dct_ref)
            da_ref[...] = jnp.zeros_like(da_ref)
        us = [_bf(u) for u in _blocks_from_rows([u_ref[pl.ds(tok, nc, stride=CHUNK), :] for tok in range(CHUNK)])]
        dys = [_bf(u) for u in _blocks_from_rows([dy_ref[pl.ds(tok, nc, stride=CHUNK), :] for tok in range(CHUNK)])]
        row8 = lax.broadcasted_iota(jnp.int32, (8, LANES), 0)
        dus = []
        for ga in range(0, 8, 2):
            gb = ga + 1
            hs, hp = _pair_states(_nn(us[ga], bt_ref[ga]), _nn(us[gb], bt_ref[gb]), pr_ref, pi_ref, ga, gb, ncc, False)
            gs, gp = _pair_states(_nt(dys[ga], ct_ref[ga]), _nt(dys[gb], ct_ref[gb]), pr_ref, pi_ref, ga, gb, ncc, True)
            for g, h, gg in zip((ga, gb), hs, gs):
                hcat, gcat = _bf(h), _bf(gg)
                dus.append(_nt(dys[g], _bf(_toeplitz(qf_ref[g], qb_ref[g]))) + _nt(gcat, bt_ref[g]))
                dqf, dqb = _toeplitz_t(_tn(us[g], dys[g]))
                dqf_ref[g] += dqf
                dqb_ref[g] += dqb
                dct_ref[g] += _tn(hcat, dys[g])
                dbt_ref[g] += _tn(us[g], gcat)
            da = jnp.zeros((8, LANES), F32)
            for dirn in range(2):
                (hr, hi), (gr, gi) = hp[dirn], gp[dirn]
                da = da + jnp.where(row8 == 2 * dirn, jnp.sum(gr * hr + gi * hi, axis=0, keepdims=True), 0.0)
                da = da + jnp.where(row8 == 2 * dirn + 1, jnp.sum(gi * hr - gr * hi, axis=0, keepdims=True), 0.0)
            da_ref[ga // 2] += da
        for tok, rows in enumerate(_rows_from_blocks(dus)):
            du_ref[pl.ds(tok, nc, stride=CHUNK), :] = rows

    op_out = pl.BlockSpec((8, 256, 256), lambda s, b: (s, 0, 0))
    op_shape = jax.ShapeDtypeStruct((SSM_GROUPS, 256, 256), F32)
    q_out = pl.BlockSpec((8, SSM_GROUP, 256), lambda s, b: (s, 0, 0))
    q_shape = jax.ShapeDtypeStruct((SSM_GROUPS, SSM_GROUP, 256), F32)
    return _call(
        body, name=name, grid=(4, bsz),
        in_specs=[pl.BlockSpec((None, t, LANES), lambda s, b: (b, 0, col0 + s)),
                  pl.BlockSpec((None, t, LANES), lambda s, b: (b, 0, s))] + _s5_op_specs(lambda s, b: s, jl),
        out_specs=[pl.BlockSpec((None, t, LANES), lambda s, b: (b, 0, s)), q_out, q_out, op_out, op_out,
                   pl.BlockSpec((4, 8, LANES), lambda s, b: (s, 0, 0))],
        out_shape=(jax.ShapeDtypeStruct((bsz, t, 512), F32), q_shape, q_shape, op_shape, op_shape,
                   jax.ShapeDtypeStruct((SSM_GROUPS // 2, 8, LANES), F32)),
        vmem=VMEM_BIG, xchg=xchg, operands=[z, dy, *s5_ops])


def _glu_in(y, u, dsk):
    return jax.nn.gelu(y + u * dsk)


def _even_mix(oa, ga, zg, tg, gs):
    return jnp.concatenate([oa * jax.nn.silu(ga), zg * jax.nn.sigmoid(tg) * jax.nn.silu(gs)], axis=1)


def _even_specs(d, i, jl, nct):
    return [_tile(d), _mod_spec(i, nct, d), _tile(512), _tile(512)] + [_tile(256, cb) for cb in range(3, 9)] + [
        _layer_spec(jl, 1, 512), _layer_spec(0, 512, 512), _layer_spec(jl, 1, 512), _layer_spec(0, 1024, d)]


def _cat2(a_ref, b_ref):
    return jnp.concatenate([a_ref[...], b_ref[...]], axis=1)


def _even_out_fwd(hs, mods, oa, y, z, dsk, gw, gb, wout, i, jl, nct, name):
    bsz, t, d = hs.shape

    def body(h_ref, mod_ref, oa_ref, y_ref, ga0, ga1, u0, u1, gs0, gs1, dsk_ref, gw_ref, gb_ref, wo_ref, o_ref):
        zg = _glu_in(y_ref[...], _cat2(u0, u1), dsk_ref[...])
        tg = _nn(_bf(zg), gw_ref[...]) + gb_ref[...]
        mix = _even_mix(oa_ref[...], _cat2(ga0, ga1), zg, tg, _cat2(gs0, gs1))
        o_ref[...] = h_ref[...] + mod_ref[2:3, :] * _nn(_bf(mix), wo_ref[...])

    return pl.pallas_call(
        body, name=name, grid=(bsz, t // ROWT), in_specs=_even_specs(d, i, jl, nct),
        out_specs=_tile(d), out_shape=jax.ShapeDtypeStruct((bsz, t, d), F32), compiler_params=_cp(VMEM_BIG),
    )(hs, mods, oa, y, z, z, z, z, z, z, dsk, gw, gb, wout)


def _even_out_bwd(dh, mods, oa, y, z, dsk, gw, gb, wout, i, jl, nct, name):
    bsz, t, d = dh.shape

    def body(dh_ref, mod_ref, oa_ref, y_ref, ga0, ga1, u0, u1, gs0, gs1, dsk_ref, gw_ref, gb_ref, wo_ref,
             doa_ref, dga_ref, dy_ref, dgs_ref, dmod_ref, ddsk_ref, dgw_ref, dgb_ref, dwo_ref):
        b, j = pl.program_id(0), pl.program_id(1)
        zg, vj1 = jax.vjp(_glu_in, y_ref[...], _cat2(u0, u1), dsk_ref[...])
        zgb = _bf(zg)
        tg = _nn(zgb, gw_ref[...]) + gb_ref[...]
        mix, vj2 = jax.vjp(_even_mix, oa_ref[...], _cat2(ga0, ga1), zg, tg, _cat2(gs0, gs1))
        mixb = _bf(mix)
        dhv = dh_ref[...]
        dyo = _bf(dhv * mod_ref[2:3, :])
        yout = _nn(mixb, wo_ref[...])
        doa, dga, dzg, dtg, dgs = vj2(_nt(dyo, wo_ref[...]))
        dtb = _bf(dtg)
        dzg = dzg + _nt(dtb, gw_ref[...])
        dy, _, ddsk = vj1(dzg)
        doa_ref[...], dga_ref[...], dy_ref[...], dgs_ref[...] = doa, dga, dy, dgs

        @pl.when((j == 0) | (j == nct))
        def _():
            dmod_ref[...] = jnp.zeros_like(dmod_ref)

        @pl.when((b == 0) & (j == 0))
        def _():
            ddsk_ref[...] = jnp.zeros_like(ddsk_ref)
            dgw_ref[...] = jnp.zeros_like(dgw_ref)
            dgb_ref[...] = jnp.zeros_like(dgb_ref)
            dwo_ref[...] = jnp.zeros_like(dwo_ref)
        dmod_ref[2:3, :] += jnp.sum(dhv * yout, axis=0, keepdims=True)
        ddsk_ref[...] += ddsk
        dgw_ref[...] += _tn(zgb, dtb)
        dgb_ref[...] += jnp.sum(dtg, axis=0, keepdims=True)
        dwo_ref[...] += _tn(mixb, dyo)

    full = lambda r, c: pl.BlockSpec((r, c), lambda b, j: (0, 0))
    f512 = jax.ShapeDtypeStruct((bsz, t, 512), F32)
    return pl.pallas_call(
        body, name=name, grid=(bsz, t // ROWT), in_specs=_even_specs(d, i, jl, nct),
        out_specs=[_tile(512), _tile(512), _tile(512), _tile(512), _dmod_spec(nct, d),
                   full(1, 512), full(512, 512), full(1, 512), full(1024, d)],
        out_shape=(f512, f512, f512, f512, jax.ShapeDtypeStruct((2 * bsz, 3, d), F32),
                   jax.ShapeDtypeStruct((1, 512), F32), jax.ShapeDtypeStruct((512, 512), F32),
                   jax.ShapeDtypeStruct((1, 512), F32), jax.ShapeDtypeStruct((1024, d), F32)),
        compiler_params=_cp(VMEM_BIG),
    )(dh, mods, oa, y, z, z, z, z, z, z, dsk, gw, gb, wout)


def _pool(src, col_blk, r, lc, transpose, name):
    bsz, t, _ = src.shape
    segs = [(0, lc, 16), (lc, t - lc, 32 + lc)]
    halo = [(0, 16), (16 + lc, 16), (32 + t, 16)]
    cw = 256

    def inv_count(t0, rows, ln):
        pos = t0 + lax.broadcasted_iota(jnp.int32, (rows, cw), 0)
        cnt = jnp.minimum(pos + r + 1, ln) - jnp.maximum(pos - r, 0)
        return 1.0 / cnt.astype(F32)

    def body(x_ref, o_ref, s_ref):
        for h0, hn in halo:
            s_ref[pl.ds(h0, hn), :] = jnp.zeros((hn, cw), F32)
        for r0, ln, s0 in segs:
            step = min(512, ln)
            for c0 in range(0, ln, step):
                v = x_ref[pl.ds(r0 + c0, step), :]
                s_ref[pl.ds(s0 + c0, step), :] = v * inv_count(c0, step, ln) if transpose else v
        for r0, ln, s0 in segs:
            step = min(512, ln)
            for c0 in range(0, ln, step):
                acc = s_ref[pl.ds(s0 + c0 - r, step), :]
                for dlt in range(-r + 1, r + 1):
                    acc = acc + s_ref[pl.ds(s0 + c0 + dlt, step), :]
                ctr = x_ref[pl.ds(r0 + c0, step), :]
                o_ref[pl.ds(r0 + c0, step), :] = (acc if transpose else acc * inv_count(c0, step, ln)) - ctr

    return pl.pallas_call(
        body, name=name, grid=(bsz,),
        in_specs=[pl.BlockSpec((None, t, cw), lambda b: (b, 0, col_blk))],
        out_specs=pl.BlockSpec((None, t, cw), lambda b: (b, 0, 0)),
        out_shape=jax.ShapeDtypeStruct((bsz, t, cw), F32),
        scratch_shapes=[pltpu.VMEM((t + 48, cw), F32)],
        compiler_params=_cp(VMEM_BIG),
    )(src)


def _odd_specs(d, i, jl, nct):
    return [_tile(d), _mod_spec(i, nct, d), _tile(256), _tile(256), _tile(256), _tile(256), _tile(1024, 1),
            _layer_spec(0, 4, 256, 256), _layer_spec(jl, 1, 1024), _layer_spec(0, 1024, d)]


def _odd_mix(pm, psc, gz):
    return pm * psc * jax.nn.silu(gz)


def _odd_out_fwd(hs, mods, ps, z, pw, psc, wout, i, jl, nct, name):
    bsz, t, d = hs.shape

    def body(h_ref, mod_ref, p0, p1, p2, p3, gz_ref, pw_ref, psc_ref, wo_ref, o_ref):
        pm = jnp.concatenate([_nn(_bf(p[...]), pw_ref[q]) for q, p in enumerate((p0, p1, p2, p3))], axis=1)
        mix = _odd_mix(pm, psc_ref[...], gz_ref[...])
        o_ref[...] = h_ref[...] + mod_ref[2:3, :] * _nn(_bf(mix), wo_ref[...])

    return pl.pallas_call(
        body, name=name, grid=(bsz, t // ROWT), in_specs=_odd_specs(d, i, jl, nct),
        out_specs=_tile(d), out_shape=jax.ShapeDtypeStruct((bsz, t, d), F32), compiler_params=_cp(VMEM_BIG),
    )(hs, mods, *ps, z, pw, psc, wout)


def _odd_out_bwd(dh, mods, ps, z, pw, psc, wout, i, jl, nct, name):
    bsz, t, d = dh.shape

    def body(dh_ref, mod_ref, p0, p1, p2, p3, gz_ref, pw_ref, psc_ref, wo_ref,
             dp_ref, dgz_ref, dmod_ref, dpw_ref, dpsc_ref, dwo_ref):
        b, j = pl.program_id(0), pl.program_id(1)
        pbs = [_bf(p[...]) for p in (p0, p1, p2, p3)]
        pm = jnp.concatenate([_nn(pb, pw_ref[q]) for q, pb in enumerate(pbs)], axis=1)
        mix, vj = jax.vjp(_odd_mix, pm, psc_ref[...], gz_ref[...])
        mixb = _bf(mix)
        dhv = dh_ref[...]
        dyo = _bf(dhv * mod_ref[2:3, :])
        yout = _nn(mixb, wo_ref[...])
        dpm, dpsc, dgz = vj(_nt(dyo, wo_ref[...]))
        dgz_ref[...] = dgz
        dpmb = _bf(dpm)
        dp_ref[...] = jnp.concatenate([_nt(dpmb[:, q * 256:(q + 1) * 256], pw_ref[q]) for q in range(4)], axis=1)

        @pl.when((j == 0) | (j == nct))
        def _():
            dmod_ref[...] = jnp.zeros_like(dmod_ref)

        @pl.when((b == 0) & (j == 0))
        def _():
            dpw_ref[...] = jnp.zeros_like(dpw_ref)
            dpsc_ref[...] = jnp.zeros_like(dpsc_ref)
            dwo_ref[...] = jnp.zeros_like(dwo_ref)
        dmod_ref[2:3, :] += jnp.sum(dhv * yout, axis=0, keepdims=True)
        for q in range(4):
            dpw_ref[q] += _tn(pbs[q], dpmb[:, q * 256:(q + 1) * 256])
        dpsc_ref[...] += dpsc
        dwo_ref[...] += _tn(mixb, dyo)

    full = lambda *s: pl.BlockSpec(s, lambda b, j: (0,) * len(s))
    return pl.pallas_call(
        body, name=name, grid=(bsz, t // ROWT), in_specs=_odd_specs(d, i, jl, nct),
        out_specs=[_tile(1024), _tile(1024), _dmod_spec(nct, d), full(4, 256, 256), full(1, 1024), full(1024, d)],
        out_shape=(jax.ShapeDtypeStruct((bsz, t, 1024), F32), jax.ShapeDtypeStruct((bsz, t, 1024), F32),
                   jax.ShapeDtypeStruct((2 * bsz, 3, d), F32), jax.ShapeDtypeStruct((4, 256, 256), F32),
                   jax.ShapeDtypeStruct((1, 1024), F32), jax.ShapeDtypeStruct((1024, d), F32)),
        compiler_params=_cp(VMEM_BIG),
    )(dh, mods, *ps, z, pw, psc, wout)


def _rms(h, g):
    return h * lax.rsqrt(jnp.mean(h * h, axis=-1, keepdims=True) + EPS) * g


def _final_loss(hs, g, target, nct, name):
    bsz, t, d = hs.shape

    def body(h_ref, g_ref, t_ref, dh_ref, loss_ref, dg_ref):
        b, j = pl.program_id(0), pl.program_id(1)
        y, vj = jax.vjp(_rms, h_ref[...], g_ref[...])
        err = jnp.where(j >= nct, y - t_ref[...], 0.0)
        dh, dg = vj(err * (1.0 / d))
        dh_ref[...] = dh

        @pl.when(j == 0)
        def _():
            loss_ref[...] = jnp.zeros_like(loss_ref)

        @pl.when((b == 0) & (j == 0))
        def _():
            dg_ref[...] = jnp.zeros_like(dg_ref)
        loss_ref[...] += 0.5 * jnp.sum(jnp.mean(err * err, axis=-1))
        dg_ref[...] += dg

    return pl.pallas_call(
        body, name=name, grid=(bsz, t // ROWT),
        in_specs=[_tile(d), pl.BlockSpec((1, d), lambda b, j: (0, 0)),
                  pl.BlockSpec((None, ROWT, d), lambda b, j: (b, jnp.maximum(j - nct, 0), 0))],
        out_specs=[_tile(d), pl.BlockSpec((None, 8, LANES), lambda b, j: (b, 0, 0)),
                   pl.BlockSpec((1, d), lambda b, j: (0, 0))],
        out_shape=(jax.ShapeDtypeStruct((bsz, t, d), F32), jax.ShapeDtypeStruct((bsz, 8, LANES), F32),
                   jax.ShapeDtypeStruct((1, d), F32)),
    )(hs, g, target)


def _adamw(w, g_list, m, v, transpose_g, rows, name):
    nl, r, c = w.shape
    ns = g_list[0].shape[1]
    rt = rows or r
    offs = np.cumsum([0] + [g.shape[0] for g in g_list])
    ng = len(g_list)

    def body(*refs):
        w_ref, m_ref, v_ref = refs[0], refs[1 + ng], refs[2 + ng]
        go_ref, d_ref, mo_ref, vo_ref = refs[3 + ng:]
        layer = pl.program_id(0)
        for k in range(ng):
            g_ref = refs[1 + k]

            @pl.when((layer >= offs[k]) & (layer < offs[k + 1]))
            def _():
                g = g_ref[0]
                for s in range(1, ns):
                    g = g + g_ref[s]
                if transpose_g:
                    g = g.T
                mn = ADAM_B1 * m_ref[...] + (1.0 - ADAM_B1) * g
                vn = ADAM_B2 * v_ref[...] + (1.0 - ADAM_B2) * jnp.square(g)
                m_hat = mn / (1.0 - ADAM_B1 ** ADAM_STEP)
                v_hat = vn / (1.0 - ADAM_B2 ** ADAM_STEP)
                go_ref[...] = g
                d_ref[...] = -ADAM_LR * (m_hat / (jnp.sqrt(v_hat) + ADAM_EPS) + ADAM_WD * w_ref[...])
                mo_ref[...] = mn
                vo_ref[...] = vn

    def g_spec(k):
        lo, n_k = int(offs[k]), int(offs[k + 1] - offs[k])

        def idx(l, q):
            mine = (l >= lo) & (l < lo + n_k)
            q = jnp.where(mine, q, 0)
            return (jnp.clip(l - lo, 0, n_k - 1), 0, 0, q) if transpose_g else (jnp.clip(l - lo, 0, n_k - 1), 0, q, 0)
        return pl.BlockSpec((None, ns, c, rt) if transpose_g else (None, ns, rt, c), idx)

    blk = pl.BlockSpec((None, rt, c), lambda l, q: (l, q, 0))
    out = jax.ShapeDtypeStruct((nl, r, c), F32)
    return pl.pallas_call(
        body, name=name, grid=(nl, r // rt),
        in_specs=[blk] + [g_spec(k) for k in range(ng)] + [blk, blk],
        out_specs=[blk, blk, blk, blk], out_shape=(out, out, out, out), compiler_params=_cp(VMEM_BIG),
    )(w, *g_list, m, v)


def _adamw_nd(ws, gs, ms, vs, lead, name):
    shapes = [w.shape for w in ws]
    as2 = lambda a: a.reshape(1, -1) if a.ndim == 1 else a
    ws, gs, ms, vs = ([as2(a) for a in lst] for lst in (ws, gs, ms, vs))
    nts = len(ws)

    def body(*refs):
        for k in range(nts):
            w_ref, g_ref, m_ref, v_ref = refs[4 * k:4 * k + 4]
            d_ref, mo_ref, vo_ref = refs[4 * nts + 3 * k:4 * nts + 3 * k + 3]
            g = g_ref[...]
            mn = ADAM_B1 * m_ref[...] + (1.0 - ADAM_B1) * g
            vn = ADAM_B2 * v_ref[...] + (1.0 - ADAM_B2) * jnp.square(g)
            m_hat = mn / (1.0 - ADAM_B1 ** ADAM_STEP)
            v_hat = vn / (1.0 - ADAM_B2 ** ADAM_STEP)
            d_ref[...] = -ADAM_LR * (m_hat / (jnp.sqrt(v_hat) + ADAM_EPS) + ADAM_WD * w_ref[...])
            mo_ref[...] = mn
            vo_ref[...] = vn

    grid = tuple(ws[0].shape[:lead])

    def spec(a):
        rest = a.shape[lead:]
        return pl.BlockSpec((None,) * lead + tuple(rest), lambda *ids: tuple(ids) + (0,) * len(rest))

    ins = [a for k in range(nts) for a in (ws[k], gs[k], ms[k], vs[k])]
    outs = [w for w in ws for _ in range(3)]
    res = pl.pallas_call(
        body, name=name, grid=grid, in_specs=[spec(a) for a in ins], out_specs=[spec(a) for a in outs],
        out_shape=[jax.ShapeDtypeStruct(a.shape, F32) for a in outs], compiler_params=_cp(VMEM_BIG))(*ins)
    return [tuple(res[3 * k + q].reshape(shapes[k]) for q in range(3)) for k in range(nts)]


def _sum_slots(slots, name):
    ns, r, c = slots.shape

    def body(s_ref, o_ref):
        g = s_ref[0]
        for s in range(1, ns):
            g = g + s_ref[s]
        o_ref[...] = g

    return pl.pallas_call(body, name=name, out_shape=jax.ShapeDtypeStruct((r, c), F32), compiler_params=_cp(VMEM_BIG))(slots)


def _pack(arrs):
    flat = jnp.concatenate([a.reshape(-1) for a in arrs])
    return jnp.pad(flat, (0, (-flat.shape[0]) % (PACK_ROWS * PACK_W))).reshape(-1, PACK_W)


def _unpack(buf, shapes):
    flat = buf.reshape(-1)
    out, off = [], 0
    for s in shapes:
        n = int(np.prod(s))
        out.append(flat[off:off + n].reshape(s))
        off += n
    return out


def kernel(x, c, ctx, c_ctx, ada_w, ada_b, norm_g, even_w_in, even_w_out, attn_sink, ssm_a_re, ssm_a_im, ssm_log_dt, ssm_b_re, ssm_b_im, ssm_c_re, ssm_c_im, ssm_d, glu_w, glu_b, odd_w_in, odd_w_out, pool_w, pool_scale, final_g, loss_target, m_c_ctx, m_ada_w, m_ada_b, m_norm_g, m_even_w_in, m_even_w_out, m_attn_sink, m_ssm_a_re, m_ssm_a_im, m_ssm_log_dt, m_ssm_b_re, m_ssm_b_im, m_ssm_c_re, m_ssm_c_im, m_ssm_d, m_glu_w, m_glu_b, m_odd_w_in, m_odd_w_out, m_pool_w, m_pool_scale, m_final_g, v_c_ctx, v_ada_w, v_ada_b, v_norm_g, v_even_w_in, v_even_w_out, v_attn_sink, v_ssm_a_re, v_ssm_a_im, v_ssm_log_dt, v_ssm_b_re, v_ssm_b_im, v_ssm_c_re, v_ssm_c_im, v_ssm_d, v_glu_w, v_glu_b, v_odd_w_in, v_odd_w_out, v_pool_w, v_pool_scale, v_final_g):
    args = dict(locals())
    bsz, l, d = x.shape
    lc = ctx.shape[1]
    t = lc + l
    nct = lc // ROWT
    depth = ada_w.shape[0]
    nl2 = even_w_in.shape[0]
    me = _my_index()
    assert lc % ROWT == 0 and l % ROWT == 0 and d == 1024 and bsz * N_DEV < 32

    npw = pool_w.shape[2]
    shards = {"even": [_bf(jnp.transpose(even_w_in, (0, 2, 1))), _bf(even_w_out), _bf(glu_w)],
              "odd": [_bf(jnp.transpose(odd_w_in, (0, 2, 1))), _bf(odd_w_out), _bf(pool_w.reshape(nl2, -1, pool_w.shape[-1]))]}

    def wgather(kind, jl):
        return shards[kind], [("gather", [(ii, jl)], 0) for ii in range(3)]

    def wjoin(kind, res):
        full = [a.reshape(1, N_DEV * a.shape[2], a.shape[3]) for a in res]
        if kind == "odd":
            full[2] = jnp.transpose(res[2].reshape(1, N_DEV, 4, npw, 256), (0, 2, 1, 3, 4)).reshape(1, 4, 256, 256)
        return full

    xin, xsp = wgather("even", 0)
    got = _exchange(xin + [c, pool_scale], xsp + [("gather", [(3, None)], 0), ("gather", [(4, None)], 0)], "gather_first")
    weights = {("even", 0): wjoin("even", got[:3])}
    c_all = got[3].reshape(N_DEV * bsz, d)
    psc_full = jnp.transpose(got[4][0], (1, 0, 2)).reshape(nl2, 1, d)

    n_rows = 32
    craw = jnp.concatenate([c_all, c_ctx[None], jnp.zeros((n_rows - N_DEV * bsz - 1, d), F32)], axis=0)
    ncol = ada_w.shape[2]
    ada_b_loc = lax.dynamic_slice_in_dim(ada_b, me * ncol, ncol, axis=1)[:, None, :]
    m_loc = _adaln_fwd(craw, ada_w, ada_b_loc)
    m_all = _exchange([m_loc], [("gather", [(0, None)], 0)], "gather_mod")[0][0]
    m_all = jnp.transpose(m_all, (1, 2, 0, 3)).reshape(depth, n_rows, 3 * d)
    lat = lax.dynamic_slice_in_dim(m_all, me * bsz, bsz, axis=1).reshape(depth, bsz, 1, 3, d)
    cmod = jnp.broadcast_to(m_all[:, N_DEV * bsz].reshape(depth, 1, 1, 3, d), (depth, bsz, 1, 3, d))
    mods = jnp.concatenate([cmod, lat], axis=2).reshape(depth, 2 * bsz, 3, d)

    cos, sin = _rope_tables(lc, l)
    hs = jnp.concatenate([ctx, x], axis=1)
    g3 = norm_g[:, None, :]
    dsk3, gb3 = ssm_d[:, None, :], glu_b[:, None, :]
    u_col = 1280 // LANES

    s5_prm = (ssm_a_re, ssm_a_im, ssm_log_dt, ssm_b_re, ssm_b_im, ssm_c_re, ssm_c_im)
    ops = jax.vmap(_s5_operators)(*s5_prm)
    pr, pi = jax.vmap(lambda r, q: _scan_powers(r, q, t // CHUNK))(ops[4], ops[5])
    s5_ops = (ops[0], ops[1], _bf(ops[2]), _bf(ops[3]), pr, pi)
    saved = []
    for i in range(depth):
        jl = i // 2
        if i % 2 == 0:
            wt, wo, gwf = weights[("even", jl)]
            z = _fused_in_fwd(hs, mods, g3, wt, i, 0, nct, f"in_fwd{i}")
            qp, kvp = _rope_fwd(z, cos, sin, 2, f"rope_fwd{i}")
            (oa, lse), got = _attn_fwd(qp, kvp, attn_sink, jl, lc, f"attn_fwd{i}", xchg=wgather("odd", jl))
            weights[("odd", jl)] = wjoin("odd", got)
            nxt = wgather("even", jl + 1) if jl + 1 < nl2 else None
            (y,), got = _s5_fwd(z, u_col, s5_ops, jl, lc, f"s5_fwd{i}", xchg=nxt)
            if nxt:
                weights[("even", jl + 1)] = wjoin("even", got)
            hn = _even_out_fwd(hs, mods, oa, y, z, dsk3, gwf, gb3, wo, i, jl, nct, f"out_fwd{i}")
            saved.append(dict(hs=hs, z=z, qp=qp, kvp=kvp, oa=oa, lse=lse, y=y))
        else:
            wt, wo, pwf = weights[("odd", jl)]
            z = _fused_in_fwd(hs, mods, g3, wt, i, 0, nct, f"in_fwd{i}")
            ps = [_pool(z, gi, wd // 2, lc, False, f"pool_fwd{i}_{gi}") for gi, wd in enumerate(POOL_WINDOWS)]
            hn = _odd_out_fwd(hs, mods, ps, z, pwf, psc_full, wo, i, jl, nct, f"out_fwd{i}")
            saved.append(dict(hs=hs, z=z, ps=ps))
        hs = hn

    dh, loss_p, d_final_g = _final_loss(hs, final_g[None], loss_target, nct, "final_loss")
    loss = lax.psum(jnp.sum(loss_p[:, 0, 0]), ("x", "y", "c"))

    slots = {n: [None] * nl2 for n in ("even_w_in", "even_w_out", "glu_w", "odd_w_in", "odd_w_out", "pool_w", "pool_scale")}
    gsmall = {n: [None] * nl2 for n in ("attn_sink", "ssm_d", "glu_b")}
    d_norm_g, d_mods = [None] * depth, [None] * depth
    d_ops = [None] * nl2

    def rows_xchg(named):
        arrs = [a for _, _, a in named]
        specs = [("rows2" if n == "pool_w" else "rows", [(k, None)], a.shape[1 if n == "pool_w" else 0] // N_DEV)
                 for k, (n, _, a) in enumerate(named)]
        return (arrs, specs)

    def keep(named, res):
        for (n, jl_, _), r_ in zip(named, res):
            slots[n][jl_] = r_

    pending = None
    for i in reversed(range(depth)):
        jl = i // 2
        sv = saved[i]
        if i % 2 == 0:
            wt, wo, gwf = weights[("even", jl)]
            doa, dga, dy, dgs, dmod_g, ddsk, dgw, dgb, dwo = _even_out_bwd(
                dh, mods, sv["oa"], sv["y"], sv["z"], dsk3, gwf, gb3, wo, i, jl, nct, f"out_bwd{i}")
            (dq_r, dkv_acc, dsink), got = _attn_bwd(sv["qp"], sv["kvp"], attn_sink, jl, sv["oa"], sv["lse"], doa, lc,
                                                     f"attn_bwd{i}", xchg=rows_xchg(pending) if pending else None)
            if pending:
                keep(pending, got)
            dq, dkv = _rope_bwd(dq_r, dkv_acc, cos, sin, t, f"rope_bwd{i}")
            mine = [("even_w_out", jl, dwo), ("glu_w", jl, dgw)]
            (du, dqf, dqb, dbt, dct, dav), got = _s5_bwd(sv["z"], u_col, dy, s5_ops, jl, lc, f"s5_bwd{i}", xchg=rows_xchg(mine))
            keep(mine, got)
            per_group = lambda q: dav[:, q].reshape(SSM_GROUPS, SSM_STATE)
            d_ops[jl] = (dqf, dqb, dbt, dct, jnp.stack([per_group(0), per_group(2)]), jnp.stack([per_group(1), per_group(3)]))
            gsmall["attn_sink"][jl] = jnp.sum(dsink[:, :, 0], axis=0)
            gsmall["ssm_d"][jl] = ddsk[0]
            gsmall["glu_b"][jl] = dgb[0]
            pieces = [(dq, 512, 0), (dkv, 256, 0), (dga, 512, 0), (du, 512, 0), (dy, 512, 0), (dgs, 512, 0)]
            asm = lambda p, s: [p[0], p[1], p[2], p[3] + p[4] * s, p[5]]
            (dh, dmod_in, dg, dw), _ = _fused_in_bwd(sv["hs"], mods, g3, wt, dh, pieces, (dsk3, jl), asm, i, 0, nct, f"in_bwd{i}",
                                                     latent_only=(i == 0))
            pending = [("even_w_in", jl, dw)]
        else:
            wt, wo, pwf = weights[("odd", jl)]
            dp, dgz, dmod_g, dpw, dpsc, dwo = _odd_out_bwd(
                dh, mods, sv["ps"], sv["z"], pwf, psc_full, wo, i, jl, nct, f"out_bwd{i}")
            dus = [_pool(dp, gi, wd // 2, lc, True, f"pool_bwd{i}_{gi}") for gi, wd in enumerate(POOL_WINDOWS)]
            dpsc8 = jnp.broadcast_to(dpsc.reshape(N_DEV, 1, -1), (N_DEV, 8, d // N_DEV)).reshape(N_DEV * 8, -1)
            pieces = [(u_, 256, 0) for u_ in dus] + [(dgz, 1024, 0)]
            asm = lambda p, s: p
            (dh, dmod_in, dg, dw), got = _fused_in_bwd(sv["hs"], mods, g3, wt, dh, pieces, None, asm, i, 0, nct, f"in_bwd{i}",
                                                       xchg=rows_xchg(pending) if pending else None)
            if pending:
                keep(pending, got)
            pending = [("odd_w_in", jl, dw), ("odd_w_out", jl, dwo), ("pool_w", jl, dpw), ("pool_scale", jl, dpsc8)]
        d_norm_g[i] = dg[0]
        d_mods[i] = (dmod_g + dmod_in).reshape(bsz, 2, 3 * d)
    grad_x = dh

    _, op_vjp = jax.vjp(jax.vmap(_s5_operators), *s5_prm)
    dprm = op_vjp(tuple(jnp.stack([d_ops[jl][q] for jl in range(nl2)]) for q in range(6)))

    dmd = jnp.stack(d_mods)
    dm_loc = jnp.concatenate([dmd[:, :, 1], jnp.sum(dmd[:, :, 0], axis=1, keepdims=True)], axis=1)
    dm_all = _exchange([dm_loc], [("gather", [(0, None)], 0)], "gather_dmod")[0][0]
    dm_lat = jnp.transpose(dm_all[:, :, :bsz], (1, 0, 2, 3)).reshape(depth, N_DEV * bsz, 3 * d)
    dm_ctx = dm_all[0, :, bsz]
    for e in range(1, N_DEV):
        dm_ctx = dm_ctx + dm_all[e, :, bsz]
    dm_rows = jnp.concatenate([dm_lat, dm_ctx[:, None], jnp.zeros((depth, n_rows - N_DEV * bsz - 1, 3 * d), F32)], axis=1)
    dm_cols = lax.dynamic_slice_in_dim(dm_rows, me * ncol, ncol, axis=2)
    g_ada_w, dcraw = _adaln_bwd(craw, dm_cols, ada_w)

    small = {"c_ctx": dcraw[N_DEV * bsz], "ada_b": jnp.sum(dm_loc, axis=1), "norm_g": jnp.stack(d_norm_g),
             "final_g": d_final_g[0]}
    for n, v2 in gsmall.items():
        small[n] = jnp.stack(v2)
    for n, gval in zip(("ssm_a_re", "ssm_a_im", "ssm_log_dt", "ssm_b_re", "ssm_b_im", "ssm_c_re", "ssm_c_im"), dprm):
        small[n] = gval
    snames = list(small)
    sshapes = [args[n].shape for n in snames]
    spack = _pack([small[n].reshape(args[n].shape) for n in snames])

    last = pending + [("small", 0, spack)]
    red = _exchange(*rows_xchg(last), "reduce_last")
    keep(pending, red[:-1])
    s_sum = _sum_slots(red[-1][0], "sum_small")
    s_all = _exchange([s_sum], [("gather", [(0, None)], 0)], "gather_small")[0]
    s_all = s_all.reshape(1, 1, -1, PACK_W)

    out = {}

    def put(n, res, shape):
        for kind, buf in zip(("grad_", "delta_", "new_m_", "new_v_"), res):
            out[kind + n] = buf.reshape(shape)

    def as3(a):
        return a.reshape(a.shape[0], -1, a.shape[-1])

    for n in slots:
        w = args[n]
        g_list = [s_[:, :, :1] if n == "pool_scale" else s_ for s_ in slots[n]]
        g_list = [s_.reshape(1, N_DEV, -1, s_.shape[-1]) for s_ in g_list]
        tr = n in ("even_w_in", "odd_w_in")
        put(n, _adamw(as3(w), g_list, as3(args["m_" + n]), as3(args["v_" + n]), tr, 256 if tr else None, "adamw_" + n),
            w.shape)
    put("ada_w", _adamw(ada_w, [g_ada_w[:, None]], m_ada_w, v_ada_w, False, None, "adamw_ada_w"), ada_w.shape)
    for n, a in zip(snames, _unpack(s_all, sshapes)):
        out["grad_" + n] = a
    for lead, name in ((0, "adamw_small"), (2, "adamw_ssm_bc")):
        grp = [n for n in snames if (args[n].ndim == 5) == (lead == 2)]
        res = _adamw_nd([args[n] for n in grp], [out["grad_" + n] for n in grp], [args["m_" + n] for n in grp],
                        [args["v_" + n] for n in grp], lead, name)
        for n, (dl, mn, vn) in zip(grp, res):
            out["delta_" + n], out["new_m_" + n], out["new_v_" + n] = dl, mn, vn

    order = ["c_ctx", "ada_w", "ada_b", "norm_g", "even_w_in", "even_w_out", "attn_sink", "ssm_a_re", "ssm_a_im",
             "ssm_log_dt", "ssm_b_re", "ssm_b_im", "ssm_c_re", "ssm_c_im", "ssm_d", "glu_w", "glu_b", "odd_w_in",
             "odd_w_out", "pool_w", "pool_scale", "final_g"]
    return (loss, grad_x, *[out[k + n] for k in ("grad_", "delta_", "new_m_", "new_v_") for n in order])
```

```python
import functools

import numpy as np
import jax
import jax.numpy as jnp
from jax import lax
from jax.experimental import pallas as pl
from jax.experimental.pallas import tpu as pltpu

F32 = jnp.float32
BF16 = jnp.bfloat16
EPS = 1e-6
NEG_INF = -1e30
N_DEV = 8
GRID_W = 64
WINDOW = 128
QBLK = 128
ROWT = 256
CHUNK = 16
SSM_GROUPS = 32
SSM_GROUP = 16
SSM_STATE = 64
POOL_WINDOWS = (2, 4, 8, 16)
ROPE_BASE = 10000.0
ADAM_LR, ADAM_B1, ADAM_B2, ADAM_EPS, ADAM_WD, ADAM_STEP = 0.001, 0.9, 0.999, 1e-08, 0.01, 10
LANES = 128
PACK_W = 1024
PACK_ROWS = 64
VMEM_BIG = 56 << 20


def _cp(vmem=None):
    return pltpu.CompilerParams(vmem_limit_bytes=vmem) if vmem else pltpu.CompilerParams()


def _nt(a, b):
    return lax.dot_general(a, b, (((1,), (1,)), ((), ())), preferred_element_type=F32)


def _tn(a, b):
    return lax.dot_general(a, b, (((0,), (0,)), ((), ())), preferred_element_type=F32)


def _nn(a, b):
    return jnp.dot(a, b, preferred_element_type=F32)


def _bf(x):
    return x.astype(BF16)


def _my_index():
    return 4 * lax.axis_index("x") + 2 * lax.axis_index("y") + lax.axis_index("c")


def _xchg_shapes(inputs, specs):
    out_shapes = []
    for kind, parts, r in specs:
        ii, li = parts[0]
        shp = tuple(inputs[ii].shape if li is None else inputs[ii].shape[1:])
        piece = shp if kind == "gather" else ((r,) + shp[1:] if kind == "rows" else (shp[0], r) + shp[2:])
        out_shapes.append(jax.ShapeDtypeStruct((len(parts), N_DEV) + piece, inputs[ii].dtype))
    return out_shapes, sum(len(parts) for _, parts, _ in specs)


def _xchg_sems(n_parts):
    return [pltpu.SemaphoreType.DMA((n_parts * (N_DEV - 1),)), pltpu.SemaphoreType.DMA((n_parts * (N_DEV - 1),)),
            pltpu.SemaphoreType.DMA((n_parts,))]


def _xchg_copies(specs, in_refs, out_refs, send_sems, recv_sems, loc_sems):
    x, y, c = lax.axis_index("x"), lax.axis_index("y"), lax.axis_index("c")
    me = 4 * x + 2 * y + c

    def piece(ref, kind, r, p):
        if kind == "gather":
            return ref
        if kind == "rows":
            return ref.at[pl.ds(p * r, r)]
        return ref.at[:, pl.ds(p * r, r)]

    copies, q = [], 0
    for si, (kind, parts, r) in enumerate(specs):
        for pi, (ii, li) in enumerate(parts):
            src = in_refs[ii] if li is None else in_refs[ii].at[li]
            dst = out_refs[si].at[pi, me]
            copies.append(pltpu.make_async_copy(piece(src, kind, r, me), dst, loc_sems.at[q]))
            for k in range(1, N_DEV):
                px = 1 - x if k & 4 else x
                py = 1 - y if k & 2 else y
                pc = 1 - c if k & 1 else c
                copies.append(pltpu.make_async_remote_copy(
                    src_ref=piece(src, kind, r, 4 * px + 2 * py + pc), dst_ref=dst,
                    send_sem=send_sems.at[q * (N_DEV - 1) + k - 1], recv_sem=recv_sems.at[q * (N_DEV - 1) + k - 1],
                    device_id=(px, py, pc), device_id_type=pl.DeviceIdType.MESH))
            q += 1
    return copies


def _exchange(inputs, specs, name):
    out_shapes, n_parts = _xchg_shapes(inputs, specs)
    ni, ns = len(inputs), len(specs)

    def body(*refs):
        copies = _xchg_copies(specs, refs[:ni], refs[ni:ni + ns], *refs[ni + ns:])
        for cp in copies:
            cp.start()
        for cp in copies:
            cp.wait()

    return pl.pallas_call(
        body, name=name, out_shape=tuple(out_shapes),
        in_specs=[pl.BlockSpec(memory_space=pl.ANY)] * ni,
        out_specs=tuple(pl.BlockSpec(memory_space=pl.ANY) for _ in specs),
        scratch_shapes=_xchg_sems(n_parts),
        compiler_params=pltpu.CompilerParams(has_side_effects=True),
    )(*inputs)


def _call(body, *, name, grid, in_specs, out_specs, out_shape, operands, vmem=None, scratch=(), xchg=None):
    out_shape, out_specs = tuple(out_shape), list(out_specs)
    if xchg is None:
        res = pl.pallas_call(body, name=name, grid=grid, in_specs=in_specs, out_specs=out_specs, out_shape=out_shape,
                             scratch_shapes=list(scratch), compiler_params=_cp(vmem))(*operands)
        return list(res), None
    xin, xspecs = xchg
    xshapes, n_parts = _xchg_shapes(xin, xspecs)
    ni, no, nxi, nxo, nsc = len(operands), len(out_shape), len(xin), len(xspecs), len(scratch)
    anyspec = pl.BlockSpec(memory_space=pl.ANY)

    def hosted(*refs):
        ins, xins = refs[:ni], refs[ni:ni + nxi]
        outs, xouts = refs[ni + nxi:ni + nxi + no], refs[ni + nxi + no:ni + nxi + no + nxo]
        scr, sems = refs[ni + nxi + no + nxo:ni + nxi + no + nxo + nsc], refs[ni + nxi + no + nxo + nsc:]
        copies = _xchg_copies(xspecs, xins, xouts, *sems)
        first, last = True, True
        for ax, n in enumerate(grid):
            first = first & (pl.program_id(ax) == 0)
            last = last & (pl.program_id(ax) == n - 1)

        @pl.when(first)
        def _():
            for cp in copies:
                cp.start()
        body(*ins, *outs, *scr)

        @pl.when(last)
        def _():
            for cp in copies:
                cp.wait()

    res = pl.pallas_call(
        hosted, name=name, grid=grid, in_specs=list(in_specs) + [anyspec] * nxi,
        out_specs=out_specs + [anyspec] * nxo, out_shape=out_shape + tuple(xshapes),
        scratch_shapes=list(scratch) + _xchg_sems(n_parts), compiler_params=_cp(vmem))(*operands, *xin)
    return list(res[:no]), list(res[no:])


def _adaln_fwd(craw, ada_w, ada_b):
    nl, d, n = ada_w.shape
    r = craw.shape[0]

    def body(c_ref, w_ref, b_ref, o_ref):
        s = jax.nn.silu(c_ref[...])
        o_ref[...] = _nn(_bf(s), _bf(w_ref[...])) + b_ref[...]

    return pl.pallas_call(
        body, name="adaln_fwd", grid=(nl,),
        in_specs=[pl.BlockSpec((r, d), lambda i: (0, 0)), pl.BlockSpec((None, d, n), lambda i: (i, 0, 0)),
                  pl.BlockSpec((None, 1, n), lambda i: (i, 0, 0))],
        out_specs=pl.BlockSpec((None, r, n), lambda i: (i, 0, 0)),
        out_shape=jax.ShapeDtypeStruct((nl, r, n), F32),
    )(craw, ada_w, ada_b)


def _adaln_bwd(craw, dm, ada_w):
    nl, d, n = ada_w.shape
    r = craw.shape[0]

    def body(c_ref, dm_ref, w_ref, gw_ref, dc_ref, acc_ref):
        i = pl.program_id(0)
        s, vj = jax.vjp(jax.nn.silu, c_ref[...])
        dmb = _bf(dm_ref[...])
        gw_ref[...] = _tn(_bf(s), dmb)

        @pl.when(i == 0)
        def _():
            acc_ref[...] = jnp.zeros_like(acc_ref)
        acc_ref[...] += _nt(dmb, _bf(w_ref[...]))

        @pl.when(i == nl - 1)
        def _():
            dc_ref[...] = vj(acc_ref[...])[0]

    return pl.pallas_call(
        body, name="adaln_bwd", grid=(nl,),
        in_specs=[pl.BlockSpec((r, d), lambda i: (0, 0)), pl.BlockSpec((None, r, n), lambda i: (i, 0, 0)),
                  pl.BlockSpec((None, d, n), lambda i: (i, 0, 0))],
        out_specs=[pl.BlockSpec((None, d, n), lambda i: (i, 0, 0)), pl.BlockSpec((r, d), lambda i: (0, 0))],
        out_shape=(jax.ShapeDtypeStruct((nl, d, n), F32), jax.ShapeDtypeStruct((r, d), F32)),
        scratch_shapes=[pltpu.VMEM((r, d), F32)],
    )(craw, dm, ada_w)


def _norm_mod(h, mod, g):
    ms = jnp.mean(h * h, axis=-1, keepdims=True)
    y = h * lax.rsqrt(ms + EPS) * g
    return y * (1.0 + mod[1:2]) + mod[0:1]


def _mod_spec(i, nct, d):
    return pl.BlockSpec((None, None, 3, d), lambda b, j: (i, 2 * b + jnp.where(j >= nct, 1, 0), 0, 0))


def _dmod_spec(nct, d):
    return pl.BlockSpec((None, 3, d), lambda b, j: (2 * b + jnp.where(j >= nct, 1, 0), 0, 0))


def _layer_spec(li, *shape):
    return pl.BlockSpec((None,) + tuple(shape), lambda b, j: (li,) + (0,) * len(shape))


def _tile(width, col_blk=0):
    return pl.BlockSpec((None, ROWT, width), lambda b, j: (b, j, col_blk))


def _fused_in_fwd(hs, mods, g, wt, i, jl, nct, name, xchg=None):
    bsz, t, d = hs.shape
    n = wt.shape[1]

    def body(h_ref, mod_ref, g_ref, w_ref, z_ref):
        a = _norm_mod(h_ref[...], mod_ref[...], g_ref[...])
        z_ref[...] = _nt(_bf(a), w_ref[...])

    (z,), got = _call(
        body, name=name, grid=(bsz, t // ROWT),
        in_specs=[_tile(d), _mod_spec(i, nct, d), _layer_spec(i, 1, d), _layer_spec(jl, n, d)],
        out_specs=[_tile(n)], out_shape=[jax.ShapeDtypeStruct((bsz, t, n), F32)],
        vmem=VMEM_BIG, xchg=xchg, operands=[hs, mods, g, wt])
    return z, got


def _fused_in_bwd(hs, mods, g, wt, dh_out, pieces, small, assemble, i, jl, nct, name, xchg=None, latent_only=False):
    bsz, t, d = hs.shape
    n = wt.shape[1]
    npc, nsm = len(pieces), int(small is not None)

    def body(*refs):
        h_ref, mod_ref, g_ref, w_ref, dho_ref = refs[:5]
        p_refs = refs[5:5 + npc]
        s_val = refs[5 + npc][...] if nsm else None
        dhi_ref, dmod_ref, dg_ref, dw_ref = refs[5 + npc + nsm:]
        b, j = pl.program_id(0), pl.program_id(1)
        a, vj = jax.vjp(_norm_mod, h_ref[...], mod_ref[...], g_ref[...])
        dz = jnp.concatenate([_bf(p) for p in assemble([r[...] for r in p_refs], s_val)], axis=1)
        dh, dmod, dg = vj(_nn(dz, w_ref[...]))
        dhi_ref[...] = dho_ref[...] + dh

        @pl.when((j == 0) | (j == nct))
        def _():
            dmod_ref[...] = jnp.zeros_like(dmod_ref)

        @pl.when((b == 0) & (j == 0))
        def _():
            dg_ref[...] = jnp.zeros_like(dg_ref)
            dw_ref[...] = jnp.zeros_like(dw_ref)
        dmod_ref[...] += dmod
        dg_ref[...] += dg
        dw_ref[...] += _tn(dz, _bf(a))

    full = lambda r, c: pl.BlockSpec((r, c), lambda b, j: (0, 0))
    dh_rows = t - nct * ROWT if latent_only else t
    dh_spec = pl.BlockSpec((None, ROWT, d), lambda b, j: (b, jnp.maximum(j - nct, 0), 0)) if latent_only else _tile(d)
    return _call(
        body, name=name, grid=(bsz, t // ROWT),
        in_specs=[_tile(d), _mod_spec(i, nct, d), _layer_spec(i, 1, d), _layer_spec(jl, n, d), _tile(d)]
                 + [_tile(wd, cb) for _, wd, cb in pieces]
                 + ([_layer_spec(small[1], 1, small[0].shape[2])] if nsm else []),
        out_specs=[dh_spec, _dmod_spec(nct, d), full(1, d), full(n, d)],
        out_shape=(jax.ShapeDtypeStruct((bsz, dh_rows, d), F32), jax.ShapeDtypeStruct((2 * bsz, 3, d), F32),
                   jax.ShapeDtypeStruct((1, d), F32), jax.ShapeDtypeStruct((n, d), F32)),
        vmem=VMEM_BIG, xchg=xchg,
        operands=[hs, mods, g, wt, dh_out, *[p for p, _, _ in pieces], *([small[0]] if nsm else [])])


def _rope_tables(lc, l):
    tpos = np.arange(l)
    row = (tpos // GRID_W).astype(np.float32)
    col = (tpos % GRID_W).astype(np.float32)
    nf = 16
    inv = (np.float32(ROPE_BASE) ** (-np.arange(nf, dtype=np.float32) / np.float32(nf))).astype(np.float32)
    ang_r = row[:, None] * inv[None]
    ang_c = col[:, None] * inv[None]
    cos64 = np.concatenate([np.cos(ang_r), np.cos(ang_r), np.cos(ang_c), np.cos(ang_c)], axis=1)
    sin64 = np.concatenate([-np.sin(ang_r), np.sin(ang_r), -np.sin(ang_c), np.sin(ang_c)], axis=1)
    cos = np.concatenate([np.ones((lc, 128), np.float32), np.tile(cos64, (1, 2)).astype(np.float32)], axis=0)
    sin = np.concatenate([np.zeros((lc, 128), np.float32), np.tile(sin64, (1, 2)).astype(np.float32)], axis=0)
    return jnp.asarray(cos), jnp.asarray(sin)


def _rot(x, cos, sin):
    lane = lax.broadcasted_iota(jnp.int32, x.shape, 1)
    first = jnp.bitwise_and(jnp.right_shift(lane, 4), 1) == 0
    partner = jnp.where(first, pltpu.roll(x, LANES - 16, 1), pltpu.roll(x, 16, 1))
    return x * cos + partner * sin


def _split_heads(x):
    low = lax.broadcasted_iota(jnp.int32, x.shape, 1) < 64
    return jnp.where(low, x, 0.0), jnp.where(low, pltpu.roll(x, 64, 1), 0.0)


def _merge_heads(a, b):
    return a + pltpu.roll(b, 64, 1)


def _rope_fwd(z, cos, sin, kv_blk, name):
    bsz, t, _ = z.shape

    def body(q_ref, kv_ref, cos_ref, sin_ref, qp_ref, kvp_ref):
        c, s = cos_ref[...], sin_ref[...]
        outs = []
        for sl in range(4):
            xr = _rot(q_ref[:, sl * LANES:(sl + 1) * LANES], c, s) * 0.125
            outs += list(_split_heads(xr))
        qp_ref[...] = _bf(jnp.concatenate(outs, axis=1))
        k0, k1 = _split_heads(_rot(kv_ref[:, 0:LANES], c, s))
        v0, v1 = _split_heads(kv_ref[:, LANES:2 * LANES])
        kvp_ref[...] = _bf(jnp.concatenate([k0, k1, v0, v1], axis=1))

    return pl.pallas_call(
        body, name=name, grid=(bsz, t // ROWT),
        in_specs=[_tile(512), _tile(256, kv_blk),
                  pl.BlockSpec((ROWT, LANES), lambda b, j: (j, 0)), pl.BlockSpec((ROWT, LANES), lambda b, j: (j, 0))],
        out_specs=[_tile(1024), _tile(512)],
        out_shape=(jax.ShapeDtypeStruct((bsz, t, 1024), BF16), jax.ShapeDtypeStruct((bsz, t, 512), BF16)),
    )(z, z, cos, sin)


def _rope_bwd(dq_r, dkv_acc, cos, sin, t, name):
    bsz = dq_r.shape[0]

    def body(dq_ref, acc_ref, cos_ref, sin_ref, dqo_ref, dkvo_ref):
        c, s = cos_ref[...], -sin_ref[...]
        dqo_ref[...] = jnp.concatenate(
            [_rot(dq_ref[:, sl * LANES:(sl + 1) * LANES], c, s) for sl in range(4)], axis=1)
        dk = _merge_heads(acc_ref[:, 0:LANES], acc_ref[:, LANES:2 * LANES])
        dv = _merge_heads(acc_ref[:, 2 * LANES:3 * LANES], acc_ref[:, 3 * LANES:4 * LANES])
        dkvo_ref[...] = jnp.concatenate([_rot(dk, c, s), dv], axis=1)

    return pl.pallas_call(
        body, name=name, grid=(bsz, t // ROWT),
        in_specs=[_tile(512), _tile(512),
                  pl.BlockSpec((ROWT, LANES), lambda b, j: (j, 0)), pl.BlockSpec((ROWT, LANES), lambda b, j: (j, 0))],
        out_specs=[_tile(512), _tile(256)],
        out_shape=(jax.ShapeDtypeStruct((bsz, t, 512), F32), jax.ShapeDtypeStruct((bsz, t, 256), F32)),
    )(dq_r, dkv_acc, cos, sin)


def _attn_bias(j, ncb, l):
    n = j - ncb
    r = lax.broadcasted_iota(jnp.int32, (QBLK, 3 * QBLK), 0)
    cidx = lax.broadcasted_iota(jnp.int32, (QBLK, 3 * QBLK), 1)
    kpos = (n - 1) * QBLK + cidx
    valid = (jnp.abs(r - cidx + QBLK) <= WINDOW) & (kpos >= 0) & (kpos < l) & (j >= ncb)
    return jnp.where(valid, 0.0, NEG_INF).astype(F32)


def _attn_bias4(j, ncb, l):
    b1 = _attn_bias(j, ncb, l)
    return jnp.concatenate([b1, b1, b1, b1], axis=0)


def _sink_rows(sink_ref, jl, hk):
    head = jnp.right_shift(lax.broadcasted_iota(jnp.int32, (4 * QBLK, 1), 0), 7)
    sk = jnp.zeros((4 * QBLK, 1), F32)
    for g in range(4):
        sk = jnp.where(head == g, sink_ref[jl, 4 * hk + g], sk)
    return sk


def _kv_specs(nb, lc):
    return [pl.BlockSpec((None, QBLK, 512), lambda b, j: (b, jnp.maximum(j - 1, 0), 0)),
            pl.BlockSpec((None, QBLK, 512), lambda b, j: (b, j, 0)),
            pl.BlockSpec((None, QBLK, 512), lambda b, j: (b, jnp.minimum(j + 1, nb - 1), 0)),
            pl.BlockSpec((None, lc, 512), lambda b, j: (b, 0, 0))]


def _lane_pick(x, h):
    lane = lax.broadcasted_iota(jnp.int32, x.shape, 1)
    return jnp.sum(jnp.where(lane == h, x, 0.0), axis=1, keepdims=True)


def _attn_fwd(qp, kvp, sinks, jl, lc, name, xchg=None):
    bsz, t, _ = qp.shape
    nb, ncb, l = t // QBLK, lc // QBLK, t - lc

    def body(sink_ref, q_ref, kp_ref, kc_ref, kn_ref, kx_ref, o_ref, lse_ref):
        j = pl.program_id(1)
        kloc = jnp.concatenate([kp_ref[...], kc_ref[...], kn_ref[...]], axis=0)
        kctx = kx_ref[...]
        bias = _attn_bias4(j, ncb, l)
        lane = lax.broadcasted_iota(jnp.int32, (QBLK, LANES), 1)
        lse_all = jnp.zeros((QBLK, LANES), F32)
        outs = []
        for hk in range(2):
            ks, vs = slice(hk * LANES, (hk + 1) * LANES), slice((2 + hk) * LANES, (3 + hk) * LANES)
            q4 = jnp.concatenate([q_ref[:, (4 * hk + g) * LANES:(4 * hk + g + 1) * LANES] for g in range(4)], axis=0)
            s_loc = _nt(q4, kloc[:, ks]) + bias
            s_ctx = _nt(q4, kctx[:, ks])
            sk = _sink_rows(sink_ref, jl, hk)
            m = jnp.maximum(jnp.maximum(jnp.max(s_loc, axis=1, keepdims=True), jnp.max(s_ctx, axis=1, keepdims=True)), sk)
            p_loc, p_ctx = jnp.exp(s_loc - m), jnp.exp(s_ctx - m)
            den = jnp.sum(p_loc, axis=1, keepdims=True) + jnp.sum(p_ctx, axis=1, keepdims=True) + jnp.exp(sk - m)
            o4 = (_nn(_bf(p_loc), kloc[:, vs]) + _nn(_bf(p_ctx), kctx[:, vs])) / den
            lse4 = m + jnp.log(den)
            for g in range(4):
                outs.append(o4[g * QBLK:(g + 1) * QBLK])
                lse_all = lse_all + jnp.where(lane == 4 * hk + g, lse4[g * QBLK:(g + 1) * QBLK], 0.0)
        o_ref[...] = jnp.concatenate([_merge_heads(outs[2 * i], outs[2 * i + 1]) for i in range(4)], axis=1)
        lse_ref[...] = lse_all

    return _call(
        body, name=name, grid=(bsz, nb),
        in_specs=[pl.BlockSpec(memory_space=pltpu.SMEM),
                  pl.BlockSpec((None, QBLK, 1024), lambda b, j: (b, j, 0))] + _kv_specs(nb, lc),
        out_specs=[pl.BlockSpec((None, QBLK, 512), lambda b, j: (b, j, 0)),
                   pl.BlockSpec((None, QBLK, LANES), lambda b, j: (b, j, 0))],
        out_shape=(jax.ShapeDtypeStruct((bsz, t, 512), F32), jax.ShapeDtypeStruct((bsz, t, LANES), F32)),
        xchg=xchg, operands=[sinks, qp, kvp, kvp, kvp, kvp])


def _attn_bwd(qp, kvp, sinks, jl, o, lse, do, lc, name, xchg=None):
    bsz, t, _ = qp.shape
    nb, ncb, l = t // QBLK, lc // QBLK, t - lc
    nk = 3 * QBLK

    def body(sink_ref, q_ref, kp_ref, kc_ref, kn_ref, kx_ref, o_ref, lse_ref, do_ref, dq_ref, acc_ref, ds_ref):
        j = pl.program_id(1)

        @pl.when(j == 0)
        def _():
            acc_ref[...] = jnp.zeros_like(acc_ref)
            ds_ref[...] = jnp.zeros_like(ds_ref)
        kloc = jnp.concatenate([kp_ref[...], kc_ref[...], kn_ref[...]], axis=0)
        kctx = kx_ref[...]
        bias = _attn_bias4(j, ncb, l)
        lse = lse_ref[...]
        row8 = lax.broadcasted_iota(jnp.int32, (8, LANES), 0)
        dsink = jnp.zeros((8, LANES), F32)
        dqs = []
        d_loc, d_ctx = [None] * 4, [None] * 4
        for hk in range(2):
            ks, vs = slice(hk * LANES, (hk + 1) * LANES), slice((2 + hk) * LANES, (3 + hk) * LANES)
            q4 = jnp.concatenate([q_ref[:, (4 * hk + g) * LANES:(4 * hk + g + 1) * LANES] for g in range(4)], axis=0)
            do4 = jnp.concatenate([x_ for s2 in (2 * hk, 2 * hk + 1)
                                   for x_ in _split_heads(do_ref[:, s2 * LANES:(s2 + 1) * LANES])], axis=0)
            o4 = jnp.concatenate([x_ for s2 in (2 * hk, 2 * hk + 1)
                                  for x_ in _split_heads(o_ref[:, s2 * LANES:(s2 + 1) * LANES])], axis=0)
            delta = jnp.sum(do4 * o4, axis=1, keepdims=True)
            lse4 = jnp.concatenate([_lane_pick(lse, 4 * hk + g) for g in range(4)], axis=0)
            p_loc = jnp.exp(_nt(q4, kloc[:, ks]) + bias - lse4)
            p_ctx = jnp.exp(_nt(q4, kctx[:, ks]) - lse4)
            dob = _bf(do4)
            ds_loc = _bf(p_loc * (_nt(dob, kloc[:, vs]) - delta))
            ds_ctx = _bf(p_ctx * (_nt(dob, kctx[:, vs]) - delta))
            dq4 = (_nn(ds_loc, kloc[:, ks]) + _nn(ds_ctx, kctx[:, ks])) * 0.125
            d_loc[hk], d_ctx[hk] = _tn(ds_loc, q4), _tn(ds_ctx, q4)
            d_loc[2 + hk], d_ctx[2 + hk] = _tn(_bf(p_loc), dob), _tn(_bf(p_ctx), dob)
            dsk = -jnp.exp(_sink_rows(sink_ref, jl, hk) - lse4) * delta
            for g in range(4):
                dqs.append(dq4[g * QBLK:(g + 1) * QBLK])
                dsink = dsink + jnp.where(row8 == 4 * hk + g, jnp.sum(dsk[g * QBLK:(g + 1) * QBLK]), 0.0)
        dq_ref[...] = jnp.concatenate([_merge_heads(dqs[2 * i], dqs[2 * i + 1]) for i in range(4)], axis=1)
        ds_ref[...] += dsink
        acc_ref[pl.ds(0, lc), :] += jnp.concatenate(d_ctx, axis=1)

        @pl.when(j >= ncb)
        def _():
            st = pl.multiple_of((j - 1) * QBLK, QBLK)
            acc_ref[pl.ds(st, nk), :] += jnp.concatenate(d_loc, axis=1)

    blk = lambda wd: pl.BlockSpec((None, QBLK, wd), lambda b, j: (b, j, 0))
    return _call(
        body, name=name, grid=(bsz, nb),
        in_specs=[pl.BlockSpec(memory_space=pltpu.SMEM), blk(1024)] + _kv_specs(nb, lc)
                 + [blk(512), blk(LANES), blk(512)],
        out_specs=[blk(512), pl.BlockSpec((None, t + QBLK, 512), lambda b, j: (b, 0, 0)),
                   pl.BlockSpec((None, 8, LANES), lambda b, j: (b, 0, 0))],
        out_shape=(jax.ShapeDtypeStruct((bsz, t, 512), F32), jax.ShapeDtypeStruct((bsz, t + QBLK, 512), F32),
                   jax.ShapeDtypeStruct((bsz, 8, LANES), F32)),
        vmem=VMEM_BIG, xchg=xchg, operands=[sinks, qp, kvp, kvp, kvp, kvp, o, lse, do])


def _s5_operators(a_re, a_im, log_dt, b_re, b_im, c_re, c_im):
    hi = lax.Precision.HIGHEST
    dt = jnp.exp(log_dt)[..., None]
    tau = jnp.arange(CHUNK + 1, dtype=F32)[:, None, None, None]
    mag = jnp.exp(tau * (a_re * dt))
    pwr, pwi = mag * jnp.cos(tau * (a_im * dt)), mag * jnp.sin(tau * (a_im * dt))
    ar, ai = pwr[1], pwi[1]
    den = a_re * a_re + a_im * a_im
    fr = ((ar - 1.0) * a_re + ai * a_im) / den
    fi = (ai * a_re - (ar - 1.0) * a_im) / den
    bbr = fr[..., None] * b_re - fi[..., None] * b_im
    bbi = fr[..., None] * b_im + fi[..., None] * b_re
    wr = pwr[:CHUNK, ..., None] * bbr - pwi[:CHUNK, ..., None] * bbi
    wi = pwr[:CHUNK, ..., None] * bbi + pwi[:CHUNK, ..., None] * bbr
    kern = (jnp.einsum("dgcp,tdgpk->dgtck", c_re, wr, precision=hi)
            - jnp.einsum("dgcp,tdgpk->dgtck", c_im, wi, precision=hi))
    g = a_re.shape[1]
    nrow = CHUNK * SSM_GROUP
    qf = jnp.transpose(kern[0], (0, 3, 1, 2)).reshape(g, SSM_GROUP, nrow)
    qb = jnp.transpose(kern[1][:, ::-1], (0, 3, 1, 2)).reshape(g, SSM_GROUP, nrow)
    to_rows = lambda w: jnp.transpose(w, (1, 0, 3, 2)).reshape(g, nrow, SSM_STATE)
    bt = jnp.concatenate([to_rows(wr[::-1, 0]), to_rows(wi[::-1, 0]), to_rows(wr[:, 1]), to_rows(wi[:, 1])], axis=-1)

    def out_map(d, pw_r, pw_i):
        w2r = c_re[d][None] * pw_r[:, :, None, :] - c_im[d][None] * pw_i[:, :, None, :]
        w2i = c_re[d][None] * pw_i[:, :, None, :] + c_im[d][None] * pw_r[:, :, None, :]
        cols = lambda w: jnp.transpose(w, (1, 3, 0, 2)).reshape(g, SSM_STATE, nrow)
        return [cols(w2r), -cols(w2i)]

    ct = jnp.concatenate(out_map(0, pwr[1:, 0], pwi[1:, 0]) + out_map(1, pwr[1:, 1][::-1], pwi[1:, 1][::-1]), axis=-2)
    return qf, qb, bt, ct, pwr[CHUNK], pwi[CHUNK]


def _scan_powers(a16r, a16i, n_chunks):
    prs, pis = [], []
    r, i = a16r, a16i
    s = 1
    while s < n_chunks:
        prs.append(jnp.concatenate([r, r], axis=-1))
        pis.append(jnp.concatenate([-i, i], axis=-1))
        r, i = r * r - i * i, 2.0 * r * i
        s *= 2
    pad = 16 - len(prs)
    z = jnp.zeros_like(prs[0])
    return jnp.stack(prs + [z] * pad, axis=2), jnp.stack(pis + [z] * pad, axis=2)


def _low_half():
    return lax.broadcasted_iota(jnp.int32, (1, LANES), 1) < 64


def _to_pair(sa, sb):
    low = _low_half()
    return jnp.where(low, sa, pltpu.roll(sb, 64, 1)), jnp.where(low, pltpu.roll(sa, 64, 1), sb)


def _from_pair(hr, hi):
    low = _low_half()
    return jnp.where(low, hr, pltpu.roll(hi, 64, 1)), jnp.where(low, pltpu.roll(hr, 64, 1), hi)


def _pair_scan(sr, si, pr, pi, reverse, conj):
    n = sr.shape[0]
    row = lax.broadcasted_iota(jnp.int32, sr.shape, 0)

    def shift(x, k):
        if reverse:
            return jnp.where(row < n - k, pltpu.roll(x, n - k, 0), 0.0)
        return jnp.where(row >= k, pltpu.roll(x, k, 0), 0.0)

    hr, hi = shift(sr, 1), shift(si, 1)
    k, j = 1, 0
    while k < n:
        tr, ti = shift(hr, k), shift(hi, k)
        ar = pr[j:j + 1, :]
        ai = -pi[j:j + 1, :] if conj else pi[j:j + 1, :]
        hr, hi = hr + ar * tr - ai * ti, hi + ar * ti + ai * tr
        k, j = 2 * k, j + 1
    return hr, hi


def _transpose8(a):
    a = list(a)
    blk = jnp.right_shift(lax.broadcasted_iota(jnp.int32, (1, LANES), 1), 4)
    for dd in (4, 2, 1):
        upper = jnp.bitwise_and(blk, dd) != 0
        for i in range(8):
            if i & dd:
                continue
            lo, hi = a[i], a[i + dd]
            a[i] = jnp.where(upper, pltpu.roll(hi, dd * SSM_GROUP, 1), lo)
            a[i + dd] = jnp.where(upper, hi, pltpu.roll(lo, LANES - dd * SSM_GROUP, 1))
    return a


def _blocks_from_rows(xs):
    t0, t1 = _transpose8(xs[:8]), _transpose8(xs[8:])
    return [jnp.concatenate([t0[g], t1[g]], axis=1) for g in range(8)]


def _rows_from_blocks(ys):
    return _transpose8([y[:, :LANES] for y in ys]) + _transpose8([y[:, LANES:] for y in ys])


def _pair_states(sa, sb, pr_ref, pi_ref, ga, gb, ncc, adjoint):
    n = sa.shape[0]
    low = _low_half()
    out_a, out_b, pairs = [], [], []
    for dirn in range(2):
        xr, xi = _to_pair(sa[:, dirn * LANES:(dirn + 1) * LANES], sb[:, dirn * LANES:(dirn + 1) * LANES])
        pr = jnp.where(low, pr_ref[dirn, ga], pr_ref[dirn, gb])
        pi = jnp.where(low, -pi_ref[dirn, ga], pi_ref[dirn, gb])
        if dirn == 0:
            hr, hi = _pair_scan(xr, xi, pr, pi, adjoint, adjoint)
        else:
            hr, hi = _pair_scan(pltpu.roll(xr, n - ncc, 0), pltpu.roll(xi, n - ncc, 0), pr, pi, not adjoint, adjoint)
            hr, hi = pltpu.roll(hr, ncc, 0), pltpu.roll(hi, ncc, 0)
        pairs.append((hr, hi))
        ha, hb = _from_pair(hr, hi)
        out_a.append(ha)
        out_b.append(hb)
    return (jnp.concatenate(out_a, axis=1), jnp.concatenate(out_b, axis=1)), pairs


def _shift_lanes(x, n, left):
    if n == 0:
        return x
    lo, hi = x[:, :LANES], x[:, LANES:]
    lane = lax.broadcasted_iota(jnp.int32, lo.shape, 1)
    zero = jnp.zeros_like(lo)
    m = n % LANES
    if not left:
        if n < LANES:
            r_lo, r_hi = pltpu.roll(lo, n, 1), pltpu.roll(hi, n, 1)
            return jnp.concatenate([jnp.where(lane >= n, r_lo, 0.0), jnp.where(lane >= n, r_hi, r_lo)], axis=1)
        r_lo = pltpu.roll(lo, m, 1) if m else lo
        return jnp.concatenate([zero, jnp.where(lane >= m, r_lo, 0.0)], axis=1)
    if n < LANES:
        r_lo, r_hi = pltpu.roll(lo, LANES - n, 1), pltpu.roll(hi, LANES - n, 1)
        return jnp.concatenate([jnp.where(lane < LANES - n, r_lo, r_hi), jnp.where(lane < LANES - n, r_hi, 0.0)], axis=1)
    r_hi = pltpu.roll(hi, LANES - m, 1) if m else hi
    return jnp.concatenate([jnp.where(lane < LANES - m, r_hi, 0.0), zero], axis=1)


def _toeplitz(qf, qb):
    return jnp.concatenate([_shift_lanes(qf, SSM_GROUP * s, False) + _shift_lanes(qb, SSM_GROUP * (CHUNK - 1 - s), True)
                            for s in range(CHUNK)], axis=0)


def _toeplitz_t(dmt):
    dqf, dqb = 0.0, 0.0
    for s in range(CHUNK):
        rows = dmt[s * SSM_GROUP:(s + 1) * SSM_GROUP]
        dqf = dqf + _shift_lanes(rows, SSM_GROUP * s, True)
        dqb = dqb + _shift_lanes(rows, SSM_GROUP * (CHUNK - 1 - s), False)
    return dqf, dqb


def _s5_op_specs(idx, jl):
    q = pl.BlockSpec((None, 8, SSM_GROUP, 256), lambda a, b: (jl, idx(a, b), 0, 0))
    op = pl.BlockSpec((None, 8, 256, 256), lambda a, b: (jl, idx(a, b), 0, 0))
    pw = pl.BlockSpec((None, 2, 8, 16, LANES), lambda a, b: (jl, 0, idx(a, b), 0, 0))
    return [q, q, op, op, pw, pw]


def _s5_fwd(z, col0, s5_ops, jl, lc, name, xchg=None):
    bsz, t, _ = z.shape
    nc, ncc = t // CHUNK, lc // CHUNK

    def body(u_ref, qf_ref, qb_ref, bt_ref, ct_ref, pr_ref, pi_ref, y_ref):
        us = [_bf(u) for u in _blocks_from_rows([u_ref[pl.ds(tok, nc, stride=CHUNK), :] for tok in range(CHUNK)])]
        ys = []
        for ga in range(0, 8, 2):
            gb = ga + 1
            hs, _ = _pair_states(_nn(us[ga], bt_ref[ga]), _nn(us[gb], bt_ref[gb]), pr_ref, pi_ref, ga, gb, ncc, False)
            for g, h in zip((ga, gb), hs):
                ys.append(_nn(us[g], _bf(_toeplitz(qf_ref[g], qb_ref[g]))) + _nn(_bf(h), ct_ref[g]))
        for tok, rows in enumerate(_rows_from_blocks(ys)):
            y_ref[pl.ds(tok, nc, stride=CHUNK), :] = rows

    return _call(
        body, name=name, grid=(bsz, 4),
        in_specs=[pl.BlockSpec((None, t, LANES), lambda b, s: (b, 0, col0 + s))] + _s5_op_specs(lambda b, s: s, jl),
        out_specs=[pl.BlockSpec((None, t, LANES), lambda b, s: (b, 0, s))],
        out_shape=[jax.ShapeDtypeStruct((bsz, t, 512), F32)], vmem=VMEM_BIG, xchg=xchg,
        operands=[z, *s5_ops])


def _s5_bwd(z, col0, dy, s5_ops, jl, lc, name, xchg=None):
    bsz, t, _ = z.shape
    nc, ncc = t // CHUNK, lc // CHUNK

    def body(u_ref, dy_ref, qf_ref, qb_ref, bt_ref, ct_ref, pr_ref, pi_ref,
             du_ref, dqf_ref, dqb_ref, dbt_ref, dct_ref, da_ref):
        b = pl.program_id(1)

        @pl.when(b == 0)
        def _():
            dqf_ref[...] = jnp.zeros_like(dqf_ref)
            dqb_ref[...] = jnp.zeros_like(dqb_ref)
            dbt_ref[...] = jnp.zeros_like(dbt_ref)
            dct_ref[...] = jnp.zeros_like(dct_ref)
            da_ref[...] = jnp.zeros_like(da_ref)
        us = [_bf(u) for u in _blocks_from_rows([u_ref[pl.ds(tok, nc, stride=CHUNK), :] for tok in range(CHUNK)])]
        dys = [_bf(u) for u in _blocks_from_rows([dy_ref[pl.ds(tok, nc, stride=CHUNK), :] for tok in range(CHUNK)])]
        row8 = lax.broadcasted_iota(jnp.int32, (8, LANES), 0)
        dus = []
        for ga in range(0, 8, 2):
            gb = ga + 1
            hs, hp = _pair_states(_nn(us[ga], bt_ref[ga]), _nn(us[gb], bt_ref[gb]), pr_ref, pi_ref, ga, gb, ncc, False)
            gs, gp = _pair_states(_nt(dys[ga], ct_ref[ga]), _nt(dys[gb], ct_ref[gb]), pr_ref, pi_ref, ga, gb, ncc, True)
            for g, h, gg in zip((ga, gb), hs, gs):
                hcat, gcat = _bf(h), _bf(gg)
                dus.append(_nt(dys[g], _bf(_toeplitz(qf_ref[g], qb_ref[g]))) + _nt(gcat, bt_ref[g]))
                dqf, dqb = _toeplitz_t(_tn(us[g], dys[g]))
                dqf_ref[g] += dqf
                dqb_ref[g] += dqb
                dct_ref[g] += _tn(hcat, dys[g])
                dbt_ref[g] += _tn(us[g], gcat)
            da = jnp.zeros((8, LANES), F32)
            for dirn in range(2):
                (hr, hi), (gr, gi) = hp[dirn], gp[dirn]
                da = da + jnp.where(row8 == 2 * dirn, jnp.sum(gr * hr + gi * hi, axis=0, keepdims=True), 0.0)
                da = da + jnp.where(row8 == 2 * dirn + 1, jnp.sum(gi * hr - gr * hi, axis=0, keepdims=True), 0.0)
            da_ref[ga // 2] += da
        for tok, rows in enumerate(_rows_from_blocks(dus)):
            du_ref[pl.ds(tok, nc, stride=CHUNK), :] = rows

    op_out = pl.BlockSpec((8, 256, 256), lambda s, b: (s, 0, 0))
    op_shape = jax.ShapeDtypeStruct((SSM_GROUPS, 256, 256), F32)
    q_out = pl.BlockSpec((8, SSM_GROUP, 256), lambda s, b: (s, 0, 0))
    q_shape = jax.ShapeDtypeStruct((SSM_GROUPS, SSM_GROUP, 256), F32)
    return _call(
        body, name=name, grid=(4, bsz),
        in_specs=[pl.BlockSpec((None, t, LANES), lambda s, b: (b, 0, col0 + s)),
                  pl.BlockSpec((None, t, LANES), lambda s, b: (b, 0, s))] + _s5_op_specs(lambda s, b: s, jl),
        out_specs=[pl.BlockSpec((None, t, LANES), lambda s, b: (b, 0, s)), q_out, q_out, op_out, op_out,
                   pl.BlockSpec((4, 8, LANES), lambda s, b: (s, 0, 0))],
        out_shape=(jax.ShapeDtypeStruct((bsz, t, 512), F32), q_shape, q_shape, op_shape, op_shape,
                   jax.ShapeDtypeStruct((SSM_GROUPS // 2, 8, LANES), F32)),
        vmem=VMEM_BIG, xchg=xchg, operands=[z, dy, *s5_ops])


def _glu_in(y, u, dsk):
    return jax.nn.gelu(y + u * dsk)


def _even_mix(oa, ga, zg, tg, gs):
    return jnp.concatenate([oa * jax.nn.silu(ga), zg * jax.nn.sigmoid(tg) * jax.nn.silu(gs)], axis=1)


def _even_specs(d, i, jl, nct):
    return [_tile(d), _mod_spec(i, nct, d), _tile(512), _tile(512)] + [_tile(256, cb) for cb in range(3, 9)] + [
        _layer_spec(jl, 1, 512), _layer_spec(0, 512, 512), _layer_spec(jl, 1, 512), _layer_spec(0, 1024, d)]


def _cat2(a_ref, b_ref):
    return jnp.concatenate([a_ref[...], b_ref[...]], axis=1)


def _even_out_fwd(hs, mods, oa, y, z, dsk, gw, gb, wout, i, jl, nct, name):
    bsz, t, d = hs.shape

    def body(h_ref, mod_ref, oa_ref, y_ref, ga0, ga1, u0, u1, gs0, gs1, dsk_ref, gw_ref, gb_ref, wo_ref, o_ref):
        zg = _glu_in(y_ref[...], _cat2(u0, u1), dsk_ref[...])
        tg = _nn(_bf(zg), gw_ref[...]) + gb_ref[...]
        mix = _even_mix(oa_ref[...], _cat2(ga0, ga1), zg, tg, _cat2(gs0, gs1))
        o_ref[...] = h_ref[...] + mod_ref[2:3, :] * _nn(_bf(mix), wo_ref[...])

    return pl.pallas_call(
        body, name=name, grid=(bsz, t // ROWT), in_specs=_even_specs(d, i, jl, nct),
        out_specs=_tile(d), out_shape=jax.ShapeDtypeStruct((bsz, t, d), F32), compiler_params=_cp(VMEM_BIG),
    )(hs, mods, oa, y, z, z, z, z, z, z, dsk, gw, gb, wout)


def _even_out_bwd(dh, mods, oa, y, z, dsk, gw, gb, wout, i, jl, nct, name):
    bsz, t, d = dh.shape

    def body(dh_ref, mod_ref, oa_ref, y_ref, ga0, ga1, u0, u1, gs0, gs1, dsk_ref, gw_ref, gb_ref, wo_ref,
             doa_ref, dga_ref, dy_ref, dgs_ref, dmod_ref, ddsk_ref, dgw_ref, dgb_ref, dwo_ref):
        b, j = pl.program_id(0), pl.program_id(1)
        zg, vj1 = jax.vjp(_glu_in, y_ref[...], _cat2(u0, u1), dsk_ref[...])
        zgb = _bf(zg)
        tg = _nn(zgb, gw_ref[...]) + gb_ref[...]
        mix, vj2 = jax.vjp(_even_mix, oa_ref[...], _cat2(ga0, ga1), zg, tg, _cat2(gs0, gs1))
        mixb = _bf(mix)
        dhv = dh_ref[...]
        dyo = _bf(dhv * mod_ref[2:3, :])
        yout = _nn(mixb, wo_ref[...])
        doa, dga, dzg, dtg, dgs = vj2(_nt(dyo, wo_ref[...]))
        dtb = _bf(dtg)
        dzg = dzg + _nt(dtb, gw_ref[...])
        dy, _, ddsk = vj1(dzg)
        doa_ref[...], dga_ref[...], dy_ref[...], dgs_ref[...] = doa, dga, dy, dgs

        @pl.when((j == 0) | (j == nct))
        def _():
            dmod_ref[...] = jnp.zeros_like(dmod_ref)

        @pl.when((b == 0) & (j == 0))
        def _():
            ddsk_ref[...] = jnp.zeros_like(ddsk_ref)
            dgw_ref[...] = jnp.zeros_like(dgw_ref)
            dgb_ref[...] = jnp.zeros_like(dgb_ref)
            dwo_ref[...] = jnp.zeros_like(dwo_ref)
        dmod_ref[2:3, :] += jnp.sum(dhv * yout, axis=0, keepdims=True)
        ddsk_ref[...] += ddsk
        dgw_ref[...] += _tn(zgb, dtb)
        dgb_ref[...] += jnp.sum(dtg, axis=0, keepdims=True)
        dwo_ref[...] += _tn(mixb, dyo)

    full = lambda r, c: pl.BlockSpec((r, c), lambda b, j: (0, 0))
    f512 = jax.ShapeDtypeStruct((bsz, t, 512), F32)
    return pl.pallas_call(
        body, name=name, grid=(bsz, t // ROWT), in_specs=_even_specs(d, i, jl, nct),
        out_specs=[_tile(512), _tile(512), _tile(512), _tile(512), _dmod_spec(nct, d),
                   full(1, 512), full(512, 512), full(1, 512), full(1024, d)],
        out_shape=(f512, f512, f512, f512, jax.ShapeDtypeStruct((2 * bsz, 3, d), F32),
                   jax.ShapeDtypeStruct((1, 512), F32), jax.ShapeDtypeStruct((512, 512), F32),
                   jax.ShapeDtypeStruct((1, 512), F32), jax.ShapeDtypeStruct((1024, d), F32)),
        compiler_params=_cp(VMEM_BIG),
    )(dh, mods, oa, y, z, z, z, z, z, z, dsk, gw, gb, wout)


def _pool(src, col_blk, r, lc, transpose, name):
    bsz, t, _ = src.shape
    segs = [(0, lc, 16), (lc, t - lc, 32 + lc)]
    halo = [(0, 16), (16 + lc, 16), (32 + t, 16)]
    cw = 256

    def inv_count(t0, rows, ln):
        pos = t0 + lax.broadcasted_iota(jnp.int32, (rows, cw), 0)
        cnt = jnp.minimum(pos + r + 1, ln) - jnp.maximum(pos - r, 0)
        return 1.0 / cnt.astype(F32)

    def body(x_ref, o_ref, s_ref):
        for h0, hn in halo:
            s_ref[pl.ds(h0, hn), :] = jnp.zeros((hn, cw), F32)
        for r0, ln, s0 in segs:
            step = min(512, ln)
            for c0 in range(0, ln, step):
                v = x_ref[pl.ds(r0 + c0, step), :]
                s_ref[pl.ds(s0 + c0, step), :] = v * inv_count(c0, step, ln) if transpose else v
        for r0, ln, s0 in segs:
            step = min(512, ln)
            for c0 in range(0, ln, step):
                acc = s_ref[pl.ds(s0 + c0 - r, step), :]
                for dlt in range(-r + 1, r + 1):
                    acc = acc + s_ref[pl.ds(s0 + c0 + dlt, step), :]
                ctr = x_ref[pl.ds(r0 + c0, step), :]
                o_ref[pl.ds(r0 + c0, step), :] = (acc if transpose else acc * inv_count(c0, step, ln)) - ctr

    return pl.pallas_call(
        body, name=name, grid=(bsz,),
        in_specs=[pl.BlockSpec((None, t, cw), lambda b: (b, 0, col_blk))],
        out_specs=pl.BlockSpec((None, t, cw), lambda b: (b, 0, 0)),
        out_shape=jax.ShapeDtypeStruct((bsz, t, cw), F32),
        scratch_shapes=[pltpu.VMEM((t + 48, cw), F32)],
        compiler_params=_cp(VMEM_BIG),
    )(src)


def _odd_specs(d, i, jl, nct):
    return [_tile(d), _mod_spec(i, nct, d), _tile(256), _tile(256), _tile(256), _tile(256), _tile(1024, 1),
            _layer_spec(0, 4, 256, 256), _layer_spec(jl, 1, 1024), _layer_spec(0, 1024, d)]


def _odd_mix(pm, psc, gz):
    return pm * psc * jax.nn.silu(gz)


def _odd_out_fwd(hs, mods, ps, z, pw, psc, wout, i, jl, nct, name, loss=None):
    bsz, t, d = hs.shape

    def h_new(h_ref, mod_ref, p0, p1, p2, p3, gz_ref, pw_ref, psc_ref, wo_ref):
        pm = jnp.concatenate([_nn(_bf(p[...]), pw_ref[q]) for q, p in enumerate((p0, p1, p2, p3))], axis=1)
        mix = _odd_mix(pm, psc_ref[...], gz_ref[...])
        return h_ref[...] + mod_ref[2:3, :] * _nn(_bf(mix), wo_ref[...])

    if loss is None:
        def body(*refs):
            refs[-1][...] = h_new(*refs[:-1])

        return pl.pallas_call(
            body, name=name, grid=(bsz, t // ROWT), in_specs=_odd_specs(d, i, jl, nct),
            out_specs=_tile(d), out_shape=jax.ShapeDtypeStruct((bsz, t, d), F32), compiler_params=_cp(VMEM_BIG),
        )(hs, mods, *ps, z, pw, psc, wout)

    def body(*refs):
        g_ref, t_ref, dh_ref, loss_ref, dg_ref = refs[-5:]
        b, j = pl.program_id(0), pl.program_id(1)
        y, vj = jax.vjp(_rms, h_new(*refs[:-5]), g_ref[...])
        err = jnp.where(j >= nct, y - t_ref[...], 0.0)
        dh, dg = vj(err * (1.0 / d))
        dh_ref[...] = dh

        @pl.when(j == 0)
        def _():
            loss_ref[...] = jnp.zeros_like(loss_ref)

        @pl.when((b == 0) & (j == 0))
        def _():
            dg_ref[...] = jnp.zeros_like(dg_ref)
        loss_ref[...] += 0.5 * jnp.sum(jnp.mean(err * err, axis=-1))
        dg_ref[...] += dg

    return pl.pallas_call(
        body, name=name, grid=(bsz, t // ROWT),
        in_specs=_odd_specs(d, i, jl, nct) + [pl.BlockSpec((1, d), lambda b, j: (0, 0)),
                                              pl.BlockSpec((None, ROWT, d), lambda b, j: (b, jnp.maximum(j - nct, 0), 0))],
        out_specs=[_tile(d), pl.BlockSpec((None, 8, LANES), lambda b, j: (b, 0, 0)),
                   pl.BlockSpec((1, d), lambda b, j: (0, 0))],
        out_shape=(jax.ShapeDtypeStruct((bsz, t, d), F32), jax.ShapeDtypeStruct((bsz, 8, LANES), F32),
                   jax.ShapeDtypeStruct((1, d), F32)),
        compiler_params=_cp(VMEM_BIG),
    )(hs, mods, *ps, z, pw, psc, wout, *loss)


def _odd_out_bwd(dh, mods, ps, z, pw, psc, wout, i, jl, nct, name):
    bsz, t, d = dh.shape

    def body(dh_ref, mod_ref, p0, p1, p2, p3, gz_ref, pw_ref, psc_ref, wo_ref,
             dp_ref, dgz_ref, dmod_ref, dpw_ref, dpsc_ref, dwo_ref):
        b, j = pl.program_id(0), pl.program_id(1)
        pbs = [_bf(p[...]) for p in (p0, p1, p2, p3)]
        pm = jnp.concatenate([_nn(pb, pw_ref[q]) for q, pb in enumerate(pbs)], axis=1)
        mix, vj = jax.vjp(_odd_mix, pm, psc_ref[...], gz_ref[...])
        mixb = _bf(mix)
        dhv = dh_ref[...]
        dyo = _bf(dhv * mod_ref[2:3, :])
        yout = _nn(mixb, wo_ref[...])
        dpm, dpsc, dgz = vj(_nt(dyo, wo_ref[...]))
        dgz_ref[...] = dgz
        dpmb = _bf(dpm)
        dp_ref[...] = jnp.concatenate([_nt(dpmb[:, q * 256:(q + 1) * 256], pw_ref[q]) for q in range(4)], axis=1)

        @pl.when((j == 0) | (j == nct))
        def _():
            dmod_ref[...] = jnp.zeros_like(dmod_ref)

        @pl.when((b == 0) & (j == 0))
        def _():
            dpw_ref[...] = jnp.zeros_like(dpw_ref)
            dpsc_ref[...] = jnp.zeros_like(dpsc_ref)
            dwo_ref[...] = jnp.zeros_like(dwo_ref)
        dmod_ref[2:3, :] += jnp.sum(dhv * yout, axis=0, keepdims=True)
        for q in range(4):
            dpw_ref[q] += _tn(pbs[q], dpmb[:, q * 256:(q + 1) * 256])
        dpsc_ref[...] += dpsc
        dwo_ref[...] += _tn(mixb, dyo)

    full = lambda *s: pl.BlockSpec(s, lambda b, j: (0,) * len(s))
    return pl.pallas_call(
        body, name=name, grid=(bsz, t // ROWT), in_specs=_odd_specs(d, i, jl, nct),
        out_specs=[_tile(1024), _tile(1024), _dmod_spec(nct, d), full(4, 256, 256), full(1, 1024), full(1024, d)],
        out_shape=(jax.ShapeDtypeStruct((bsz, t, 1024), F32), jax.ShapeDtypeStruct((bsz, t, 1024), F32),
                   jax.ShapeDtypeStruct((2 * bsz, 3, d), F32), jax.ShapeDtypeStruct((4, 256, 256), F32),
                   jax.ShapeDtypeStruct((1, 1024), F32), jax.ShapeDtypeStruct((1024, d), F32)),
        compiler_params=_cp(VMEM_BIG),
    )(dh, mods, *ps, z, pw, psc, wout)


def _rms(h, g):
    return h * lax.rsqrt(jnp.mean(h * h, axis=-1, keepdims=True) + EPS) * g


def _adamw(w, g_list, m, v, transpose_g, rows, name):
    nl, r, c = w.shape
    ns = g_list[0].shape[1]
    rt = rows or r
    offs = np.cumsum([0] + [g.shape[0] for g in g_list])
    ng = len(g_list)

    def body(*refs):
        w_ref, m_ref, v_ref = refs[0], refs[1 + ng], refs[2 + ng]
        go_ref, d_ref, mo_ref, vo_ref = refs[3 + ng:]
        layer = pl.program_id(0)
        for k in range(ng):
            g_ref = refs[1 + k]

            @pl.when((layer >= offs[k]) & (layer < offs[k + 1]))
            def _():
                g = g_ref[0].astype(F32)
                for s in range(1, ns):
                    g = g + g_ref[s].astype(F32)
                if transpose_g:
                    g = g.T
                mn = ADAM_B1 * m_ref[...] + (1.0 - ADAM_B1) * g
                vn = ADAM_B2 * v_ref[...] + (1.0 - ADAM_B2) * jnp.square(g)
                m_hat = mn / (1.0 - ADAM_B1 ** ADAM_STEP)
                v_hat = vn / (1.0 - ADAM_B2 ** ADAM_STEP)
                go_ref[...] = g
                d_ref[...] = -ADAM_LR * (m_hat / (jnp.sqrt(v_hat) + ADAM_EPS) + ADAM_WD * w_ref[...])
                mo_ref[...] = mn
                vo_ref[...] = vn

    def g_spec(k):
        lo, n_k = int(offs[k]), int(offs[k + 1] - offs[k])

        def idx(l, q):
            mine = (l >= lo) & (l < lo + n_k)
            q = jnp.where(mine, q, 0)
            return (jnp.clip(l - lo, 0, n_k - 1), 0, 0, q) if transpose_g else (jnp.clip(l - lo, 0, n_k - 1), 0, q, 0)
        return pl.BlockSpec((None, ns, c, rt) if transpose_g else (None, ns, rt, c), idx)

    blk = pl.BlockSpec((None, rt, c), lambda l, q: (l, q, 0))
    out = jax.ShapeDtypeStruct((nl, r, c), F32)
    return pl.pallas_call(
        body, name=name, grid=(nl, r // rt),
        in_specs=[blk] + [g_spec(k) for k in range(ng)] + [blk, blk],
        out_specs=[blk, blk, blk, blk], out_shape=(out, out, out, out), compiler_params=_cp(VMEM_BIG),
    )(w, *g_list, m, v)


def _adamw_nd(ws, gs, ms, vs, lead, name):
    shapes = [w.shape for w in ws]
    as2 = lambda a: a.reshape(1, -1) if a.ndim == 1 else a
    ws, gs, ms, vs = ([as2(a) for a in lst] for lst in (ws, gs, ms, vs))
    nts = len(ws)

    def body(*refs):
        for k in range(nts):
            w_ref, g_ref, m_ref, v_ref = refs[4 * k:4 * k + 4]
            d_ref, mo_ref, vo_ref = refs[4 * nts + 3 * k:4 * nts + 3 * k + 3]
            g = g_ref[...]
            mn = ADAM_B1 * m_ref[...] + (1.0 - ADAM_B1) * g
            vn = ADAM_B2 * v_ref[...] + (1.0 - ADAM_B2) * jnp.square(g)
            m_hat = mn / (1.0 - ADAM_B1 ** ADAM_STEP)
            v_hat = vn / (1.0 - ADAM_B2 ** ADAM_STEP)
            d_ref[...] = -ADAM_LR * (m_hat / (jnp.sqrt(v_hat) + ADAM_EPS) + ADAM_WD * w_ref[...])
            mo_ref[...] = mn
            vo_ref[...] = vn

    grid = tuple(ws[0].shape[:lead])

    def spec(a):
        rest = a.shape[lead:]
        return pl.BlockSpec((None,) * lead + tuple(rest), lambda *ids: tuple(ids) + (0,) * len(rest))

    ins = [a for k in range(nts) for a in (ws[k], gs[k], ms[k], vs[k])]
    outs = [w for w in ws for _ in range(3)]
    res = pl.pallas_call(
        body, name=name, grid=grid, in_specs=[spec(a) for a in ins], out_specs=[spec(a) for a in outs],
        out_shape=[jax.ShapeDtypeStruct(a.shape, F32) for a in outs], compiler_params=_cp(VMEM_BIG))(*ins)
    return [tuple(res[3 * k + q].reshape(shapes[k]) for q in range(3)) for k in range(nts)]


def _sum_slots(slots, name):
    ns, r, c = slots.shape

    def body(s_ref, o_ref):
        g = s_ref[0]
        for s in range(1, ns):
            g = g + s_ref[s]
        o_ref[...] = g

    return pl.pallas_call(body, name=name, out_shape=jax.ShapeDtypeStruct((r, c), F32), compiler_params=_cp(VMEM_BIG))(slots)


def _pack(arrs):
    flat = jnp.concatenate([a.reshape(-1) for a in arrs])
    return jnp.pad(flat, (0, (-flat.shape[0]) % (PACK_ROWS * PACK_W))).reshape(-1, PACK_W)


def _unpack(buf, shapes):
    flat = buf.reshape(-1)
    out, off = [], 0
    for s in shapes:
        n = int(np.prod(s))
        out.append(flat[off:off + n].reshape(s))
        off += n
    return out


def kernel(x, c, ctx, c_ctx, ada_w, ada_b, norm_g, even_w_in, even_w_out, attn_sink, ssm_a_re, ssm_a_im, ssm_log_dt, ssm_b_re, ssm_b_im, ssm_c_re, ssm_c_im, ssm_d, glu_w, glu_b, odd_w_in, odd_w_out, pool_w, pool_scale, final_g, loss_target, m_c_ctx, m_ada_w, m_ada_b, m_norm_g, m_even_w_in, m_even_w_out, m_attn_sink, m_ssm_a_re, m_ssm_a_im, m_ssm_log_dt, m_ssm_b_re, m_ssm_b_im, m_ssm_c_re, m_ssm_c_im, m_ssm_d, m_glu_w, m_glu_b, m_odd_w_in, m_odd_w_out, m_pool_w, m_pool_scale, m_final_g, v_c_ctx, v_ada_w, v_ada_b, v_norm_g, v_even_w_in, v_even_w_out, v_attn_sink, v_ssm_a_re, v_ssm_a_im, v_ssm_log_dt, v_ssm_b_re, v_ssm_b_im, v_ssm_c_re, v_ssm_c_im, v_ssm_d, v_glu_w, v_glu_b, v_odd_w_in, v_odd_w_out, v_pool_w, v_pool_scale, v_final_g):
    args = dict(locals())
    bsz, l, d = x.shape
    lc = ctx.shape[1]
    t = lc + l
    nct = lc // ROWT
    depth = ada_w.shape[0]
    nl2 = even_w_in.shape[0]
    me = _my_index()
    assert lc % ROWT == 0 and l % ROWT == 0 and d == 1024 and bsz * N_DEV < 32 and depth % 2 == 0

    npw = pool_w.shape[2]
    shards = {"even": [_bf(jnp.transpose(even_w_in, (0, 2, 1))), _bf(even_w_out), _bf(glu_w)],
              "odd": [_bf(jnp.transpose(odd_w_in, (0, 2, 1))), _bf(odd_w_out), _bf(pool_w.reshape(nl2, -1, pool_w.shape[-1]))]}

    def wgather(kind, jl):
        return shards[kind], [("gather", [(ii, jl)], 0) for ii in range(3)]

    def wjoin(kind, res):
        full = [a.reshape(1, N_DEV * a.shape[2], a.shape[3]) for a in res]
        if kind == "odd":
            full[2] = jnp.transpose(res[2].reshape(1, N_DEV, 4, npw, 256), (0, 2, 1, 3, 4)).reshape(1, 4, 256, 256)
        return full

    got = _exchange([shards["even"][0], c, pool_scale],
                    [("gather", [(0, 0)], 0), ("gather", [(1, None)], 0), ("gather", [(2, None)], 0)], "gather_first")
    first_wt = got[0]
    c_all = got[1].reshape(N_DEV * bsz, d)
    psc_full = jnp.transpose(got[2][0], (1, 0, 2)).reshape(nl2, 1, d)
    weights = {}

    n_rows = 32
    craw = jnp.concatenate([c_all, c_ctx[None], jnp.zeros((n_rows - N_DEV * bsz - 1, d), F32)], axis=0)
    ncol = ada_w.shape[2]
    ada_b_loc = lax.dynamic_slice_in_dim(ada_b, me * ncol, ncol, axis=1)[:, None, :]
    m_loc = _adaln_fwd(craw, ada_w, ada_b_loc)
    m_all = _exchange([m_loc], [("gather", [(0, None)], 0)], "gather_mod")[0][0]
    m_all = jnp.transpose(m_all, (1, 2, 0, 3)).reshape(depth, n_rows, 3 * d)
    lat = lax.dynamic_slice_in_dim(m_all, me * bsz, bsz, axis=1).reshape(depth, bsz, 1, 3, d)
    cmod = jnp.broadcast_to(m_all[:, N_DEV * bsz].reshape(depth, 1, 1, 3, d), (depth, bsz, 1, 3, d))
    mods = jnp.concatenate([cmod, lat], axis=2).reshape(depth, 2 * bsz, 3, d)

    cos, sin = _rope_tables(lc, l)
    hs = jnp.concatenate([ctx, x], axis=1)
    g3 = norm_g[:, None, :]
    dsk3, gb3 = ssm_d[:, None, :], glu_b[:, None, :]
    u_col = 1280 // LANES

    s5_prm = (ssm_a_re, ssm_a_im, ssm_log_dt, ssm_b_re, ssm_b_im, ssm_c_re, ssm_c_im)
    ops = jax.vmap(_s5_operators)(*s5_prm)
    pr, pi = jax.vmap(lambda r, q: _scan_powers(r, q, t // CHUNK))(ops[4], ops[5])
    s5_ops = (ops[0], ops[1], _bf(ops[2]), _bf(ops[3]), pr, pi)
    saved = []
    for i in range(depth):
        jl = i // 2
        if i % 2 == 0:
            if i == 0:
                wt = first_wt.reshape(1, -1, d)
                z, got = _fused_in_fwd(hs, mods, g3, wt, i, 0, nct, f"in_fwd{i}",
                                       xchg=(shards["even"][1:], [("gather", [(0, 0)], 0), ("gather", [(1, 0)], 0)]))
                weights[("even", 0)] = wjoin("even", [first_wt] + got)
            wt, wo, gwf = weights[("even", jl)]
            if i > 0:
                z, _ = _fused_in_fwd(hs, mods, g3, wt, i, 0, nct, f"in_fwd{i}")
            qp, kvp = _rope_fwd(z, cos, sin, 2, f"rope_fwd{i}")
            (oa, lse), got = _attn_fwd(qp, kvp, attn_sink, jl, lc, f"attn_fwd{i}", xchg=wgather("odd", jl))
            weights[("odd", jl)] = wjoin("odd", got)
            nxt = wgather("even", jl + 1) if jl + 1 < nl2 else None
            (y,), got = _s5_fwd(z, u_col, s5_ops, jl, lc, f"s5_fwd{i}", xchg=nxt)
            if nxt:
                weights[("even", jl + 1)] = wjoin("even", got)
            hn = _even_out_fwd(hs, mods, oa, y, z, dsk3, gwf, gb3, wo, i, jl, nct, f"out_fwd{i}")
            saved.append(dict(hs=hs, z=z, qp=qp, kvp=kvp, oa=oa, lse=lse, y=y))
        else:
            wt, wo, pwf = weights[("odd", jl)]
            z, _ = _fused_in_fwd(hs, mods, g3, wt, i, 0, nct, f"in_fwd{i}")
            ps = [_pool(z, gi, wd // 2, lc, False, f"pool_fwd{i}_{gi}") for gi, wd in enumerate(POOL_WINDOWS)]
            saved.append(dict(hs=hs, z=z, ps=ps))
            if i == depth - 1:
                dh, loss_p, d_final_g = _odd_out_fwd(hs, mods, ps, z, pwf, psc_full, wo, i, jl, nct, f"out_fwd{i}",
                                                     loss=(final_g[None], loss_target))
                break
            hn = _odd_out_fwd(hs, mods, ps, z, pwf, psc_full, wo, i, jl, nct, f"out_fwd{i}")
        hs = hn

    loss = lax.psum(jnp.sum(loss_p[:, 0, 0]), ("x", "y", "c"))

    slots = {n: [None] * nl2 for n in ("even_w_in", "even_w_out", "glu_w", "odd_w_in", "odd_w_out", "pool_w", "pool_scale")}
    gsmall = {n: [None] * nl2 for n in ("attn_sink", "ssm_d", "glu_b")}
    d_norm_g, d_mods = [None] * depth, [None] * depth
    d_ops = [None] * nl2

    def rows_xchg(named):
        arrs = [a for _, _, a in named]
        specs = [("rows2" if n == "pool_w" else "rows", [(k, None)], a.shape[1 if n == "pool_w" else 0] // N_DEV)
                 for k, (n, _, a) in enumerate(named)]
        return (arrs, specs)

    def keep(named, res):
        for (n, jl_, _), r_ in zip(named, res):
            slots[n][jl_] = r_

    pending = None
    for i in reversed(range(depth)):
        jl = i // 2
        sv = saved[i]
        if i % 2 == 0:
            wt, wo, gwf = weights[("even", jl)]
            doa, dga, dy, dgs, dmod_g, ddsk, dgw, dgb, dwo = _even_out_bwd(
                dh, mods, sv["oa"], sv["y"], sv["z"], dsk3, gwf, gb3, wo, i, jl, nct, f"out_bwd{i}")
            (dq_r, dkv_acc, dsink), got = _attn_bwd(sv["qp"], sv["kvp"], attn_sink, jl, sv["oa"], sv["lse"], doa, lc,
                                                     f"attn_bwd{i}", xchg=rows_xchg(pending) if pending else None)
            if pending:
                keep(pending, got)
            dq, dkv = _rope_bwd(dq_r, dkv_acc, cos, sin, t, f"rope_bwd{i}")
            mine = [("even_w_out", jl, dwo), ("glu_w", jl, dgw)]
            (du, dqf, dqb, dbt, dct, dav), got = _s5_bwd(sv["z"], u_col, dy, s5_ops, jl, lc, f"s5_bwd{i}", xchg=rows_xchg(mine))
            keep(mine, got)
            per_group = lambda q: dav[:, q].reshape(SSM_GROUPS, SSM_STATE)
            d_ops[jl] = (dqf, dqb, dbt, dct, jnp.stack([per_group(0), per_group(2)]), jnp.stack([per_group(1), per_group(3)]))
            gsmall["attn_sink"][jl] = jnp.sum(dsink[:, :, 0], axis=0)
            gsmall["ssm_d"][jl] = ddsk[0]
            gsmall["glu_b"][jl] = dgb[0]
            pieces = [(dq, 512, 0), (dkv, 256, 0), (dga, 512, 0), (du, 512, 0), (dy, 512, 0), (dgs, 512, 0)]
            asm = lambda p, s: [p[0], p[1], p[2], p[3] + p[4] * s, p[5]]
            (dh, dmod_in, dg, dw), _ = _fused_in_bwd(sv["hs"], mods, g3, wt, dh, pieces, (dsk3, jl), asm, i, 0, nct, f"in_bwd{i}",
                                                     latent_only=(i == 0))
            pending = [("even_w_in", jl, dw)]
        else:
            wt, wo, pwf = weights[("odd", jl)]
            dp, dgz, dmod_g, dpw, dpsc, dwo = _odd_out_bwd(
                dh, mods, sv["ps"], sv["z"], pwf, psc_full, wo, i, jl, nct, f"out_bwd{i}")
            dus = [_pool(dp, gi, wd // 2, lc, True, f"pool_bwd{i}_{gi}") for gi, wd in enumerate(POOL_WINDOWS)]
            dpsc8 = jnp.broadcast_to(dpsc.reshape(N_DEV, 1, -1), (N_DEV, 8, d // N_DEV)).reshape(N_DEV * 8, -1)
            pieces = [(u_, 256, 0) for u_ in dus] + [(dgz, 1024, 0)]
            asm = lambda p, s: p
            (dh, dmod_in, dg, dw), got = _fused_in_bwd(sv["hs"], mods, g3, wt, dh, pieces, None, asm, i, 0, nct, f"in_bwd{i}",
                                                       xchg=rows_xchg(pending) if pending else None)
            if pending:
                keep(pending, got)
            pending = [("odd_w_in", jl, dw), ("odd_w_out", jl, dwo), ("pool_w", jl, dpw), ("pool_scale", jl, dpsc8)]
        d_norm_g[i] = dg[0]
        d_mods[i] = (dmod_g + dmod_in).reshape(bsz, 2, 3 * d)
    grad_x = dh

    _, op_vjp = jax.vjp(jax.vmap(_s5_operators), *s5_prm)
    dprm = op_vjp(tuple(jnp.stack([d_ops[jl][q] for jl in range(nl2)]) for q in range(6)))

    dmd = jnp.stack(d_mods)
    dm_loc = jnp.concatenate([dmd[:, :, 1], jnp.sum(dmd[:, :, 0], axis=1, keepdims=True)], axis=1)
    dm_all = _exchange([dm_loc], [("gather", [(0, None)], 0)], "gather_dmod")[0][0]
    dm_lat = jnp.transpose(dm_all[:, :, :bsz], (1, 0, 2, 3)).reshape(depth, N_DEV * bsz, 3 * d)
    dm_ctx = dm_all[0, :, bsz]
    for e in range(1, N_DEV):
        dm_ctx = dm_ctx + dm_all[e, :, bsz]
    dm_rows = jnp.concatenate([dm_lat, dm_ctx[:, None], jnp.zeros((depth, n_rows - N_DEV * bsz - 1, 3 * d), F32)], axis=1)
    dm_cols = lax.dynamic_slice_in_dim(dm_rows, me * ncol, ncol, axis=2)
    g_ada_w, dcraw = _adaln_bwd(craw, dm_cols, ada_w)

    small = {"c_ctx": dcraw[N_DEV * bsz], "ada_b": jnp.sum(dm_loc, axis=1), "norm_g": jnp.stack(d_norm_g),
             "final_g": d_final_g[0]}
    for n, v2 in gsmall.items():
        small[n] = jnp.stack(v2)
    for n, gval in zip(("ssm_a_re", "ssm_a_im", "ssm_log_dt", "ssm_b_re", "ssm_b_im", "ssm_c_re", "ssm_c_im"), dprm):
        small[n] = gval
    snames = list(small)
    sshapes = [args[n].shape for n in snames]
    spack = _pack([small[n].reshape(args[n].shape) for n in snames])

    pending = [(n, l_, _bf(a)) for n, l_, a in pending]
    last = pending + [("small", 0, spack)]
    red = _exchange(*rows_xchg(last), "reduce_last")
    keep(pending, red[:-1])
    s_sum = _sum_slots(red[-1][0], "sum_small")
    s_all = _exchange([s_sum], [("gather", [(0, None)], 0)], "gather_small")[0]
    s_all = s_all.reshape(1, 1, -1, PACK_W)

    out = {}

    def put(n, res, shape):
        for kind, buf in zip(("grad_", "delta_", "new_m_", "new_v_"), res):
            out[kind + n] = buf.reshape(shape)

    def as3(a):
        return a.reshape(a.shape[0], -1, a.shape[-1])

    for n in slots:
        w = args[n]
        g_list = [s_[:, :, :1] if n == "pool_scale" else s_ for s_ in slots[n]]
        g_list = [s_.reshape(1, N_DEV, -1, s_.shape[-1]) for s_ in g_list]
        tr = n in ("even_w_in", "odd_w_in")
        put(n, _adamw(as3(w), g_list, as3(args["m_" + n]), as3(args["v_" + n]), tr, 256 if tr else None, "adamw_" + n),
            w.shape)
    put("ada_w", _adamw(ada_w, [g_ada_w[:, None]], m_ada_w, v_ada_w, False, None, "adamw_ada_w"), ada_w.shape)
    for n, a in zip(snames, _unpack(s_all, sshapes)):
        out["grad_" + n] = a
    for lead, name in ((0, "adamw_small"), (2, "adamw_ssm_bc")):
        grp = [n for n in snames if (args[n].ndim == 5) == (lead == 2)]
        res = _adamw_nd([args[n] for n in grp], [out["grad_" + n] for n in grp], [args["m_" + n] for n in grp],
                        [args["v_" + n] for n in grp], lead, name)
        for n, (dl, mn, vn) in zip(grp, res):
            out["delta_" + n], out["new_m_" + n], out["new_v_" + n] = dl, mn, vn

    order = ["c_ctx", "ada_w", "ada_b", "norm_g", "even_w_in", "even_w_out", "attn_sink", "ssm_a_re", "ssm_a_im",
             "ssm_log_dt", "ssm_b_re", "ssm_b_im", "ssm_c_re", "ssm_c_im", "ssm_d", "glu_w", "glu_b", "odd_w_in",
             "odd_w_out", "pool_w", "pool_scale", "final_g"]
    return (loss, grad_x, *[out[k + n] for k in ("grad_", "delta_", "new_m_", "new_v_") for n in order])
```

```python
import functools

import numpy as np
import jax
import jax.numpy as jnp
from jax import lax
from jax.experimental import pallas as pl
from jax.experimental.pallas import tpu as pltpu

F32 = jnp.float32
BF16 = jnp.bfloat16
EPS = 1e-6
NEG_INF = -1e30
N_DEV = 8
GRID_W = 64
WINDOW = 128
QBLK = 128
ROWT = 256
CHUNK = 16
SSM_GROUPS = 32
SSM_GROUP = 16
SSM_STATE = 64
POOL_WINDOWS = (2, 4, 8, 16)
ROPE_BASE = 10000.0
ADAM_LR, ADAM_B1, ADAM_B2, ADAM_EPS, ADAM_WD, ADAM_STEP = 0.001, 0.9, 0.999, 1e-08, 0.01, 10
LANES = 128
PACK_W = 1024
PACK_ROWS = 64
VMEM_BIG = 56 << 20


def _cp(vmem=None):
    return pltpu.CompilerParams(vmem_limit_bytes=vmem) if vmem else pltpu.CompilerParams()


def _nt(a, b):
    return lax.dot_general(a, b, (((1,), (1,)), ((), ())), preferred_element_type=F32)


def _tn(a, b):
    return lax.dot_general(a, b, (((0,), (0,)), ((), ())), preferred_element_type=F32)


def _nn(a, b):
    return jnp.dot(a, b, preferred_element_type=F32)


def _bf(x):
    return x.astype(BF16)


def _my_index():
    return 4 * lax.axis_index("x") + 2 * lax.axis_index("y") + lax.axis_index("c")


def _xchg_shapes(inputs, specs):
    out_shapes = []
    for kind, parts, r in specs:
        ii, li = parts[0]
        shp = tuple(inputs[ii].shape if li is None else inputs[ii].shape[1:])
        piece = shp if kind == "gather" else ((r,) + shp[1:] if kind == "rows" else (shp[0], r) + shp[2:])
        out_shapes.append(jax.ShapeDtypeStruct((len(parts), N_DEV) + piece, inputs[ii].dtype))
    return out_shapes, sum(len(parts) for _, parts, _ in specs)


def _xchg_sems(n_parts):
    return [pltpu.SemaphoreType.DMA((n_parts * (N_DEV - 1),)), pltpu.SemaphoreType.DMA((n_parts * (N_DEV - 1),)),
            pltpu.SemaphoreType.DMA((n_parts,))]


def _xchg_copies(specs, in_refs, out_refs, send_sems, recv_sems, loc_sems):
    x, y, c = lax.axis_index("x"), lax.axis_index("y"), lax.axis_index("c")
    me = 4 * x + 2 * y + c

    def piece(ref, kind, r, p):
        if kind == "gather":
            return ref
        if kind == "rows":
            return ref.at[pl.ds(p * r, r)]
        return ref.at[:, pl.ds(p * r, r)]

    copies, q = [], 0
    for si, (kind, parts, r) in enumerate(specs):
        for pi, (ii, li) in enumerate(parts):
            src = in_refs[ii] if li is None else in_refs[ii].at[li]
            dst = out_refs[si].at[pi, me]
            copies.append(pltpu.make_async_copy(piece(src, kind, r, me), dst, loc_sems.at[q]))
            for k in range(1, N_DEV):
                px = 1 - x if k & 4 else x
                py = 1 - y if k & 2 else y
                pc = 1 - c if k & 1 else c
                copies.append(pltpu.make_async_remote_copy(
                    src_ref=piece(src, kind, r, 4 * px + 2 * py + pc), dst_ref=dst,
                    send_sem=send_sems.at[q * (N_DEV - 1) + k - 1], recv_sem=recv_sems.at[q * (N_DEV - 1) + k - 1],
                    device_id=(px, py, pc), device_id_type=pl.DeviceIdType.MESH))
            q += 1
    return copies


def _exchange(inputs, specs, name):
    out_shapes, n_parts = _xchg_shapes(inputs, specs)
    ni, ns = len(inputs), len(specs)

    def body(*refs):
        copies = _xchg_copies(specs, refs[:ni], refs[ni:ni + ns], *refs[ni + ns:])
        for cp in copies:
            cp.start()
        for cp in copies:
            cp.wait()

    return pl.pallas_call(
        body, name=name, out_shape=tuple(out_shapes),
        in_specs=[pl.BlockSpec(memory_space=pl.ANY)] * ni,
        out_specs=tuple(pl.BlockSpec(memory_space=pl.ANY) for _ in specs),
        scratch_shapes=_xchg_sems(n_parts),
        compiler_params=pltpu.CompilerParams(has_side_effects=True),
    )(*inputs)


def _call(body, *, name, grid, in_specs, out_specs, out_shape, operands, vmem=None, scratch=(), xchg=None):
    out_shape, out_specs = tuple(out_shape), list(out_specs)
    if xchg is None:
        res = pl.pallas_call(body, name=name, grid=grid, in_specs=in_specs, out_specs=out_specs, out_shape=out_shape,
                             scratch_shapes=list(scratch), compiler_params=_cp(vmem))(*operands)
        return list(res), None
    xin, xspecs = xchg
    xshapes, n_parts = _xchg_shapes(xin, xspecs)
    ni, no, nxi, nxo, nsc = len(operands), len(out_shape), len(xin), len(xspecs), len(scratch)
    anyspec = pl.BlockSpec(memory_space=pl.ANY)

    def hosted(*refs):
        ins, xins = refs[:ni], refs[ni:ni + nxi]
        outs, xouts = refs[ni + nxi:ni + nxi + no], refs[ni + nxi + no:ni + nxi + no + nxo]
        scr, sems = refs[ni + nxi + no + nxo:ni + nxi + no + nxo + nsc], refs[ni + nxi + no + nxo + nsc:]
        copies = _xchg_copies(xspecs, xins, xouts, *sems)
        first, last = True, True
        for ax, n in enumerate(grid):
            first = first & (pl.program_id(ax) == 0)
            last = last & (pl.program_id(ax) == n - 1)

        @pl.when(first)
        def _():
            for cp in copies:
                cp.start()
        body(*ins, *outs, *scr)

        @pl.when(last)
        def _():
            for cp in copies:
                cp.wait()

    res = pl.pallas_call(
        hosted, name=name, grid=grid, in_specs=list(in_specs) + [anyspec] * nxi,
        out_specs=out_specs + [anyspec] * nxo, out_shape=out_shape + tuple(xshapes),
        scratch_shapes=list(scratch) + _xchg_sems(n_parts), compiler_params=_cp(vmem))(*operands, *xin)
    return list(res[:no]), list(res[no:])


def _adaln_fwd(craw, ada_w, ada_b):
    nl, d, n = ada_w.shape
    r = craw.shape[0]

    def body(c_ref, w_ref, b_ref, o_ref):
        s = jax.nn.silu(c_ref[...])
        o_ref[...] = _nn(_bf(s), _bf(w_ref[...])) + b_ref[...]

    return pl.pallas_call(
        body, name="adaln_fwd", grid=(nl,),
        in_specs=[pl.BlockSpec((r, d), lambda i: (0, 0)), pl.BlockSpec((None, d, n), lambda i: (i, 0, 0)),
                  pl.BlockSpec((None, 1, n), lambda i: (i, 0, 0))],
        out_specs=pl.BlockSpec((None, r, n), lambda i: (i, 0, 0)),
        out_shape=jax.ShapeDtypeStruct((nl, r, n), F32),
    )(craw, ada_w, ada_b)


def _adaln_bwd(craw, dm, ada_w):
    nl, d, n = ada_w.shape
    r = craw.shape[0]

    def body(c_ref, dm_ref, w_ref, gw_ref, dc_ref, acc_ref):
        i = pl.program_id(0)
        s, vj = jax.vjp(jax.nn.silu, c_ref[...])
        dmb = _bf(dm_ref[...])
        gw_ref[...] = _tn(_bf(s), dmb)

        @pl.when(i == 0)
        def _():
            acc_ref[...] = jnp.zeros_like(acc_ref)
        acc_ref[...] += _nt(dmb, _bf(w_ref[...]))

        @pl.when(i == nl - 1)
        def _():
            dc_ref[...] = vj(acc_ref[...])[0]

    return pl.pallas_call(
        body, name="adaln_bwd", grid=(nl,),
        in_specs=[pl.BlockSpec((r, d), lambda i: (0, 0)), pl.BlockSpec((None, r, n), lambda i: (i, 0, 0)),
                  pl.BlockSpec((None, d, n), lambda i: (i, 0, 0))],
        out_specs=[pl.BlockSpec((None, d, n), lambda i: (i, 0, 0)), pl.BlockSpec((r, d), lambda i: (0, 0))],
        out_shape=(jax.ShapeDtypeStruct((nl, d, n), F32), jax.ShapeDtypeStruct((r, d), F32)),
        scratch_shapes=[pltpu.VMEM((r, d), F32)],
    )(craw, dm, ada_w)


def _norm_mod(h, mod, g):
    ms = jnp.mean(h * h, axis=-1, keepdims=True)
    y = h * lax.rsqrt(ms + EPS) * g
    return y * (1.0 + mod[1:2]) + mod[0:1]


def _mod_spec(i, nct, d):
    return pl.BlockSpec((None, None, 3, d), lambda b, j: (i, 2 * b + jnp.where(j >= nct, 1, 0), 0, 0))


def _dmod_spec(nct, d):
    return pl.BlockSpec((None, 3, d), lambda b, j: (2 * b + jnp.where(j >= nct, 1, 0), 0, 0))


def _layer_spec(li, *shape):
    return pl.BlockSpec((None,) + tuple(shape), lambda b, j: (li,) + (0,) * len(shape))


def _tile(width, col_blk=0):
    return pl.BlockSpec((None, ROWT, width), lambda b, j: (b, j, col_blk))


def _fused_in_fwd(hs, mods, g, wt, i, jl, nct, name, xchg=None, rope=None):
    bsz, t, d = hs.shape
    n = wt.shape[1]

    def body(h_ref, mod_ref, g_ref, w_ref, *rest):
        a = _norm_mod(h_ref[...], mod_ref[...], g_ref[...])
        z = _nt(_bf(a), w_ref[...])
        if rope is None:
            rest[0][...] = z
        else:
            cos_ref, sin_ref, z_ref, qp_ref, kvp_ref = rest
            z_ref[...] = z
            qp_ref[...], kvp_ref[...] = _rope_pad(z[:, :512], z[:, 512:768], cos_ref[...], sin_ref[...])

    table = pl.BlockSpec((ROWT, LANES), lambda b, j: (j, 0))
    return _call(
        body, name=name, grid=(bsz, t // ROWT),
        in_specs=[_tile(d), _mod_spec(i, nct, d), _layer_spec(i, 1, d), _layer_spec(jl, n, d)] + ([table, table] if rope else []),
        out_specs=[_tile(n)] + ([_tile(1024), _tile(512)] if rope else []),
        out_shape=[jax.ShapeDtypeStruct((bsz, t, n), F32)]
                  + ([jax.ShapeDtypeStruct((bsz, t, 1024), BF16), jax.ShapeDtypeStruct((bsz, t, 512), BF16)] if rope else []),
        vmem=VMEM_BIG, xchg=xchg, operands=[hs, mods, g, wt] + (list(rope) if rope else []))


def _fused_in_bwd(hs, mods, g, wt, dh_out, pieces, small, assemble, i, jl, nct, name, xchg=None, latent_only=False):
    bsz, t, d = hs.shape
    n = wt.shape[1]
    npc, nsm = len(pieces), int(small is not None)

    def body(*refs):
        h_ref, mod_ref, g_ref, w_ref, dho_ref = refs[:5]
        p_refs = refs[5:5 + npc]
        s_val = refs[5 + npc][...] if nsm else None
        dhi_ref, dmod_ref, dg_ref, dw_ref = refs[5 + npc + nsm:]
        b, j = pl.program_id(0), pl.program_id(1)
        a, vj = jax.vjp(_norm_mod, h_ref[...], mod_ref[...], g_ref[...])
        dz = jnp.concatenate([_bf(p) for p in assemble([r[...] for r in p_refs], s_val)], axis=1)
        dh, dmod, dg = vj(_nn(dz, w_ref[...]))
        dhi_ref[...] = dho_ref[...] + dh

        @pl.when((j == 0) | (j == nct))
        def _():
            dmod_ref[...] = jnp.zeros_like(dmod_ref)

        @pl.when((b == 0) & (j == 0))
        def _():
            dg_ref[...] = jnp.zeros_like(dg_ref)
            dw_ref[...] = jnp.zeros_like(dw_ref)
        dmod_ref[...] += dmod
        dg_ref[...] += dg
        dw_ref[...] += _tn(dz, _bf(a))

    full = lambda r, c: pl.BlockSpec((r, c), lambda b, j: (0, 0))
    dh_rows = t - nct * ROWT if latent_only else t
    dh_spec = pl.BlockSpec((None, ROWT, d), lambda b, j: (b, jnp.maximum(j - nct, 0), 0)) if latent_only else _tile(d)
    return _call(
        body, name=name, grid=(bsz, t // ROWT),
        in_specs=[_tile(d), _mod_spec(i, nct, d), _layer_spec(i, 1, d), _layer_spec(jl, n, d), _tile(d)]
                 + [_tile(wd, cb) for _, wd, cb in pieces]
                 + ([_layer_spec(small[1], 1, small[0].shape[2])] if nsm else []),
        out_specs=[dh_spec, _dmod_spec(nct, d), full(1, d), full(n, d)],
        out_shape=(jax.ShapeDtypeStruct((bsz, dh_rows, d), F32), jax.ShapeDtypeStruct((2 * bsz, 3, d), F32),
                   jax.ShapeDtypeStruct((1, d), F32), jax.ShapeDtypeStruct((n, d), F32)),
        vmem=VMEM_BIG, xchg=xchg,
        operands=[hs, mods, g, wt, dh_out, *[p for p, _, _ in pieces], *([small[0]] if nsm else [])])


def _rope_tables(lc, l):
    tpos = np.arange(l)
    row = (tpos // GRID_W).astype(np.float32)
    col = (tpos % GRID_W).astype(np.float32)
    nf = 16
    inv = (np.float32(ROPE_BASE) ** (-np.arange(nf, dtype=np.float32) / np.float32(nf))).astype(np.float32)
    ang_r = row[:, None] * inv[None]
    ang_c = col[:, None] * inv[None]
    cos64 = np.concatenate([np.cos(ang_r), np.cos(ang_r), np.cos(ang_c), np.cos(ang_c)], axis=1)
    sin64 = np.concatenate([-np.sin(ang_r), np.sin(ang_r), -np.sin(ang_c), np.sin(ang_c)], axis=1)
    cos = np.concatenate([np.ones((lc, 128), np.float32), np.tile(cos64, (1, 2)).astype(np.float32)], axis=0)
    sin = np.concatenate([np.zeros((lc, 128), np.float32), np.tile(sin64, (1, 2)).astype(np.float32)], axis=0)
    return jnp.asarray(cos), jnp.asarray(sin)


def _rot(x, cos, sin):
    lane = lax.broadcasted_iota(jnp.int32, x.shape, 1)
    first = jnp.bitwise_and(jnp.right_shift(lane, 4), 1) == 0
    partner = jnp.where(first, pltpu.roll(x, LANES - 16, 1), pltpu.roll(x, 16, 1))
    return x * cos + partner * sin


def _split_heads(x):
    low = lax.broadcasted_iota(jnp.int32, x.shape, 1) < 64
    return jnp.where(low, x, 0.0), jnp.where(low, pltpu.roll(x, 64, 1), 0.0)


def _merge_heads(a, b):
    return a + pltpu.roll(b, 64, 1)


def _rope_pad(q, kv, c, s):
    outs = []
    for sl in range(4):
        outs += list(_split_heads(_rot(q[:, sl * LANES:(sl + 1) * LANES], c, s) * 0.125))
    k0, k1 = _split_heads(_rot(kv[:, 0:LANES], c, s))
    v0, v1 = _split_heads(kv[:, LANES:2 * LANES])
    return _bf(jnp.concatenate(outs, axis=1)), _bf(jnp.concatenate([k0, k1, v0, v1], axis=1))


def _rope_bwd(dq_r, dkv_acc, cos, sin, t, name):
    bsz = dq_r.shape[0]

    def body(dq_ref, acc_ref, cos_ref, sin_ref, dqo_ref, dkvo_ref):
        c, s = cos_ref[...], -sin_ref[...]
        dqo_ref[...] = jnp.concatenate(
            [_rot(dq_ref[:, sl * LANES:(sl + 1) * LANES], c, s) for sl in range(4)], axis=1)
        dk = _merge_heads(acc_ref[:, 0:LANES], acc_ref[:, LANES:2 * LANES])
        dv = _merge_heads(acc_ref[:, 2 * LANES:3 * LANES], acc_ref[:, 3 * LANES:4 * LANES])
        dkvo_ref[...] = jnp.concatenate([_rot(dk, c, s), dv], axis=1)

    return pl.pallas_call(
        body, name=name, grid=(bsz, t // ROWT),
        in_specs=[_tile(512), _tile(512),
                  pl.BlockSpec((ROWT, LANES), lambda b, j: (j, 0)), pl.BlockSpec((ROWT, LANES), lambda b, j: (j, 0))],
        out_specs=[_tile(512), _tile(256)],
        out_shape=(jax.ShapeDtypeStruct((bsz, t, 512), F32), jax.ShapeDtypeStruct((bsz, t, 256), F32)),
    )(dq_r, dkv_acc, cos, sin)


def _attn_bias(j, ncb, l):
    n = j - ncb
    r = lax.broadcasted_iota(jnp.int32, (QBLK, 3 * QBLK), 0)
    cidx = lax.broadcasted_iota(jnp.int32, (QBLK, 3 * QBLK), 1)
    kpos = (n - 1) * QBLK + cidx
    valid = (jnp.abs(r - cidx + QBLK) <= WINDOW) & (kpos >= 0) & (kpos < l) & (j >= ncb)
    return jnp.where(valid, 0.0, NEG_INF).astype(F32)


def _attn_bias4(j, ncb, l):
    b1 = _attn_bias(j, ncb, l)
    return jnp.concatenate([b1, b1, b1, b1], axis=0)


def _sink_rows(sink_ref, jl, hk):
    head = jnp.right_shift(lax.broadcasted_iota(jnp.int32, (4 * QBLK, 1), 0), 7)
    sk = jnp.zeros((4 * QBLK, 1), F32)
    for g in range(4):
        sk = jnp.where(head == g, sink_ref[jl, 4 * hk + g], sk)
    return sk


def _kv_specs(nb, lc):
    return [pl.BlockSpec((None, QBLK, 512), lambda b, j: (b, jnp.maximum(j - 1, 0), 0)),
            pl.BlockSpec((None, QBLK, 512), lambda b, j: (b, j, 0)),
            pl.BlockSpec((None, QBLK, 512), lambda b, j: (b, jnp.minimum(j + 1, nb - 1), 0)),
            pl.BlockSpec((None, lc, 512), lambda b, j: (b, 0, 0))]


def _lane_pick(x, h):
    lane = lax.broadcasted_iota(jnp.int32, x.shape, 1)
    return jnp.sum(jnp.where(lane == h, x, 0.0), axis=1, keepdims=True)


def _attn_fwd(qp, kvp, sinks, jl, lc, name, xchg=None):
    bsz, t, _ = qp.shape
    nb, ncb, l = t // QBLK, lc // QBLK, t - lc

    def body(sink_ref, q_ref, kp_ref, kc_ref, kn_ref, kx_ref, o_ref, lse_ref):
        j = pl.program_id(1)
        kloc = jnp.concatenate([kp_ref[...], kc_ref[...], kn_ref[...]], axis=0)
        kctx = kx_ref[...]
        bias = _attn_bias4(j, ncb, l)
        lane = lax.broadcasted_iota(jnp.int32, (QBLK, LANES), 1)
        lse_all = jnp.zeros((QBLK, LANES), F32)
        outs = []
        for hk in range(2):
            ks, vs = slice(hk * LANES, (hk + 1) * LANES), slice((2 + hk) * LANES, (3 + hk) * LANES)
            q4 = jnp.concatenate([q_ref[:, (4 * hk + g) * LANES:(4 * hk + g + 1) * LANES] for g in range(4)], axis=0)
            s_loc = _nt(q4, kloc[:, ks]) + bias
            s_ctx = _nt(q4, kctx[:, ks])
            sk = _sink_rows(sink_ref, jl, hk)
            m = jnp.maximum(jnp.maximum(jnp.max(s_loc, axis=1, keepdims=True), jnp.max(s_ctx, axis=1, keepdims=True)), sk)
            p_loc, p_ctx = jnp.exp(s_loc - m), jnp.exp(s_ctx - m)
            den = jnp.sum(p_loc, axis=1, keepdims=True) + jnp.sum(p_ctx, axis=1, keepdims=True) + jnp.exp(sk - m)
            o4 = (_nn(_bf(p_loc), kloc[:, vs]) + _nn(_bf(p_ctx), kctx[:, vs])) / den
            lse4 = m + jnp.log(den)
            for g in range(4):
                outs.append(o4[g * QBLK:(g + 1) * QBLK])
                lse_all = lse_all + jnp.where(lane == 4 * hk + g, lse4[g * QBLK:(g + 1) * QBLK], 0.0)
        o_ref[...] = jnp.concatenate([_merge_heads(outs[2 * i], outs[2 * i + 1]) for i in range(4)], axis=1)
        lse_ref[...] = lse_all

    return _call(
        body, name=name, grid=(bsz, nb),
        in_specs=[pl.BlockSpec(memory_space=pltpu.SMEM),
                  pl.BlockSpec((None, QBLK, 1024), lambda b, j: (b, j, 0))] + _kv_specs(nb, lc),
        out_specs=[pl.BlockSpec((None, QBLK, 512), lambda b, j: (b, j, 0)),
                   pl.BlockSpec((None, QBLK, LANES), lambda b, j: (b, j, 0))],
        out_shape=(jax.ShapeDtypeStruct((bsz, t, 512), F32), jax.ShapeDtypeStruct((bsz, t, LANES), F32)),
        xchg=xchg, operands=[sinks, qp, kvp, kvp, kvp, kvp])


def _attn_bwd(qp, kvp, sinks, jl, o, lse, do, lc, name, xchg=None):
    bsz, t, _ = qp.shape
    nb, ncb, l = t // QBLK, lc // QBLK, t - lc
    nk = 3 * QBLK

    def body(sink_ref, q_ref, kp_ref, kc_ref, kn_ref, kx_ref, o_ref, lse_ref, do_ref, dq_ref, acc_ref, ds_ref):
        j = pl.program_id(1)

        @pl.when(j == 0)
        def _():
            acc_ref[...] = jnp.zeros_like(acc_ref)
            ds_ref[...] = jnp.zeros_like(ds_ref)
        kloc = jnp.concatenate([kp_ref[...], kc_ref[...], kn_ref[...]], axis=0)
        kctx = kx_ref[...]
        bias = _attn_bias4(j, ncb, l)
        lse = lse_ref[...]
        row8 = lax.broadcasted_iota(jnp.int32, (8, LANES), 0)
        dsink = jnp.zeros((8, LANES), F32)
        dqs = []
        d_loc, d_ctx = [None] * 4, [None] * 4
        for hk in range(2):
            ks, vs = slice(hk * LANES, (hk + 1) * LANES), slice((2 + hk) * LANES, (3 + hk) * LANES)
            q4 = jnp.concatenate([q_ref[:, (4 * hk + g) * LANES:(4 * hk + g + 1) * LANES] for g in range(4)], axis=0)
            do4 = jnp.concatenate([x_ for s2 in (2 * hk, 2 * hk + 1)
                                   for x_ in _split_heads(do_ref[:, s2 * LANES:(s2 + 1) * LANES])], axis=0)
            o4 = jnp.concatenate([x_ for s2 in (2 * hk, 2 * hk + 1)
                                  for x_ in _split_heads(o_ref[:, s2 * LANES:(s2 + 1) * LANES])], axis=0)
            delta = jnp.sum(do4 * o4, axis=1, keepdims=True)
            lse4 = jnp.concatenate([_lane_pick(lse, 4 * hk + g) for g in range(4)], axis=0)
            p_loc = jnp.exp(_nt(q4, kloc[:, ks]) + bias - lse4)
            p_ctx = jnp.exp(_nt(q4, kctx[:, ks]) - lse4)
            dob = _bf(do4)
            ds_loc = _bf(p_loc * (_nt(dob, kloc[:, vs]) - delta))
            ds_ctx = _bf(p_ctx * (_nt(dob, kctx[:, vs]) - delta))
            dq4 = (_nn(ds_loc, kloc[:, ks]) + _nn(ds_ctx, kctx[:, ks])) * 0.125
            d_loc[hk], d_ctx[hk] = _tn(ds_loc, q4), _tn(ds_ctx, q4)
            d_loc[2 + hk], d_ctx[2 + hk] = _tn(_bf(p_loc), dob), _tn(_bf(p_ctx), dob)
            dsk = -jnp.exp(_sink_rows(sink_ref, jl, hk) - lse4) * delta
            for g in range(4):
                dqs.append(dq4[g * QBLK:(g + 1) * QBLK])
                dsink = dsink + jnp.where(row8 == 4 * hk + g, jnp.sum(dsk[g * QBLK:(g + 1) * QBLK]), 0.0)
        dq_ref[...] = jnp.concatenate([_merge_heads(dqs[2 * i], dqs[2 * i + 1]) for i in range(4)], axis=1)
        ds_ref[...] += dsink
        acc_ref[pl.ds(0, lc), :] += jnp.concatenate(d_ctx, axis=1)

        @pl.when(j >= ncb)
        def _():
            st = pl.multiple_of((j - 1) * QBLK, QBLK)
            acc_ref[pl.ds(st, nk), :] += jnp.concatenate(d_loc, axis=1)

    blk = lambda wd: pl.BlockSpec((None, QBLK, wd), lambda b, j: (b, j, 0))
    return _call(
        body, name=name, grid=(bsz, nb),
        in_specs=[pl.BlockSpec(memory_space=pltpu.SMEM), blk(1024)] + _kv_specs(nb, lc)
                 + [blk(512), blk(LANES), blk(512)],
        out_specs=[blk(512), pl.BlockSpec((None, t + QBLK, 512), lambda b, j: (b, 0, 0)),
                   pl.BlockSpec((None, 8, LANES), lambda b, j: (b, 0, 0))],
        out_shape=(jax.ShapeDtypeStruct((bsz, t, 512), F32), jax.ShapeDtypeStruct((bsz, t + QBLK, 512), F32),
                   jax.ShapeDtypeStruct((bsz, 8, LANES), F32)),
        vmem=VMEM_BIG, xchg=xchg, operands=[sinks, qp, kvp, kvp, kvp, kvp, o, lse, do])


def _s5_operators(a_re, a_im, log_dt, b_re, b_im, c_re, c_im):
    hi = lax.Precision.HIGHEST
    dt = jnp.exp(log_dt)[..., None]
    tau = jnp.arange(CHUNK + 1, dtype=F32)[:, None, None, None]
    mag = jnp.exp(tau * (a_re * dt))
    pwr, pwi = mag * jnp.cos(tau * (a_im * dt)), mag * jnp.sin(tau * (a_im * dt))
    ar, ai = pwr[1], pwi[1]
    den = a_re * a_re + a_im * a_im
    fr = ((ar - 1.0) * a_re + ai * a_im) / den
    fi = (ai * a_re - (ar - 1.0) * a_im) / den
    bbr = fr[..., None] * b_re - fi[..., None] * b_im
    bbi = fr[..., None] * b_im + fi[..., None] * b_re
    wr = pwr[:CHUNK, ..., None] * bbr - pwi[:CHUNK, ..., None] * bbi
    wi = pwr[:CHUNK, ..., None] * bbi + pwi[:CHUNK, ..., None] * bbr
    kern = (jnp.einsum("dgcp,tdgpk->dgtck", c_re, wr, precision=hi)
            - jnp.einsum("dgcp,tdgpk->dgtck", c_im, wi, precision=hi))
    g = a_re.shape[1]
    nrow = CHUNK * SSM_GROUP
    qf = jnp.transpose(kern[0], (0, 3, 1, 2)).reshape(g, SSM_GROUP, nrow)
    qb = jnp.transpose(kern[1][:, ::-1], (0, 3, 1, 2)).reshape(g, SSM_GROUP, nrow)
    to_rows = lambda w: jnp.transpose(w, (1, 0, 3, 2)).reshape(g, nrow, SSM_STATE)
    bt = jnp.concatenate([to_rows(wr[::-1, 0]), to_rows(wi[::-1, 0]), to_rows(wr[:, 1]), to_rows(wi[:, 1])], axis=-1)

    def out_map(d, pw_r, pw_i):
        w2r = c_re[d][None] * pw_r[:, :, None, :] - c_im[d][None] * pw_i[:, :, None, :]
        w2i = c_re[d][None] * pw_i[:, :, None, :] + c_im[d][None] * pw_r[:, :, None, :]
        cols = lambda w: jnp.transpose(w, (1, 3, 0, 2)).reshape(g, SSM_STATE, nrow)
        return [cols(w2r), -cols(w2i)]

    ct = jnp.concatenate(out_map(0, pwr[1:, 0], pwi[1:, 0]) + out_map(1, pwr[1:, 1][::-1], pwi[1:, 1][::-1]), axis=-2)
    return qf, qb, bt, ct, pwr[CHUNK], pwi[CHUNK]


def _scan_powers(a16r, a16i, n_chunks):
    prs, pis = [], []
    r, i = a16r, a16i
    s = 1
    while s < n_chunks:
        prs.append(jnp.concatenate([r, r], axis=-1))
        pis.append(jnp.concatenate([-i, i], axis=-1))
        r, i = r * r - i * i, 2.0 * r * i
        s *= 2
    pad = 16 - len(prs)
    z = jnp.zeros_like(prs[0])
    return jnp.stack(prs + [z] * pad, axis=2), jnp.stack(pis + [z] * pad, axis=2)


def _low_half():
    return lax.broadcasted_iota(jnp.int32, (1, LANES), 1) < 64


def _to_pair(sa, sb):
    low = _low_half()
    return jnp.where(low, sa, pltpu.roll(sb, 64, 1)), jnp.where(low, pltpu.roll(sa, 64, 1), sb)


def _from_pair(hr, hi):
    low = _low_half()
    return jnp.where(low, hr, pltpu.roll(hi, 64, 1)), jnp.where(low, pltpu.roll(hr, 64, 1), hi)


def _pair_scan(sr, si, pr, pi, reverse, conj):
    n = sr.shape[0]
    row = lax.broadcasted_iota(jnp.int32, sr.shape, 0)

    def shift(x, k):
        if reverse:
            return jnp.where(row < n - k, pltpu.roll(x, n - k, 0), 0.0)
        return jnp.where(row >= k, pltpu.roll(x, k, 0), 0.0)

    hr, hi = shift(sr, 1), shift(si, 1)
    k, j = 1, 0
    while k < n:
        tr, ti = shift(hr, k), shift(hi, k)
        ar = pr[j:j + 1, :]
        ai = -pi[j:j + 1, :] if conj else pi[j:j + 1, :]
        hr, hi = hr + ar * tr - ai * ti, hi + ar * ti + ai * tr
        k, j = 2 * k, j + 1
    return hr, hi


def _transpose8(a):
    a = list(a)
    blk = jnp.right_shift(lax.broadcasted_iota(jnp.int32, (1, LANES), 1), 4)
    for dd in (4, 2, 1):
        upper = jnp.bitwise_and(blk, dd) != 0
        for i in range(8):
            if i & dd:
                continue
            lo, hi = a[i], a[i + dd]
            a[i] = jnp.where(upper, pltpu.roll(hi, dd * SSM_GROUP, 1), lo)
            a[i + dd] = jnp.where(upper, hi, pltpu.roll(lo, LANES - dd * SSM_GROUP, 1))
    return a


def _blocks_from_rows(xs):
    t0, t1 = _transpose8(xs[:8]), _transpose8(xs[8:])
    return [jnp.concatenate([t0[g], t1[g]], axis=1) for g in range(8)]


def _rows_from_blocks(ys):
    return _transpose8([y[:, :LANES] for y in ys]) + _transpose8([y[:, LANES:] for y in ys])


def _pair_states(sa, sb, pr_ref, pi_ref, ga, gb, ncc, adjoint):
    n = sa.shape[0]
    low = _low_half()
    out_a, out_b, pairs = [], [], []
    for dirn in range(2):
        xr, xi = _to_pair(sa[:, dirn * LANES:(dirn + 1) * LANES], sb[:, dirn * LANES:(dirn + 1) * LANES])
        pr = jnp.where(low, pr_ref[dirn, ga], pr_ref[dirn, gb])
        pi = jnp.where(low, -pi_ref[dirn, ga], pi_ref[dirn, gb])
        if dirn == 0:
            hr, hi = _pair_scan(xr, xi, pr, pi, adjoint, adjoint)
        else:
            hr, hi = _pair_scan(pltpu.roll(xr, n - ncc, 0), pltpu.roll(xi, n - ncc, 0), pr, pi, not adjoint, adjoint)
            hr, hi = pltpu.roll(hr, ncc, 0), pltpu.roll(hi, ncc, 0)
        pairs.append((hr, hi))
        ha, hb = _from_pair(hr, hi)
        out_a.append(ha)
        out_b.append(hb)
    return (jnp.concatenate(out_a, axis=1), jnp.concatenate(out_b, axis=1)), pairs


def _shift_lanes(x, n, left):
    if n == 0:
        return x
    lo, hi = x[:, :LANES], x[:, LANES:]
    lane = lax.broadcasted_iota(jnp.int32, lo.shape, 1)
    zero = jnp.zeros_like(lo)
    m = n % LANES
    if not left:
        if n < LANES:
            r_lo, r_hi = pltpu.roll(lo, n, 1), pltpu.roll(hi, n, 1)
            return jnp.concatenate([jnp.where(lane >= n, r_lo, 0.0), jnp.where(lane >= n, r_hi, r_lo)], axis=1)
        r_lo = pltpu.roll(lo, m, 1) if m else lo
        return jnp.concatenate([zero, jnp.where(lane >= m, r_lo, 0.0)], axis=1)
    if n < LANES:
        r_lo, r_hi = pltpu.roll(lo, LANES - n, 1), pltpu.roll(hi, LANES - n, 1)
        return jnp.concatenate([jnp.where(lane < LANES - n, r_lo, r_hi), jnp.where(lane < LANES - n, r_hi, 0.0)], axis=1)
    r_hi = pltpu.roll(hi, LANES - m, 1) if m else hi
    return jnp.concatenate([jnp.where(lane < LANES - m, r_hi, 0.0), zero], axis=1)


def _toeplitz(qf, qb):
    return jnp.concatenate([_shift_lanes(qf, SSM_GROUP * s, False) + _shift_lanes(qb, SSM_GROUP * (CHUNK - 1 - s), True)
                            for s in range(CHUNK)], axis=0)


def _toeplitz_t(dmt):
    dqf, dqb = 0.0, 0.0
    for s in range(CHUNK):
        rows = dmt[s * SSM_GROUP:(s + 1) * SSM_GROUP]
        dqf = dqf + _shift_lanes(rows, SSM_GROUP * s, True)
        dqb = dqb + _shift_lanes(rows, SSM_GROUP * (CHUNK - 1 - s), False)
    return dqf, dqb


def _s5_op_specs(idx, jl):
    q = pl.BlockSpec((None, 8, SSM_GROUP, 256), lambda a, b: (jl, idx(a, b), 0, 0))
    op = pl.BlockSpec((None, 8, 256, 256), lambda a, b: (jl, idx(a, b), 0, 0))
    pw = pl.BlockSpec((None, 2, 8, 16, LANES), lambda a, b: (jl, 0, idx(a, b), 0, 0))
    return [q, q, op, op, pw, pw]


def _s5_fwd(z, col0, s5_ops, jl, lc, name, xchg=None):
    bsz, t, _ = z.shape
    nc, ncc = t // CHUNK, lc // CHUNK

    def body(u_ref, qf_ref, qb_ref, bt_ref, ct_ref, pr_ref, pi_ref, y_ref):
        us = [_bf(u) for u in _blocks_from_rows([u_ref[pl.ds(tok, nc, stride=CHUNK), :] for tok in range(CHUNK)])]
        ys = []
        for ga in range(0, 8, 2):
            gb = ga + 1
            hs, _ = _pair_states(_nn(us[ga], bt_ref[ga]), _nn(us[gb], bt_ref[gb]), pr_ref, pi_ref, ga, gb, ncc, False)
            for g, h in zip((ga, gb), hs):
                ys.append(_nn(us[g], _bf(_toeplitz(qf_ref[g], qb_ref[g]))) + _nn(_bf(h), ct_ref[g]))
        for tok, rows in enumerate(_rows_from_blocks(ys)):
            y_ref[pl.ds(tok, nc, stride=CHUNK), :] = rows

    return _call(
        body, name=name, grid=(bsz, 4),
        in_specs=[pl.BlockSpec((None, t, LANES), lambda b, s: (b, 0, col0 + s))] + _s5_op_specs(lambda b, s: s, jl),
        out_specs=[pl.BlockSpec((None, t, LANES), lambda b, s: (b, 0, s))],
        out_shape=[jax.ShapeDtypeStruct((bsz, t, 512), F32)], vmem=VMEM_BIG, xchg=xchg,
        operands=[z, *s5_ops])


def _s5_bwd(z, col0, dy, s5_ops, jl, lc, name, xchg=None):
    bsz, t, _ = z.shape
    nc, ncc = t // CHUNK, lc // CHUNK

    def body(u_ref, dy_ref, qf_ref, qb_ref, bt_ref, ct_ref, pr_ref, pi_ref,
             du_ref, dqf_ref, dqb_ref, dbt_ref, dct_ref, da_ref):
        b = pl.program_id(1)

        @pl.when(b == 0)
        def _():
            dqf_ref[...] = jnp.zeros_like(dqf_ref)
            dqb_ref[...] = jnp.zeros_like(dqb_ref)
            dbt_ref[...] = jnp.zeros_like(dbt_ref)
            dct_ref[...] = jnp.zeros_like(dct_ref)
            da_ref[...] = jnp.zeros_like(da_ref)
        us = [_bf(u) for u in _blocks_from_rows([u_ref[pl.ds(tok, nc, stride=CHUNK), :] for tok in range(CHUNK)])]
        dys = [_bf(u) for u in _blocks_from_rows([dy_ref[pl.ds(tok, nc, stride=CHUNK), :] for tok in range(CHUNK)])]
        row8 = lax.broadcasted_iota(jnp.int32, (8, LANES), 0)
        dus = []
        for ga in range(0, 8, 2):
            gb = ga + 1
            hs, hp = _pair_states(_nn(us[ga], bt_ref[ga]), _nn(us[gb], bt_ref[gb]), pr_ref, pi_ref, ga, gb, ncc, False)
            gs, gp = _pair_states(_nt(dys[ga], ct_ref[ga]), _nt(dys[gb], ct_ref[gb]), pr_ref, pi_ref, ga, gb, ncc, True)
            for g, h, gg in zip((ga, gb), hs, gs):
                hcat, gcat = _bf(h), _bf(gg)
                dus.append(_nt(dys[g], _bf(_toeplitz(qf_ref[g], qb_ref[g]))) + _nt(gcat, bt_ref[g]))
                dqf, dqb = _toeplitz_t(_tn(us[g], dys[g]))
                dqf_ref[g] += dqf
                dqb_ref[g] += dqb
                dct_ref[g] += _tn(hcat, dys[g])
                dbt_ref[g] += _tn(us[g], gcat)
            da = jnp.zeros((8, LANES), F32)
            for dirn in range(2):
                (hr, hi), (gr, gi) = hp[dirn], gp[dirn]
                da = da + jnp.where(row8 == 2 * dirn, jnp.sum(gr * hr + gi * hi, axis=0, keepdims=True), 0.0)
                da = da + jnp.where(row8 == 2 * dirn + 1, jnp.sum(gi * hr - gr * hi, axis=0, keepdims=True), 0.0)
            da_ref[ga // 2] += da
        for tok, rows in enumerate(_rows_from_blocks(dus)):
            du_ref[pl.ds(tok, nc, stride=CHUNK), :] = rows

    op_out = pl.BlockSpec((8, 256, 256), lambda s, b: (s, 0, 0))
    op_shape = jax.ShapeDtypeStruct((SSM_GROUPS, 256, 256), F32)
    q_out = pl.BlockSpec((8, SSM_GROUP, 256), lambda s, b: (s, 0, 0))
    q_shape = jax.ShapeDtypeStruct((SSM_GROUPS, SSM_GROUP, 256), F32)
    return _call(
        body, name=name, grid=(4, bsz),
        in_specs=[pl.BlockSpec((None, t, LANES), lambda s, b: (b, 0, col0 + s)),
                  pl.BlockSpec((None, t, LANES), lambda s, b: (b, 0, s))] + _s5_op_specs(lambda s, b: s, jl),
        out_specs=[pl.BlockSpec((None, t, LANES), lambda s, b: (b, 0, s)), q_out, q_out, op_out, op_out,
                   pl.BlockSpec((4, 8, LANES), lambda s, b: (s, 0, 0))],
        out_shape=(jax.ShapeDtypeStruct((bsz, t, 512), F32), q_shape, q_shape, op_shape, op_shape,
                   jax.ShapeDtypeStruct((SSM_GROUPS // 2, 8, LANES), F32)),
        vmem=VMEM_BIG, xchg=xchg, operands=[z, dy, *s5_ops])


def _glu_in(y, u, dsk):
    return jax.nn.gelu(y + u * dsk)


def _even_mix(oa, ga, zg, tg, gs):
    return jnp.concatenate([oa * jax.nn.silu(ga), zg * jax.nn.sigmoid(tg) * jax.nn.silu(gs)], axis=1)


def _even_specs(d, i, jl, nct):
    return [_tile(d), _mod_spec(i, nct, d), _tile(512), _tile(512)] + [_tile(256, cb) for cb in range(3, 9)] + [
        _layer_spec(jl, 1, 512), _layer_spec(0, 512, 512), _layer_spec(jl, 1, 512), _layer_spec(0, 1024, d)]


def _cat2(a_ref, b_ref):
    return jnp.concatenate([a_ref[...], b_ref[...]], axis=1)


def _even_out_fwd(hs, mods, oa, y, z, dsk, gw, gb, wout, i, jl, nct, name):
    bsz, t, d = hs.shape

    def body(h_ref, mod_ref, oa_ref, y_ref, ga0, ga1, u0, u1, gs0, gs1, dsk_ref, gw_ref, gb_ref, wo_ref, o_ref):
        zg = _glu_in(y_ref[...], _cat2(u0, u1), dsk_ref[...])
        tg = _nn(_bf(zg), gw_ref[...]) + gb_ref[...]
        mix = _even_mix(oa_ref[...], _cat2(ga0, ga1), zg, tg, _cat2(gs0, gs1))
        o_ref[...] = h_ref[...] + mod_ref[2:3, :] * _nn(_bf(mix), wo_ref[...])

    return pl.pallas_call(
        body, name=name, grid=(bsz, t // ROWT), in_specs=_even_specs(d, i, jl, nct),
        out_specs=_tile(d), out_shape=jax.ShapeDtypeStruct((bsz, t, d), F32), compiler_params=_cp(VMEM_BIG),
    )(hs, mods, oa, y, z, z, z, z, z, z, dsk, gw, gb, wout)


def _even_out_bwd(dh, mods, oa, y, z, dsk, gw, gb, wout, i, jl, nct, name):
    bsz, t, d = dh.shape

    def body(dh_ref, mod_ref, oa_ref, y_ref, ga0, ga1, u0, u1, gs0, gs1, dsk_ref, gw_ref, gb_ref, wo_ref,
             doa_ref, dga_ref, dy_ref, dgs_ref, dmod_ref, ddsk_ref, dgw_ref, dgb_ref, dwo_ref):
        b, j = pl.program_id(0), pl.program_id(1)
        zg, vj1 = jax.vjp(_glu_in, y_ref[...], _cat2(u0, u1), dsk_ref[...])
        zgb = _bf(zg)
        tg = _nn(zgb, gw_ref[...]) + gb_ref[...]
        mix, vj2 = jax.vjp(_even_mix, oa_ref[...], _cat2(ga0, ga1), zg, tg, _cat2(gs0, gs1))
        mixb = _bf(mix)
        dhv = dh_ref[...]
        dyo = _bf(dhv * mod_ref[2:3, :])
        yout = _nn(mixb, wo_ref[...])
        doa, dga, dzg, dtg, dgs = vj2(_nt(dyo, wo_ref[...]))
        dtb = _bf(dtg)
        dzg = dzg + _nt(dtb, gw_ref[...])
        dy, _, ddsk = vj1(dzg)
        doa_ref[...], dga_ref[...], dy_ref[...], dgs_ref[...] = doa, dga, dy, dgs

        @pl.when((j == 0) | (j == nct))
        def _():
            dmod_ref[...] = jnp.zeros_like(dmod_ref)

        @pl.when((b == 0) & (j == 0))
        def _():
            ddsk_ref[...] = jnp.zeros_like(ddsk_ref)
            dgw_ref[...] = jnp.zeros_like(dgw_ref)
            dgb_ref[...] = jnp.zeros_like(dgb_ref)
            dwo_ref[...] = jnp.zeros_like(dwo_ref)
        dmod_ref[2:3, :] += jnp.sum(dhv * yout, axis=0, keepdims=True)
        ddsk_ref[...] += ddsk
        dgw_ref[...] += _tn(zgb, dtb)
        dgb_ref[...] += jnp.sum(dtg, axis=0, keepdims=True)
        dwo_ref[...] += _tn(mixb, dyo)

    full = lambda r, c: pl.BlockSpec((r, c), lambda b, j: (0, 0))
    f512 = jax.ShapeDtypeStruct((bsz, t, 512), F32)
    return pl.pallas_call(
        body, name=name, grid=(bsz, t // ROWT), in_specs=_even_specs(d, i, jl, nct),
        out_specs=[_tile(512), _tile(512), _tile(512), _tile(512), _dmod_spec(nct, d),
                   full(1, 512), full(512, 512), full(1, 512), full(1024, d)],
        out_shape=(f512, f512, f512, f512, jax.ShapeDtypeStruct((2 * bsz, 3, d), F32),
                   jax.ShapeDtypeStruct((1, 512), F32), jax.ShapeDtypeStruct((512, 512), F32),
                   jax.ShapeDtypeStruct((1, 512), F32), jax.ShapeDtypeStruct((1024, d), F32)),
        compiler_params=_cp(VMEM_BIG),
    )(dh, mods, oa, y, z, z, z, z, z, z, dsk, gw, gb, wout)


def _pool(src, col_blk, r, lc, transpose, name):
    bsz, t, _ = src.shape
    segs = [(0, lc, 16), (lc, t - lc, 32 + lc)]
    halo = [(0, 16), (16 + lc, 16), (32 + t, 16)]
    cw = 256

    def inv_count(t0, rows, ln):
        pos = t0 + lax.broadcasted_iota(jnp.int32, (rows, cw), 0)
        cnt = jnp.minimum(pos + r + 1, ln) - jnp.maximum(pos - r, 0)
        return 1.0 / cnt.astype(F32)

    def body(x_ref, o_ref, s_ref):
        for h0, hn in halo:
            s_ref[pl.ds(h0, hn), :] = jnp.zeros((hn, cw), F32)
        for r0, ln, s0 in segs:
            step = min(512, ln)
            for c0 in range(0, ln, step):
                v = x_ref[pl.ds(r0 + c0, step), :]
                s_ref[pl.ds(s0 + c0, step), :] = v * inv_count(c0, step, ln) if transpose else v
        for r0, ln, s0 in segs:
            step = min(512, ln)
            for c0 in range(0, ln, step):
                acc = s_ref[pl.ds(s0 + c0 - r, step), :]
                for dlt in range(-r + 1, r + 1):
                    acc = acc + s_ref[pl.ds(s0 + c0 + dlt, step), :]
                ctr = x_ref[pl.ds(r0 + c0, step), :]
                o_ref[pl.ds(r0 + c0, step), :] = (acc if transpose else acc * inv_count(c0, step, ln)) - ctr

    return pl.pallas_call(
        body, name=name, grid=(bsz,),
        in_specs=[pl.BlockSpec((None, t, cw), lambda b: (b, 0, col_blk))],
        out_specs=pl.BlockSpec((None, t, cw), lambda b: (b, 0, 0)),
        out_shape=jax.ShapeDtypeStruct((bsz, t, cw), F32),
        scratch_shapes=[pltpu.VMEM((t + 48, cw), F32)],
        compiler_params=_cp(VMEM_BIG),
    )(src)


def _odd_specs(d, i, jl, nct):
    return [_tile(d), _mod_spec(i, nct, d), _tile(256), _tile(256), _tile(256), _tile(256), _tile(1024, 1),
            _layer_spec(0, 4, 256, 256), _layer_spec(jl, 1, 1024), _layer_spec(0, 1024, d)]


def _odd_mix(pm, psc, gz):
    return pm * psc * jax.nn.silu(gz)


def _odd_out_fwd(hs, mods, ps, z, pw, psc, wout, i, jl, nct, name, loss=None):
    bsz, t, d = hs.shape

    def h_new(h_ref, mod_ref, p0, p1, p2, p3, gz_ref, pw_ref, psc_ref, wo_ref):
        pm = jnp.concatenate([_nn(_bf(p[...]), pw_ref[q]) for q, p in enumerate((p0, p1, p2, p3))], axis=1)
        mix = _odd_mix(pm, psc_ref[...], gz_ref[...])
        return h_ref[...] + mod_ref[2:3, :] * _nn(_bf(mix), wo_ref[...])

    if loss is None:
        def body(*refs):
            refs[-1][...] = h_new(*refs[:-1])

        return pl.pallas_call(
            body, name=name, grid=(bsz, t // ROWT), in_specs=_odd_specs(d, i, jl, nct),
            out_specs=_tile(d), out_shape=jax.ShapeDtypeStruct((bsz, t, d), F32), compiler_params=_cp(VMEM_BIG),
        )(hs, mods, *ps, z, pw, psc, wout)

    def body(*refs):
        g_ref, t_ref, dh_ref, loss_ref, dg_ref = refs[-5:]
        b, j = pl.program_id(0), pl.program_id(1)
        y, vj = jax.vjp(_rms, h_new(*refs[:-5]), g_ref[...])
        err = jnp.where(j >= nct, y - t_ref[...], 0.0)
        dh, dg = vj(err * (1.0 / d))
        dh_ref[...] = dh

        @pl.when(j == 0)
        def _():
            loss_ref[...] = jnp.zeros_like(loss_ref)

        @pl.when((b == 0) & (j == 0))
        def _():
            dg_ref[...] = jnp.zeros_like(dg_ref)
        loss_ref[...] += 0.5 * jnp.sum(jnp.mean(err * err, axis=-1))
        dg_ref[...] += dg

    return pl.pallas_call(
        body, name=name, grid=(bsz, t // ROWT),
        in_specs=_odd_specs(d, i, jl, nct) + [pl.BlockSpec((1, d), lambda b, j: (0, 0)),
                                              pl.BlockSpec((None, ROWT, d), lambda b, j: (b, jnp.maximum(j - nct, 0), 0))],
        out_specs=[_tile(d), pl.BlockSpec((None, 8, LANES), lambda b, j: (b, 0, 0)),
                   pl.BlockSpec((1, d), lambda b, j: (0, 0))],
        out_shape=(jax.ShapeDtypeStruct((bsz, t, d), F32), jax.ShapeDtypeStruct((bsz, 8, LANES), F32),
                   jax.ShapeDtypeStruct((1, d), F32)),
        compiler_params=_cp(VMEM_BIG),
    )(hs, mods, *ps, z, pw, psc, wout, *loss)


def _odd_out_bwd(dh, mods, ps, z, pw, psc, wout, i, jl, nct, name):
    bsz, t, d = dh.shape

    def body(dh_ref, mod_ref, p0, p1, p2, p3, gz_ref, pw_ref, psc_ref, wo_ref,
             dp_ref, dgz_ref, dmod_ref, dpw_ref, dpsc_ref, dwo_ref):
        b, j = pl.program_id(0), pl.program_id(1)
        pbs = [_bf(p[...]) for p in (p0, p1, p2, p3)]
        pm = jnp.concatenate([_nn(pb, pw_ref[q]) for q, pb in enumerate(pbs)], axis=1)
        mix, vj = jax.vjp(_odd_mix, pm, psc_ref[...], gz_ref[...])
        mixb = _bf(mix)
        dhv = dh_ref[...]
        dyo = _bf(dhv * mod_ref[2:3, :])
        yout = _nn(mixb, wo_ref[...])
        dpm, dpsc, dgz = vj(_nt(dyo, wo_ref[...]))
        dgz_ref[...] = dgz
        dpmb = _bf(dpm)
        dp_ref[...] = jnp.concatenate([_nt(dpmb[:, q * 256:(q + 1) * 256], pw_ref[q]) for q in range(4)], axis=1)

        @pl.when((j == 0) | (j == nct))
        def _():
            dmod_ref[...] = jnp.zeros_like(dmod_ref)

        @pl.when((b == 0) & (j == 0))
        def _():
            dpw_ref[...] = jnp.zeros_like(dpw_ref)
            dpsc_ref[...] = jnp.zeros_like(dpsc_ref)
            dwo_ref[...] = jnp.zeros_like(dwo_ref)
        dmod_ref[2:3, :] += jnp.sum(dhv * yout, axis=0, keepdims=True)
        for q in range(4):
            dpw_ref[q] += _tn(pbs[q], dpmb[:, q * 256:(q + 1) * 256])
        dpsc_ref[...] += dpsc
        dwo_ref[...] += _tn(mixb, dyo)

    full = lambda *s: pl.BlockSpec(s, lambda b, j: (0,) * len(s))
    return pl.pallas_call(
        body, name=name, grid=(bsz, t // ROWT), in_specs=_odd_specs(d, i, jl, nct),
        out_specs=[_tile(1024), _tile(1024), _dmod_spec(nct, d), full(4, 256, 256), full(1, 1024), full(1024, d)],
        out_shape=(jax.ShapeDtypeStruct((bsz, t, 1024), F32), jax.ShapeDtypeStruct((bsz, t, 1024), F32),
                   jax.ShapeDtypeStruct((2 * bsz, 3, d), F32), jax.ShapeDtypeStruct((4, 256, 256), F32),
                   jax.ShapeDtypeStruct((1, 1024), F32), jax.ShapeDtypeStruct((1024, d), F32)),
        compiler_params=_cp(VMEM_BIG),
    )(dh, mods, *ps, z, pw, psc, wout)


def _rms(h, g):
    return h * lax.rsqrt(jnp.mean(h * h, axis=-1, keepdims=True) + EPS) * g


def _adamw(w, g_list, m, v, transpose_g, rows, name):
    nl, r, c = w.shape
    ns = g_list[0].shape[1]
    rt = rows or r
    offs = np.cumsum([0] + [g.shape[0] for g in g_list])
    ng = len(g_list)

    def body(*refs):
        w_ref, m_ref, v_ref = refs[0], refs[1 + ng], refs[2 + ng]
        go_ref, d_ref, mo_ref, vo_ref = refs[3 + ng:]
        layer = pl.program_id(0)
        for k in range(ng):
            g_ref = refs[1 + k]

            @pl.when((layer >= offs[k]) & (layer < offs[k + 1]))
            def _():
                g = g_ref[0].astype(F32)
                for s in range(1, ns):
                    g = g + g_ref[s].astype(F32)
                if transpose_g:
                    g = g.T
                mn = ADAM_B1 * m_ref[...] + (1.0 - ADAM_B1) * g
                vn = ADAM_B2 * v_ref[...] + (1.0 - ADAM_B2) * jnp.square(g)
                m_hat = mn / (1.0 - ADAM_B1 ** ADAM_STEP)
                v_hat = vn / (1.0 - ADAM_B2 ** ADAM_STEP)
                go_ref[...] = g
                d_ref[...] = -ADAM_LR * (m_hat / (jnp.sqrt(v_hat) + ADAM_EPS) + ADAM_WD * w_ref[...])
                mo_ref[...] = mn
                vo_ref[...] = vn

    def g_spec(k):
        lo, n_k = int(offs[k]), int(offs[k + 1] - offs[k])

        def idx(l, q):
            mine = (l >= lo) & (l < lo + n_k)
            q = jnp.where(mine, q, 0)
            return (jnp.clip(l - lo, 0, n_k - 1), 0, 0, q) if transpose_g else (jnp.clip(l - lo, 0, n_k - 1), 0, q, 0)
        return pl.BlockSpec((None, ns, c, rt) if transpose_g else (None, ns, rt, c), idx)

    blk = pl.BlockSpec((None, rt, c), lambda l, q: (l, q, 0))
    out = jax.ShapeDtypeStruct((nl, r, c), F32)
    return pl.pallas_call(
        body, name=name, grid=(nl, r // rt),
        in_specs=[blk] + [g_spec(k) for k in range(ng)] + [blk, blk],
        out_specs=[blk, blk, blk, blk], out_shape=(out, out, out, out), compiler_params=_cp(VMEM_BIG),
    )(w, *g_list, m, v)


def _adamw_nd(ws, gs, ms, vs, lead, name):
    shapes = [w.shape for w in ws]
    as2 = lambda a: a.reshape(1, -1) if a.ndim == 1 else a
    ws, gs, ms, vs = ([as2(a) for a in lst] for lst in (ws, gs, ms, vs))
    nts = len(ws)

    def body(*refs):
        for k in range(nts):
            w_ref, g_ref, m_ref, v_ref = refs[4 * k:4 * k + 4]
            d_ref, mo_ref, vo_ref = refs[4 * nts + 3 * k:4 * nts + 3 * k + 3]
            g = g_ref[...]
            mn = ADAM_B1 * m_ref[...] + (1.0 - ADAM_B1) * g
            vn = ADAM_B2 * v_ref[...] + (1.0 - ADAM_B2) * jnp.square(g)
            m_hat = mn / (1.0 - ADAM_B1 ** ADAM_STEP)
            v_hat = vn / (1.0 - ADAM_B2 ** ADAM_STEP)
            d_ref[...] = -ADAM_LR * (m_hat / (jnp.sqrt(v_hat) + ADAM_EPS) + ADAM_WD * w_ref[...])
            mo_ref[...] = mn
            vo_ref[...] = vn

    grid = tuple(ws[0].shape[:lead])

    def spec(a):
        rest = a.shape[lead:]
        return pl.BlockSpec((None,) * lead + tuple(rest), lambda *ids: tuple(ids) + (0,) * len(rest))

    ins = [a for k in range(nts) for a in (ws[k], gs[k], ms[k], vs[k])]
    outs = [w for w in ws for _ in range(3)]
    res = pl.pallas_call(
        body, name=name, grid=grid, in_specs=[spec(a) for a in ins], out_specs=[spec(a) for a in outs],
        out_shape=[jax.ShapeDtypeStruct(a.shape, F32) for a in outs], compiler_params=_cp(VMEM_BIG))(*ins)
    return [tuple(res[3 * k + q].reshape(shapes[k]) for q in range(3)) for k in range(nts)]


def _sum_slots(slots, name):
    ns, r, c = slots.shape

    def body(s_ref, o_ref):
        g = s_ref[0]
        for s in range(1, ns):
            g = g + s_ref[s]
        o_ref[...] = g

    return pl.pallas_call(body, name=name, out_shape=jax.ShapeDtypeStruct((r, c), F32), compiler_params=_cp(VMEM_BIG))(slots)


def _pack(arrs):
    flat = jnp.concatenate([a.reshape(-1) for a in arrs])
    return jnp.pad(flat, (0, (-flat.shape[0]) % (PACK_ROWS * PACK_W))).reshape(-1, PACK_W)


def _unpack(buf, shapes):
    flat = buf.reshape(-1)
    out, off = [], 0
    for s in shapes:
        n = int(np.prod(s))
        out.append(flat[off:off + n].reshape(s))
        off += n
    return out


def kernel(x, c, ctx, c_ctx, ada_w, ada_b, norm_g, even_w_in, even_w_out, attn_sink, ssm_a_re, ssm_a_im, ssm_log_dt, ssm_b_re, ssm_b_im, ssm_c_re, ssm_c_im, ssm_d, glu_w, glu_b, odd_w_in, odd_w_out, pool_w, pool_scale, final_g, loss_target, m_c_ctx, m_ada_w, m_ada_b, m_norm_g, m_even_w_in, m_even_w_out, m_attn_sink, m_ssm_a_re, m_ssm_a_im, m_ssm_log_dt, m_ssm_b_re, m_ssm_b_im, m_ssm_c_re, m_ssm_c_im, m_ssm_d, m_glu_w, m_glu_b, m_odd_w_in, m_odd_w_out, m_pool_w, m_pool_scale, m_final_g, v_c_ctx, v_ada_w, v_ada_b, v_norm_g, v_even_w_in, v_even_w_out, v_attn_sink, v_ssm_a_re, v_ssm_a_im, v_ssm_log_dt, v_ssm_b_re, v_ssm_b_im, v_ssm_c_re, v_ssm_c_im, v_ssm_d, v_glu_w, v_glu_b, v_odd_w_in, v_odd_w_out, v_pool_w, v_pool_scale, v_final_g):
    args = dict(locals())
    bsz, l, d = x.shape
    lc = ctx.shape[1]
    t = lc + l
    nct = lc // ROWT
    depth = ada_w.shape[0]
    nl2 = even_w_in.shape[0]
    me = _my_index()
    assert lc % ROWT == 0 and l % ROWT == 0 and d == 1024 and bsz * N_DEV < 32 and depth % 2 == 0

    npw = pool_w.shape[2]
    shards = {"even": [_bf(jnp.transpose(even_w_in, (0, 2, 1))), _bf(even_w_out), _bf(glu_w)],
              "odd": [_bf(jnp.transpose(odd_w_in, (0, 2, 1))), _bf(odd_w_out), _bf(pool_w.reshape(nl2, -1, pool_w.shape[-1]))]}

    def wgather(kind, jl):
        return shards[kind], [("gather", [(ii, jl)], 0) for ii in range(3)]

    def wjoin(kind, res):
        full = [a.reshape(1, N_DEV * a.shape[2], a.shape[3]) for a in res]
        if kind == "odd":
            full[2] = jnp.transpose(res[2].reshape(1, N_DEV, 4, npw, 256), (0, 2, 1, 3, 4)).reshape(1, 4, 256, 256)
        return full

    got = _exchange([shards["even"][0], c, pool_scale],
                    [("gather", [(0, 0)], 0), ("gather", [(1, None)], 0), ("gather", [(2, None)], 0)], "gather_first")
    first_wt = got[0]
    c_all = got[1].reshape(N_DEV * bsz, d)
    psc_full = jnp.transpose(got[2][0], (1, 0, 2)).reshape(nl2, 1, d)
    weights = {}

    n_rows = 32
    craw = jnp.concatenate([c_all, c_ctx[None], jnp.zeros((n_rows - N_DEV * bsz - 1, d), F32)], axis=0)
    ncol = ada_w.shape[2]
    ada_b_loc = lax.dynamic_slice_in_dim(ada_b, me * ncol, ncol, axis=1)[:, None, :]
    m_loc = _adaln_fwd(craw, ada_w, ada_b_loc)
    m_all = _exchange([m_loc], [("gather", [(0, None)], 0)], "gather_mod")[0][0]
    m_all = jnp.transpose(m_all, (1, 2, 0, 3)).reshape(depth, n_rows, 3 * d)
    lat = lax.dynamic_slice_in_dim(m_all, me * bsz, bsz, axis=1).reshape(depth, bsz, 1, 3, d)
    cmod = jnp.broadcast_to(m_all[:, N_DEV * bsz].reshape(depth, 1, 1, 3, d), (depth, bsz, 1, 3, d))
    mods = jnp.concatenate([cmod, lat], axis=2).reshape(depth, 2 * bsz, 3, d)

    cos, sin = _rope_tables(lc, l)
    hs = jnp.concatenate([ctx, x], axis=1)
    g3 = norm_g[:, None, :]
    dsk3, gb3 = ssm_d[:, None, :], glu_b[:, None, :]
    u_col = 1280 // LANES

    s5_prm = (ssm_a_re, ssm_a_im, ssm_log_dt, ssm_b_re, ssm_b_im, ssm_c_re, ssm_c_im)
    ops = jax.vmap(_s5_operators)(*s5_prm)
    pr, pi = jax.vmap(lambda r, q: _scan_powers(r, q, t // CHUNK))(ops[4], ops[5])
    s5_ops = (ops[0], ops[1], _bf(ops[2]), _bf(ops[3]), pr, pi)
    saved = []
    for i in range(depth):
        jl = i // 2
        if i % 2 == 0:
            if i == 0:
                wt = first_wt.reshape(1, -1, d)
                (z, qp, kvp), got = _fused_in_fwd(
                    hs, mods, g3, wt, i, 0, nct, f"in_fwd{i}", rope=(cos, sin),
                    xchg=(shards["even"][1:], [("gather", [(0, 0)], 0), ("gather", [(1, 0)], 0)]))
                weights[("even", 0)] = wjoin("even", [first_wt] + got)
            wt, wo, gwf = weights[("even", jl)]
            if i > 0:
                (z, qp, kvp), _ = _fused_in_fwd(hs, mods, g3, wt, i, 0, nct, f"in_fwd{i}", rope=(cos, sin))
            (oa, lse), got = _attn_fwd(qp, kvp, attn_sink, jl, lc, f"attn_fwd{i}", xchg=wgather("odd", jl))
            weights[("odd", jl)] = wjoin("odd", got)
            nxt = wgather("even", jl + 1) if jl + 1 < nl2 else None
            (y,), got = _s5_fwd(z, u_col, s5_ops, jl, lc, f"s5_fwd{i}", xchg=nxt)
            if nxt:
                weights[("even", jl + 1)] = wjoin("even", got)
            hn = _even_out_fwd(hs, mods, oa, y, z, dsk3, gwf, gb3, wo, i, jl, nct, f"out_fwd{i}")
            saved.append(dict(hs=hs, z=z, qp=qp, kvp=kvp, oa=oa, lse=lse, y=y))
        else:
            wt, wo, pwf = weights[("odd", jl)]
            (z,), _ = _fused_in_fwd(hs, mods, g3, wt, i, 0, nct, f"in_fwd{i}")
            ps = [_pool(z, gi, wd // 2, lc, False, f"pool_fwd{i}_{gi}") for gi, wd in enumerate(POOL_WINDOWS)]
            saved.append(dict(hs=hs, z=z, ps=ps))
            if i == depth - 1:
                dh, loss_p, d_final_g = _odd_out_fwd(hs, mods, ps, z, pwf, psc_full, wo, i, jl, nct, f"out_fwd{i}",
                                                     loss=(final_g[None], loss_target))
                break
            hn = _odd_out_fwd(hs, mods, ps, z, pwf, psc_full, wo, i, jl, nct, f"out_fwd{i}")
        hs = hn

    loss = lax.psum(jnp.sum(loss_p[:, 0, 0]), ("x", "y", "c"))

    slots = {n: [None] * nl2 for n in ("even_w_in", "even_w_out", "glu_w", "odd_w_in", "odd_w_out", "pool_w", "pool_scale")}
    gsmall = {n: [None] * nl2 for n in ("attn_sink", "ssm_d", "glu_b")}
    d_norm_g, d_mods = [None] * depth, [None] * depth
    d_ops = [None] * nl2

    def rows_xchg(named):
        arrs = [a for _, _, a in named]
        specs = [("rows2" if n == "pool_w" else "rows", [(k, None)], a.shape[1 if n == "pool_w" else 0] // N_DEV)
                 for k, (n, _, a) in enumerate(named)]
        return (arrs, specs)

    def keep(named, res):
        for (n, jl_, _), r_ in zip(named, res):
            slots[n][jl_] = r_

    pending = None
    for i in reversed(range(depth)):
        jl = i // 2
        sv = saved[i]
        if i % 2 == 0:
            wt, wo, gwf = weights[("even", jl)]
            doa, dga, dy, dgs, dmod_g, ddsk, dgw, dgb, dwo = _even_out_bwd(
                dh, mods, sv["oa"], sv["y"], sv["z"], dsk3, gwf, gb3, wo, i, jl, nct, f"out_bwd{i}")
            (dq_r, dkv_acc, dsink), got = _attn_bwd(sv["qp"], sv["kvp"], attn_sink, jl, sv["oa"], sv["lse"], doa, lc,
                                                     f"attn_bwd{i}", xchg=rows_xchg(pending) if pending else None)
            if pending:
                keep(pending, got)
            dq, dkv = _rope_bwd(dq_r, dkv_acc, cos, sin, t, f"rope_bwd{i}")
            mine = [("even_w_out", jl, dwo), ("glu_w", jl, dgw)]
            (du, dqf, dqb, dbt, dct, dav), got = _s5_bwd(sv["z"], u_col, dy, s5_ops, jl, lc, f"s5_bwd{i}", xchg=rows_xchg(mine))
            keep(mine, got)
            per_group = lambda q: dav[:, q].reshape(SSM_GROUPS, SSM_STATE)
            d_ops[jl] = (dqf, dqb, dbt, dct, jnp.stack([per_group(0), per_group(2)]), jnp.stack([per_group(1), per_group(3)]))
            gsmall["attn_sink"][jl] = jnp.sum(dsink[:, :, 0], axis=0)
            gsmall["ssm_d"][jl] = ddsk[0]
            gsmall["glu_b"][jl] = dgb[0]
            pieces = [(dq, 512, 0), (dkv, 256, 0), (dga, 512, 0), (du, 512, 0), (dy, 512, 0), (dgs, 512, 0)]
            asm = lambda p, s: [p[0], p[1], p[2], p[3] + p[4] * s, p[5]]
            (dh, dmod_in, dg, dw), _ = _fused_in_bwd(sv["hs"], mods, g3, wt, dh, pieces, (dsk3, jl), asm, i, 0, nct, f"in_bwd{i}",
                                                     latent_only=(i == 0))
            pending = [("even_w_in", jl, dw)]
        else:
            wt, wo, pwf = weights[("odd", jl)]
            dp, dgz, dmod_g, dpw, dpsc, dwo = _odd_out_bwd(
                dh, mods, sv["ps"], sv["z"], pwf, psc_full, wo, i, jl, nct, f"out_bwd{i}")
            dus = [_pool(dp, gi, wd // 2, lc, True, f"pool_bwd{i}_{gi}") for gi, wd in enumerate(POOL_WINDOWS)]
            dpsc8 = jnp.broadcast_to(dpsc.reshape(N_DEV, 1, -1), (N_DEV, 8, d // N_DEV)).reshape(N_DEV * 8, -1)
            pieces = [(u_, 256, 0) for u_ in dus] + [(dgz, 1024, 0)]
            asm = lambda p, s: p
            (dh, dmod_in, dg, dw), got = _fused_in_bwd(sv["hs"], mods, g3, wt, dh, pieces, None, asm, i, 0, nct, f"in_bwd{i}",
                                                       xchg=rows_xchg(pending) if pending else None)
            if pending:
                keep(pending, got)
            pending = [("odd_w_in", jl, dw), ("odd_w_out", jl, dwo), ("pool_w", jl, dpw), ("pool_scale", jl, dpsc8)]
        d_norm_g[i] = dg[0]
        d_mods[i] = (dmod_g + dmod_in).reshape(bsz, 2, 3 * d)
    grad_x = dh

    _, op_vjp = jax.vjp(jax.vmap(_s5_operators), *s5_prm)
    dprm = op_vjp(tuple(jnp.stack([d_ops[jl][q] for jl in range(nl2)]) for q in range(6)))

    dmd = jnp.stack(d_mods)
    dm_loc = jnp.concatenate([dmd[:, :, 1], jnp.sum(dmd[:, :, 0], axis=1, keepdims=True)], axis=1)
    dm_all = _exchange([dm_loc], [("gather", [(0, None)], 0)], "gather_dmod")[0][0]
    dm_lat = jnp.transpose(dm_all[:, :, :bsz], (1, 0, 2, 3)).reshape(depth, N_DEV * bsz, 3 * d)
    dm_ctx = dm_all[0, :, bsz]
    for e in range(1, N_DEV):
        dm_ctx = dm_ctx + dm_all[e, :, bsz]
    dm_rows = jnp.concatenate([dm_lat, dm_ctx[:, None], jnp.zeros((depth, n_rows - N_DEV * bsz - 1, 3 * d), F32)], axis=1)
    dm_cols = lax.dynamic_slice_in_dim(dm_rows, me * ncol, ncol, axis=2)
    g_ada_w, dcraw = _adaln_bwd(craw, dm_cols, ada_w)

    small = {"c_ctx": dcraw[N_DEV * bsz], "ada_b": jnp.sum(dm_loc, axis=1), "norm_g": jnp.stack(d_norm_g),
             "final_g": d_final_g[0]}
    for n, v2 in gsmall.items():
        small[n] = jnp.stack(v2)
    for n, gval in zip(("ssm_a_re", "ssm_a_im", "ssm_log_dt", "ssm_b_re", "ssm_b_im", "ssm_c_re", "ssm_c_im"), dprm):
        small[n] = gval
    snames = list(small)
    sshapes = [args[n].shape for n in snames]
    spack = _pack([small[n].reshape(args[n].shape) for n in snames])

    pending = [(n, l_, _bf(a)) for n, l_, a in pending]
    last = pending + [("small", 0, spack)]
    red = _exchange(*rows_xchg(last), "reduce_last")
    keep(pending, red[:-1])
    s_sum = _sum_slots(red[-1][0], "sum_small")
    s_all = _exchange([s_sum], [("gather", [(0, None)], 0)], "gather_small")[0]
    s_all = s_all.reshape(1, 1, -1, PACK_W)

    out = {}

    def put(n, res, shape):
        for kind, buf in zip(("grad_", "delta_", "new_m_", "new_v_"), res):
            out[kind + n] = buf.reshape(shape)

    def as3(a):
        return a.reshape(a.shape[0], -1, a.shape[-1])

    for n in slots:
        w = args[n]
        g_list = [s_[:, :, :1] if n == "pool_scale" else s_ for s_ in slots[n]]
        g_list = [s_.reshape(1, N_DEV, -1, s_.shape[-1]) for s_ in g_list]
        tr = n in ("even_w_in", "odd_w_in")
        put(n, _adamw(as3(w), g_list, as3(args["m_" + n]), as3(args["v_" + n]), tr, 256 if tr else None, "adamw_" + n),
            w.shape)
    put("ada_w", _adamw(ada_w, [g_ada_w[:, None]], m_ada_w, v_ada_w, False, None, "adamw_ada_w"), ada_w.shape)
    for n, a in zip(snames, _unpack(s_all, sshapes)):
        out["grad_" + n] = a
    for lead, name in ((0, "adamw_small"), (2, "adamw_ssm_bc")):
        grp = [n for n in snames if (args[n].ndim == 5) == (lead == 2)]
        res = _adamw_nd([args[n] for n in grp], [out["grad_" + n] for n in grp], [args["m_" + n] for n in grp],
                        [args["v_" + n] for n in grp], lead, name)
        for n, (dl, mn, vn) in zip(grp, res):
            out["delta_" + n], out["new_m_" + n], out["new_v_" + n] = dl, mn, vn

    order = ["c_ctx", "ada_w", "ada_b", "norm_g", "even_w_in", "even_w_out", "attn_sink", "ssm_a_re", "ssm_a_im",
             "ssm_log_dt", "ssm_b_re", "ssm_b_im", "ssm_c_re", "ssm_c_im", "ssm_d", "glu_w", "glu_b", "odd_w_in",
             "odd_w_out", "pool_w", "pool_scale", "final_g"]
    return (loss, grad_x, *[out[k + n] for k in ("grad_", "delta_", "new_m_", "new_v_") for n in order])
```

```python
import functools

import numpy as np
import jax
import jax.numpy as jnp
from jax import lax
from jax.experimental import pallas as pl
from jax.experimental.pallas import tpu as pltpu

F32 = jnp.float32
BF16 = jnp.bfloat16
EPS = 1e-6
NEG_INF = -1e30
N_DEV = 8
GRID_W = 64
WINDOW = 128
QBLK = 128
ROWT = 256
CHUNK = 16
SSM_GROUPS = 32
SSM_GROUP = 16
SSM_STATE = 64
POOL_WINDOWS = (2, 4, 8, 16)
ROPE_BASE = 10000.0
ADAM_LR, ADAM_B1, ADAM_B2, ADAM_EPS, ADAM_WD, ADAM_STEP = 0.001, 0.9, 0.999, 1e-08, 0.01, 10
LANES = 128
PACK_W = 1024
PACK_ROWS = 64
VMEM_BIG = 56 << 20


def _cp(vmem=None):
    return pltpu.CompilerParams(vmem_limit_bytes=vmem) if vmem else pltpu.CompilerParams()


def _nt(a, b):
    return lax.dot_general(a, b, (((1,), (1,)), ((), ())), preferred_element_type=F32)


def _tn(a, b):
    return lax.dot_general(a, b, (((0,), (0,)), ((), ())), preferred_element_type=F32)


def _nn(a, b):
    return jnp.dot(a, b, preferred_element_type=F32)


def _bf(x):
    return x.astype(BF16)


def _my_index():
    return 4 * lax.axis_index("x") + 2 * lax.axis_index("y") + lax.axis_index("c")


def _xchg_shapes(inputs, specs):
    out_shapes = []
    for kind, parts, r in specs:
        ii, li = parts[0]
        shp = tuple(inputs[ii].shape if li is None else inputs[ii].shape[1:])
        piece = shp if kind == "gather" else ((r,) + shp[1:] if kind == "rows" else (shp[0], r) + shp[2:])
        out_shapes.append(jax.ShapeDtypeStruct((len(parts), N_DEV) + piece, inputs[ii].dtype))
    return out_shapes, sum(len(parts) for _, parts, _ in specs)


def _xchg_sems(n_parts):
    return [pltpu.SemaphoreType.DMA((n_parts * (N_DEV - 1),)), pltpu.SemaphoreType.DMA((n_parts * (N_DEV - 1),)),
            pltpu.SemaphoreType.DMA((n_parts,))]


def _xchg_copies(specs, in_refs, out_refs, send_sems, recv_sems, loc_sems):
    x, y, c = lax.axis_index("x"), lax.axis_index("y"), lax.axis_index("c")
    me = 4 * x + 2 * y + c

    def piece(ref, kind, r, p):
        if kind == "gather":
            return ref
        if kind == "rows":
            return ref.at[pl.ds(p * r, r)]
        return ref.at[:, pl.ds(p * r, r)]

    copies, q = [], 0
    for si, (kind, parts, r) in enumerate(specs):
        for pi, (ii, li) in enumerate(parts):
            src = in_refs[ii] if li is None else in_refs[ii].at[li]
            dst = out_refs[si].at[pi, me]
            copies.append(pltpu.make_async_copy(piece(src, kind, r, me), dst, loc_sems.at[q]))
            for k in range(1, N_DEV):
                px = 1 - x if k & 4 else x
                py = 1 - y if k & 2 else y
                pc = 1 - c if k & 1 else c
                copies.append(pltpu.make_async_remote_copy(
                    src_ref=piece(src, kind, r, 4 * px + 2 * py + pc), dst_ref=dst,
                    send_sem=send_sems.at[q * (N_DEV - 1) + k - 1], recv_sem=recv_sems.at[q * (N_DEV - 1) + k - 1],
                    device_id=(px, py, pc), device_id_type=pl.DeviceIdType.MESH))
            q += 1
    return copies


def _exchange(inputs, specs, name):
    out_shapes, n_parts = _xchg_shapes(inputs, specs)
    ni, ns = len(inputs), len(specs)

    def body(*refs):
        copies = _xchg_copies(specs, refs[:ni], refs[ni:ni + ns], *refs[ni + ns:])
        for cp in copies:
            cp.start()
        for cp in copies:
            cp.wait()

    return pl.pallas_call(
        body, name=name, out_shape=tuple(out_shapes),
        in_specs=[pl.BlockSpec(memory_space=pl.ANY)] * ni,
        out_specs=tuple(pl.BlockSpec(memory_space=pl.ANY) for _ in specs),
        scratch_shapes=_xchg_sems(n_parts),
        compiler_params=pltpu.CompilerParams(has_side_effects=True),
    )(*inputs)


def _call(body, *, name, grid, in_specs, out_specs, out_shape, operands, vmem=None, scratch=(), xchg=None):
    out_shape, out_specs = tuple(out_shape), list(out_specs)
    if xchg is None:
        res = pl.pallas_call(body, name=name, grid=grid, in_specs=in_specs, out_specs=out_specs, out_shape=out_shape,
                             scratch_shapes=list(scratch), compiler_params=_cp(vmem))(*operands)
        return list(res), None
    xin, xspecs = xchg
    xshapes, n_parts = _xchg_shapes(xin, xspecs)
    ni, no, nxi, nxo, nsc = len(operands), len(out_shape), len(xin), len(xspecs), len(scratch)
    anyspec = pl.BlockSpec(memory_space=pl.ANY)

    def hosted(*refs):
        ins, xins = refs[:ni], refs[ni:ni + nxi]
        outs, xouts = refs[ni + nxi:ni + nxi + no], refs[ni + nxi + no:ni + nxi + no + nxo]
        scr, sems = refs[ni + nxi + no + nxo:ni + nxi + no + nxo + nsc], refs[ni + nxi + no + nxo + nsc:]
        copies = _xchg_copies(xspecs, xins, xouts, *sems)
        first, last = True, True
        for ax, n in enumerate(grid):
            first = first & (pl.program_id(ax) == 0)
            last = last & (pl.program_id(ax) == n - 1)

        @pl.when(first)
        def _():
            for cp in copies:
                cp.start()
        body(*ins, *outs, *scr)

        @pl.when(last)
        def _():
            for cp in copies:
                cp.wait()

    res = pl.pallas_call(
        hosted, name=name, grid=grid, in_specs=list(in_specs) + [anyspec] * nxi,
        out_specs=out_specs + [anyspec] * nxo, out_shape=out_shape + tuple(xshapes),
        scratch_shapes=list(scratch) + _xchg_sems(n_parts), compiler_params=_cp(vmem))(*operands, *xin)
    return list(res[:no]), list(res[no:])


def _adaln_fwd(craw, ada_w, ada_b):
    nl, d, n = ada_w.shape
    r = craw.shape[0]

    def body(c_ref, w_ref, b_ref, o_ref):
        s = jax.nn.silu(c_ref[...])
        o_ref[...] = _nn(_bf(s), _bf(w_ref[...])) + b_ref[...]

    return pl.pallas_call(
        body, name="adaln_fwd", grid=(nl,),
        in_specs=[pl.BlockSpec((r, d), lambda i: (0, 0)), pl.BlockSpec((None, d, n), lambda i: (i, 0, 0)),
                  pl.BlockSpec((None, 1, n), lambda i: (i, 0, 0))],
        out_specs=pl.BlockSpec((None, r, n), lambda i: (i, 0, 0)),
        out_shape=jax.ShapeDtypeStruct((nl, r, n), F32),
    )(craw, ada_w, ada_b)


def _adaln_bwd(craw, dm, ada_w):
    nl, d, n = ada_w.shape
    r = craw.shape[0]

    def body(c_ref, dm_ref, w_ref, gw_ref, dc_ref, acc_ref):
        i = pl.program_id(0)
        s, vj = jax.vjp(jax.nn.silu, c_ref[...])
        dmb = _bf(dm_ref[...])
        gw_ref[...] = _tn(_bf(s), dmb)

        @pl.when(i == 0)
        def _():
            acc_ref[...] = jnp.zeros_like(acc_ref)
        acc_ref[...] += _nt(dmb, _bf(w_ref[...]))

        @pl.when(i == nl - 1)
        def _():
            dc_ref[...] = vj(acc_ref[...])[0]

    return pl.pallas_call(
        body, name="adaln_bwd", grid=(nl,),
        in_specs=[pl.BlockSpec((r, d), lambda i: (0, 0)), pl.BlockSpec((None, r, n), lambda i: (i, 0, 0)),
                  pl.BlockSpec((None, d, n), lambda i: (i, 0, 0))],
        out_specs=[pl.BlockSpec((None, d, n), lambda i: (i, 0, 0)), pl.BlockSpec((r, d), lambda i: (0, 0))],
        out_shape=(jax.ShapeDtypeStruct((nl, d, n), F32), jax.ShapeDtypeStruct((r, d), F32)),
        scratch_shapes=[pltpu.VMEM((r, d), F32)],
    )(craw, dm, ada_w)


def _norm_mod(h, mod, g):
    ms = jnp.mean(h * h, axis=-1, keepdims=True)
    y = h * lax.rsqrt(ms + EPS) * g
    return y * (1.0 + mod[1:2]) + mod[0:1]


def _mod_spec(i, nct, d):
    return pl.BlockSpec((None, None, 3, d), lambda b, j: (i, 2 * b + jnp.where(j >= nct, 1, 0), 0, 0))


def _dmod_spec(nct, d):
    return pl.BlockSpec((None, 3, d), lambda b, j: (2 * b + jnp.where(j >= nct, 1, 0), 0, 0))


def _layer_spec(li, *shape):
    return pl.BlockSpec((None,) + tuple(shape), lambda b, j: (li,) + (0,) * len(shape))


def _tile(width, col_blk=0):
    return pl.BlockSpec((None, ROWT, width), lambda b, j: (b, j, col_blk))


def _fused_in_fwd(hs, mods, g, wt, i, jl, nct, name, xchg=None, rope=None):
    bsz, t, d = hs.shape
    n = wt.shape[1]

    def body(h_ref, mod_ref, g_ref, w_ref, *rest):
        a = _norm_mod(h_ref[...], mod_ref[...], g_ref[...])
        z = _nt(_bf(a), w_ref[...])
        if rope is None:
            rest[0][...] = z
        else:
            cos_ref, sin_ref, z_ref, qp_ref, kvp_ref = rest
            z_ref[...] = z
            qp_ref[...], kvp_ref[...] = _rope_pad(z[:, :512], z[:, 512:768], cos_ref[...], sin_ref[...])

    table = pl.BlockSpec((ROWT, LANES), lambda b, j: (j, 0))
    return _call(
        body, name=name, grid=(bsz, t // ROWT),
        in_specs=[_tile(d), _mod_spec(i, nct, d), _layer_spec(i, 1, d), _layer_spec(jl, n, d)] + ([table, table] if rope else []),
        out_specs=[_tile(n)] + ([_tile(1024), _tile(512)] if rope else []),
        out_shape=[jax.ShapeDtypeStruct((bsz, t, n), F32)]
                  + ([jax.ShapeDtypeStruct((bsz, t, 1024), BF16), jax.ShapeDtypeStruct((bsz, t, 512), BF16)] if rope else []),
        vmem=VMEM_BIG, xchg=xchg, operands=[hs, mods, g, wt] + (list(rope) if rope else []))


def _fused_in_bwd(hs, mods, g, wt, dh_out, pieces, small, assemble, i, jl, nct, name, xchg=None, latent_only=False,
                  tables=()):
    bsz, t, d = hs.shape
    n = wt.shape[1]
    npc, nsm, ntab = len(pieces), int(small is not None), len(tables)

    def body(*refs):
        h_ref, mod_ref, g_ref, w_ref, dho_ref = refs[:5]
        p_refs = refs[5:5 + npc]
        s_val = refs[5 + npc][...] if nsm else None
        tabs = [r[...] for r in refs[5 + npc + nsm:5 + npc + nsm + ntab]]
        dhi_ref, dmod_ref, dg_ref, dw_ref = refs[5 + npc + nsm + ntab:]
        b, j = pl.program_id(0), pl.program_id(1)
        a, vj = jax.vjp(_norm_mod, h_ref[...], mod_ref[...], g_ref[...])
        dz = jnp.concatenate([_bf(p) for p in assemble([r[...] for r in p_refs], s_val, tabs)], axis=1)
        dh, dmod, dg = vj(_nn(dz, w_ref[...]))
        dhi_ref[...] = dho_ref[...] + dh

        @pl.when((j == 0) | (j == nct))
        def _():
            dmod_ref[...] = jnp.zeros_like(dmod_ref)

        @pl.when((b == 0) & (j == 0))
        def _():
            dg_ref[...] = jnp.zeros_like(dg_ref)
            dw_ref[...] = jnp.zeros_like(dw_ref)
        dmod_ref[...] += dmod
        dg_ref[...] += dg
        dw_ref[...] += _tn(dz, _bf(a))

    full = lambda r, c: pl.BlockSpec((r, c), lambda b, j: (0, 0))
    dh_rows = t - nct * ROWT if latent_only else t
    dh_spec = pl.BlockSpec((None, ROWT, d), lambda b, j: (b, jnp.maximum(j - nct, 0), 0)) if latent_only else _tile(d)
    return _call(
        body, name=name, grid=(bsz, t // ROWT),
        in_specs=[_tile(d), _mod_spec(i, nct, d), _layer_spec(i, 1, d), _layer_spec(jl, n, d), _tile(d)]
                 + [_tile(wd, cb) for _, wd, cb in pieces]
                 + ([_layer_spec(small[1], 1, small[0].shape[2])] if nsm else [])
                 + [pl.BlockSpec((ROWT, LANES), lambda b, j: (j, 0))] * ntab,
        out_specs=[dh_spec, _dmod_spec(nct, d), full(1, d), full(n, d)],
        out_shape=(jax.ShapeDtypeStruct((bsz, dh_rows, d), F32), jax.ShapeDtypeStruct((2 * bsz, 3, d), F32),
                   jax.ShapeDtypeStruct((1, d), F32), jax.ShapeDtypeStruct((n, d), F32)),
        vmem=VMEM_BIG, xchg=xchg,
        operands=[hs, mods, g, wt, dh_out, *[p for p, _, _ in pieces], *([small[0]] if nsm else []), *tables])


def _rope_tables(lc, l):
    tpos = np.arange(l)
    row = (tpos // GRID_W).astype(np.float32)
    col = (tpos % GRID_W).astype(np.float32)
    nf = 16
    inv = (np.float32(ROPE_BASE) ** (-np.arange(nf, dtype=np.float32) / np.float32(nf))).astype(np.float32)
    ang_r = row[:, None] * inv[None]
    ang_c = col[:, None] * inv[None]
    cos64 = np.concatenate([np.cos(ang_r), np.cos(ang_r), np.cos(ang_c), np.cos(ang_c)], axis=1)
    sin64 = np.concatenate([-np.sin(ang_r), np.sin(ang_r), -np.sin(ang_c), np.sin(ang_c)], axis=1)
    cos = np.concatenate([np.ones((lc, 128), np.float32), np.tile(cos64, (1, 2)).astype(np.float32)], axis=0)
    sin = np.concatenate([np.zeros((lc, 128), np.float32), np.tile(sin64, (1, 2)).astype(np.float32)], axis=0)
    return jnp.asarray(cos), jnp.asarray(sin)


def _rot(x, cos, sin):
    lane = lax.broadcasted_iota(jnp.int32, x.shape, 1)
    first = jnp.bitwise_and(jnp.right_shift(lane, 4), 1) == 0
    partner = jnp.where(first, pltpu.roll(x, LANES - 16, 1), pltpu.roll(x, 16, 1))
    return x * cos + partner * sin


def _split_heads(x):
    low = lax.broadcasted_iota(jnp.int32, x.shape, 1) < 64
    return jnp.where(low, x, 0.0), jnp.where(low, pltpu.roll(x, 64, 1), 0.0)


def _merge_heads(a, b):
    return a + pltpu.roll(b, 64, 1)


def _rope_pad(q, kv, c, s):
    outs = []
    for sl in range(4):
        outs += list(_split_heads(_rot(q[:, sl * LANES:(sl + 1) * LANES], c, s) * 0.125))
    k0, k1 = _split_heads(_rot(kv[:, 0:LANES], c, s))
    v0, v1 = _split_heads(kv[:, LANES:2 * LANES])
    return _bf(jnp.concatenate(outs, axis=1)), _bf(jnp.concatenate([k0, k1, v0, v1], axis=1))


def _rope_unpad(dq_r, acc, c, s):
    s = -s
    dk = _merge_heads(acc[:, 0:LANES], acc[:, LANES:2 * LANES])
    dv = _merge_heads(acc[:, 2 * LANES:3 * LANES], acc[:, 3 * LANES:4 * LANES])
    return [_rot(dq_r[:, sl * LANES:(sl + 1) * LANES], c, s) for sl in range(4)] + [_rot(dk, c, s), dv]


def _attn_bias(j, ncb, l):
    n = j - ncb
    r = lax.broadcasted_iota(jnp.int32, (QBLK, 3 * QBLK), 0)
    cidx = lax.broadcasted_iota(jnp.int32, (QBLK, 3 * QBLK), 1)
    kpos = (n - 1) * QBLK + cidx
    valid = (jnp.abs(r - cidx + QBLK) <= WINDOW) & (kpos >= 0) & (kpos < l) & (j >= ncb)
    return jnp.where(valid, 0.0, NEG_INF).astype(F32)


def _attn_bias4(j, ncb, l):
    b1 = _attn_bias(j, ncb, l)
    return jnp.concatenate([b1, b1, b1, b1], axis=0)


def _sink_rows(sink_ref, jl, hk):
    head = jnp.right_shift(lax.broadcasted_iota(jnp.int32, (4 * QBLK, 1), 0), 7)
    sk = jnp.zeros((4 * QBLK, 1), F32)
    for g in range(4):
        sk = jnp.where(head == g, sink_ref[jl, 4 * hk + g], sk)
    return sk


def _kv_specs(nb, lc):
    return [pl.BlockSpec((None, QBLK, 512), lambda b, j: (b, jnp.maximum(j - 1, 0), 0)),
            pl.BlockSpec((None, QBLK, 512), lambda b, j: (b, j, 0)),
            pl.BlockSpec((None, QBLK, 512), lambda b, j: (b, jnp.minimum(j + 1, nb - 1), 0)),
            pl.BlockSpec((None, lc, 512), lambda b, j: (b, 0, 0))]


def _lane_pick(x, h):
    lane = lax.broadcasted_iota(jnp.int32, x.shape, 1)
    return jnp.sum(jnp.where(lane == h, x, 0.0), axis=1, keepdims=True)


def _attn_fwd(qp, kvp, sinks, jl, lc, name, xchg=None):
    bsz, t, _ = qp.shape
    nb, ncb, l = t // QBLK, lc // QBLK, t - lc

    def body(sink_ref, q_ref, kp_ref, kc_ref, kn_ref, kx_ref, o_ref, lse_ref):
        j = pl.program_id(1)
        kloc = jnp.concatenate([kp_ref[...], kc_ref[...], kn_ref[...]], axis=0)
        kctx = kx_ref[...]
        bias = _attn_bias4(j, ncb, l)
        lane = lax.broadcasted_iota(jnp.int32, (QBLK, LANES), 1)
        lse_all = jnp.zeros((QBLK, LANES), F32)
        outs = []
        for hk in range(2):
            ks, vs = slice(hk * LANES, (hk + 1) * LANES), slice((2 + hk) * LANES, (3 + hk) * LANES)
            q4 = jnp.concatenate([q_ref[:, (4 * hk + g) * LANES:(4 * hk + g + 1) * LANES] for g in range(4)], axis=0)
            s_loc = _nt(q4, kloc[:, ks]) + bias
            s_ctx = _nt(q4, kctx[:, ks])
            sk = _sink_rows(sink_ref, jl, hk)
            m = jnp.maximum(jnp.maximum(jnp.max(s_loc, axis=1, keepdims=True), jnp.max(s_ctx, axis=1, keepdims=True)), sk)
            p_loc, p_ctx = jnp.exp(s_loc - m), jnp.exp(s_ctx - m)
            den = jnp.sum(p_loc, axis=1, keepdims=True) + jnp.sum(p_ctx, axis=1, keepdims=True) + jnp.exp(sk - m)
            o4 = (_nn(_bf(p_loc), kloc[:, vs]) + _nn(_bf(p_ctx), kctx[:, vs])) / den
            lse4 = m + jnp.log(den)
            for g in range(4):
                outs.append(o4[g * QBLK:(g + 1) * QBLK])
                lse_all = lse_all + jnp.where(lane == 4 * hk + g, lse4[g * QBLK:(g + 1) * QBLK], 0.0)
        o_ref[...] = jnp.concatenate([_merge_heads(outs[2 * i], outs[2 * i + 1]) for i in range(4)], axis=1)
        lse_ref[...] = lse_all

    return _call(
        body, name=name, grid=(bsz, nb),
        in_specs=[pl.BlockSpec(memory_space=pltpu.SMEM),
                  pl.BlockSpec((None, QBLK, 1024), lambda b, j: (b, j, 0))] + _kv_specs(nb, lc),
        out_specs=[pl.BlockSpec((None, QBLK, 512), lambda b, j: (b, j, 0)),
                   pl.BlockSpec((None, QBLK, LANES), lambda b, j: (b, j, 0))],
        out_shape=(jax.ShapeDtypeStruct((bsz, t, 512), F32), jax.ShapeDtypeStruct((bsz, t, LANES), F32)),
        xchg=xchg, operands=[sinks, qp, kvp, kvp, kvp, kvp])


def _attn_bwd(qp, kvp, sinks, jl, o, lse, do, lc, name, xchg=None):
    bsz, t, _ = qp.shape
    nb, ncb, l = t // QBLK, lc // QBLK, t - lc
    nk = 3 * QBLK

    def body(sink_ref, q_ref, kp_ref, kc_ref, kn_ref, kx_ref, o_ref, lse_ref, do_ref, dq_ref, acc_ref, ds_ref):
        j = pl.program_id(1)

        @pl.when(j == 0)
        def _():
            acc_ref[...] = jnp.zeros_like(acc_ref)
            ds_ref[...] = jnp.zeros_like(ds_ref)
        kloc = jnp.concatenate([kp_ref[...], kc_ref[...], kn_ref[...]], axis=0)
        kctx = kx_ref[...]
        bias = _attn_bias4(j, ncb, l)
        lse = lse_ref[...]
        row8 = lax.broadcasted_iota(jnp.int32, (8, LANES), 0)
        dsink = jnp.zeros((8, LANES), F32)
        dqs = []
        d_loc, d_ctx = [None] * 4, [None] * 4
        for hk in range(2):
            ks, vs = slice(hk * LANES, (hk + 1) * LANES), slice((2 + hk) * LANES, (3 + hk) * LANES)
            q4 = jnp.concatenate([q_ref[:, (4 * hk + g) * LANES:(4 * hk + g + 1) * LANES] for g in range(4)], axis=0)
            do4 = jnp.concatenate([x_ for s2 in (2 * hk, 2 * hk + 1)
                                   for x_ in _split_heads(do_ref[:, s2 * LANES:(s2 + 1) * LANES])], axis=0)
            o4 = jnp.concatenate([x_ for s2 in (2 * hk, 2 * hk + 1)
                                  for x_ in _split_heads(o_ref[:, s2 * LANES:(s2 + 1) * LANES])], axis=0)
            delta = jnp.sum(do4 * o4, axis=1, keepdims=True)
            lse4 = jnp.concatenate([_lane_pick(lse, 4 * hk + g) for g in range(4)], axis=0)
            p_loc = jnp.exp(_nt(q4, kloc[:, ks]) + bias - lse4)
            p_ctx = jnp.exp(_nt(q4, kctx[:, ks]) - lse4)
            dob = _bf(do4)
            ds_loc = _bf(p_loc * (_nt(dob, kloc[:, vs]) - delta))
            ds_ctx = _bf(p_ctx * (_nt(dob, kctx[:, vs]) - delta))
            dq4 = (_nn(ds_loc, kloc[:, ks]) + _nn(ds_ctx, kctx[:, ks])) * 0.125
            d_loc[hk], d_ctx[hk] = _tn(ds_loc, q4), _tn(ds_ctx, q4)
            d_loc[2 + hk], d_ctx[2 + hk] = _tn(_bf(p_loc), dob), _tn(_bf(p_ctx), dob)
            dsk = -jnp.exp(_sink_rows(sink_ref, jl, hk) - lse4) * delta
            for g in range(4):
                dqs.append(dq4[g * QBLK:(g + 1) * QBLK])
                dsink = dsink + jnp.where(row8 == 4 * hk + g, jnp.sum(dsk[g * QBLK:(g + 1) * QBLK]), 0.0)
        dq_ref[...] = jnp.concatenate([_merge_heads(dqs[2 * i], dqs[2 * i + 1]) for i in range(4)], axis=1)
        ds_ref[...] += dsink
        acc_ref[pl.ds(0, lc), :] += jnp.concatenate(d_ctx, axis=1)

        @pl.when(j >= ncb)
        def _():
            st = pl.multiple_of((j - 1) * QBLK, QBLK)
            acc_ref[pl.ds(st, nk), :] += jnp.concatenate(d_loc, axis=1)

    blk = lambda wd: pl.BlockSpec((None, QBLK, wd), lambda b, j: (b, j, 0))
    return _call(
        body, name=name, grid=(bsz, nb),
        in_specs=[pl.BlockSpec(memory_space=pltpu.SMEM), blk(1024)] + _kv_specs(nb, lc)
                 + [blk(512), blk(LANES), blk(512)],
        out_specs=[blk(512), pl.BlockSpec((None, t + QBLK, 512), lambda b, j: (b, 0, 0)),
                   pl.BlockSpec((None, 8, LANES), lambda b, j: (b, 0, 0))],
        out_shape=(jax.ShapeDtypeStruct((bsz, t, 512), F32), jax.ShapeDtypeStruct((bsz, t + QBLK, 512), F32),
                   jax.ShapeDtypeStruct((bsz, 8, LANES), F32)),
        vmem=VMEM_BIG, xchg=xchg, operands=[sinks, qp, kvp, kvp, kvp, kvp, o, lse, do])


def _s5_operators(a_re, a_im, log_dt, b_re, b_im, c_re, c_im):
    hi = lax.Precision.HIGHEST
    dt = jnp.exp(log_dt)[..., None]
    tau = jnp.arange(CHUNK + 1, dtype=F32)[:, None, None, None]
    mag = jnp.exp(tau * (a_re * dt))
    pwr, pwi = mag * jnp.cos(tau * (a_im * dt)), mag * jnp.sin(tau * (a_im * dt))
    ar, ai = pwr[1], pwi[1]
    den = a_re * a_re + a_im * a_im
    fr = ((ar - 1.0) * a_re + ai * a_im) / den
    fi = (ai * a_re - (ar - 1.0) * a_im) / den
    bbr = fr[..., None] * b_re - fi[..., None] * b_im
    bbi = fr[..., None] * b_im + fi[..., None] * b_re
    wr = pwr[:CHUNK, ..., None] * bbr - pwi[:CHUNK, ..., None] * bbi
    wi = pwr[:CHUNK, ..., None] * bbi + pwi[:CHUNK, ..., None] * bbr
    kern = (jnp.einsum("dgcp,tdgpk->dgtck", c_re, wr, precision=hi)
            - jnp.einsum("dgcp,tdgpk->dgtck", c_im, wi, precision=hi))
    g = a_re.shape[1]
    nrow = CHUNK * SSM_GROUP
    qf = jnp.transpose(kern[0], (0, 3, 1, 2)).reshape(g, SSM_GROUP, nrow)
    qb = jnp.transpose(kern[1][:, ::-1], (0, 3, 1, 2)).reshape(g, SSM_GROUP, nrow)
    to_rows = lambda w: jnp.transpose(w, (1, 0, 3, 2)).reshape(g, nrow, SSM_STATE)
    bt = jnp.concatenate([to_rows(wr[::-1, 0]), to_rows(wi[::-1, 0]), to_rows(wr[:, 1]), to_rows(wi[:, 1])], axis=-1)

    def out_map(d, pw_r, pw_i):
        w2r = c_re[d][None] * pw_r[:, :, None, :] - c_im[d][None] * pw_i[:, :, None, :]
        w2i = c_re[d][None] * pw_i[:, :, None, :] + c_im[d][None] * pw_r[:, :, None, :]
        cols = lambda w: jnp.transpose(w, (1, 3, 0, 2)).reshape(g, SSM_STATE, nrow)
        return [cols(w2r), -cols(w2i)]

    ct = jnp.concatenate(out_map(0, pwr[1:, 0], pwi[1:, 0]) + out_map(1, pwr[1:, 1][::-1], pwi[1:, 1][::-1]), axis=-2)
    return qf, qb, bt, ct, pwr[CHUNK], pwi[CHUNK]


def _scan_powers(a16r, a16i, n_chunks):
    prs, pis = [], []
    r, i = a16r, a16i
    s = 1
    while s < n_chunks:
        prs.append(jnp.concatenate([r, r], axis=-1))
        pis.append(jnp.concatenate([-i, i], axis=-1))
        r, i = r * r - i * i, 2.0 * r * i
        s *= 2
    pad = 16 - len(prs)
    z = jnp.zeros_like(prs[0])
    return jnp.stack(prs + [z] * pad, axis=2), jnp.stack(pis + [z] * pad, axis=2)


def _low_half():
    return lax.broadcasted_iota(jnp.int32, (1, LANES), 1) < 64


def _to_pair(sa, sb):
    low = _low_half()
    return jnp.where(low, sa, pltpu.roll(sb, 64, 1)), jnp.where(low, pltpu.roll(sa, 64, 1), sb)


def _from_pair(hr, hi):
    low = _low_half()
    return jnp.where(low, hr, pltpu.roll(hi, 64, 1)), jnp.where(low, pltpu.roll(hr, 64, 1), hi)


def _pair_scan(sr, si, pr, pi, reverse, conj):
    n = sr.shape[0]
    row = lax.broadcasted_iota(jnp.int32, sr.shape, 0)

    def shift(x, k):
        if reverse:
            return jnp.where(row < n - k, pltpu.roll(x, n - k, 0), 0.0)
        return jnp.where(row >= k, pltpu.roll(x, k, 0), 0.0)

    hr, hi = shift(sr, 1), shift(si, 1)
    k, j = 1, 0
    while k < n:
        tr, ti = shift(hr, k), shift(hi, k)
        ar = pr[j:j + 1, :]
        ai = -pi[j:j + 1, :] if conj else pi[j:j + 1, :]
        hr, hi = hr + ar * tr - ai * ti, hi + ar * ti + ai * tr
        k, j = 2 * k, j + 1
    return hr, hi


def _transpose8(a):
    a = list(a)
    blk = jnp.right_shift(lax.broadcasted_iota(jnp.int32, (1, LANES), 1), 4)
    for dd in (4, 2, 1):
        upper = jnp.bitwise_and(blk, dd) != 0
        for i in range(8):
            if i & dd:
                continue
            lo, hi = a[i], a[i + dd]
            a[i] = jnp.where(upper, pltpu.roll(hi, dd * SSM_GROUP, 1), lo)
            a[i + dd] = jnp.where(upper, hi, pltpu.roll(lo, LANES - dd * SSM_GROUP, 1))
    return a


def _blocks_from_rows(xs):
    t0, t1 = _transpose8(xs[:8]), _transpose8(xs[8:])
    return [jnp.concatenate([t0[g], t1[g]], axis=1) for g in range(8)]


def _rows_from_blocks(ys):
    return _transpose8([y[:, :LANES] for y in ys]) + _transpose8([y[:, LANES:] for y in ys])


def _pair_states(sa, sb, pr_ref, pi_ref, ga, gb, ncc, adjoint):
    n = sa.shape[0]
    low = _low_half()
    out_a, out_b, pairs = [], [], []
    for dirn in range(2):
        xr, xi = _to_pair(sa[:, dirn * LANES:(dirn + 1) * LANES], sb[:, dirn * LANES:(dirn + 1) * LANES])
        pr = jnp.where(low, pr_ref[dirn, ga], pr_ref[dirn, gb])
        pi = jnp.where(low, -pi_ref[dirn, ga], pi_ref[dirn, gb])
        if dirn == 0:
            hr, hi = _pair_scan(xr, xi, pr, pi, adjoint, adjoint)
        else:
            hr, hi = _pair_scan(pltpu.roll(xr, n - ncc, 0), pltpu.roll(xi, n - ncc, 0), pr, pi, not adjoint, adjoint)
            hr, hi = pltpu.roll(hr, ncc, 0), pltpu.roll(hi, ncc, 0)
        pairs.append((hr, hi))
        ha, hb = _from_pair(hr, hi)
        out_a.append(ha)
        out_b.append(hb)
    return (jnp.concatenate(out_a, axis=1), jnp.concatenate(out_b, axis=1)), pairs


def _shift_lanes(x, n, left):
    if n == 0:
        return x
    lo, hi = x[:, :LANES], x[:, LANES:]
    lane = lax.broadcasted_iota(jnp.int32, lo.shape, 1)
    zero = jnp.zeros_like(lo)
    m = n % LANES
    if not left:
        if n < LANES:
            r_lo, r_hi = pltpu.roll(lo, n, 1), pltpu.roll(hi, n, 1)
            return jnp.concatenate([jnp.where(lane >= n, r_lo, 0.0), jnp.where(lane >= n, r_hi, r_lo)], axis=1)
        r_lo = pltpu.roll(lo, m, 1) if m else lo
        return jnp.concatenate([zero, jnp.where(lane >= m, r_lo, 0.0)], axis=1)
    if n < LANES:
        r_lo, r_hi = pltpu.roll(lo, LANES - n, 1), pltpu.roll(hi, LANES - n, 1)
        return jnp.concatenate([jnp.where(lane < LANES - n, r_lo, r_hi), jnp.where(lane < LANES - n, r_hi, 0.0)], axis=1)
    r_hi = pltpu.roll(hi, LANES - m, 1) if m else hi
    return jnp.concatenate([jnp.where(lane < LANES - m, r_hi, 0.0), zero], axis=1)


def _toeplitz(qf, qb):
    return jnp.concatenate([_shift_lanes(qf, SSM_GROUP * s, False) + _shift_lanes(qb, SSM_GROUP * (CHUNK - 1 - s), True)
                            for s in range(CHUNK)], axis=0)


def _toeplitz_t(dmt):
    dqf, dqb = 0.0, 0.0
    for s in range(CHUNK):
        rows = dmt[s * SSM_GROUP:(s + 1) * SSM_GROUP]
        dqf = dqf + _shift_lanes(rows, SSM_GROUP * s, True)
        dqb = dqb + _shift_lanes(rows, SSM_GROUP * (CHUNK - 1 - s), False)
    return dqf, dqb


def _s5_op_specs(idx, jl):
    q = pl.BlockSpec((None, 8, SSM_GROUP, 256), lambda a, b: (jl, idx(a, b), 0, 0))
    op = pl.BlockSpec((None, 8, 256, 256), lambda a, b: (jl, idx(a, b), 0, 0))
    pw = pl.BlockSpec((None, 2, 8, 16, LANES), lambda a, b: (jl, 0, idx(a, b), 0, 0))
    return [q, q, op, op, pw, pw]


def _s5_fwd(z, col0, s5_ops, jl, lc, name, xchg=None):
    bsz, t, _ = z.shape
    nc, ncc = t // CHUNK, lc // CHUNK

    def body(u_ref, qf_ref, qb_ref, bt_ref, ct_ref, pr_ref, pi_ref, y_ref):
        us = [_bf(u) for u in _blocks_from_rows([u_ref[pl.ds(tok, nc, stride=CHUNK), :] for tok in range(CHUNK)])]
        ys = []
        for ga in range(0, 8, 2):
            gb = ga + 1
            hs, _ = _pair_states(_nn(us[ga], bt_ref[ga]), _nn(us[gb], bt_ref[gb]), pr_ref, pi_ref, ga, gb, ncc, False)
            for g, h in zip((ga, gb), hs):
                ys.append(_nn(us[g], _bf(_toeplitz(qf_ref[g], qb_ref[g]))) + _nn(_bf(h), ct_ref[g]))
        for tok, rows in enumerate(_rows_from_blocks(ys)):
            y_ref[pl.ds(tok, nc, stride=CHUNK), :] = rows

    return _call(
        body, name=name, grid=(bsz, 4),
        in_specs=[pl.BlockSpec((None, t, LANES), lambda b, s: (b, 0, col0 + s))] + _s5_op_specs(lambda b, s: s, jl),
        out_specs=[pl.BlockSpec((None, t, LANES), lambda b, s: (b, 0, s))],
        out_shape=[jax.ShapeDtypeStruct((bsz, t, 512), F32)], vmem=VMEM_BIG, xchg=xchg,
        operands=[z, *s5_ops])


def _s5_bwd(z, col0, dy, s5_ops, jl, lc, name, xchg=None):
    bsz, t, _ = z.shape
    nc, ncc = t // CHUNK, lc // CHUNK

    def body(u_ref, dy_ref, qf_ref, qb_ref, bt_ref, ct_ref, pr_ref, pi_ref,
             du_ref, dqf_ref, dqb_ref, dbt_ref, dct_ref, da_ref):
        b = pl.program_id(1)

        @pl.when(b == 0)
        def _():
            dqf_ref[...] = jnp.zeros_like(dqf_ref)
            dqb_ref[...] = jnp.zeros_like(dqb_ref)
            dbt_ref[...] = jnp.zeros_like(dbt_ref)
            dct_ref[...] = jnp.zeros_like(dct_ref)
            da_ref[...] = jnp.zeros_like(da_ref)
        us = [_bf(u) for u in _blocks_from_rows([u_ref[pl.ds(tok, nc, stride=CHUNK), :] for tok in range(CHUNK)])]
        dys = [_bf(u) for u in _blocks_from_rows([dy_ref[pl.ds(tok, nc, stride=CHUNK), :] for tok in range(CHUNK)])]
        row8 = lax.broadcasted_iota(jnp.int32, (8, LANES), 0)
        dus = []
        for ga in range(0, 8, 2):
            gb = ga + 1
            hs, hp = _pair_states(_nn(us[ga], bt_ref[ga]), _nn(us[gb], bt_ref[gb]), pr_ref, pi_ref, ga, gb, ncc, False)
            gs, gp = _pair_states(_nt(dys[ga], ct_ref[ga]), _nt(dys[gb], ct_ref[gb]), pr_ref, pi_ref, ga, gb, ncc, True)
            for g, h, gg in zip((ga, gb), hs, gs):
                hcat, gcat = _bf(h), _bf(gg)
                dus.append(_nt(dys[g], _bf(_toeplitz(qf_ref[g], qb_ref[g]))) + _nt(gcat, bt_ref[g]))
                dqf, dqb = _toeplitz_t(_tn(us[g], dys[g]))
                dqf_ref[g] += dqf
                dqb_ref[g] += dqb
                dct_ref[g] += _tn(hcat, dys[g])
                dbt_ref[g] += _tn(us[g], gcat)
            da = jnp.zeros((8, LANES), F32)
            for dirn in range(2):
                (hr, hi), (gr, gi) = hp[dirn], gp[dirn]
                da = da + jnp.where(row8 == 2 * dirn, jnp.sum(gr * hr + gi * hi, axis=0, keepdims=True), 0.0)
                da = da + jnp.where(row8 == 2 * dirn + 1, jnp.sum(gi * hr - gr * hi, axis=0, keepdims=True), 0.0)
            da_ref[ga // 2] += da
        for tok, rows in enumerate(_rows_from_blocks(dus)):
            du_ref[pl.ds(tok, nc, stride=CHUNK), :] = rows

    op_out = pl.BlockSpec((8, 256, 256), lambda s, b: (s, 0, 0))
    op_shape = jax.ShapeDtypeStruct((SSM_GROUPS, 256, 256), F32)
    q_out = pl.BlockSpec((8, SSM_GROUP, 256), lambda s, b: (s, 0, 0))
    q_shape = jax.ShapeDtypeStruct((SSM_GROUPS, SSM_GROUP, 256), F32)
    return _call(
        body, name=name, grid=(4, bsz),
        in_specs=[pl.BlockSpec((None, t, LANES), lambda s, b: (b, 0, col0 + s)),
                  pl.BlockSpec((None, t, LANES), lambda s, b: (b, 0, s))] + _s5_op_specs(lambda s, b: s, jl),
        out_specs=[pl.BlockSpec((None, t, LANES), lambda s, b: (b, 0, s)), q_out, q_out, op_out, op_out,
                   pl.BlockSpec((4, 8, LANES), lambda s, b: (s, 0, 0))],
        out_shape=(jax.ShapeDtypeStruct((bsz, t, 512), F32), q_shape, q_shape, op_shape, op_shape,
                   jax.ShapeDtypeStruct((SSM_GROUPS // 2, 8, LANES), F32)),
        vmem=VMEM_BIG, xchg=xchg, operands=[z, dy, *s5_ops])


def _glu_in(y, u, dsk):
    return jax.nn.gelu(y + u * dsk)


def _even_mix(oa, ga, zg, tg, gs):
    return jnp.concatenate([oa * jax.nn.silu(ga), zg * jax.nn.sigmoid(tg) * jax.nn.silu(gs)], axis=1)


def _even_specs(d, i, jl, nct):
    return [_tile(d), _mod_spec(i, nct, d), _tile(512), _tile(512)] + [_tile(256, cb) for cb in range(3, 9)] + [
        _layer_spec(jl, 1, 512), _layer_spec(0, 512, 512), _layer_spec(jl, 1, 512), _layer_spec(0, 1024, d)]


def _cat2(a_ref, b_ref):
    return jnp.concatenate([a_ref[...], b_ref[...]], axis=1)


def _even_out_fwd(hs, mods, oa, y, z, dsk, gw, gb, wout, i, jl, nct, name):
    bsz, t, d = hs.shape

    def body(h_ref, mod_ref, oa_ref, y_ref, ga0, ga1, u0, u1, gs0, gs1, dsk_ref, gw_ref, gb_ref, wo_ref, o_ref):
        zg = _glu_in(y_ref[...], _cat2(u0, u1), dsk_ref[...])
        tg = _nn(_bf(zg), gw_ref[...]) + gb_ref[...]
        mix = _even_mix(oa_ref[...], _cat2(ga0, ga1), zg, tg, _cat2(gs0, gs1))
        o_ref[...] = h_ref[...] + mod_ref[2:3, :] * _nn(_bf(mix), wo_ref[...])

    return pl.pallas_call(
        body, name=name, grid=(bsz, t // ROWT), in_specs=_even_specs(d, i, jl, nct),
        out_specs=_tile(d), out_shape=jax.ShapeDtypeStruct((bsz, t, d), F32), compiler_params=_cp(VMEM_BIG),
    )(hs, mods, oa, y, z, z, z, z, z, z, dsk, gw, gb, wout)


def _even_out_bwd(dh, mods, oa, y, z, dsk, gw, gb, wout, i, jl, nct, name):
    bsz, t, d = dh.shape

    def body(dh_ref, mod_ref, oa_ref, y_ref, ga0, ga1, u0, u1, gs0, gs1, dsk_ref, gw_ref, gb_ref, wo_ref,
             doa_ref, dga_ref, dy_ref, dgs_ref, dmod_ref, ddsk_ref, dgw_ref, dgb_ref, dwo_ref):
        b, j = pl.program_id(0), pl.program_id(1)
        zg, vj1 = jax.vjp(_glu_in, y_ref[...], _cat2(u0, u1), dsk_ref[...])
        zgb = _bf(zg)
        tg = _nn(zgb, gw_ref[...]) + gb_ref[...]
        mix, vj2 = jax.vjp(_even_mix, oa_ref[...], _cat2(ga0, ga1), zg, tg, _cat2(gs0, gs1))
        mixb = _bf(mix)
        dhv = dh_ref[...]
        dyo = _bf(dhv * mod_ref[2:3, :])
        yout = _nn(mixb, wo_ref[...])
        doa, dga, dzg, dtg, dgs = vj2(_nt(dyo, wo_ref[...]))
        dtb = _bf(dtg)
        dzg = dzg + _nt(dtb, gw_ref[...])
        dy, _, ddsk = vj1(dzg)
        doa_ref[...], dga_ref[...], dy_ref[...], dgs_ref[...] = doa, dga, dy, dgs

        @pl.when((j == 0) | (j == nct))
        def _():
            dmod_ref[...] = jnp.zeros_like(dmod_ref)

        @pl.when((b == 0) & (j == 0))
        def _():
            ddsk_ref[...] = jnp.zeros_like(ddsk_ref)
            dgw_ref[...] = jnp.zeros_like(dgw_ref)
            dgb_ref[...] = jnp.zeros_like(dgb_ref)
            dwo_ref[...] = jnp.zeros_like(dwo_ref)
        dmod_ref[2:3, :] += jnp.sum(dhv * yout, axis=0, keepdims=True)
        ddsk_ref[...] += ddsk
        dgw_ref[...] += _tn(zgb, dtb)
        dgb_ref[...] += jnp.sum(dtg, axis=0, keepdims=True)
        dwo_ref[...] += _tn(mixb, dyo)

    full = lambda r, c: pl.BlockSpec((r, c), lambda b, j: (0, 0))
    f512 = jax.ShapeDtypeStruct((bsz, t, 512), F32)
    return pl.pallas_call(
        body, name=name, grid=(bsz, t // ROWT), in_specs=_even_specs(d, i, jl, nct),
        out_specs=[_tile(512), _tile(512), _tile(512), _tile(512), _dmod_spec(nct, d),
                   full(1, 512), full(512, 512), full(1, 512), full(1024, d)],
        out_shape=(f512, f512, f512, f512, jax.ShapeDtypeStruct((2 * bsz, 3, d), F32),
                   jax.ShapeDtypeStruct((1, 512), F32), jax.ShapeDtypeStruct((512, 512), F32),
                   jax.ShapeDtypeStruct((1, 512), F32), jax.ShapeDtypeStruct((1024, d), F32)),
        compiler_params=_cp(VMEM_BIG),
    )(dh, mods, oa, y, z, z, z, z, z, z, dsk, gw, gb, wout)


def _pool(src, col_blk, r, lc, transpose, name):
    bsz, t, _ = src.shape
    segs = [(0, lc, 16), (lc, t - lc, 32 + lc)]
    halo = [(0, 16), (16 + lc, 16), (32 + t, 16)]
    cw = 256

    def inv_count(t0, rows, ln):
        pos = t0 + lax.broadcasted_iota(jnp.int32, (rows, cw), 0)
        cnt = jnp.minimum(pos + r + 1, ln) - jnp.maximum(pos - r, 0)
        return 1.0 / cnt.astype(F32)

    def body(x_ref, o_ref, s_ref):
        for h0, hn in halo:
            s_ref[pl.ds(h0, hn), :] = jnp.zeros((hn, cw), F32)
        for r0, ln, s0 in segs:
            step = min(512, ln)
            for c0 in range(0, ln, step):
                v = x_ref[pl.ds(r0 + c0, step), :]
                s_ref[pl.ds(s0 + c0, step), :] = v * inv_count(c0, step, ln) if transpose else v
        for r0, ln, s0 in segs:
            step = min(512, ln)
            for c0 in range(0, ln, step):
                acc = s_ref[pl.ds(s0 + c0 - r, step), :]
                for dlt in range(-r + 1, r + 1):
                    acc = acc + s_ref[pl.ds(s0 + c0 + dlt, step), :]
                ctr = x_ref[pl.ds(r0 + c0, step), :]
                o_ref[pl.ds(r0 + c0, step), :] = (acc if transpose else acc * inv_count(c0, step, ln)) - ctr

    return pl.pallas_call(
        body, name=name, grid=(bsz,),
        in_specs=[pl.BlockSpec((None, t, cw), lambda b: (b, 0, col_blk))],
        out_specs=pl.BlockSpec((None, t, cw), lambda b: (b, 0, 0)),
        out_shape=jax.ShapeDtypeStruct((bsz, t, cw), F32),
        scratch_shapes=[pltpu.VMEM((t + 48, cw), F32)],
        compiler_params=_cp(VMEM_BIG),
    )(src)


def _odd_specs(d, i, jl, nct):
    return [_tile(d), _mod_spec(i, nct, d), _tile(256), _tile(256), _tile(256), _tile(256), _tile(1024, 1),
            _layer_spec(0, 4, 256, 256), _layer_spec(jl, 1, 1024), _layer_spec(0, 1024, d)]


def _odd_mix(pm, psc, gz):
    return pm * psc * jax.nn.silu(gz)


def _odd_out_fwd(hs, mods, ps, z, pw, psc, wout, i, jl, nct, name, loss=None):
    bsz, t, d = hs.shape

    def h_new(h_ref, mod_ref, p0, p1, p2, p3, gz_ref, pw_ref, psc_ref, wo_ref):
        pm = jnp.concatenate([_nn(_bf(p[...]), pw_ref[q]) for q, p in enumerate((p0, p1, p2, p3))], axis=1)
        mix = _odd_mix(pm, psc_ref[...], gz_ref[...])
        return h_ref[...] + mod_ref[2:3, :] * _nn(_bf(mix), wo_ref[...])

    if loss is None:
        def body(*refs):
            refs[-1][...] = h_new(*refs[:-1])

        return pl.pallas_call(
            body, name=name, grid=(bsz, t // ROWT), in_specs=_odd_specs(d, i, jl, nct),
            out_specs=_tile(d), out_shape=jax.ShapeDtypeStruct((bsz, t, d), F32), compiler_params=_cp(VMEM_BIG),
        )(hs, mods, *ps, z, pw, psc, wout)

    def body(*refs):
        g_ref, t_ref, dh_ref, loss_ref, dg_ref = refs[-5:]
        b, j = pl.program_id(0), pl.program_id(1)
        y, vj = jax.vjp(_rms, h_new(*refs[:-5]), g_ref[...])
        err = jnp.where(j >= nct, y - t_ref[...], 0.0)
        dh, dg = vj(err * (1.0 / d))
        dh_ref[...] = dh

        @pl.when(j == 0)
        def _():
            loss_ref[...] = jnp.zeros_like(loss_ref)

        @pl.when((b == 0) & (j == 0))
        def _():
            dg_ref[...] = jnp.zeros_like(dg_ref)
        loss_ref[...] += 0.5 * jnp.sum(jnp.mean(err * err, axis=-1))
        dg_ref[...] += dg

    return pl.pallas_call(
        body, name=name, grid=(bsz, t // ROWT),
        in_specs=_odd_specs(d, i, jl, nct) + [pl.BlockSpec((1, d), lambda b, j: (0, 0)),
                                              pl.BlockSpec((None, ROWT, d), lambda b, j: (b, jnp.maximum(j - nct, 0), 0))],
        out_specs=[_tile(d), pl.BlockSpec((None, 8, LANES), lambda b, j: (b, 0, 0)),
                   pl.BlockSpec((1, d), lambda b, j: (0, 0))],
        out_shape=(jax.ShapeDtypeStruct((bsz, t, d), F32), jax.ShapeDtypeStruct((bsz, 8, LANES), F32),
                   jax.ShapeDtypeStruct((1, d), F32)),
        compiler_params=_cp(VMEM_BIG),
    )(hs, mods, *ps, z, pw, psc, wout, *loss)


def _odd_out_bwd(dh, mods, ps, z, pw, psc, wout, i, jl, nct, name):
    bsz, t, d = dh.shape

    def body(dh_ref, mod_ref, p0, p1, p2, p3, gz_ref, pw_ref, psc_ref, wo_ref,
             dp_ref, dgz_ref, dmod_ref, dpw_ref, dpsc_ref, dwo_ref):
        b, j = pl.program_id(0), pl.program_id(1)
        pbs = [_bf(p[...]) for p in (p0, p1, p2, p3)]
        pm = jnp.concatenate([_nn(pb, pw_ref[q]) for q, pb in enumerate(pbs)], axis=1)
        mix, vj = jax.vjp(_odd_mix, pm, psc_ref[...], gz_ref[...])
        mixb = _bf(mix)
        dhv = dh_ref[...]
        dyo = _bf(dhv * mod_ref[2:3, :])
        yout = _nn(mixb, wo_ref[...])
        dpm, dpsc, dgz = vj(_nt(dyo, wo_ref[...]))
        dgz_ref[...] = dgz
        dpmb = _bf(dpm)
        dp_ref[...] = jnp.concatenate([_nt(dpmb[:, q * 256:(q + 1) * 256], pw_ref[q]) for q in range(4)], axis=1)

        @pl.when((j == 0) | (j == nct))
        def _():
            dmod_ref[...] = jnp.zeros_like(dmod_ref)

        @pl.when((b == 0) & (j == 0))
        def _():
            dpw_ref[...] = jnp.zeros_like(dpw_ref)
            dpsc_ref[...] = jnp.zeros_like(dpsc_ref)
            dwo_ref[...] = jnp.zeros_like(dwo_ref)
        dmod_ref[2:3, :] += jnp.sum(dhv * yout, axis=0, keepdims=True)
        for q in range(4):
            dpw_ref[q] += _tn(pbs[q], dpmb[:, q * 256:(q + 1) * 256])
        dpsc_ref[...] += dpsc
        dwo_ref[...] += _tn(mixb, dyo)

    full = lambda *s: pl.BlockSpec(s, lambda b, j: (0,) * len(s))
    return pl.pallas_call(
        body, name=name, grid=(bsz, t // ROWT), in_specs=_odd_specs(d, i, jl, nct),
        out_specs=[_tile(1024), _tile(1024), _dmod_spec(nct, d), full(4, 256, 256), full(1, 1024), full(1024, d)],
        out_shape=(jax.ShapeDtypeStruct((bsz, t, 1024), F32), jax.ShapeDtypeStruct((bsz, t, 1024), F32),
                   jax.ShapeDtypeStruct((2 * bsz, 3, d), F32), jax.ShapeDtypeStruct((4, 256, 256), F32),
                   jax.ShapeDtypeStruct((1, 1024), F32), jax.ShapeDtypeStruct((1024, d), F32)),
        compiler_params=_cp(VMEM_BIG),
    )(dh, mods, *ps, z, pw, psc, wout)


def _rms(h, g):
    return h * lax.rsqrt(jnp.mean(h * h, axis=-1, keepdims=True) + EPS) * g


def _adamw(w, g_list, m, v, transpose_g, rows, name):
    nl, r, c = w.shape
    ns = g_list[0].shape[1]
    rt = rows or r
    offs = np.cumsum([0] + [g.shape[0] for g in g_list])
    ng = len(g_list)

    def body(*refs):
        w_ref, m_ref, v_ref = refs[0], refs[1 + ng], refs[2 + ng]
        go_ref, d_ref, mo_ref, vo_ref = refs[3 + ng:]
        layer = pl.program_id(0)
        for k in range(ng):
            g_ref = refs[1 + k]

            @pl.when((layer >= offs[k]) & (layer < offs[k + 1]))
            def _():
                g = g_ref[0].astype(F32)
                for s in range(1, ns):
                    g = g + g_ref[s].astype(F32)
                if transpose_g:
                    g = g.T
                mn = ADAM_B1 * m_ref[...] + (1.0 - ADAM_B1) * g
                vn = ADAM_B2 * v_ref[...] + (1.0 - ADAM_B2) * jnp.square(g)
                m_hat = mn / (1.0 - ADAM_B1 ** ADAM_STEP)
                v_hat = vn / (1.0 - ADAM_B2 ** ADAM_STEP)
                go_ref[...] = g
                d_ref[...] = -ADAM_LR * (m_hat / (jnp.sqrt(v_hat) + ADAM_EPS) + ADAM_WD * w_ref[...])
                mo_ref[...] = mn
                vo_ref[...] = vn

    def g_spec(k):
        lo, n_k = int(offs[k]), int(offs[k + 1] - offs[k])

        def idx(l, q):
            mine = (l >= lo) & (l < lo + n_k)
            q = jnp.where(mine, q, 0)
            return (jnp.clip(l - lo, 0, n_k - 1), 0, 0, q) if transpose_g else (jnp.clip(l - lo, 0, n_k - 1), 0, q, 0)
        return pl.BlockSpec((None, ns, c, rt) if transpose_g else (None, ns, rt, c), idx)

    blk = pl.BlockSpec((None, rt, c), lambda l, q: (l, q, 0))
    out = jax.ShapeDtypeStruct((nl, r, c), F32)
    return pl.pallas_call(
        body, name=name, grid=(nl, r // rt),
        in_specs=[blk] + [g_spec(k) for k in range(ng)] + [blk, blk],
        out_specs=[blk, blk, blk, blk], out_shape=(out, out, out, out), compiler_params=_cp(VMEM_BIG),
    )(w, *g_list, m, v)


def _adamw_nd(ws, gs, ms, vs, lead, name):
    shapes = [w.shape for w in ws]
    as2 = lambda a: a.reshape(1, -1) if a.ndim == 1 else a
    ws, gs, ms, vs = ([as2(a) for a in lst] for lst in (ws, gs, ms, vs))
    nts = len(ws)

    def body(*refs):
        for k in range(nts):
            w_ref, g_ref, m_ref, v_ref = refs[4 * k:4 * k + 4]
            d_ref, mo_ref, vo_ref = refs[4 * nts + 3 * k:4 * nts + 3 * k + 3]
            g = g_ref[...]
            mn = ADAM_B1 * m_ref[...] + (1.0 - ADAM_B1) * g
            vn = ADAM_B2 * v_ref[...] + (1.0 - ADAM_B2) * jnp.square(g)
            m_hat = mn / (1.0 - ADAM_B1 ** ADAM_STEP)
            v_hat = vn / (1.0 - ADAM_B2 ** ADAM_STEP)
            d_ref[...] = -ADAM_LR * (m_hat / (jnp.sqrt(v_hat) + ADAM_EPS) + ADAM_WD * w_ref[...])
            mo_ref[...] = mn
            vo_ref[...] = vn

    grid = tuple(ws[0].shape[:lead])

    def spec(a):
        rest = a.shape[lead:]
        return pl.BlockSpec((None,) * lead + tuple(rest), lambda *ids: tuple(ids) + (0,) * len(rest))

    ins = [a for k in range(nts) for a in (ws[k], gs[k], ms[k], vs[k])]
    outs = [w for w in ws for _ in range(3)]
    res = pl.pallas_call(
        body, name=name, grid=grid, in_specs=[spec(a) for a in ins], out_specs=[spec(a) for a in outs],
        out_shape=[jax.ShapeDtypeStruct(a.shape, F32) for a in outs], compiler_params=_cp(VMEM_BIG))(*ins)
    return [tuple(res[3 * k + q].reshape(shapes[k]) for q in range(3)) for k in range(nts)]


def _sum_slots(slots, name):
    ns, r, c = slots.shape

    def body(s_ref, o_ref):
        g = s_ref[0]
        for s in range(1, ns):
            g = g + s_ref[s]
        o_ref[...] = g

    return pl.pallas_call(body, name=name, out_shape=jax.ShapeDtypeStruct((r, c), F32), compiler_params=_cp(VMEM_BIG))(slots)


def _pack(arrs):
    flat = jnp.concatenate([a.reshape(-1) for a in arrs])
    return jnp.pad(flat, (0, (-flat.shape[0]) % (PACK_ROWS * PACK_W))).reshape(-1, PACK_W)


def _unpack(buf, shapes):
    flat = buf.reshape(-1)
    out, off = [], 0
    for s in shapes:
        n = int(np.prod(s))
        out.append(flat[off:off + n].reshape(s))
        off += n
    return out


def kernel(x, c, ctx, c_ctx, ada_w, ada_b, norm_g, even_w_in, even_w_out, attn_sink, ssm_a_re, ssm_a_im, ssm_log_dt, ssm_b_re, ssm_b_im, ssm_c_re, ssm_c_im, ssm_d, glu_w, glu_b, odd_w_in, odd_w_out, pool_w, pool_scale, final_g, loss_target, m_c_ctx, m_ada_w, m_ada_b, m_norm_g, m_even_w_in, m_even_w_out, m_attn_sink, m_ssm_a_re, m_ssm_a_im, m_ssm_log_dt, m_ssm_b_re, m_ssm_b_im, m_ssm_c_re, m_ssm_c_im, m_ssm_d, m_glu_w, m_glu_b, m_odd_w_in, m_odd_w_out, m_pool_w, m_pool_scale, m_final_g, v_c_ctx, v_ada_w, v_ada_b, v_norm_g, v_even_w_in, v_even_w_out, v_attn_sink, v_ssm_a_re, v_ssm_a_im, v_ssm_log_dt, v_ssm_b_re, v_ssm_b_im, v_ssm_c_re, v_ssm_c_im, v_ssm_d, v_glu_w, v_glu_b, v_odd_w_in, v_odd_w_out, v_pool_w, v_pool_scale, v_final_g):
    args = dict(locals())
    bsz, l, d = x.shape
    lc = ctx.shape[1]
    t = lc + l
    nct = lc // ROWT
    depth = ada_w.shape[0]
    nl2 = even_w_in.shape[0]
    me = _my_index()
    assert lc % ROWT == 0 and l % ROWT == 0 and d == 1024 and bsz * N_DEV < 32 and depth % 2 == 0

    npw = pool_w.shape[2]
    shards = {"even": [_bf(jnp.transpose(even_w_in, (0, 2, 1))), _bf(even_w_out), _bf(glu_w)],
              "odd": [_bf(jnp.transpose(odd_w_in, (0, 2, 1))), _bf(odd_w_out), _bf(pool_w.reshape(nl2, -1, pool_w.shape[-1]))]}

    def wgather(kind, jl):
        return shards[kind], [("gather", [(ii, jl)], 0) for ii in range(3)]

    def wjoin(kind, res):
        full = [a.reshape(1, N_DEV * a.shape[2], a.shape[3]) for a in res]
        if kind == "odd":
            full[2] = jnp.transpose(res[2].reshape(1, N_DEV, 4, npw, 256), (0, 2, 1, 3, 4)).reshape(1, 4, 256, 256)
        return full

    got = _exchange([shards["even"][0], c, pool_scale],
                    [("gather", [(0, 0)], 0), ("gather", [(1, None)], 0), ("gather", [(2, None)], 0)], "gather_first")
    first_wt = got[0]
    c_all = got[1].reshape(N_DEV * bsz, d)
    psc_full = jnp.transpose(got[2][0], (1, 0, 2)).reshape(nl2, 1, d)
    weights = {}

    n_rows = 32
    craw = jnp.concatenate([c_all, c_ctx[None], jnp.zeros((n_rows - N_DEV * bsz - 1, d), F32)], axis=0)
    ncol = ada_w.shape[2]
    ada_b_loc = lax.dynamic_slice_in_dim(ada_b, me * ncol, ncol, axis=1)[:, None, :]
    m_loc = _adaln_fwd(craw, ada_w, ada_b_loc)
    m_all = _exchange([m_loc], [("gather", [(0, None)], 0)], "gather_mod")[0][0]
    m_all = jnp.transpose(m_all, (1, 2, 0, 3)).reshape(depth, n_rows, 3 * d)
    lat = lax.dynamic_slice_in_dim(m_all, me * bsz, bsz, axis=1).reshape(depth, bsz, 1, 3, d)
    cmod = jnp.broadcast_to(m_all[:, N_DEV * bsz].reshape(depth, 1, 1, 3, d), (depth, bsz, 1, 3, d))
    mods = jnp.concatenate([cmod, lat], axis=2).reshape(depth, 2 * bsz, 3, d)

    cos, sin = _rope_tables(lc, l)
    hs = jnp.concatenate([ctx, x], axis=1)
    g3 = norm_g[:, None, :]
    dsk3, gb3 = ssm_d[:, None, :], glu_b[:, None, :]
    u_col = 1280 // LANES

    s5_prm = (ssm_a_re, ssm_a_im, ssm_log_dt, ssm_b_re, ssm_b_im, ssm_c_re, ssm_c_im)
    ops = jax.vmap(_s5_operators)(*s5_prm)
    pr, pi = jax.vmap(lambda r, q: _scan_powers(r, q, t // CHUNK))(ops[4], ops[5])
    s5_ops = (ops[0], ops[1], _bf(ops[2]), _bf(ops[3]), pr, pi)
    saved = []
    for i in range(depth):
        jl = i // 2
        if i % 2 == 0:
            if i == 0:
                wt = first_wt.reshape(1, -1, d)
                (z, qp, kvp), got = _fused_in_fwd(
                    hs, mods, g3, wt, i, 0, nct, f"in_fwd{i}", rope=(cos, sin),
                    xchg=(shards["even"][1:], [("gather", [(0, 0)], 0), ("gather", [(1, 0)], 0)]))
                weights[("even", 0)] = wjoin("even", [first_wt] + got)
            wt, wo, gwf = weights[("even", jl)]
            if i > 0:
                (z, qp, kvp), _ = _fused_in_fwd(hs, mods, g3, wt, i, 0, nct, f"in_fwd{i}", rope=(cos, sin))
            (oa, lse), got = _attn_fwd(qp, kvp, attn_sink, jl, lc, f"attn_fwd{i}", xchg=wgather("odd", jl))
            weights[("odd", jl)] = wjoin("odd", got)
            nxt = wgather("even", jl + 1) if jl + 1 < nl2 else None
            (y,), got = _s5_fwd(z, u_col, s5_ops, jl, lc, f"s5_fwd{i}", xchg=nxt)
            if nxt:
                weights[("even", jl + 1)] = wjoin("even", got)
            hn = _even_out_fwd(hs, mods, oa, y, z, dsk3, gwf, gb3, wo, i, jl, nct, f"out_fwd{i}")
            saved.append(dict(hs=hs, z=z, qp=qp, kvp=kvp, oa=oa, lse=lse, y=y))
        else:
            wt, wo, pwf = weights[("odd", jl)]
            (z,), _ = _fused_in_fwd(hs, mods, g3, wt, i, 0, nct, f"in_fwd{i}")
            ps = [_pool(z, gi, wd // 2, lc, False, f"pool_fwd{i}_{gi}") for gi, wd in enumerate(POOL_WINDOWS)]
            saved.append(dict(hs=hs, z=z, ps=ps))
            if i == depth - 1:
                dh, loss_p, d_final_g = _odd_out_fwd(hs, mods, ps, z, pwf, psc_full, wo, i, jl, nct, f"out_fwd{i}",
                                                     loss=(final_g[None], loss_target))
                break
            hn = _odd_out_fwd(hs, mods, ps, z, pwf, psc_full, wo, i, jl, nct, f"out_fwd{i}")
        hs = hn

    loss = lax.psum(jnp.sum(loss_p[:, 0, 0]), ("x", "y", "c"))

    slots = {n: [None] * nl2 for n in ("even_w_in", "even_w_out", "glu_w", "odd_w_in", "odd_w_out", "pool_w", "pool_scale")}
    gsmall = {n: [None] * nl2 for n in ("attn_sink", "ssm_d", "glu_b")}
    d_norm_g, d_mods = [None] * depth, [None] * depth
    d_ops = [None] * nl2

    def rows_xchg(named):
        arrs = [a for _, _, a in named]
        specs = [("rows2" if n == "pool_w" else "rows", [(k, None)], a.shape[1 if n == "pool_w" else 0] // N_DEV)
                 for k, (n, _, a) in enumerate(named)]
        return (arrs, specs)

    def keep(named, res):
        for (n, jl_, _), r_ in zip(named, res):
            slots[n][jl_] = r_

    pending = None
    for i in reversed(range(depth)):
        jl = i // 2
        sv = saved[i]
        if i % 2 == 0:
            wt, wo, gwf = weights[("even", jl)]
            doa, dga, dy, dgs, dmod_g, ddsk, dgw, dgb, dwo = _even_out_bwd(
                dh, mods, sv["oa"], sv["y"], sv["z"], dsk3, gwf, gb3, wo, i, jl, nct, f"out_bwd{i}")
            (dq_r, dkv_acc, dsink), got = _attn_bwd(sv["qp"], sv["kvp"], attn_sink, jl, sv["oa"], sv["lse"], doa, lc,
                                                     f"attn_bwd{i}", xchg=rows_xchg(pending) if pending else None)
            if pending:
                keep(pending, got)
            mine =[("even_w_out", jl, dwo), ("glu_w", jl, dgw)]
            (du, dqf, dqb, dbt, dct, dav), got = _s5_bwd(sv["z"], u_col, dy, s5_ops, jl, lc, f"s5_bwd{i}", xchg=rows_xchg(mine))
            keep(mine, got)
            per_group = lambda q: dav[:, q].reshape(SSM_GROUPS, SSM_STATE)
            d_ops[jl] = (dqf, dqb, dbt, dct, jnp.stack([per_group(0), per_group(2)]), jnp.stack([per_group(1), per_group(3)]))
            gsmall["attn_sink"][jl] = jnp.sum(dsink[:, :, 0], axis=0)
            gsmall["ssm_d"][jl] = ddsk[0]
            gsmall["glu_b"][jl] = dgb[0]
            pieces = [(dq_r, 512, 0), (dkv_acc, 512, 0), (dga, 512, 0), (du, 512, 0), (dy, 512, 0), (dgs, 512, 0)]
            asm = lambda p, s, tb: _rope_unpad(p[0], p[1], tb[0], tb[1]) + [p[2], p[3] + p[4] * s, p[5]]
            (dh, dmod_in, dg, dw), _ = _fused_in_bwd(sv["hs"], mods, g3, wt, dh, pieces, (dsk3, jl), asm, i, 0, nct, f"in_bwd{i}",
                                                     latent_only=(i == 0), tables=(cos, sin))
            pending = [("even_w_in", jl, dw)]
        else:
            wt, wo, pwf = weights[("odd", jl)]
            dp, dgz, dmod_g, dpw, dpsc, dwo = _odd_out_bwd(
                dh, mods, sv["ps"], sv["z"], pwf, psc_full, wo, i, jl, nct, f"out_bwd{i}")
            dus = [_pool(dp, gi, wd // 2, lc, True, f"pool_bwd{i}_{gi}") for gi, wd in enumerate(POOL_WINDOWS)]
            dpsc8 = jnp.broadcast_to(dpsc.reshape(N_DEV, 1, -1), (N_DEV, 8, d // N_DEV)).reshape(N_DEV * 8, -1)
            pieces = [(u_, 256, 0) for u_ in dus] + [(dgz, 1024, 0)]
            asm = lambda p, s, tb: p
            (dh, dmod_in, dg, dw), got = _fused_in_bwd(sv["hs"], mods, g3, wt, dh, pieces, None, asm, i, 0, nct, f"in_bwd{i}",
                                                       xchg=rows_xchg(pending) if pending else None)
            if pending:
                keep(pending, got)
            pending = [("odd_w_in", jl, dw), ("odd_w_out", jl, dwo), ("pool_w", jl, dpw), ("pool_scale", jl, dpsc8)]
        d_norm_g[i] = dg[0]
        d_mods[i] = (dmod_g + dmod_in).reshape(bsz, 2, 3 * d)
    grad_x = dh

    _, op_vjp = jax.vjp(jax.vmap(_s5_operators), *s5_prm)
    dprm = op_vjp(tuple(jnp.stack([d_ops[jl][q] for jl in range(nl2)]) for q in range(6)))

    dmd = jnp.stack(d_mods)
    dm_loc = jnp.concatenate([dmd[:, :, 1], jnp.sum(dmd[:, :, 0], axis=1, keepdims=True)], axis=1)
    dm_all = _exchange([dm_loc], [("gather", [(0, None)], 0)], "gather_dmod")[0][0]
    dm_lat = jnp.transpose(dm_all[:, :, :bsz], (1, 0, 2, 3)).reshape(depth, N_DEV * bsz, 3 * d)
    dm_ctx = dm_all[0, :, bsz]
    for e in range(1, N_DEV):
        dm_ctx = dm_ctx + dm_all[e, :, bsz]
    dm_rows = jnp.concatenate([dm_lat, dm_ctx[:, None], jnp.zeros((depth, n_rows - N_DEV * bsz - 1, 3 * d), F32)], axis=1)
    dm_cols = lax.dynamic_slice_in_dim(dm_rows, me * ncol, ncol, axis=2)
    g_ada_w, dcraw = _adaln_bwd(craw, dm_cols, ada_w)

    small = {"c_ctx": dcraw[N_DEV * bsz], "ada_b": jnp.sum(dm_loc, axis=1), "norm_g": jnp.stack(d_norm_g),
             "final_g": d_final_g[0]}
    for n, v2 in gsmall.items():
        small[n] = jnp.stack(v2)
    for n, gval in zip(("ssm_a_re", "ssm_a_im", "ssm_log_dt", "ssm_b_re", "ssm_b_im", "ssm_c_re", "ssm_c_im"), dprm):
        small[n] = gval
    snames = list(small)
    sshapes = [args[n].shape for n in snames]
    spack = _pack([small[n].reshape(args[n].shape) for n in snames])

    pending = [(n, l_, _bf(a)) for n, l_, a in pending]
    last = pending + [("small", 0, spack)]
    red = _exchange(*rows_xchg(last), "reduce_last")
    keep(pending, red[:-1])
    s_sum = _sum_slots(red[-1][0], "sum_small")
    s_all = _exchange([s_sum], [("gather", [(0, None)], 0)], "gather_small")[0]
    s_all = s_all.reshape(1, 1, -1, PACK_W)

    out = {}

    def put(n, res, shape):
        for kind, buf in zip(("grad_", "delta_", "new_m_", "new_v_"), res):
            out[kind + n] = buf.reshape(shape)

    def as3(a):
        return a.reshape(a.shape[0], -1, a.shape[-1])

    for n in slots:
        w = args[n]
        g_list = [s_[:, :, :1] if n == "pool_scale" else s_ for s_ in slots[n]]
        g_list = [s_.reshape(1, N_DEV, -1, s_.shape[-1]) for s_ in g_list]
        tr = n in ("even_w_in", "odd_w_in")
        put(n, _adamw(as3(w), g_list, as3(args["m_" + n]), as3(args["v_" + n]), tr, 256 if tr else None, "adamw_" + n),
            w.shape)
    put("ada_w", _adamw(ada_w, [g_ada_w[:, None]], m_ada_w, v_ada_w, False, None, "adamw_ada_w"), ada_w.shape)
    for n, a in zip(snames, _unpack(s_all, sshapes)):
        out["grad_" + n] = a
    for lead, name in ((0, "adamw_small"), (2, "adamw_ssm_bc")):
        grp = [n for n in snames if (args[n].ndim == 5) == (lead == 2)]
        res = _adamw_nd([args[n] for n in grp], [out["grad_" + n] for n in grp], [args["m_" + n] for n in grp],
                        [args["v_" + n] for n in grp], lead, name)
        for n, (dl, mn, vn) in zip(grp, res):
            out["delta_" + n], out["new_m_" + n], out["new_v_" + n] = dl, mn, vn

    order = ["c_ctx", "ada_w", "ada_b", "norm_g", "even_w_in", "even_w_out", "attn_sink", "ssm_a_re", "ssm_a_im",
             "ssm_log_dt", "ssm_b_re", "ssm_b_im", "ssm_c_re", "ssm_c_im", "ssm_d", "glu_w", "glu_b", "odd_w_in",
             "odd_w_out", "pool_w", "pool_scale", "final_g"]
    return (loss, grad_x, *[out[k + n] for k in ("grad_", "delta_", "new_m_", "new_v_") for n in order])
```
